```python
import math
import jax, jax.numpy as jnp
from jax import lax
import numpy as np

D_MODEL = 1024
BATCH = 8
SEQ = 2048
DEPTH = 4
DEC_BATCH = 8
DEC_SEQ = 64
PAST_LEN = 1024

CHUNK = 64
N_MIXERS = 4
N_A = (DEPTH + 3) // 4
N_B = (DEPTH + 2) // 4
N_C = (DEPTH + 1) // 4
N_D = DEPTH // 4
D_FF = 4 * D_MODEL
EPS = 1e-6

GDN_HEADS = 8
GDN_DK = 128
GDN_DV = 128
GDN_CONV = 4
GDN_QK = GDN_HEADS * GDN_DK
GDN_VW = GDN_HEADS * GDN_DV
GDN_CONV_CH = 2 * GDN_QK + GDN_VW
GDN_IN = GDN_CONV_CH + GDN_VW + 2 * GDN_HEADS

GLA_HEADS = 4
GLA_DK = D_MODEL // 2 // GLA_HEADS
GLA_DV = D_MODEL // GLA_HEADS
GLA_QK = GLA_HEADS * GLA_DK
GLA_RANK = 16
GLA_TAU = 16.0
GLA_IN = 2 * GLA_QK + 2 * D_MODEL + GLA_RANK

GMLP_BLOCK = 128
GMLP_GROUPS = 4
GMLP_HALF = D_MODEL
GMLP_GC = GMLP_HALF // GMLP_GROUPS

CONV_W = 31
CONV_CH = D_MODEL

kernel_name = 'hybrid_streaming_gdn_gla_gmlp_conv_step'


def rmsnorm(x, w):
    xf = x.astype(jnp.float32)
    y = xf * lax.rsqrt(jnp.mean(xf * xf, axis=-1, keepdims=True) + EPS)
    return (y * w.astype(jnp.float32)).astype(x.dtype)


def layernorm(x, w, b):
    xf = x.astype(jnp.float32)
    xc = xf - jnp.mean(xf, axis=-1, keepdims=True)
    y = xc * lax.rsqrt(jnp.mean(xc * xc, axis=-1, keepdims=True) + EPS)
    return (y * w.astype(jnp.float32) + b.astype(jnp.float32)).astype(x.dtype)


def l2norm(x):
    xf = x.astype(jnp.float32)
    return xf * lax.rsqrt(jnp.sum(xf * xf, axis=-1, keepdims=True) + EPS)


def causal_dwconv(x, buf, w, bias=None):
    width, ch = w.shape
    xc = jnp.concatenate([buf.astype(x.dtype), x], axis=1)
    y = lax.conv_general_dilated(xc, w[:, None, :].astype(x.dtype), window_strides=(1,), padding='VALID',
                                 dimension_numbers=('NWC', 'WIO', 'NWC'), feature_group_count=ch)
    if bias is not None:
        y = y + bias
    return y, xc[:, -(width - 1):]


def _to_blocks(t, n_blk):
    b, l = t.shape[:2]
    t = jnp.pad(t, [(0, 0), (0, n_blk * CHUNK - l)] + [(0, 0)] * (t.ndim - 2))
    t = t.reshape((b, n_blk, CHUNK) + t.shape[2:])
    return t.transpose((1, 0, 3, 2) + tuple(range(4, t.ndim)))


def _from_blocks(o, l):
    n, b, h, c, d = o.shape
    return o.transpose(1, 0, 3, 2, 4).reshape(b, n * c, h, d)[:, :l]


def gated_delta_rule(q, k, v, g, beta, s0):
    l = q.shape[1]
    n_blk = -(-l // CHUNK)
    qb, kb, vb, gb, bb = (_to_blocks(t, n_blk) for t in (q, k, v, g, beta))
    incl = jnp.tril(jnp.ones((CHUNK, CHUNK), bool))
    strict = jnp.tril(jnp.ones((CHUNK, CHUNK), bool), -1)
    eye = jnp.eye(CHUNK, dtype=jnp.float32)

    def step(s, blk):
        qc, kc, vc, gc, bc = blk
        gcum = jnp.cumsum(gc, axis=-1)
        decay = jnp.exp(jnp.where(incl, gcum[..., :, None] - gcum[..., None, :], -jnp.inf))
        kbeta = kc * bc[..., None]
        a = jnp.where(strict, jnp.einsum('bhik,bhjk->bhij', kbeta, kc) * decay, 0.0)
        t = lax.linalg.triangular_solve(eye + a, jnp.broadcast_to(eye, a.shape), left_side=True,
                                        lower=True, unit_diagonal=True)
        u = t @ (vc * bc[..., None])
        w = t @ (kbeta * jnp.exp(gcum)[..., None])
        v_new = u - w @ s
        attn = jnp.einsum('bhik,bhjk->bhij', qc, kc) * decay
        o = (qc * jnp.exp(gcum)[..., None]) @ s + attn @ v_new
        g_last = gcum[..., -1:]
        s = s * jnp.exp(g_last)[..., None] + jnp.einsum(
            'bhck,bhcv->bhkv', kc * jnp.exp(g_last - gcum)[..., None], v_new)
        return s, o

    s, ob = lax.scan(step, s0, (qb, kb, vb, gb, bb))
    return _from_blocks(ob, l), s


def gla_recurrence(q, k, v, g, s0):
    l = q.shape[1]
    n_blk = -(-l // CHUNK)
    qb, kb, vb, gb = (_to_blocks(t, n_blk) for t in (q, k, v, g))
    incl = jnp.tril(jnp.ones((CHUNK, CHUNK), bool))

    def step(s, blk):
        qc, kc, vc, gc = blk
        bcum = jnp.cumsum(gc, axis=-2)
        rel = jnp.exp(jnp.where(incl[:, :, None], bcum[..., :, None, :] - bcum[..., None, :, :], -jnp.inf))
        attn = jnp.einsum('bhik,bhjk,bhijk->bhij', qc, kc, rel)
        o = (qc * jnp.exp(bcum)) @ s + attn @ vc
        b_last = bcum[..., -1:, :]
        s = s * jnp.exp(b_last)[..., 0, :, None] + jnp.einsum(
            'bhck,bhcv->bhkv', kc * jnp.exp(b_last - bcum), vc)
        return s, o

    s, ob = lax.scan(step, s0, (qb, kb, vb, gb))
    return _from_blocks(ob, l), s


def gdn_mixer(h, conv_buf, s0, w_in, conv_w, a_log, dt_bias, norm_w, w_out):
    bn, l, _ = h.shape
    p = h @ w_in
    o1, o2, o3 = GDN_CONV_CH, GDN_CONV_CH + GDN_VW, GDN_CONV_CH + GDN_VW + GDN_HEADS
    qkv, new_buf = causal_dwconv(p[..., :o1], conv_buf, conv_w)
    qkv = jax.nn.silu(qkv)
    q = l2norm(qkv[..., :GDN_QK].reshape(bn, l, GDN_HEADS, GDN_DK)) * (GDN_DK ** -0.5)
    k = l2norm(qkv[..., GDN_QK:2 * GDN_QK].reshape(bn, l, GDN_HEADS, GDN_DK))
    v = qkv[..., 2 * GDN_QK:].reshape(bn, l, GDN_HEADS, GDN_DV).astype(jnp.float32)
    z = p[..., o1:o2].reshape(bn, l, GDN_HEADS, GDN_DV)
    beta = jax.nn.sigmoid(p[..., o2:o3].astype(jnp.float32))
    g = -jnp.exp(a_log.astype(jnp.float32)) * jax.nn.softplus(
        p[..., o3:].astype(jnp.float32) + dt_bias.astype(jnp.float32))
    o, s = gated_delta_rule(q, k, v, g, beta, s0.astype(jnp.float32))
    o = rmsnorm(o.astype(h.dtype), norm_w) * jax.nn.silu(z)
    return o.reshape(bn, l, GDN_VW) @ w_out, new_buf, s.astype(s0.dtype)


def gla_mixer(h, s0, w_in, w_g2, b_g, norm_w, w_out):
    bn, l, _ = h.shape
    p = h @ w_in
    q = p[..., :GLA_QK].reshape(bn, l, GLA_HEADS, GLA_DK).astype(jnp.float32) * (GLA_DK ** -0.5)
    k = p[..., GLA_QK:2 * GLA_QK].reshape(bn, l, GLA_HEADS, GLA_DK).astype(jnp.float32)
    v = p[..., 2 * GLA_QK:2 * GLA_QK + D_MODEL].reshape(bn, l, GLA_HEADS, GLA_DV).astype(jnp.float32)
    r = p[..., 2 * GLA_QK + D_MODEL:2 * GLA_QK + 2 * D_MODEL].reshape(bn, l, GLA_HEADS, GLA_DV)
    g_low = p[..., 2 * GLA_QK + 2 * D_MODEL:]
    g = jax.nn.log_sigmoid((g_low @ w_g2 + b_g).astype(jnp.float32)) / GLA_TAU
    g = g.reshape(bn, l, GLA_HEADS, GLA_DK)
    o, s = gla_recurrence(q, k, v, g, s0.astype(jnp.float32))
    o = rmsnorm(o.astype(h.dtype), norm_w) * jax.nn.silu(r)
    return o.reshape(bn, l, D_MODEL) @ w_out, s.astype(s0.dtype)


def gmlp_mixer(h, w_in, b_in, ln_w, ln_b, w_s, b_s, w_out, b_out):
    bn, l, _ = h.shape
    zz = jax.nn.gelu(h @ w_in + b_in)
    u = zz[..., :GMLP_HALF]
    v = layernorm(zz[..., GMLP_HALF:], ln_w, ln_b)
    n_blk = -(-l // GMLP_BLOCK)
    vp = jnp.pad(v, ((0, 0), (0, n_blk * GMLP_BLOCK - l), (0, 0)))
    vp = vp.reshape(bn, n_blk, GMLP_BLOCK, GMLP_GROUPS, GMLP_GC)
    ws = jnp.where(jnp.tril(jnp.ones((GMLP_BLOCK, GMLP_BLOCK), bool)), w_s, 0.0).astype(v.dtype)
    mixed = jnp.einsum('gts,bnsgc->bntgc', ws, vp) + b_s.T[:, :, None]
    mixed = mixed.reshape(bn, n_blk * GMLP_BLOCK, GMLP_HALF)[:, :l]
    v_tail = v[:, (l // GMLP_BLOCK) * GMLP_BLOCK:]
    return (u * mixed) @ w_out + b_out, v_tail


def conformer_conv_mixer(h, buf, w_pw1, b_pw1, w_dw, b_dw, ln_w, ln_b, w_pw2, b_pw2):
    a = h @ w_pw1 + b_pw1
    glu = a[..., :CONV_CH] * jax.nn.sigmoid(a[..., CONV_CH:])
    y, new_buf = causal_dwconv(glu, buf, w_dw, b_dw)
    y = jax.nn.silu(layernorm(y, ln_w, ln_b))
    return y @ w_pw2 + b_pw2, new_buf


def sq_relu_mlp(h, w1, w2):
    return jnp.square(jax.nn.relu(h @ w1)) @ w2


def setup_inputs(seed: int = 0) -> dict:
    key = jax.random.key(seed)
    kit = iter(list(jax.random.split(key, 64)))

    def nrm(shape, scale):
        return jax.random.normal(next(kit), shape, jnp.float32) * scale

    def gain(shape):
        return 1.0 + nrm(shape, 0.02)

    d = D_MODEL
    dt = jnp.exp(jax.random.uniform(next(kit), (N_A, GDN_HEADS), jnp.float32,
                                    minval=math.log(1e-3), maxval=math.log(1e-1)))
    inp = {
        'x_prompt': nrm((BATCH, SEQ, d), 1.0),
        'x_sample': nrm((DEC_BATCH, DEC_SEQ, d), 1.0),
        'cache_gdn_conv': nrm((N_A, DEC_BATCH, GDN_CONV - 1, GDN_CONV_CH), 1.0),
        'state_gdn': nrm((N_A, DEC_BATCH, GDN_HEADS, GDN_DK, GDN_DV), 0.1),
        'state_gla': nrm((N_B, DEC_BATCH, GLA_HEADS, GLA_DK, GLA_DV), 0.1),
        'cache_conformer': nrm((N_D, DEC_BATCH, CONV_W - 1, CONV_CH), 1.0),
        'mix_norm_w': gain((DEPTH, d)),
        'ffn_norm_w': gain((DEPTH, d)),
        'ffn_w1': nrm((DEPTH, d, D_FF), d ** -0.5),
        'ffn_w2': nrm((DEPTH, D_FF, d), D_FF ** -0.5),
        'final_norm_w': gain((d,)),
        'gdn_w_in': nrm((N_A, d, GDN_IN), d ** -0.5),
        'gdn_conv_w': nrm((N_A, GDN_CONV, GDN_CONV_CH), GDN_CONV ** -0.5),
        'gdn_a_log': jnp.log(jax.random.uniform(next(kit), (N_A, GDN_HEADS), jnp.float32, minval=1.0, maxval=16.0)),
        'gdn_dt_bias': dt + jnp.log(-jnp.expm1(-dt)),
        'gdn_norm_w': gain((N_A, GDN_DV)),
        'gdn_w_out': nrm((N_A, GDN_VW, d), GDN_VW ** -0.5),
        'gla_w_in': nrm((N_B, d, GLA_IN), d ** -0.5),
        'gla_w_g2': nrm((N_B, GLA_RANK, GLA_QK), GLA_RANK ** -0.5),
        'gla_b_g': nrm((N_B, GLA_QK), 0.1),
        'gla_norm_w': gain((N_B, GLA_HEADS, GLA_DV)),
        'gla_w_out': nrm((N_B, d, d), d ** -0.5),
        'gmlp_w_in': nrm((N_C, d, 2 * GMLP_HALF), d ** -0.5),
        'gmlp_b_in': nrm((N_C, 2 * GMLP_HALF), 0.02),
        'gmlp_ln_w': gain((N_C, GMLP_HALF)),
        'gmlp_ln_b': nrm((N_C, GMLP_HALF), 0.02),
        'gmlp_w_s': nrm((N_C, GMLP_GROUPS, GMLP_BLOCK, GMLP_BLOCK), GMLP_BLOCK ** -0.5),
        'gmlp_b_s': gain((N_C, GMLP_GROUPS, GMLP_BLOCK)),
        'gmlp_w_out': nrm((N_C, GMLP_HALF, d), GMLP_HALF ** -0.5),
        'gmlp_b_out': nrm((N_C, d), 0.02),
        'cnv_w_pw1': nrm((N_D, d, 2 * CONV_CH), d ** -0.5),
        'cnv_b_pw1': nrm((N_D, 2 * CONV_CH), 0.02),
        'cnv_w_dw': nrm((N_D, CONV_W, CONV_CH), CONV_W ** -0.5),
        'cnv_b_dw': nrm((N_D, CONV_CH), 0.02),
        'cnv_ln_w': gain((N_D, CONV_CH)),
        'cnv_ln_b': nrm((N_D, CONV_CH), 0.02),
        'cnv_w_pw2': nrm((N_D, CONV_CH, d), CONV_CH ** -0.5),
        'cnv_b_pw2': nrm((N_D, d), 0.02),
    }
    return inp


def reference(x_prompt, x_sample, cache_gdn_conv, state_gdn, state_gla, cache_conformer,
              mix_norm_w, ffn_norm_w, ffn_w1, ffn_w2, final_norm_w,
              gdn_w_in, gdn_conv_w, gdn_a_log, gdn_dt_bias, gdn_norm_w, gdn_w_out,
              gla_w_in, gla_w_g2, gla_b_g, gla_norm_w, gla_w_out,
              gmlp_w_in, gmlp_b_in, gmlp_ln_w, gmlp_ln_b, gmlp_w_s, gmlp_b_s, gmlp_w_out, gmlp_b_out,
              cnv_w_pw1, cnv_b_pw1, cnv_w_dw, cnv_b_dw, cnv_ln_w, cnv_ln_b, cnv_w_pw2, cnv_b_pw2):
    xp, xs = x_prompt, x_sample
    bp = xp.shape[0]
    gdn_conv_p, gdn_conv_s, gdn_s_p, gdn_s_s = [], [], [], []
    gla_s_p, gla_s_s, gmlp_v_s, cnv_p, cnv_s = [], [], [], [], []
    for i in range(DEPTH):
        kind, j = i % N_MIXERS, i // N_MIXERS
        hp = rmsnorm(xp, mix_norm_w[i])
        hs = rmsnorm(xs, mix_norm_w[i])
        if kind == 0:
            wa = (gdn_w_in[j], gdn_conv_w[j], gdn_a_log[j], gdn_dt_bias[j], gdn_norm_w[j], gdn_w_out[j])
            buf0 = jnp.zeros((bp,) + cache_gdn_conv.shape[2:], cache_gdn_conv.dtype)
            s0 = jnp.zeros((bp,) + state_gdn.shape[2:], state_gdn.dtype)
            mp, bufp, sp = gdn_mixer(hp, buf0, s0, *wa)
            ms, bufs, ss = gdn_mixer(hs, cache_gdn_conv[j], state_gdn[j], *wa)
            gdn_conv_p.append(bufp); gdn_conv_s.append(bufs)
            gdn_s_p.append(sp); gdn_s_s.append(ss)
        elif kind == 1:
            wb = (gla_w_in[j], gla_w_g2[j], gla_b_g[j], gla_norm_w[j], gla_w_out[j])
            s0 = jnp.zeros((bp,) + state_gla.shape[2:], state_gla.dtype)
            mp, sp = gla_mixer(hp, s0, *wb)
            ms, ss = gla_mixer(hs, state_gla[j], *wb)
            gla_s_p.append(sp); gla_s_s.append(ss)
        elif kind == 2:
            wc = (gmlp_w_in[j], gmlp_b_in[j], gmlp_ln_w[j], gmlp_ln_b[j], gmlp_w_s[j], gmlp_b_s[j],
                  gmlp_w_out[j], gmlp_b_out[j])
            mp, _ = gmlp_mixer(hp, *wc)
            ms, vs = gmlp_mixer(hs, *wc)
            gmlp_v_s.append(vs)
        else:
            wd = (cnv_w_pw1[j], cnv_b_pw1[j], cnv_w_dw[j], cnv_b_dw[j], cnv_ln_w[j], cnv_ln_b[j],
                  cnv_w_pw2[j], cnv_b_pw2[j])
            buf0 = jnp.zeros((bp,) + cache_conformer.shape[2:], cache_conformer.dtype)
            mp, bufp = conformer_conv_mixer(hp, buf0, *wd)
            ms, bufs = conformer_conv_mixer(hs, cache_conformer[j], *wd)
            cnv_p.append(bufp); cnv_s.append(bufs)
        xp = xp + mp
        xs = xs + ms
        xp = xp + sq_relu_mlp(rmsnorm(xp, ffn_norm_w[i]), ffn_w1[i], ffn_w2[i])
        xs = xs + sq_relu_mlp(rmsnorm(xs, ffn_norm_w[i]), ffn_w1[i], ffn_w2[i])
    y_prompt = rmsnorm(xp, final_norm_w)
    y_sample = rmsnorm(xs, final_norm_w)
    return (y_prompt, y_sample,
            jnp.stack(gdn_conv_p), jnp.stack(gdn_conv_s),
            jnp.stack(gdn_s_p), jnp.stack(gdn_s_s),
            jnp.stack(gla_s_p), jnp.stack(gla_s_s),
            jnp.stack(gmlp_v_s),
            jnp.stack(cnv_p), jnp.stack(cnv_s))
```

```python
import functools
import math

import numpy as np
import jax
import jax.numpy as jnp
from jax import lax
from jax.experimental import pallas as pl
from jax.experimental.pallas import tpu as pltpu

F32 = jnp.float32
BF16 = jnp.bfloat16

D_MODEL = 1024
D_FF = 4 * D_MODEL
EPS = 1e-6
CHUNK = 64
ROW_TILE = 512
LANES = 128
VMEM_LIMIT = 56 * 1024 * 1024

GDN_HEADS, GDN_DK, GDN_DV, GDN_CONV = 8, 128, 128, 4
GDN_QK = GDN_HEADS * GDN_DK
GDN_CONV_CH = 2 * GDN_QK + GDN_HEADS * GDN_DV
GDN_Z0 = GDN_CONV_CH
GDN_BG0 = GDN_CONV_CH + GDN_HEADS * GDN_DV
GDN_P = GDN_BG0 + LANES

GLA_HEADS, GLA_DK, GLA_DV, GLA_RANK, GLA_TAU = 4, 128, 256, 16, 16.0
GLA_QK = GLA_HEADS * GLA_DK
GLA_V0, GLA_R0, GLA_G0 = 2 * GLA_QK, 2 * GLA_QK + D_MODEL, 2 * GLA_QK + 2 * D_MODEL
GLA_P = GLA_G0 + LANES
GLA_LEVELS = (32, 16, 8, 4, 2, 1)

GMLP_BLOCK, GMLP_GROUPS, GMLP_GC = 128, 4, 256
CONV_W = 31
CONV_HALO = 32


def _mm(a, b):
    return jnp.dot(a.astype(BF16), b.astype(BF16), preferred_element_type=F32)


def _mm_nt(a, b):
    return lax.dot_general(a.astype(BF16), b.astype(BF16), (((1,), (1,)), ((), ())),
                           preferred_element_type=F32)


def _mm_tn(a, b):
    return lax.dot_general(a.astype(BF16), b.astype(BF16), (((0,), (0,)), ((), ())),
                           preferred_element_type=F32)


def _rms(x, w):
    return x * lax.rsqrt(jnp.mean(x * x, axis=-1, keepdims=True) + EPS) * w


def _layernorm(x, w, b):
    xc = x - jnp.mean(x, axis=-1, keepdims=True)
    return xc * lax.rsqrt(jnp.mean(xc * xc, axis=-1, keepdims=True) + EPS) * w + b


def _sigmoid(x):
    return jax.nn.sigmoid(x)


def _silu(x):
    return x * _sigmoid(x)


def _softplus(x):
    return jnp.maximum(x, 0.0) + jnp.log1p(jnp.exp(-jnp.abs(x)))


def _gelu_tanh(x):
    return 0.5 * x * (1.0 + jnp.tanh(math.sqrt(2.0 / math.pi) * (x + 0.044715 * (x * x * x))))


def _cumsum_rows(x):
    row = lax.broadcasted_iota(jnp.int32, x.shape, 0)
    s = 1
    while s < x.shape[0]:
        x = x + jnp.where(row >= s, pltpu.roll(x, s, 0), 0.0)
        s *= 2
    return x


def _pair_mask(s):
    sh = int(math.log2(s))
    bi = lax.broadcasted_iota(jnp.int32, (CHUNK, CHUNK), 0) >> sh
    bj = lax.broadcasted_iota(jnp.int32, (CHUNK, CHUNK), 1) >> sh
    return jnp.logical_and((bi & 1) == 1, bj == bi - 1)


def _chunk_flags(n_pc, cpl):
    g = pl.program_id(0)
    is_prompt = g < n_pc
    is_first = jnp.logical_or(jnp.logical_not(is_prompt), g % cpl == 0)
    is_last = jnp.logical_or(jnp.logical_not(is_prompt), g % cpl == cpl - 1)
    return g, is_prompt, is_first, is_last


def _seq_of_chunk(g, n_pc, cpl, n_p):
    return jnp.where(g < n_pc, g // cpl, g - n_pc + n_p)


def _params(n_axes=1):
    return pltpu.CompilerParams(dimension_semantics=("arbitrary",) * n_axes,
                                vmem_limit_bytes=VMEM_LIMIT)


def _resident(shape):
    zeros = (0,) * len(shape)
    return pl.BlockSpec(shape, lambda i: zeros)


def _proj_body(x_ref, nw_ref, w_ref, *rest, n_out, ck, epilogue, has_bias):
    if has_bias:
        b_ref, o_ref = rest
    else:
        (o_ref,) = rest
    xn = _rms(x_ref[...], nw_ref[...]).astype(BF16)
    for c in range(n_out // ck):
        sl = slice(c * ck, (c + 1) * ck)
        y = jnp.dot(xn, w_ref[:, sl], preferred_element_type=F32)
        if has_bias:
            y = y + b_ref[:, sl]
        if epilogue == "gelu":
            y = _gelu_tanh(y)
        elif epilogue == "glu":
            sg = slice(n_out + c * ck, n_out + (c + 1) * ck)
            gate = jnp.dot(xn, w_ref[:, sg], preferred_element_type=F32) + b_ref[:, sg]
            y = y * _sigmoid(gate)
        o_ref[:, sl] = y.astype(o_ref.dtype)


def _proj(x, nw, w, b, *, n_out, ck, epilogue="none", out_dtype=F32, name):
    t = x.shape[0]
    n_w = w.shape[1]
    has_bias = b is not None
    in_specs = [pl.BlockSpec((ROW_TILE, D_MODEL), lambda i: (i, 0)),
                _resident((1, D_MODEL)), _resident((D_MODEL, n_w))]
    args = [x, nw, w]
    if has_bias:
        in_specs.append(_resident((1, n_w)))
        args.append(b)
    return pl.pallas_call(
        functools.partial(_proj_body, n_out=n_out, ck=ck, epilogue=epilogue, has_bias=has_bias),
        grid=(t // ROW_TILE,),
        in_specs=in_specs,
        out_specs=pl.BlockSpec((ROW_TILE, n_out), lambda i: (i, 0)),
        out_shape=jax.ShapeDtypeStruct((t, n_out), out_dtype),
        compiler_params=_params(), name=name)(*args)


def _outproj_body(a_ref, res_ref, w_ref, *rest, has_bias):
    if has_bias:
        b_ref, o_ref = rest
    else:
        (o_ref,) = rest
    y = res_ref[...] + jnp.dot(a_ref[...], w_ref[...], preferred_element_type=F32)
    if has_bias:
        y = y + b_ref[...]
    o_ref[...] = y


def _outproj(a, res, w, b, *, name):
    t = a.shape[0]
    has_bias = b is not None
    in_specs = [pl.BlockSpec((ROW_TILE, D_MODEL), lambda i: (i, 0)),
                pl.BlockSpec((ROW_TILE, D_MODEL), lambda i: (i, 0)),
                _resident((D_MODEL, D_MODEL))]
    args = [a, res, w]
    if has_bias:
        in_specs.append(_resident((1, D_MODEL)))
        args.append(b)
    return pl.pallas_call(
        functools.partial(_outproj_body, has_bias=has_bias),
        grid=(t // ROW_TILE,),
        in_specs=in_specs,
        out_specs=pl.BlockSpec((ROW_TILE, D_MODEL), lambda i: (i, 0)),
        out_shape=jax.ShapeDtypeStruct((t, D_MODEL), F32),
        compiler_params=_params(), name=name)(*args)


FFN_CK = 512


def _ffn_body(x_ref, nw_ref, w1_ref, w2_ref, o_ref):
    x = x_ref[...]
    xn = _rms(x, nw_ref[...]).astype(BF16)
    acc = x
    for c in range(D_FF // FFN_CK):
        sl = slice(c * FFN_CK, (c + 1) * FFN_CK)
        h = jnp.dot(xn, w1_ref[:, sl], preferred_element_type=F32)
        h = jnp.square(jnp.maximum(h, 0.0)).astype(BF16)
        acc = acc + jnp.dot(h, w2_ref[sl, :], preferred_element_type=F32)
    o_ref[...] = acc


def _ffn(x, nw, w1, w2, *, name):
    t = x.shape[0]
    return pl.pallas_call(
        _ffn_body,
        grid=(t // ROW_TILE,),
        in_specs=[pl.BlockSpec((ROW_TILE, D_MODEL), lambda i: (i, 0)),
                  _resident((1, D_MODEL)),
                  pl.BlockSpec((D_MODEL, D_FF), lambda i: (0, 0), pipeline_mode=pl.Buffered(1)),
                  pl.BlockSpec((D_FF, D_MODEL), lambda i: (0, 0), pipeline_mode=pl.Buffered(1))],
        out_specs=pl.BlockSpec((ROW_TILE, D_MODEL), lambda i: (i, 0)),
        out_shape=jax.ShapeDtypeStruct((t, D_MODEL), F32),
        compiler_params=_params(), name=name)(x, nw, w1, w2)


def _final_norm_body(x_ref, nw_ref, op_ref, os_ref, *, n_pt):
    y = _rms(x_ref[...], nw_ref[...])
    i = pl.program_id(0)

    @pl.when(i < n_pt)
    def _():
        op_ref[...] = y

    @pl.when(i >= n_pt)
    def _():
        os_ref[...] = y


def _final_norm(x, nw, *, t_p, name):
    t = x.shape[0]
    n_pt = t_p // ROW_TILE
    return pl.pallas_call(
        functools.partial(_final_norm_body, n_pt=n_pt),
        grid=(t // ROW_TILE,),
        in_specs=[pl.BlockSpec((ROW_TILE, D_MODEL), lambda i: (i, 0)), _resident((1, D_MODEL))],
        out_specs=[pl.BlockSpec((ROW_TILE, D_MODEL), lambda i: (jnp.minimum(i, n_pt - 1), 0)),
                   pl.BlockSpec((ROW_TILE, D_MODEL), lambda i: (jnp.maximum(i - n_pt, 0), 0))],
        out_shape=[jax.ShapeDtypeStruct((t_p, D_MODEL), F32),
                   jax.ShapeDtypeStruct((t - t_p, D_MODEL), F32)],
        compiler_params=_params(), name=name)(x, nw)


def _gdn_body(p_ref, halo_ref, halo0_ref, s0_ref, cw_ref, alog_ref, dtb_ref, nw_ref,
              o_ref, sout_ref, cout_ref, s_scr, xc_scr, *, n_pc, cpl):
    g, is_prompt, is_first, is_last = _chunk_flags(n_pc, cpl)

    @pl.when(jnp.logical_and(is_first, is_prompt))
    def _():
        s_scr[...] = jnp.zeros_like(s_scr)

    @pl.when(jnp.logical_not(is_prompt))
    def _():
        s_scr[...] = s0_ref[0]

    xc_scr[0:8, :] = jnp.where(is_first, halo0_ref[0], halo_ref[...])
    xc_scr[8:8 + CHUNK, :] = p_ref[:, 0:GDN_CONV_CH]

    def conv_silu(c0):
        y = cw_ref[0:1, c0:c0 + LANES] * xc_scr[5:5 + CHUNK, c0:c0 + LANES]
        for w in range(1, GDN_CONV):
            y = y + cw_ref[w:w + 1, c0:c0 + LANES] * xc_scr[5 + w:5 + w + CHUNK, c0:c0 + LANES]
        return _silu(y)

    bg = p_ref[:, GDN_BG0:GDN_BG0 + LANES]
    beta = _sigmoid(bg)
    glog = -jnp.exp(alog_ref[...]) * _softplus(bg + dtb_ref[...])
    gcum = _cumsum_rows(glog)
    egc = jnp.exp(gcum)
    g_last = gcum[CHUNK - 1:CHUNK, :]
    eg_last = jnp.exp(g_last)
    e_rest = jnp.exp(g_last - gcum)
    gcum_t = jnp.transpose(jnp.concatenate([gcum, jnp.zeros_like(gcum)], axis=0))

    row = lax.broadcasted_iota(jnp.int32, (CHUNK, CHUNK), 0)
    col = lax.broadcasted_iota(jnp.int32, (CHUNK, CHUNK), 1)
    incl = row >= col
    strict = row > col
    eye = (row == col).astype(F32)
    masks = [_pair_mask(s) for s in (1, 2, 4, 8, 16, 32)]

    for h in range(GDN_HEADS):
        q = conv_silu(h * GDN_DK)
        k = conv_silu(GDN_QK + h * GDN_DK)
        v = conv_silu(2 * GDN_QK + h * GDN_DV)
        q = q * lax.rsqrt(jnp.sum(q * q, axis=-1, keepdims=True) + EPS) * (GDN_DK ** -0.5)
        k = k * lax.rsqrt(jnp.sum(k * k, axis=-1, keepdims=True) + EPS)
        beta_h = beta[:, h:h + 1]
        gc = gcum[:, 8 + h:9 + h]
        gr = gcum_t[8 + h:9 + h, 0:CHUNK]
        decay = jnp.exp(jnp.minimum(gc - gr, 0.0))
        kb = k * beta_h
        a = jnp.where(strict, _mm_nt(kb, k) * decay, 0.0)
        t = eye - jnp.where(masks[0], a, 0.0)
        for m in masks[1:]:
            t = t - _mm(t, _mm(jnp.where(m, a, 0.0), t))
        u = _mm(t, v * beta_h)
        w = _mm(t, kb * egc[:, 8 + h:9 + h])
        s = s_scr[h]
        v_new = u - _mm(w, s)
        attn = jnp.where(incl, _mm_nt(q, k) * decay, 0.0)
        o = _mm(q * egc[:, 8 + h:9 + h], s) + _mm(attn, v_new)
        s_scr[h] = s * eg_last[:, 8 + h:9 + h] + _mm_tn(k * e_rest[:, 8 + h:9 + h], v_new)
        z = p_ref[:, GDN_Z0 + h * GDN_DV:GDN_Z0 + (h + 1) * GDN_DV]
        o_ref[:, h * GDN_DV:(h + 1) * GDN_DV] = (_rms(o, nw_ref[...]) * _silu(z)).astype(o_ref.dtype)

    @pl.when(is_last)
    def _():
        sout_ref[0] = s_scr[...]
        cout_ref[0] = xc_scr[CHUNK:CHUNK + 8, :]


def _gdn_scan(p, halo0, s0, cw, alog, dtb, nw, *, n_p, cpl, n_s):
    t = p.shape[0]
    n_pc = n_p * cpl
    n_seq = n_p + n_s
    seq = lambda g: _seq_of_chunk(g, n_pc, cpl, n_p)
    return pl.pallas_call(
        functools.partial(_gdn_body, n_pc=n_pc, cpl=cpl),
        grid=(t // CHUNK,),
        in_specs=[pl.BlockSpec((CHUNK, GDN_P), lambda g: (g, 0)),
                  pl.BlockSpec((8, GDN_CONV_CH), lambda g: (jnp.maximum(8 * g - 1, 0), 0)),
                  pl.BlockSpec((1, 8, GDN_CONV_CH), lambda g: (seq(g), 0, 0)),
                  pl.BlockSpec((1, GDN_HEADS, GDN_DK, GDN_DV),
                               lambda g: (jnp.maximum(g - n_pc, 0), 0, 0, 0)),
                  _resident((GDN_CONV, GDN_CONV_CH)),
                  _resident((1, LANES)), _resident((1, LANES)), _resident((1, GDN_DV))],
        out_specs=[pl.BlockSpec((CHUNK, D_MODEL), lambda g: (g, 0)),
                   pl.BlockSpec((1, GDN_HEADS, GDN_DK, GDN_DV), lambda g: (seq(g), 0, 0, 0)),
                   pl.BlockSpec((1, 8, GDN_CONV_CH), lambda g: (seq(g), 0, 0))],
        out_shape=[jax.ShapeDtypeStruct((t, D_MODEL), BF16),
                   jax.ShapeDtypeStruct((n_seq, GDN_HEADS, GDN_DK, GDN_DV), F32),
                   jax.ShapeDtypeStruct((n_seq, 8, GDN_CONV_CH), F32)],
        scratch_shapes=[pltpu.VMEM((GDN_HEADS, GDN_DK, GDN_DV), F32),
                        pltpu.VMEM((CHUNK + 8, GDN_CONV_CH), F32)],
        compiler_params=_params(), name="gdn_scan")(p, p, halo0, s0, cw, alog, dtb, nw)


def _gla_exponent_matrix():
    i = np.arange(CHUNK)[:, None]
    t = np.arange(CHUNK)[None, :]
    blocks = [(t <= i)]
    for s in GLA_LEVELS:
        c = (i // (2 * s)) * (2 * s) + s - 1
        blocks.append(np.where(i > c, (t > c) & (t <= i), (t > i) & (t <= c)))
    w = np.concatenate(blocks, axis=0).astype(np.float32)
    return np.concatenate([w, w, w], axis=1)


def _split3(x):
    hi = x.astype(BF16)
    r1 = x - hi.astype(F32)
    mid = r1.astype(BF16)
    lo = (r1 - mid.astype(F32)).astype(BF16)
    return hi, mid, lo


def _gla_body(p_ref, s0_ref, wg2_ref, bg_ref, wexp_ref, nw_ref,
              o_ref, sout_ref, st_scr, *, n_pc, cpl):
    g, is_prompt, is_first, is_last = _chunk_flags(n_pc, cpl)

    @pl.when(jnp.logical_and(is_first, is_prompt))
    def _():
        st_scr[...] = jnp.zeros_like(st_scr)

    @pl.when(jnp.logical_not(is_prompt))
    def _():
        st_scr[...] = s0_ref[0]

    x = _mm(p_ref[:, GLA_G0:GLA_G0 + LANES], wg2_ref[...]) + bg_ref[...]
    glog = -_softplus(-x) * (1.0 / GLA_TAU)
    e_all = jnp.dot(wexp_ref[...], jnp.concatenate(_split3(glog), axis=0),
                    preferred_element_type=F32)

    row = lax.broadcasted_iota(jnp.int32, (CHUNK, CHUNK), 0)
    col = lax.broadcasted_iota(jnp.int32, (CHUNK, CHUNK), 1)
    eye = row == col
    masks = [_pair_mask(s) for s in GLA_LEVELS]

    for h in range(GLA_HEADS):
        ck = slice(h * GLA_DK, (h + 1) * GLA_DK)
        q = p_ref[:, h * GLA_DK:(h + 1) * GLA_DK] * (GLA_DK ** -0.5)
        k = p_ref[:, GLA_QK + h * GLA_DK:GLA_QK + (h + 1) * GLA_DK]
        v = p_ref[:, GLA_V0 + h * GLA_DV:GLA_V0 + (h + 1) * GLA_DV]
        b = e_all[0:CHUNK, ck]
        attn = jnp.where(eye, _mm_nt(q, k), 0.0)
        for lvl, m in enumerate(masks):
            f = jnp.exp(e_all[(lvl + 1) * CHUNK:(lvl + 2) * CHUNK, ck])
            attn = attn + jnp.where(m, _mm_nt(q * f, k * f), 0.0)
        st = st_scr[h]
        o = _mm_nt(q * jnp.exp(b), st) + _mm(attn, v)
        b_last = b[CHUNK - 1:CHUNK, :]
        st_scr[h] = st * jnp.exp(b_last) + _mm_tn(v, k * jnp.exp(b_last - b))
        r = p_ref[:, GLA_R0 + h * GLA_DV:GLA_R0 + (h + 1) * GLA_DV]
        o_ref[:, h * GLA_DV:(h + 1) * GLA_DV] = (
            _rms(o, nw_ref[h:h + 1, :]) * _silu(r)).astype(o_ref.dtype)

    @pl.when(is_last)
    def _():
        sout_ref[0] = st_scr[...]


def _gla_scan(p, s0t, wg2, bg, wexp, nw, *, n_p, cpl, n_s):
    t = p.shape[0]
    n_pc = n_p * cpl
    n_seq = n_p + n_s
    seq = lambda g: _seq_of_chunk(g, n_pc, cpl, n_p)
    return pl.pallas_call(
        functools.partial(_gla_body, n_pc=n_pc, cpl=cpl),
        grid=(t // CHUNK,),
        in_specs=[pl.BlockSpec((CHUNK, GLA_P), lambda g: (g, 0)),
                  pl.BlockSpec((1, GLA_HEADS, GLA_DV, GLA_DK),
                               lambda g: (jnp.maximum(g - n_pc, 0), 0, 0, 0)),
                  _resident((LANES, GLA_QK)), _resident((1, GLA_QK)),
                  _resident(((len(GLA_LEVELS) + 1) * CHUNK, 3 * CHUNK)),
                  _resident((GLA_HEADS, GLA_DV))],
        out_specs=[pl.BlockSpec((CHUNK, D_MODEL), lambda g: (g, 0)),
                   pl.BlockSpec((1, GLA_HEADS, GLA_DV, GLA_DK), lambda g: (seq(g), 0, 0, 0))],
        out_shape=[jax.ShapeDtypeStruct((t, D_MODEL), BF16),
                   jax.ShapeDtypeStruct((n_seq, GLA_HEADS, GLA_DV, GLA_DK), F32)],
        scratch_shapes=[pltpu.VMEM((GLA_HEADS, GLA_DV, GLA_DK), F32)],
        compiler_params=_params(), name="gla_scan")(p, s0t, wg2, bg, wexp, nw)


def _gmlp_body(zz_ref, lnw_ref, lnb_ref, ws_ref, bs_ref, o_ref, vs_ref, vprev_scr, *, n_pc):
    g = pl.program_id(0)
    is_prompt = g < n_pc
    second = jnp.logical_and(is_prompt, g % 2 == 1)

    @pl.when(g == 0)
    def _():
        vprev_scr[...] = jnp.zeros_like(vprev_scr)

    v = _layernorm(zz_ref[:, D_MODEL:2 * D_MODEL], lnw_ref[...], lnb_ref[...])

    @pl.when(jnp.logical_not(is_prompt))
    def _():
        vs_ref[...] = v

    vb = v.astype(BF16)
    v_a = jnp.where(second, vprev_scr[...], vb)
    r0 = pl.multiple_of(jnp.where(second, CHUNK, 0), CHUNK)
    t_idx = r0 + lax.broadcasted_iota(jnp.int32, (CHUNK, GMLP_BLOCK), 0)
    s_idx = lax.broadcasted_iota(jnp.int32, (CHUNK, GMLP_BLOCK), 1)
    causal = s_idx <= t_idx
    bias = bs_ref[pl.ds(r0, CHUNK), :]
    for grp in range(GMLP_GROUPS):
        cs = slice(grp * GMLP_GC, (grp + 1) * GMLP_GC)
        wrow = jnp.where(causal, ws_ref[grp, pl.ds(r0, CHUNK), :], 0.0).astype(BF16)
        mixed = (jnp.dot(wrow[:, 0:CHUNK], v_a[:, cs], preferred_element_type=F32)
                 + jnp.dot(wrow[:, CHUNK:GMLP_BLOCK], vb[:, cs], preferred_element_type=F32))
        o_ref[:, cs] = (zz_ref[:, cs] * (mixed + bias[:, cs])).astype(o_ref.dtype)
    vprev_scr[...] = vb


def _gmlp_mix(zz, lnw, lnb, ws, bs_full, *, n_pc, n_s):
    t = zz.shape[0]
    return pl.pallas_call(
        functools.partial(_gmlp_body, n_pc=n_pc),
        grid=(t // CHUNK,),
        in_specs=[pl.BlockSpec((CHUNK, 2 * D_MODEL), lambda g: (g, 0)),
                  _resident((1, D_MODEL)), _resident((1, D_MODEL)),
                  _resident((GMLP_GROUPS, GMLP_BLOCK, GMLP_BLOCK)),
                  _resident((GMLP_BLOCK, D_MODEL))],
        out_specs=[pl.BlockSpec((CHUNK, D_MODEL), lambda g: (g, 0)),
                   pl.BlockSpec((CHUNK, D_MODEL), lambda g: (jnp.maximum(g - n_pc, 0), 0))],
        out_shape=[jax.ShapeDtypeStruct((t, D_MODEL), BF16),
                   jax.ShapeDtypeStruct((n_s * CHUNK, D_MODEL), F32)],
        scratch_shapes=[pltpu.VMEM((CHUNK, D_MODEL), BF16)],
        compiler_params=_params(), name="gmlp_mix")(zz, lnw, lnb, ws, bs_full)


def _cnv_body(x_ref, halo_ref, halo0_ref, wdw_ref, bdw_ref,
              o_ref, cout_ref, xc_scr, *, n_pc, cpl):
    g, is_prompt, is_first, is_last = _chunk_flags(n_pc, cpl)
    xc_scr[0:CONV_HALO, :] = jnp.where(is_first, halo0_ref[0], halo_ref[...])
    xc_scr[CONV_HALO:CONV_HALO + CHUNK, :] = x_ref[...]
    off = CONV_HALO - (CONV_W - 1)
    for c0 in range(0, D_MODEL, 2 * LANES):
        cs = slice(c0, c0 + 2 * LANES)
        y = wdw_ref[0:1, cs] * xc_scr[off:off + CHUNK, cs]
        for w in range(1, CONV_W):
            y = y + wdw_ref[w:w + 1, cs] * xc_scr[off + w:off + w + CHUNK, cs]
        o_ref[:, cs] = (y + bdw_ref[:, cs]).astype(o_ref.dtype)

    @pl.when(is_last)
    def _():
        cout_ref[0] = x_ref[CHUNK - CONV_HALO:CHUNK, :]


def _cnv_conv(glu, halo0, wdw, bdw, *, n_p, cpl, n_s):
    t = glu.shape[0]
    n_pc = n_p * cpl
    n_seq = n_p + n_s
    seq = lambda g: _seq_of_chunk(g, n_pc, cpl, n_p)
    return pl.pallas_call(
        functools.partial(_cnv_body, n_pc=n_pc, cpl=cpl),
        grid=(t // CHUNK,),
        in_specs=[pl.BlockSpec((CHUNK, D_MODEL), lambda g: (g, 0)),
                  pl.BlockSpec((CONV_HALO, D_MODEL), lambda g: (jnp.maximum(2 * g - 1, 0), 0)),
                  pl.BlockSpec((1, CONV_HALO, D_MODEL), lambda g: (seq(g), 0, 0)),
                  _resident((CONV_HALO, D_MODEL)), _resident((1, D_MODEL))],
        out_specs=[pl.BlockSpec((CHUNK, D_MODEL), lambda g: (g, 0)),
                   pl.BlockSpec((1, CONV_HALO, D_MODEL), lambda g: (seq(g), 0, 0))],
        out_shape=[jax.ShapeDtypeStruct((t, D_MODEL), F32),
                   jax.ShapeDtypeStruct((n_seq, CONV_HALO, D_MODEL), F32)],
        scratch_shapes=[pltpu.VMEM((CONV_HALO + CHUNK, D_MODEL), F32)],
        compiler_params=_params(), name="cnv_conv")(glu, glu, halo0, wdw, bdw)


def _cnv_out_body(y_ref, res_ref, lnw_ref, lnb_ref, w_ref, b_ref, o_ref):
    a = _silu(_layernorm(y_ref[...], lnw_ref[...], lnb_ref[...])).astype(BF16)
    o_ref[...] = res_ref[...] + jnp.dot(a, w_ref[...], preferred_element_type=F32) + b_ref[...]


def _cnv_out(y, res, lnw, lnb, w, b):
    t = y.shape[0]
    return pl.pallas_call(
        _cnv_out_body,
        grid=(t // ROW_TILE,),
        in_specs=[pl.BlockSpec((ROW_TILE, D_MODEL), lambda i: (i, 0)),
                  pl.BlockSpec((ROW_TILE, D_MODEL), lambda i: (i, 0)),
                  _resident((1, D_MODEL)), _resident((1, D_MODEL)),
                  _resident((D_MODEL, D_MODEL)), _resident((1, D_MODEL))],
        out_specs=pl.BlockSpec((ROW_TILE, D_MODEL), lambda i: (i, 0)),
        out_shape=jax.ShapeDtypeStruct((t, D_MODEL), F32),
        compiler_params=_params(), name="cnv_out")(y, res, lnw, lnb, w, b)


def _pad_cols(w, n):
    return jnp.pad(w, ((0, 0), (0, n - w.shape[1])))


def _row(v):
    return v.reshape(1, -1).astype(F32)


def _gdn_layer(x, lay, mix_nw, cache, state, w_in, conv_w, a_log, dt_bias, norm_w, w_out):
    n_p, cpl, n_s = lay
    p = _proj(x, _row(mix_nw), _pad_cols(w_in, GDN_P).astype(BF16), None,
              n_out=GDN_P, ck=384, name="gdn_in")
    halo0 = jnp.concatenate([jnp.zeros((n_p, 8, GDN_CONV_CH), F32),
                             jnp.pad(cache, ((0, 0), (8 - (GDN_CONV - 1), 0), (0, 0)))], axis=0)
    lane_pad = lambda v: jnp.pad(v.reshape(1, -1), ((0, 0), (GDN_HEADS, LANES - 2 * GDN_HEADS)))
    o, s_all, c_all = _gdn_scan(p, halo0, state, conv_w, lane_pad(a_log), lane_pad(dt_bias),
                                _row(norm_w), n_p=n_p, cpl=cpl, n_s=n_s)
    x = _outproj(o, x, w_out.astype(BF16), None, name="gdn_out")
    conv_rows = c_all[:, 8 - (GDN_CONV - 1):, :]
    return x, conv_rows[:n_p], conv_rows[n_p:], s_all[:n_p], s_all[n_p:]


def _gla_layer(x, lay, mix_nw, state, w_in, w_g2, b_g, norm_w, w_out):
    n_p, cpl, n_s = lay
    p = _proj(x, _row(mix_nw), _pad_cols(w_in, GLA_P).astype(BF16), None,
              n_out=GLA_P, ck=640, name="gla_in")
    wg2 = jnp.pad(w_g2, ((0, LANES - GLA_RANK), (0, 0))).astype(BF16)
    wexp = jnp.asarray(_gla_exponent_matrix(), BF16)
    o, st_all = _gla_scan(p, jnp.swapaxes(state, 2, 3), wg2, _row(b_g), wexp, norm_w,
                          n_p=n_p, cpl=cpl, n_s=n_s)
    x = _outproj(o, x, w_out.astype(BF16), None, name="gla_out")
    s_all = jnp.swapaxes(st_all, 2, 3)
    return x, s_all[:n_p], s_all[n_p:]


def _gmlp_layer(x, lay, mix_nw, w_in, b_in, ln_w, ln_b, w_s, b_s, w_out, b_out):
    n_p, cpl, n_s = lay
    zz = _proj(x, _row(mix_nw), w_in.astype(BF16), _row(b_in),
               n_out=2 * D_MODEL, ck=512, epilogue="gelu", name="gmlp_in")
    bs_full = jnp.repeat(b_s.T, GMLP_GC, axis=1)
    a, v_s = _gmlp_mix(zz, _row(ln_w), _row(ln_b), w_s, bs_full, n_pc=n_p * cpl, n_s=n_s)
    x = _outproj(a, x, w_out.astype(BF16), _row(b_out), name="gmlp_out")
    return x, v_s.reshape(n_s, CHUNK, D_MODEL)


def _cnv_layer(x, lay, mix_nw, cache, w_pw1, b_pw1, w_dw, b_dw, ln_w, ln_b, w_pw2, b_pw2):
    n_p, cpl, n_s = lay
    glu = _proj(x, _row(mix_nw), w_pw1.astype(BF16), _row(b_pw1),
                n_out=D_MODEL, ck=512, epilogue="glu", name="cnv_in")
    pad_rows = CONV_HALO - (CONV_W - 1)
    halo0 = jnp.concatenate([jnp.zeros((n_p, CONV_HALO, D_MODEL), F32),
                             jnp.pad(cache, ((0, 0), (pad_rows, 0), (0, 0)))], axis=0)
    wdw = jnp.pad(w_dw, ((0, CONV_HALO - CONV_W), (0, 0)))
    y, c_all = _cnv_conv(glu, halo0, wdw, _row(b_dw), n_p=n_p, cpl=cpl, n_s=n_s)
    x = _cnv_out(y, x, _row(ln_w), _row(ln_b), w_pw2.astype(BF16), _row(b_pw2))
    rows = c_all[:, pad_rows:, :]
    return x, rows[:n_p], rows[n_p:]


def kernel(x_prompt, x_sample, cache_gdn_conv, state_gdn, state_gla, cache_conformer, mix_norm_w, ffn_norm_w, ffn_w1, ffn_w2, final_norm_w, gdn_w_in, gdn_conv_w, gdn_a_log, gdn_dt_bias, gdn_norm_w, gdn_w_out, gla_w_in, gla_w_g2, gla_b_g, gla_norm_w, gla_w_out, gmlp_w_in, gmlp_b_in, gmlp_ln_w, gmlp_ln_b, gmlp_w_s, gmlp_b_s, gmlp_w_out, gmlp_b_out, cnv_w_pw1, cnv_b_pw1, cnv_w_dw, cnv_b_dw, cnv_ln_w, cnv_ln_b, cnv_w_pw2, cnv_b_pw2):
    n_p, l_p, d = x_prompt.shape
    n_s, l_s, _ = x_sample.shape
    assert d == D_MODEL and l_s == CHUNK and l_p % GMLP_BLOCK == 0
    t_p = n_p * l_p
    assert t_p % ROW_TILE == 0 and (n_s * l_s) % ROW_TILE == 0
    lay = (n_p, l_p // CHUNK, n_s)
    depth = mix_norm_w.shape[0]

    x = jnp.concatenate([x_prompt.reshape(t_p, d), x_sample.reshape(n_s * l_s, d)], axis=0)
    outs = {k: [] for k in ("gdn_cp", "gdn_cs", "gdn_sp", "gdn_ss", "gla_sp", "gla_ss",
                            "gmlp_vs", "cnv_p", "cnv_s")}
    for i in range(depth):
        kind, j = i % 4, i // 4
        if kind == 0:
            x, cp, cs, sp, ss = _gdn_layer(x, lay, mix_norm_w[i], cache_gdn_conv[j], state_gdn[j],
                                           gdn_w_in[j], gdn_conv_w[j], gdn_a_log[j], gdn_dt_bias[j],
                                           gdn_norm_w[j], gdn_w_out[j])
            outs["gdn_cp"].append(cp); outs["gdn_cs"].append(cs)
            outs["gdn_sp"].append(sp); outs["gdn_ss"].append(ss)
        elif kind == 1:
            x, sp, ss = _gla_layer(x, lay, mix_norm_w[i], state_gla[j], gla_w_in[j], gla_w_g2[j],
                                   gla_b_g[j], gla_norm_w[j], gla_w_out[j])
            outs["gla_sp"].append(sp); outs["gla_ss"].append(ss)
        elif kind == 2:
            x, vs = _gmlp_layer(x, lay, mix_norm_w[i], gmlp_w_in[j], gmlp_b_in[j], gmlp_ln_w[j],
                                gmlp_ln_b[j], gmlp_w_s[j], gmlp_b_s[j], gmlp_w_out[j], gmlp_b_out[j])
            outs["gmlp_vs"].append(vs)
        else:
            x, cp, cs = _cnv_layer(x, lay, mix_norm_w[i], cache_conformer[j], cnv_w_pw1[j],
                                   cnv_b_pw1[j], cnv_w_dw[j], cnv_b_dw[j], cnv_ln_w[j], cnv_ln_b[j],
                                   cnv_w_pw2[j], cnv_b_pw2[j])
            outs["cnv_p"].append(cp); outs["cnv_s"].append(cs)
        x = _ffn(x, _row(ffn_norm_w[i]), ffn_w1[i].astype(BF16), ffn_w2[i].astype(BF16),
                 name=f"ffn_{i}")
    y_p, y_s = _final_norm(x, _row(final_norm_w), t_p=t_p, name="final_norm")
    return (y_p.reshape(n_p, l_p, d), y_s.reshape(n_s, l_s, d),
            jnp.stack(outs["gdn_cp"]), jnp.stack(outs["gdn_cs"]),
            jnp.stack(outs["gdn_sp"]), jnp.stack(outs["gdn_ss"]),
            jnp.stack(outs["gla_sp"]), jnp.stack(outs["gla_ss"]),
            jnp.stack(outs["gmlp_vs"]),
            jnp.stack(outs["cnv_p"]), jnp.stack(outs["cnv_s"]))
```

```python
import functools
import math

import numpy as np
import jax
import jax.numpy as jnp
from jax import lax
from jax.experimental import pallas as pl
from jax.experimental.pallas import tpu as pltpu

F32 = jnp.float32
BF16 = jnp.bfloat16

D_MODEL = 1024
D_FF = 4 * D_MODEL
EPS = 1e-6
CHUNK = 64
ROW_TILE = 512
LANES = 128
VMEM_LIMIT = 56 * 1024 * 1024

GDN_HEADS, GDN_DK, GDN_DV, GDN_CONV = 8, 128, 128, 4
GDN_QK = GDN_HEADS * GDN_DK
GDN_CONV_CH = 2 * GDN_QK + GDN_HEADS * GDN_DV
GDN_Z0 = GDN_CONV_CH
GDN_BG0 = GDN_CONV_CH + GDN_HEADS * GDN_DV
GDN_P = GDN_BG0 + LANES

GLA_HEADS, GLA_DK, GLA_DV, GLA_RANK, GLA_TAU = 4, 128, 256, 16, 16.0
GLA_QK = GLA_HEADS * GLA_DK
GLA_V0, GLA_R0, GLA_G0 = 2 * GLA_QK, 2 * GLA_QK + D_MODEL, 2 * GLA_QK + 2 * D_MODEL
GLA_P = GLA_G0 + LANES
GLA_LEVELS = (32, 16, 8, 4, 2, 1)

GMLP_BLOCK, GMLP_GROUPS, GMLP_GC = 128, 4, 256
CONV_W = 31
CONV_HALO = 32


def _mm(a, b):
    return jnp.dot(a.astype(BF16), b.astype(BF16), preferred_element_type=F32)


def _mm_nt(a, b):
    return lax.dot_general(a.astype(BF16), b.astype(BF16), (((1,), (1,)), ((), ())),
                           preferred_element_type=F32)


def _mm_tn(a, b):
    return lax.dot_general(a.astype(BF16), b.astype(BF16), (((0,), (0,)), ((), ())),
                           preferred_element_type=F32)


def _rms(x, w):
    return x * lax.rsqrt(jnp.mean(x * x, axis=-1, keepdims=True) + EPS) * w


def _layernorm(x, w, b):
    xc = x - jnp.mean(x, axis=-1, keepdims=True)
    return xc * lax.rsqrt(jnp.mean(xc * xc, axis=-1, keepdims=True) + EPS) * w + b


def _sigmoid(x):
    return jax.nn.sigmoid(x)


def _silu(x):
    return x * _sigmoid(x)


def _softplus(x):
    return jnp.maximum(x, 0.0) + jnp.log1p(jnp.exp(-jnp.abs(x)))


def _gelu_tanh(x):
    return 0.5 * x * (1.0 + jnp.tanh(math.sqrt(2.0 / math.pi) * (x + 0.044715 * (x * x * x))))


def _cumsum_rows(x):
    row = lax.broadcasted_iota(jnp.int32, x.shape, 0)
    s = 1
    while s < x.shape[0]:
        x = x + jnp.where(row >= s, pltpu.roll(x, s, 0), 0.0)
        s *= 2
    return x


def _pair_mask(s):
    sh = int(math.log2(s))
    bi = lax.broadcasted_iota(jnp.int32, (CHUNK, CHUNK), 0) >> sh
    bj = lax.broadcasted_iota(jnp.int32, (CHUNK, CHUNK), 1) >> sh
    return jnp.logical_and((bi & 1) == 1, bj == bi - 1)


def _chunk_flags(n_pc, cpl):
    g = pl.program_id(0)
    is_prompt = g < n_pc
    is_first = jnp.logical_or(jnp.logical_not(is_prompt), g % cpl == 0)
    is_last = jnp.logical_or(jnp.logical_not(is_prompt), g % cpl == cpl - 1)
    return g, is_prompt, is_first, is_last


def _seq_of_chunk(g, n_pc, cpl, n_p):
    return jnp.where(g < n_pc, g // cpl, g - n_pc + n_p)


def _params(n_axes=1):
    return pltpu.CompilerParams(dimension_semantics=("arbitrary",) * n_axes,
                                vmem_limit_bytes=VMEM_LIMIT)


def _resident(shape):
    zeros = (0,) * len(shape)
    return pl.BlockSpec(shape, lambda i: zeros)


def _proj_body(x_ref, nw_ref, w_ref, *rest, n_out, ck, epilogue, has_bias):
    if has_bias:
        b_ref, o_ref = rest
    else:
        (o_ref,) = rest
    xn = _rms(x_ref[...], nw_ref[...]).astype(BF16)
    for c in range(n_out // ck):
        sl = slice(c * ck, (c + 1) * ck)
        y = jnp.dot(xn, w_ref[:, sl], preferred_element_type=F32)
        if has_bias:
            y = y + b_ref[:, sl]
        if epilogue == "gelu":
            y = _gelu_tanh(y)
        elif epilogue == "glu":
            sg = slice(n_out + c * ck, n_out + (c + 1) * ck)
            gate = jnp.dot(xn, w_ref[:, sg], preferred_element_type=F32) + b_ref[:, sg]
            y = y * _sigmoid(gate)
        o_ref[:, sl] = y.astype(o_ref.dtype)


def _proj(x, nw, w, b, *, n_out, ck, epilogue="none", out_dtype=F32, name):
    t = x.shape[0]
    n_w = w.shape[1]
    has_bias = b is not None
    in_specs = [pl.BlockSpec((ROW_TILE, D_MODEL), lambda i: (i, 0)),
                _resident((1, D_MODEL)), _resident((D_MODEL, n_w))]
    args = [x, nw, w]
    if has_bias:
        in_specs.append(_resident((1, n_w)))
        args.append(b)
    return pl.pallas_call(
        functools.partial(_proj_body, n_out=n_out, ck=ck, epilogue=epilogue, has_bias=has_bias),
        grid=(t // ROW_TILE,),
        in_specs=in_specs,
        out_specs=pl.BlockSpec((ROW_TILE, n_out), lambda i: (i, 0)),
        out_shape=jax.ShapeDtypeStruct((t, n_out), out_dtype),
        compiler_params=_params(), name=name)(*args)


def _outproj_body(a_ref, res_ref, w_ref, *rest, has_bias):
    if has_bias:
        b_ref, o_ref = rest
    else:
        (o_ref,) = rest
    y = res_ref[...] + jnp.dot(a_ref[...], w_ref[...], preferred_element_type=F32)
    if has_bias:
        y = y + b_ref[...]
    o_ref[...] = y


def _outproj(a, res, w, b, *, name):
    t = a.shape[0]
    has_bias = b is not None
    in_specs = [pl.BlockSpec((ROW_TILE, D_MODEL), lambda i: (i, 0)),
                pl.BlockSpec((ROW_TILE, D_MODEL), lambda i: (i, 0)),
                _resident((D_MODEL, D_MODEL))]
    args = [a, res, w]
    if has_bias:
        in_specs.append(_resident((1, D_MODEL)))
        args.append(b)
    return pl.pallas_call(
        functools.partial(_outproj_body, has_bias=has_bias),
        grid=(t // ROW_TILE,),
        in_specs=in_specs,
        out_specs=pl.BlockSpec((ROW_TILE, D_MODEL), lambda i: (i, 0)),
        out_shape=jax.ShapeDtypeStruct((t, D_MODEL), F32),
        compiler_params=_params(), name=name)(*args)


FFN_CK = 512


def _ffn_body(x_ref, nw_ref, w1_ref, w2_ref, o_ref):
    x = x_ref[...]
    xn = _rms(x, nw_ref[...]).astype(BF16)
    acc = x
    for c in range(D_FF // FFN_CK):
        sl = slice(c * FFN_CK, (c + 1) * FFN_CK)
        h = jnp.dot(xn, w1_ref[:, sl], preferred_element_type=F32)
        h = jnp.square(jnp.maximum(h, 0.0)).astype(BF16)
        acc = acc + jnp.dot(h, w2_ref[sl, :], preferred_element_type=F32)
    o_ref[...] = acc


def _ffn(x, nw, w1, w2, *, name):
    t = x.shape[0]
    return pl.pallas_call(
        _ffn_body,
        grid=(t // ROW_TILE,),
        in_specs=[pl.BlockSpec((ROW_TILE, D_MODEL), lambda i: (i, 0)),
                  _resident((1, D_MODEL)),
                  pl.BlockSpec((D_MODEL, D_FF), lambda i: (0, 0), pipeline_mode=pl.Buffered(1)),
                  pl.BlockSpec((D_FF, D_MODEL), lambda i: (0, 0), pipeline_mode=pl.Buffered(1))],
        out_specs=pl.BlockSpec((ROW_TILE, D_MODEL), lambda i: (i, 0)),
        out_shape=jax.ShapeDtypeStruct((t, D_MODEL), F32),
        compiler_params=_params(), name=name)(x, nw, w1, w2)


def _final_norm_body(x_ref, nw_ref, op_ref, os_ref, *, n_pt):
    y = _rms(x_ref[...], nw_ref[...])
    i = pl.program_id(0)

    @pl.when(i < n_pt)
    def _():
        op_ref[...] = y

    @pl.when(i >= n_pt)
    def _():
        os_ref[...] = y


def _final_norm(x, nw, *, t_p, name):
    t = x.shape[0]
    n_pt = t_p // ROW_TILE
    return pl.pallas_call(
        functools.partial(_final_norm_body, n_pt=n_pt),
        grid=(t // ROW_TILE,),
        in_specs=[pl.BlockSpec((ROW_TILE, D_MODEL), lambda i: (i, 0)), _resident((1, D_MODEL))],
        out_specs=[pl.BlockSpec((ROW_TILE, D_MODEL), lambda i: (jnp.minimum(i, n_pt - 1), 0)),
                   pl.BlockSpec((ROW_TILE, D_MODEL), lambda i: (jnp.maximum(i - n_pt, 0), 0))],
        out_shape=[jax.ShapeDtypeStruct((t_p, D_MODEL), F32),
                   jax.ShapeDtypeStruct((t - t_p, D_MODEL), F32)],
        compiler_params=_params(), name=name)(x, nw)


GDN_NB = 2


def _gdn_pre_body(p_ref, halo_ref, halo0_ref, cw_ref, alog_ref, dtb_ref,
                  u_ref, wq_ref, kd_ref, attn_ref, egl_ref, xc_scr, *, n_pb, bps, n_p):
    blk = pl.program_id(0)
    is_prompt = blk < n_pb
    win = CHUNK + 8
    for j in range(GDN_NB):
        if j == 0:
            first = jnp.logical_or(jnp.logical_not(is_prompt), blk % bps == 0)
            prev = halo_ref[...]
        else:
            first = jnp.logical_not(is_prompt)
            prev = p_ref[j * CHUNK - 8:j * CHUNK, 0:GDN_CONV_CH]
        seq = jnp.where(is_prompt, blk // bps, n_p + (blk - n_pb) * GDN_NB + j)
        xc_scr[j * win:j * win + 8, :] = jnp.where(first, halo0_ref[seq], prev)
        xc_scr[j * win + 8:(j + 1) * win, :] = p_ref[j * CHUNK:(j + 1) * CHUNK, 0:GDN_CONV_CH]

    def conv_silu(j, c0):
        wv = xc_scr[j * win:(j + 1) * win, c0:c0 + LANES]
        y = cw_ref[GDN_CONV - 1:GDN_CONV, c0:c0 + LANES] * wv[8:win]
        for d in range(1, GDN_CONV):
            w = GDN_CONV - 1 - d
            y = y + cw_ref[w:w + 1, c0:c0 + LANES] * pltpu.roll(wv, d, 0)[8:win]
        return _silu(y)

    row = lax.broadcasted_iota(jnp.int32, (CHUNK, CHUNK), 0)
    col = lax.broadcasted_iota(jnp.int32, (CHUNK, CHUNK), 1)
    incl = row >= col
    strict = row > col
    eye = (row == col).astype(F32)
    masks = [_pair_mask(s) for s in (1, 2, 4, 8, 16, 32)]

    probs = [(j, h) for j in range(GDN_NB) for h in range(GDN_HEADS)]
    beta, gcum, egc, e_rest, gcum_t = [], [], [], [], []
    for j in range(GDN_NB):
        bg = p_ref[j * CHUNK:(j + 1) * CHUNK, GDN_BG0:GDN_BG0 + LANES]
        beta.append(_sigmoid(bg))
        gc = _cumsum_rows(-jnp.exp(alog_ref[...]) * _softplus(bg + dtb_ref[...]))
        gcum.append(gc)
        egc.append(jnp.exp(gc))
        e_rest.append(jnp.exp(gc[CHUNK - 1:CHUNK, :] - gc))
        gt = jnp.transpose(jnp.concatenate([gc, jnp.zeros_like(gc)], axis=0))
        gcum_t.append(gt)
        egl_ref[j] = jnp.broadcast_to(jnp.exp(gt[8:8 + GDN_HEADS, CHUNK - 1:CHUNK]),
                                      (GDN_HEADS, LANES))

    def col_of(vals, j, h, off=0):
        return vals[j][:, off + h:off + h + 1]

    q, k, v = [], [], []
    for j, h in probs:
        qh = conv_silu(j, h * GDN_DK)
        kh = conv_silu(j, GDN_QK + h * GDN_DK)
        q.append(qh * lax.rsqrt(jnp.sum(qh * qh, axis=-1, keepdims=True) + EPS) * (GDN_DK ** -0.5))
        k.append(kh * lax.rsqrt(jnp.sum(kh * kh, axis=-1, keepdims=True) + EPS))
        v.append(conv_silu(j, 2 * GDN_QK + h * GDN_DV))
    kb = [k[i] * col_of(beta, j, h) for i, (j, h) in enumerate(probs)]
    decay = [jnp.exp(jnp.minimum(col_of(gcum, j, h, 8) - gcum_t[j][8 + h:9 + h, 0:CHUNK], 0.0))
             for j, h in probs]
    qk = [_mm_nt(jnp.concatenate([kb[i], q[i]], axis=0), k[i]) for i in range(len(probs))]
    a = [jnp.where(strict, qk[i][0:CHUNK] * decay[i], 0.0) for i in range(len(probs))]
    for i, (j, h) in enumerate(probs):
        attn = jnp.where(incl, qk[i][CHUNK:2 * CHUNK] * decay[i], 0.0)
        attn_ref[j * CHUNK:(j + 1) * CHUNK, h * CHUNK:(h + 1) * CHUNK] = attn.astype(attn_ref.dtype)
    t = [eye - jnp.where(masks[0], ai, 0.0) for ai in a]
    for m in masks[1:]:
        at = [_mm(jnp.where(m, a[i], 0.0), t[i]) for i in range(len(probs))]
        t = [t[i] - _mm(t[i], at[i]) for i in range(len(probs))]
    for i, (j, h) in enumerate(probs):
        e_h = col_of(egc, j, h, 8)
        uw = _mm(t[i], jnp.concatenate([v[i] * col_of(beta, j, h), kb[i] * e_h], axis=1))
        rs = slice(j * CHUNK, (j + 1) * CHUNK)
        cs = slice(h * GDN_DK, (h + 1) * GDN_DK)
        u_ref[rs, cs] = uw[:, 0:GDN_DV]
        wq_ref[j, 0:CHUNK, cs] = uw[:, GDN_DV:].astype(wq_ref.dtype)
        wq_ref[j, CHUNK:2 * CHUNK, cs] = (q[i] * e_h).astype(wq_ref.dtype)
        kd_ref[rs, cs] = (k[i] * col_of(e_rest, j, h, 8)).astype(kd_ref.dtype)


def _gdn_pre(p, halo0, cw, alog, dtb, *, n_p, cpl, n_s):
    t = p.shape[0]
    rows = GDN_NB * CHUNK
    n_chunks = t // CHUNK
    n_seq = n_p + n_s
    bps = cpl // GDN_NB
    return pl.pallas_call(
        functools.partial(_gdn_pre_body, n_pb=n_p * bps, bps=bps, n_p=n_p),
        grid=(t // rows,),
        in_specs=[pl.BlockSpec((rows, GDN_P), lambda b: (b, 0)),
                  pl.BlockSpec((8, GDN_CONV_CH), lambda b: (jnp.maximum(b * (rows // 8) - 1, 0), 0)),
                  _resident((n_seq, 8, GDN_CONV_CH)),
                  _resident((GDN_CONV, GDN_CONV_CH)),
                  _resident((1, LANES)), _resident((1, LANES))],
        out_specs=[pl.BlockSpec((rows, D_MODEL), lambda b: (b, 0)),
                   pl.BlockSpec((GDN_NB, 2 * CHUNK, D_MODEL), lambda b: (b, 0, 0)),
                   pl.BlockSpec((rows, D_MODEL), lambda b: (b, 0)),
                   pl.BlockSpec((rows, GDN_HEADS * CHUNK), lambda b: (b, 0)),
                   pl.BlockSpec((GDN_NB, GDN_HEADS, LANES), lambda b: (b, 0, 0))],
        out_shape=[jax.ShapeDtypeStruct((t, D_MODEL), F32),
                   jax.ShapeDtypeStruct((n_chunks, 2 * CHUNK, D_MODEL), BF16),
                   jax.ShapeDtypeStruct((t, D_MODEL), BF16),
                   jax.ShapeDtypeStruct((t, GDN_HEADS * CHUNK), BF16),
                   jax.ShapeDtypeStruct((n_chunks, GDN_HEADS, LANES), F32)],
        scratch_shapes=[pltpu.VMEM((GDN_NB * (CHUNK + 8), GDN_CONV_CH), F32)],
        compiler_params=_params(), name="gdn_pre")(p, p, halo0, cw, alog, dtb)


def _gdn_scan_body(u_ref, wq_ref, kd_ref, attn_ref, egl_ref, z_ref, s0_ref, nw_ref,
                   o_ref, sout_ref, s_scr, *, n_pc, cpl):
    g, is_prompt, is_first, is_last = _chunk_flags(n_pc, cpl)

    @pl.when(jnp.logical_and(is_first, is_prompt))
    def _():
        s_scr[...] = jnp.zeros_like(s_scr)

    @pl.when(jnp.logical_not(is_prompt))
    def _():
        s_scr[...] = s0_ref[0]

    heads = range(GDN_HEADS)
    cs = [slice(h * GDN_DK, (h + 1) * GDN_DK) for h in heads]
    s = [s_scr[h] for h in heads]
    ws_qs = [jnp.dot(wq_ref[0, :, cs[h]], s[h].astype(BF16), preferred_element_type=F32)
             for h in heads]
    v_new = [(u_ref[:, cs[h]] - ws_qs[h][0:CHUNK]).astype(BF16) for h in heads]
    o = [ws_qs[h][CHUNK:2 * CHUNK]
         + jnp.dot(attn_ref[:, h * CHUNK:(h + 1) * CHUNK], v_new[h], preferred_element_type=F32)
         for h in heads]
    for h in heads:
        s_scr[h] = s[h] * egl_ref[0, h:h + 1, :] + _mm_tn(kd_ref[:, cs[h]], v_new[h])
        o_ref[:, cs[h]] = (_rms(o[h], nw_ref[...]) * _silu(z_ref[:, cs[h]])).astype(o_ref.dtype)

    @pl.when(is_last)
    def _():
        sout_ref[0] = s_scr[...]


def _gdn_scan(u, wq, kd, attn, egl, p, s0, nw, *, n_p, cpl, n_s):
    t = u.shape[0]
    n_pc = n_p * cpl
    seq = lambda g: _seq_of_chunk(g, n_pc, cpl, n_p)
    row_blk = lambda g: (g, 0)
    return pl.pallas_call(
        functools.partial(_gdn_scan_body, n_pc=n_pc, cpl=cpl),
        grid=(t // CHUNK,),
        in_specs=[pl.BlockSpec((CHUNK, D_MODEL), row_blk),
                  pl.BlockSpec((1, 2 * CHUNK, D_MODEL), lambda g: (g, 0, 0)),
                  pl.BlockSpec((CHUNK, D_MODEL), row_blk),
                  pl.BlockSpec((CHUNK, GDN_HEADS * CHUNK), row_blk),
                  pl.BlockSpec((1, GDN_HEADS, LANES), lambda g: (g, 0, 0)),
                  pl.BlockSpec((CHUNK, D_MODEL), lambda g: (g, GDN_Z0 // D_MODEL)),
                  pl.BlockSpec((1, GDN_HEADS, GDN_DK, GDN_DV),
                               lambda g: (jnp.maximum(g - n_pc, 0), 0, 0, 0)),
                  _resident((1, GDN_DV))],
        out_specs=[pl.BlockSpec((CHUNK, D_MODEL), row_blk),
                   pl.BlockSpec((1, GDN_HEADS, GDN_DK, GDN_DV), lambda g: (seq(g), 0, 0, 0))],
        out_shape=[jax.ShapeDtypeStruct((t, D_MODEL), BF16),
                   jax.ShapeDtypeStruct((n_p + n_s, GDN_HEADS, GDN_DK, GDN_DV), F32)],
        scratch_shapes=[pltpu.VMEM((GDN_HEADS, GDN_DK, GDN_DV), F32)],
        compiler_params=_params(), name="gdn_scan")(u, wq, kd, attn, egl, p, s0, nw)


def _gla_exponent_matrix():
    i = np.arange(CHUNK)[:, None]
    t = np.arange(CHUNK)[None, :]
    blocks = [(t <= i)]
    for s in GLA_LEVELS:
        c = (i // (2 * s)) * (2 * s) + s - 1
        blocks.append(np.where(i > c, (t > c) & (t <= i), (t > i) & (t <= c)))
    w = np.concatenate(blocks, axis=0).astype(np.float32)
    return np.concatenate([w, w, w], axis=1)


def _split3(x):
    hi = x.astype(BF16)
    r1 = x - hi.astype(F32)
    mid = r1.astype(BF16)
    lo = (r1 - mid.astype(F32)).astype(BF16)
    return hi, mid, lo


def _gla_body(p_ref, s0_ref, wg2_ref, bg_ref, wexp_ref, nw_ref,
              o_ref, sout_ref, st_scr, *, n_pc, cpl):
    g, is_prompt, is_first, is_last = _chunk_flags(n_pc, cpl)

    @pl.when(jnp.logical_and(is_first, is_prompt))
    def _():
        st_scr[...] = jnp.zeros_like(st_scr)

    @pl.when(jnp.logical_not(is_prompt))
    def _():
        st_scr[...] = s0_ref[0]

    x = _mm(p_ref[:, GLA_G0:GLA_G0 + LANES], wg2_ref[...]) + bg_ref[...]
    glog = -_softplus(-x) * (1.0 / GLA_TAU)
    e_all = jnp.dot(wexp_ref[...], jnp.concatenate(_split3(glog), axis=0),
                    preferred_element_type=F32)

    row = lax.broadcasted_iota(jnp.int32, (CHUNK, CHUNK), 0)
    col = lax.broadcasted_iota(jnp.int32, (CHUNK, CHUNK), 1)
    eye = row == col
    masks = [_pair_mask(s) for s in GLA_LEVELS]

    heads = range(GLA_HEADS)
    ck = [slice(h * GLA_DK, (h + 1) * GLA_DK) for h in heads]
    cv = [slice(h * GLA_DV, (h + 1) * GLA_DV) for h in heads]
    q_all = p_ref[:, 0:GLA_QK] * (GLA_DK ** -0.5)
    k_all = p_ref[:, GLA_QK:2 * GLA_QK]
    qb, kb = q_all.astype(BF16), k_all.astype(BF16)
    attn = [jnp.where(eye, _mm_nt(qb[:, ck[h]], kb[:, ck[h]]), 0.0) for h in heads]
    for lvl, m in enumerate(masks):
        f = jnp.exp(e_all[(lvl + 1) * CHUNK:(lvl + 2) * CHUNK, :])
        qf, kf = (q_all * f).astype(BF16), (k_all * f).astype(BF16)
        attn = [attn[h] + jnp.where(m, _mm_nt(qf[:, ck[h]], kf[:, ck[h]]), 0.0) for h in heads]
    b = e_all[0:CHUNK, :]
    b_last = b[CHUNK - 1:CHUNK, :]
    qe = (q_all * jnp.exp(b)).astype(BF16)
    kd = (k_all * jnp.exp(b_last - b)).astype(BF16)
    eb_last = jnp.exp(b_last)
    vb = p_ref[:, GLA_V0:GLA_V0 + D_MODEL].astype(BF16)
    st = [st_scr[h] for h in heads]
    o = [_mm_nt(qe[:, ck[h]], st[h]) + _mm(attn[h], vb[:, cv[h]]) for h in heads]
    for h in heads:
        st_scr[h] = st[h] * eb_last[:, ck[h]] + _mm_tn(vb[:, cv[h]], kd[:, ck[h]])
        r = p_ref[:, GLA_R0 + h * GLA_DV:GLA_R0 + (h + 1) * GLA_DV]
        o_ref[:, cv[h]] = (_rms(o[h], nw_ref[h:h + 1, :]) * _silu(r)).astype(o_ref.dtype)

    @pl.when(is_last)
    def _():
        sout_ref[0] = st_scr[...]


def _gla_scan(p, s0t, wg2, bg, wexp, nw, *, n_p, cpl, n_s):
    t = p.shape[0]
    n_pc = n_p * cpl
    n_seq = n_p + n_s
    seq = lambda g: _seq_of_chunk(g, n_pc, cpl, n_p)
    return pl.pallas_call(
        functools.partial(_gla_body, n_pc=n_pc, cpl=cpl),
        grid=(t // CHUNK,),
        in_specs=[pl.BlockSpec((CHUNK, GLA_P), lambda g: (g, 0)),
                  pl.BlockSpec((1, GLA_HEADS, GLA_DV, GLA_DK),
                               lambda g: (jnp.maximum(g - n_pc, 0), 0, 0, 0)),
                  _resident((LANES, GLA_QK)), _resident((1, GLA_QK)),
                  _resident(((len(GLA_LEVELS) + 1) * CHUNK, 3 * CHUNK)),
                  _resident((GLA_HEADS, GLA_DV))],
        out_specs=[pl.BlockSpec((CHUNK, D_MODEL), lambda g: (g, 0)),
                   pl.BlockSpec((1, GLA_HEADS, GLA_DV, GLA_DK), lambda g: (seq(g), 0, 0, 0))],
        out_shape=[jax.ShapeDtypeStruct((t, D_MODEL), BF16),
                   jax.ShapeDtypeStruct((n_seq, GLA_HEADS, GLA_DV, GLA_DK), F32)],
        scratch_shapes=[pltpu.VMEM((GLA_HEADS, GLA_DV, GLA_DK), F32)],
        compiler_params=_params(), name="gla_scan")(p, s0t, wg2, bg, wexp, nw)


def _gmlp_body(zz_ref, lnw_ref, lnb_ref, ws_ref, bs_ref, o_ref, vs_ref, vprev_scr, *, n_pc):
    g = pl.program_id(0)
    is_prompt = g < n_pc
    second = jnp.logical_and(is_prompt, g % 2 == 1)

    @pl.when(g == 0)
    def _():
        vprev_scr[...] = jnp.zeros_like(vprev_scr)

    v = _layernorm(zz_ref[:, D_MODEL:2 * D_MODEL], lnw_ref[...], lnb_ref[...])

    @pl.when(jnp.logical_not(is_prompt))
    def _():
        vs_ref[...] = v

    vb = v.astype(BF16)
    v_a = jnp.where(second, vprev_scr[...], vb)
    r0 = pl.multiple_of(jnp.where(second, CHUNK, 0), CHUNK)
    t_idx = r0 + lax.broadcasted_iota(jnp.int32, (CHUNK, GMLP_BLOCK), 0)
    s_idx = lax.broadcasted_iota(jnp.int32, (CHUNK, GMLP_BLOCK), 1)
    causal = s_idx <= t_idx
    bias = bs_ref[pl.ds(r0, CHUNK), :]
    for grp in range(GMLP_GROUPS):
        cs = slice(grp * GMLP_GC, (grp + 1) * GMLP_GC)
        wrow = jnp.where(causal, ws_ref[grp, pl.ds(r0, CHUNK), :], 0.0).astype(BF16)
        mixed = (jnp.dot(wrow[:, 0:CHUNK], v_a[:, cs], preferred_element_type=F32)
                 + jnp.dot(wrow[:, CHUNK:GMLP_BLOCK], vb[:, cs], preferred_element_type=F32))
        o_ref[:, cs] = (zz_ref[:, cs] * (mixed + bias[:, cs])).astype(o_ref.dtype)
    vprev_scr[...] = vb


def _gmlp_mix(zz, lnw, lnb, ws, bs_full, *, n_pc, n_s):
    t = zz.shape[0]
    return pl.pallas_call(
        functools.partial(_gmlp_body, n_pc=n_pc),
        grid=(t // CHUNK,),
        in_specs=[pl.BlockSpec((CHUNK, 2 * D_MODEL), lambda g: (g, 0)),
                  _resident((1, D_MODEL)), _resident((1, D_MODEL)),
                  _resident((GMLP_GROUPS, GMLP_BLOCK, GMLP_BLOCK)),
                  _resident((GMLP_BLOCK, D_MODEL))],
        out_specs=[pl.BlockSpec((CHUNK, D_MODEL), lambda g: (g, 0)),
                   pl.BlockSpec((CHUNK, D_MODEL), lambda g: (jnp.maximum(g - n_pc, 0), 0))],
        out_shape=[jax.ShapeDtypeStruct((t, D_MODEL), BF16),
                   jax.ShapeDtypeStruct((n_s * CHUNK, D_MODEL), F32)],
        scratch_shapes=[pltpu.VMEM((CHUNK, D_MODEL), BF16)],
        compiler_params=_params(), name="gmlp_mix")(zz, lnw, lnb, ws, bs_full)


def _cnv_body(x_ref, halo_ref, halo0_ref, wdw_ref, bdw_ref, o_ref, xc_scr, *, n_pc, cpl):
    g, is_prompt, is_first, is_last = _chunk_flags(n_pc, cpl)
    xc_scr[0:CONV_HALO, :] = jnp.where(is_first, halo0_ref[0], halo_ref[...])
    xc_scr[CONV_HALO:CONV_HALO + CHUNK, :] = x_ref[...]
    off = CONV_HALO - (CONV_W - 1)
    for c0 in range(0, D_MODEL, 2 * LANES):
        cs = slice(c0, c0 + 2 * LANES)
        y = wdw_ref[0:1, cs] * xc_scr[off:off + CHUNK, cs]
        for w in range(1, CONV_W):
            y = y + wdw_ref[w:w + 1, cs] * xc_scr[off + w:off + w + CHUNK, cs]
        o_ref[:, cs] = (y + bdw_ref[:, cs]).astype(o_ref.dtype)


def _cnv_conv(glu, halo0, wdw, bdw, *, n_p, cpl, n_s):
    t = glu.shape[0]
    n_pc = n_p * cpl
    seq = lambda g: _seq_of_chunk(g, n_pc, cpl, n_p)
    return pl.pallas_call(
        functools.partial(_cnv_body, n_pc=n_pc, cpl=cpl),
        grid=(t // CHUNK,),
        in_specs=[pl.BlockSpec((CHUNK, D_MODEL), lambda g: (g, 0)),
                  pl.BlockSpec((CONV_HALO, D_MODEL), lambda g: (jnp.maximum(2 * g - 1, 0), 0)),
                  pl.BlockSpec((1, CONV_HALO, D_MODEL), lambda g: (seq(g), 0, 0)),
                  _resident((CONV_HALO, D_MODEL)), _resident((1, D_MODEL))],
        out_specs=pl.BlockSpec((CHUNK, D_MODEL), lambda g: (g, 0)),
        out_shape=jax.ShapeDtypeStruct((t, D_MODEL), F32),
        scratch_shapes=[pltpu.VMEM((CONV_HALO + CHUNK, D_MODEL), F32)],
        compiler_params=_params(), name="cnv_conv")(glu, glu, halo0, wdw, bdw)


def _cnv_out_body(y_ref, res_ref, lnw_ref, lnb_ref, w_ref, b_ref, o_ref):
    a = _silu(_layernorm(y_ref[...], lnw_ref[...], lnb_ref[...])).astype(BF16)
    o_ref[...] = res_ref[...] + jnp.dot(a, w_ref[...], preferred_element_type=F32) + b_ref[...]


def _cnv_out(y, res, lnw, lnb, w, b):
    t = y.shape[0]
    return pl.pallas_call(
        _cnv_out_body,
        grid=(t // ROW_TILE,),
        in_specs=[pl.BlockSpec((ROW_TILE, D_MODEL), lambda i: (i, 0)),
                  pl.BlockSpec((ROW_TILE, D_MODEL), lambda i: (i, 0)),
                  _resident((1, D_MODEL)), _resident((1, D_MODEL)),
                  _resident((D_MODEL, D_MODEL)), _resident((1, D_MODEL))],
        out_specs=pl.BlockSpec((ROW_TILE, D_MODEL), lambda i: (i, 0)),
        out_shape=jax.ShapeDtypeStruct((t, D_MODEL), F32),
        compiler_params=_params(), name="cnv_out")(y, res, lnw, lnb, w, b)


def _pad_cols(w, n):
    return jnp.pad(w, ((0, 0), (0, n - w.shape[1])))


def _row(v):
    return v.reshape(1, -1).astype(F32)


def _tail_rows(a, lay, n):
    n_p, cpl, n_s = lay
    ends = np.concatenate([(np.arange(n_p) + 1) * cpl * CHUNK,
                           n_p * cpl * CHUNK + (np.arange(n_s) + 1) * CHUNK])
    idx = (ends[:, None] - n + np.arange(n)[None, :]).reshape(-1)
    return jnp.take(a, jnp.asarray(idx, jnp.int32), axis=0).reshape(n_p + n_s, n, a.shape[1])


def _gdn_layer(x, lay, mix_nw, cache, state, w_in, conv_w, a_log, dt_bias, norm_w, w_out):
    n_p, cpl, n_s = lay
    p = _proj(x, _row(mix_nw), _pad_cols(w_in, GDN_P).astype(BF16), None,
              n_out=GDN_P, ck=384, name="gdn_in")
    halo0 = jnp.concatenate([jnp.zeros((n_p, 8, GDN_CONV_CH), F32),
                             jnp.pad(cache, ((0, 0), (8 - (GDN_CONV - 1), 0), (0, 0)))], axis=0)
    lane_pad = lambda v: jnp.pad(v.reshape(1, -1), ((0, 0), (GDN_HEADS, LANES - 2 * GDN_HEADS)))
    u, wq, kd, attn, egl = _gdn_pre(p, halo0, conv_w, lane_pad(a_log), lane_pad(dt_bias),
                                    n_p=n_p, cpl=cpl, n_s=n_s)
    o, s_all = _gdn_scan(u, wq, kd, attn, egl, p, state, _row(norm_w), n_p=n_p, cpl=cpl, n_s=n_s)
    x = _outproj(o, x, w_out.astype(BF16), None, name="gdn_out")
    conv_rows = _tail_rows(p, lay, GDN_CONV - 1)[:, :, :GDN_CONV_CH]
    return x, conv_rows[:n_p], conv_rows[n_p:], s_all[:n_p], s_all[n_p:]


def _gla_layer(x, lay, mix_nw, state, w_in, w_g2, b_g, norm_w, w_out):
    n_p, cpl, n_s = lay
    p = _proj(x, _row(mix_nw), _pad_cols(w_in, GLA_P).astype(BF16), None,
              n_out=GLA_P, ck=640, name="gla_in")
    wg2 = jnp.pad(w_g2, ((0, LANES - GLA_RANK), (0, 0))).astype(BF16)
    wexp = jnp.asarray(_gla_exponent_matrix(), BF16)
    o, st_all = _gla_scan(p, jnp.swapaxes(state, 2, 3), wg2, _row(b_g), wexp, norm_w,
                          n_p=n_p, cpl=cpl, n_s=n_s)
    x = _outproj(o, x, w_out.astype(BF16), None, name="gla_out")
    s_all = jnp.swapaxes(st_all, 2, 3)
    return x, s_all[:n_p], s_all[n_p:]


def _gmlp_layer(x, lay, mix_nw, w_in, b_in, ln_w, ln_b, w_s, b_s, w_out, b_out):
    n_p, cpl, n_s = lay
    zz = _proj(x, _row(mix_nw), w_in.astype(BF16), _row(b_in),
               n_out=2 * D_MODEL, ck=512, epilogue="gelu", name="gmlp_in")
    bs_full = jnp.repeat(b_s.T, GMLP_GC, axis=1)
    a, v_s = _gmlp_mix(zz, _row(ln_w), _row(ln_b), w_s, bs_full, n_pc=n_p * cpl, n_s=n_s)
    x = _outproj(a, x, w_out.astype(BF16), _row(b_out), name="gmlp_out")
    return x, v_s.reshape(n_s, CHUNK, D_MODEL)


def _cnv_layer(x, lay, mix_nw, cache, w_pw1, b_pw1, w_dw, b_dw, ln_w, ln_b, w_pw2, b_pw2):
    n_p, cpl, n_s = lay
    glu = _proj(x, _row(mix_nw), w_pw1.astype(BF16), _row(b_pw1),
                n_out=D_MODEL, ck=512, epilogue="glu", name="cnv_in")
    pad_rows = CONV_HALO - (CONV_W - 1)
    halo0 = jnp.concatenate([jnp.zeros((n_p, CONV_HALO, D_MODEL), F32),
                             jnp.pad(cache, ((0, 0), (pad_rows, 0), (0, 0)))], axis=0)
    wdw = jnp.pad(w_dw, ((0, CONV_HALO - CONV_W), (0, 0)))
    y = _cnv_conv(glu, halo0, wdw, _row(b_dw), n_p=n_p, cpl=cpl, n_s=n_s)
    x = _cnv_out(y, x, _row(ln_w), _row(ln_b), w_pw2.astype(BF16), _row(b_pw2))
    rows = _tail_rows(glu, lay, CONV_W - 1)
    return x, rows[:n_p], rows[n_p:]


def kernel(x_prompt, x_sample, cache_gdn_conv, state_gdn, state_gla, cache_conformer, mix_norm_w, ffn_norm_w, ffn_w1, ffn_w2, final_norm_w, gdn_w_in, gdn_conv_w, gdn_a_log, gdn_dt_bias, gdn_norm_w, gdn_w_out, gla_w_in, gla_w_g2, gla_b_g, gla_norm_w, gla_w_out, gmlp_w_in, gmlp_b_in, gmlp_ln_w, gmlp_ln_b, gmlp_w_s, gmlp_b_s, gmlp_w_out, gmlp_b_out, cnv_w_pw1, cnv_b_pw1, cnv_w_dw, cnv_b_dw, cnv_ln_w, cnv_ln_b, cnv_w_pw2, cnv_b_pw2):
    n_p, l_p, d = x_prompt.shape
    n_s, l_s, _ = x_sample.shape
    assert d == D_MODEL and l_s == CHUNK and l_p % GMLP_BLOCK == 0
    t_p = n_p * l_p
    assert t_p % ROW_TILE == 0 and (n_s * l_s) % ROW_TILE == 0
    lay = (n_p, l_p // CHUNK, n_s)
    depth = mix_norm_w.shape[0]

    x = jnp.concatenate([x_prompt.reshape(t_p, d), x_sample.reshape(n_s * l_s, d)], axis=0)
    outs = {k: [] for k in ("gdn_cp", "gdn_cs", "gdn_sp", "gdn_ss", "gla_sp", "gla_ss",
                            "gmlp_vs", "cnv_p", "cnv_s")}
    for i in range(depth):
        kind, j = i % 4, i // 4
        if kind == 0:
            x, cp, cs, sp, ss = _gdn_layer(x, lay, mix_norm_w[i], cache_gdn_conv[j], state_gdn[j],
                                           gdn_w_in[j], gdn_conv_w[j], gdn_a_log[j], gdn_dt_bias[j],
                                           gdn_norm_w[j], gdn_w_out[j])
            outs["gdn_cp"].append(cp); outs["gdn_cs"].append(cs)
            outs["gdn_sp"].append(sp); outs["gdn_ss"].append(ss)
        elif kind == 1:
            x, sp, ss = _gla_layer(x, lay, mix_norm_w[i], state_gla[j], gla_w_in[j], gla_w_g2[j],
                                   gla_b_g[j], gla_norm_w[j], gla_w_out[j])
            outs["gla_sp"].append(sp); outs["gla_ss"].append(ss)
        elif kind == 2:
            x, vs = _gmlp_layer(x, lay, mix_norm_w[i], gmlp_w_in[j], gmlp_b_in[j], gmlp_ln_w[j],
                                gmlp_ln_b[j], gmlp_w_s[j], gmlp_b_s[j], gmlp_w_out[j], gmlp_b_out[j])
            outs["gmlp_vs"].append(vs)
        else:
            x, cp, cs = _cnv_layer(x, lay, mix_norm_w[i], cache_conformer[j], cnv_w_pw1[j],
                                   cnv_b_pw1[j], cnv_w_dw[j], cnv_b_dw[j], cnv_ln_w[j], cnv_ln_b[j],
                                   cnv_w_pw2[j], cnv_b_pw2[j])
            outs["cnv_p"].append(cp); outs["cnv_s"].append(cs)
        x = _ffn(x, _row(ffn_norm_w[i]), ffn_w1[i].astype(BF16), ffn_w2[i].astype(BF16),
                 name=f"ffn_{i}")
    y_p, y_s = _final_norm(x, _row(final_norm_w), t_p=t_p, name="final_norm")
    return (y_p.reshape(n_p, l_p, d), y_s.reshape(n_s, l_s, d),
            jnp.stack(outs["gdn_cp"]), jnp.stack(outs["gdn_cs"]),
            jnp.stack(outs["gdn_sp"]), jnp.stack(outs["gdn_ss"]),
            jnp.stack(outs["gla_sp"]), jnp.stack(outs["gla_ss"]),
            jnp.stack(outs["gmlp_vs"]),
            jnp.stack(outs["cnv_p"]), jnp.stack(outs["cnv_s"]))
```

```python
import functools
import math

import numpy as np
import jax
import jax.numpy as jnp
from jax import lax
from jax.experimental import pallas as pl
from jax.experimental.pallas import tpu as pltpu

F32 = jnp.float32
BF16 = jnp.bfloat16

D_MODEL = 1024
D_FF = 4 * D_MODEL
EPS = 1e-6
CHUNK = 64
ROW_TILE = 512
SCAN_NB = 2
LANES = 128
VMEM_LIMIT = 56 * 1024 * 1024

GDN_HEADS, GDN_DK, GDN_DV, GDN_CONV = 8, 128, 128, 4
GDN_QK = GDN_HEADS * GDN_DK
GDN_CONV_CH = 2 * GDN_QK + GDN_HEADS * GDN_DV
GDN_Z0 = GDN_CONV_CH
GDN_BG0 = GDN_CONV_CH + GDN_HEADS * GDN_DV
GDN_P = GDN_BG0 + LANES

GLA_HEADS, GLA_DK, GLA_DV, GLA_RANK, GLA_TAU = 4, 128, 256, 16, 16.0
GLA_QK = GLA_HEADS * GLA_DK
GLA_V0, GLA_R0, GLA_G0 = 2 * GLA_QK, 2 * GLA_QK + D_MODEL, 2 * GLA_QK + 2 * D_MODEL
GLA_P = GLA_G0 + LANES
GLA_LEVELS = (32, 16, 8, 4, 2, 1)

GMLP_BLOCK, GMLP_GROUPS, GMLP_GC = 128, 4, 256
CONV_W = 31
CONV_HALO = 32


def _mm(a, b):
    return jnp.dot(a.astype(BF16), b.astype(BF16), preferred_element_type=F32)


def _mm_nt(a, b):
    return lax.dot_general(a.astype(BF16), b.astype(BF16), (((1,), (1,)), ((), ())),
                           preferred_element_type=F32)


def _mm_tn(a, b):
    return lax.dot_general(a.astype(BF16), b.astype(BF16), (((0,), (0,)), ((), ())),
                           preferred_element_type=F32)


def _rms(x, w):
    return x * lax.rsqrt(jnp.mean(x * x, axis=-1, keepdims=True) + EPS) * w


def _layernorm(x, w, b):
    xc = x - jnp.mean(x, axis=-1, keepdims=True)
    return xc * lax.rsqrt(jnp.mean(xc * xc, axis=-1, keepdims=True) + EPS) * w + b


def _sigmoid(x):
    return jax.nn.sigmoid(x)


def _silu(x):
    return x * _sigmoid(x)


def _softplus(x):
    return jnp.maximum(x, 0.0) + jnp.log1p(jnp.exp(-jnp.abs(x)))


def _gelu_tanh(x):
    return 0.5 * x * (1.0 + jnp.tanh(math.sqrt(2.0 / math.pi) * (x + 0.044715 * (x * x * x))))


def _cumsum_rows(x):
    row = lax.broadcasted_iota(jnp.int32, x.shape, 0)
    s = 1
    while s < x.shape[0]:
        x = x + jnp.where(row >= s, pltpu.roll(x, s, 0), 0.0)
        s *= 2
    return x


def _pair_mask(s):
    sh = int(math.log2(s))
    bi = lax.broadcasted_iota(jnp.int32, (CHUNK, CHUNK), 0) >> sh
    bj = lax.broadcasted_iota(jnp.int32, (CHUNK, CHUNK), 1) >> sh
    return jnp.logical_and((bi & 1) == 1, bj == bi - 1)


def _block_flags(n_pb, bps):
    blk = pl.program_id(0)
    is_prompt = blk < n_pb
    first = jnp.logical_and(is_prompt, blk % bps == 0)
    last = jnp.logical_and(is_prompt, blk % bps == bps - 1)
    return blk, is_prompt, first, last


def _state_specs(n_p, n_s, n_pb, bps, shape):
    zeros = (0,) * len(shape)
    sample = pl.BlockSpec((SCAN_NB,) + shape, lambda b: (jnp.maximum(b - n_pb, 0),) + zeros)
    prompt = pl.BlockSpec((1,) + shape, lambda b: (jnp.minimum(b // bps, n_p - 1),) + zeros)
    shapes = [jax.ShapeDtypeStruct((n_p,) + shape, F32), jax.ShapeDtypeStruct((n_s,) + shape, F32)]
    return sample, [prompt, sample], shapes


def _params(n_axes=1):
    return pltpu.CompilerParams(dimension_semantics=("arbitrary",) * n_axes,
                                vmem_limit_bytes=VMEM_LIMIT)


def _resident(shape):
    zeros = (0,) * len(shape)
    return pl.BlockSpec(shape, lambda i: zeros, pipeline_mode=pl.Buffered(1))


def _row_tile_specs(x, n_pt):
    if isinstance(x, tuple):
        return ([pl.BlockSpec((ROW_TILE, D_MODEL), lambda i: (jnp.minimum(i, n_pt - 1), 0)),
                 pl.BlockSpec((ROW_TILE, D_MODEL), lambda i: (jnp.maximum(i - n_pt, 0), 0))], list(x))
    return [pl.BlockSpec((ROW_TILE, D_MODEL), lambda i: (i, 0))], [x]


def _read_row_tile(refs, n_pt):
    if len(refs) == 2:
        return jnp.where(pl.program_id(0) < n_pt, refs[0][...], refs[1][...])
    return refs[0][...]


def _proj_body(*refs, n_x, n_pt, n_out, ck, epilogue, has_bias):
    x_refs, (nw_ref, w_ref), rest = refs[:n_x], refs[n_x:n_x + 2], refs[n_x + 2:]
    if has_bias:
        b_ref, o_ref = rest
    else:
        (o_ref,) = rest
    xn = _rms(_read_row_tile(x_refs, n_pt), nw_ref[...]).astype(BF16)
    for c in range(n_out // ck):
        sl = slice(c * ck, (c + 1) * ck)
        y = jnp.dot(xn, w_ref[:, sl], preferred_element_type=F32)
        if has_bias:
            y = y + b_ref[:, sl]
        if epilogue == "gelu":
            y = _gelu_tanh(y)
        elif epilogue == "glu":
            sg = slice(n_out + c * ck, n_out + (c + 1) * ck)
            gate = jnp.dot(xn, w_ref[:, sg], preferred_element_type=F32) + b_ref[:, sg]
            y = y * _sigmoid(gate)
        o_ref[:, sl] = y.astype(o_ref.dtype)


def _proj(x, nw, w, b, *, t, n_pt, n_out, ck, epilogue="none", out_dtype=F32, name):
    n_w = w.shape[1]
    has_bias = b is not None
    in_specs, args = _row_tile_specs(x, n_pt)
    n_x = len(args)
    in_specs += [_resident((1, D_MODEL)), _resident((D_MODEL, n_w))]
    args += [nw, w]
    if has_bias:
        in_specs.append(_resident((1, n_w)))
        args.append(b)
    return pl.pallas_call(
        functools.partial(_proj_body, n_x=n_x, n_pt=n_pt, n_out=n_out, ck=ck, epilogue=epilogue,
                          has_bias=has_bias),
        grid=(t // ROW_TILE,),
        in_specs=in_specs,
        out_specs=pl.BlockSpec((ROW_TILE, n_out), lambda i: (i, 0)),
        out_shape=jax.ShapeDtypeStruct((t, n_out), out_dtype),
        compiler_params=_params(), name=name)(*args)


FFN_CK = 512


def _gmlp_gate(zz_ref, lnw_ref, lnb_ref, ws_ref, bs_ref, vs_ref, a_scr, is_sample):
    v = _layernorm(zz_ref[:, D_MODEL:2 * D_MODEL], lnw_ref[...], lnb_ref[...])

    @pl.when(is_sample)
    def _():
        vs_ref[...] = v

    vb = v.astype(BF16)
    row = lax.broadcasted_iota(jnp.int32, (GMLP_BLOCK, GMLP_BLOCK), 0)
    col = lax.broadcasted_iota(jnp.int32, (GMLP_BLOCK, GMLP_BLOCK), 1)
    tril = col <= row
    top_left = jnp.logical_and(row < CHUNK, col < CHUNK)
    brow = lax.broadcasted_iota(jnp.int32, (GMLP_BLOCK, D_MODEL), 0)
    bias = bs_ref[...]
    bias = jnp.where(jnp.logical_and(is_sample, brow >= CHUNK), pltpu.roll(bias, CHUNK, 0), bias)
    for grp in range(GMLP_GROUPS):
        cs = slice(grp * GMLP_GC, (grp + 1) * GMLP_GC)
        wt = jnp.where(tril, ws_ref[grp], 0.0)
        w11 = jnp.where(top_left, wt, 0.0)
        wd = w11 + pltpu.roll(pltpu.roll(w11, CHUNK, 0), CHUNK, 1)
        we = jnp.where(is_sample, wd, wt).astype(BF16)
        for blk in range(ROW_TILE // GMLP_BLOCK):
            rs = slice(blk * GMLP_BLOCK, (blk + 1) * GMLP_BLOCK)
            mixed = jnp.dot(we, vb[rs, cs], preferred_element_type=F32)
            a_scr[rs, cs] = (zz_ref[rs, cs] * (mixed + bias[:, cs])).astype(BF16)
    return a_scr[...]


def _post_body(*refs, prologue, n_res, has_bias, final, n_pt):
    n_in = {"plain": 1, "cnv": 3, "gmlp": 5}[prologue]
    pro, refs = refs[:n_in], refs[n_in:]
    res_refs, refs = refs[:n_res], refs[n_res:]
    wo_ref, refs = refs[0], refs[1:]
    if has_bias:
        bo_ref, refs = refs[0], refs[1:]
    fnw_ref, w1_ref, w2_ref, refs = refs[0], refs[1], refs[2], refs[3:]
    if final:
        finw_ref, refs = refs[0], refs[1:]
    n_main = 2 if final else 1
    outs, refs = refs[:n_main], refs[n_main:]
    is_sample = pl.program_id(0) >= n_pt

    if prologue == "plain":
        a = pro[0][...]
    elif prologue == "cnv":
        y_ref, lnw_ref, lnb_ref = pro
        a = _silu(_layernorm(y_ref[...], lnw_ref[...], lnb_ref[...])).astype(BF16)
    else:
        vs_ref, a_scr = refs
        a = _gmlp_gate(*pro, vs_ref, a_scr, is_sample)

    x = _read_row_tile(res_refs, n_pt) + jnp.dot(a, wo_ref[...], preferred_element_type=F32)
    if has_bias:
        x = x + bo_ref[...]
    xn = _rms(x, fnw_ref[...]).astype(BF16)
    for c in range(D_FF // FFN_CK):
        sl = slice(c * FFN_CK, (c + 1) * FFN_CK)
        h = jnp.dot(xn, w1_ref[:, sl], preferred_element_type=F32)
        h = jnp.square(jnp.maximum(h, 0.0)).astype(BF16)
        x = x + jnp.dot(h, w2_ref[sl, :], preferred_element_type=F32)
    if final:
        y = _rms(x, finw_ref[...])

        @pl.when(jnp.logical_not(is_sample))
        def _():
            outs[0][...] = y

        @pl.when(is_sample)
        def _():
            outs[1][...] = y
    else:
        outs[0][...] = x


def _post(pro_args, res, wo, bo, fnw, w1, w2, finw, *, prologue, t, n_pt, name):
    has_bias = bo is not None
    final = finw is not None
    tile = lambda c: pl.BlockSpec((ROW_TILE, c), lambda i: (i, 0))
    if prologue == "plain":
        in_specs = [tile(D_MODEL)]
    elif prologue == "cnv":
        in_specs = [tile(D_MODEL), _resident((1, D_MODEL)), _resident((1, D_MODEL))]
    else:
        in_specs = [tile(2 * D_MODEL), _resident((1, D_MODEL)), _resident((1, D_MODEL)),
                    _resident((GMLP_GROUPS, GMLP_BLOCK, GMLP_BLOCK)),
                    _resident((GMLP_BLOCK, D_MODEL))]
    args = list(pro_args)
    res_specs, res_args = _row_tile_specs(res, n_pt)
    in_specs += res_specs + [_resident((D_MODEL, D_MODEL))]
    args += res_args + [wo]
    if has_bias:
        in_specs.append(_resident((1, D_MODEL)))
        args.append(bo)
    in_specs += [_resident((1, D_MODEL)), _resident((D_MODEL, D_FF)), _resident((D_FF, D_MODEL))]
    args += [fnw, w1, w2]
    if final:
        in_specs.append(_resident((1, D_MODEL)))
        args.append(finw)
        out_specs = [pl.BlockSpec((ROW_TILE, D_MODEL), lambda i: (jnp.minimum(i, n_pt - 1), 0)),
                     pl.BlockSpec((ROW_TILE, D_MODEL), lambda i: (jnp.maximum(i - n_pt, 0), 0))]
        out_shape = [jax.ShapeDtypeStruct((n_pt * ROW_TILE, D_MODEL), F32),
                     jax.ShapeDtypeStruct((t - n_pt * ROW_TILE, D_MODEL), F32)]
    else:
        out_specs = [tile(D_MODEL)]
        out_shape = [jax.ShapeDtypeStruct((t, D_MODEL), F32)]
    scratch = []
    if prologue == "gmlp":
        out_specs.append(pl.BlockSpec((ROW_TILE, D_MODEL), lambda i: (jnp.maximum(i - n_pt, 0), 0)))
        out_shape.append(jax.ShapeDtypeStruct((t - n_pt * ROW_TILE, D_MODEL), F32))
        scratch.append(pltpu.VMEM((ROW_TILE, D_MODEL), BF16))
    return pl.pallas_call(
        functools.partial(_post_body, prologue=prologue, n_res=len(res_args), has_bias=has_bias,
                          final=final, n_pt=n_pt),
        grid=(t // ROW_TILE,),
        in_specs=in_specs, out_specs=out_specs, out_shape=out_shape, scratch_shapes=scratch,
        compiler_params=_params(), name=name)(*args)


GDN_NB = 2


def _gdn_pre_body(p_ref, halo_ref, halo0_ref, cw_ref, alog_ref, dtb_ref,
                  u_ref, wq_ref, kd_ref, attn_ref, egl_ref, xc_scr, *, n_pb, bps, n_p):
    blk = pl.program_id(0)
    is_prompt = blk < n_pb
    win = CHUNK + 8
    for j in range(GDN_NB):
        if j == 0:
            first = jnp.logical_or(jnp.logical_not(is_prompt), blk % bps == 0)
            prev = halo_ref[...]
        else:
            first = jnp.logical_not(is_prompt)
            prev = p_ref[j * CHUNK - 8:j * CHUNK, 0:GDN_CONV_CH]
        seq = jnp.where(is_prompt, blk // bps, n_p + (blk - n_pb) * GDN_NB + j)
        xc_scr[j * win:j * win + 8, :] = jnp.where(first, halo0_ref[seq], prev)
        xc_scr[j * win + 8:(j + 1) * win, :] = p_ref[j * CHUNK:(j + 1) * CHUNK, 0:GDN_CONV_CH]

    def conv_silu(j, c0):
        wv = xc_scr[j * win:(j + 1) * win, c0:c0 + LANES]
        y = cw_ref[GDN_CONV - 1:GDN_CONV, c0:c0 + LANES] * wv[8:win]
        for d in range(1, GDN_CONV):
            w = GDN_CONV - 1 - d
            y = y + cw_ref[w:w + 1, c0:c0 + LANES] * pltpu.roll(wv, d, 0)[8:win]
        return _silu(y)

    row = lax.broadcasted_iota(jnp.int32, (CHUNK, CHUNK), 0)
    col = lax.broadcasted_iota(jnp.int32, (CHUNK, CHUNK), 1)
    incl = row >= col
    strict = row > col
    eye = (row == col).astype(F32)
    masks = [_pair_mask(s) for s in (1, 2, 4, 8, 16, 32)]

    probs = [(j, h) for j in range(GDN_NB) for h in range(GDN_HEADS)]
    beta, gcum, egc, e_rest, gcum_t = [], [], [], [], []
    for j in range(GDN_NB):
        bg = p_ref[j * CHUNK:(j + 1) * CHUNK, GDN_BG0:GDN_BG0 + LANES]
        beta.append(_sigmoid(bg))
        gc = _cumsum_rows(-jnp.exp(alog_ref[...]) * _softplus(bg + dtb_ref[...]))
        gcum.append(gc)
        egc.append(jnp.exp(gc))
        e_rest.append(jnp.exp(gc[CHUNK - 1:CHUNK, :] - gc))
        gt = jnp.transpose(jnp.concatenate([gc, jnp.zeros_like(gc)], axis=0))
        gcum_t.append(gt)
        egl_ref[j] = jnp.broadcast_to(jnp.exp(gt[8:8 + GDN_HEADS, CHUNK - 1:CHUNK]),
                                      (GDN_HEADS, LANES))

    def col_of(vals, j, h, off=0):
        return vals[j][:, off + h:off + h + 1]

    q, k, v = [], [], []
    for j, h in probs:
        qh = conv_silu(j, h * GDN_DK)
        kh = conv_silu(j, GDN_QK + h * GDN_DK)
        q.append(qh * lax.rsqrt(jnp.sum(qh * qh, axis=-1, keepdims=True) + EPS) * (GDN_DK ** -0.5))
        k.append(kh * lax.rsqrt(jnp.sum(kh * kh, axis=-1, keepdims=True) + EPS))
        v.append(conv_silu(j, 2 * GDN_QK + h * GDN_DV))
    kb = [k[i] * col_of(beta, j, h) for i, (j, h) in enumerate(probs)]
    decay = [jnp.exp(jnp.minimum(col_of(gcum, j, h, 8) - gcum_t[j][8 + h:9 + h, 0:CHUNK], 0.0))
             for j, h in probs]
    qk = [_mm_nt(jnp.concatenate([kb[i], q[i]], axis=0), k[i]) for i in range(len(probs))]
    a = [jnp.where(strict, qk[i][0:CHUNK] * decay[i], 0.0) for i in range(len(probs))]
    for i, (j, h) in enumerate(probs):
        attn = jnp.where(incl, qk[i][CHUNK:2 * CHUNK] * decay[i], 0.0)
        attn_ref[j * CHUNK:(j + 1) * CHUNK, h * CHUNK:(h + 1) * CHUNK] = attn.astype(attn_ref.dtype)
    t = [eye - jnp.where(masks[0], ai, 0.0) for ai in a]
    a_b = [ai.astype(BF16) for ai in a]
    for m in masks[1:]:
        t_b = [ti.astype(BF16) for ti in t]
        at = [jnp.dot(jnp.where(m, a_b[i], jnp.zeros_like(a_b[i])), t_b[i],
                      preferred_element_type=F32) for i in range(len(probs))]
        t = [t[i] - jnp.dot(t_b[i], at[i].astype(BF16), preferred_element_type=F32)
             for i in range(len(probs))]
    for i, (j, h) in enumerate(probs):
        e_h = col_of(egc, j, h, 8)
        uw = _mm(t[i], jnp.concatenate([v[i] * col_of(beta, j, h), kb[i] * e_h], axis=1))
        rs = slice(j * CHUNK, (j + 1) * CHUNK)
        cs = slice(h * GDN_DK, (h + 1) * GDN_DK)
        u_ref[rs, cs] = uw[:, 0:GDN_DV]
        wq_ref[j, 0:CHUNK, cs] = uw[:, GDN_DV:].astype(wq_ref.dtype)
        wq_ref[j, CHUNK:2 * CHUNK, cs] = (q[i] * e_h).astype(wq_ref.dtype)
        kd_ref[rs, cs] = (k[i] * col_of(e_rest, j, h, 8)).astype(kd_ref.dtype)


def _gdn_pre(p, halo0, cw, alog, dtb, *, n_p, cpl, n_s):
    t = p.shape[0]
    rows = GDN_NB * CHUNK
    n_chunks = t // CHUNK
    n_seq = n_p + n_s
    bps = cpl // GDN_NB
    return pl.pallas_call(
        functools.partial(_gdn_pre_body, n_pb=n_p * bps, bps=bps, n_p=n_p),
        grid=(t // rows,),
        in_specs=[pl.BlockSpec((rows, GDN_P), lambda b: (b, 0)),
                  pl.BlockSpec((8, GDN_CONV_CH), lambda b: (jnp.maximum(b * (rows // 8) - 1, 0), 0)),
                  _resident((n_seq, 8, GDN_CONV_CH)),
                  _resident((GDN_CONV, GDN_CONV_CH)),
                  _resident((1, LANES)), _resident((1, LANES))],
        out_specs=[pl.BlockSpec((rows, D_MODEL), lambda b: (b, 0)),
                   pl.BlockSpec((GDN_NB, 2 * CHUNK, D_MODEL), lambda b: (b, 0, 0)),
                   pl.BlockSpec((rows, D_MODEL), lambda b: (b, 0)),
                   pl.BlockSpec((rows, GDN_HEADS * CHUNK), lambda b: (b, 0)),
                   pl.BlockSpec((GDN_NB, GDN_HEADS, LANES), lambda b: (b, 0, 0))],
        out_shape=[jax.ShapeDtypeStruct((t, D_MODEL), F32),
                   jax.ShapeDtypeStruct((n_chunks, 2 * CHUNK, D_MODEL), BF16),
                   jax.ShapeDtypeStruct((t, D_MODEL), BF16),
                   jax.ShapeDtypeStruct((t, GDN_HEADS * CHUNK), BF16),
                   jax.ShapeDtypeStruct((n_chunks, GDN_HEADS, LANES), F32)],
        scratch_shapes=[pltpu.VMEM((GDN_NB * (CHUNK + 8), GDN_CONV_CH), F32)],
        compiler_params=_params(), name="gdn_pre")(p, p, halo0, cw, alog, dtb)


def _gdn_scan_body(u_ref, wq_ref, kd_ref, attn_ref, egl_ref, z_ref, s0_ref, nw_ref,
                   o_ref, soutp_ref, souts_ref, s_scr, *, n_pb, bps):
    blk, is_prompt, first, last = _block_flags(n_pb, bps)

    @pl.when(first)
    def _():
        s_scr[...] = jnp.zeros_like(s_scr)

    heads = range(GDN_HEADS)
    cs = [slice(h * GDN_DK, (h + 1) * GDN_DK) for h in heads]
    s = [s_scr[h] for h in heads]
    for j in range(SCAN_NB):
        rs = slice(j * CHUNK, (j + 1) * CHUNK)
        s = [jnp.where(is_prompt, s[h], s0_ref[j, h]) for h in heads]
        ws_qs = [jnp.dot(wq_ref[j, :, cs[h]], s[h].astype(BF16), preferred_element_type=F32)
                 for h in heads]
        v_new = [(u_ref[rs, cs[h]] - ws_qs[h][0:CHUNK]).astype(BF16) for h in heads]
        o = [ws_qs[h][CHUNK:2 * CHUNK]
             + jnp.dot(attn_ref[rs, h * CHUNK:(h + 1) * CHUNK], v_new[h], preferred_element_type=F32)
             for h in heads]
        s = [s[h] * egl_ref[j, h:h + 1, :] + _mm_tn(kd_ref[rs, cs[h]], v_new[h]) for h in heads]
        for h in heads:
            o_ref[rs, cs[h]] = (_rms(o[h], nw_ref[...]) * _silu(z_ref[rs, cs[h]])).astype(o_ref.dtype)

        @pl.when(jnp.logical_not(is_prompt))
        def _(j=j, s=s):
            for h in heads:
                souts_ref[j, h] = s[h]

    for h in heads:
        s_scr[h] = s[h]

    @pl.when(last)
    def _():
        for h in heads:
            soutp_ref[0, h] = s[h]


def _gdn_scan(u, wq, kd, attn, egl, p, s0, nw, *, n_p, cpl, n_s):
    t = u.shape[0]
    rows = SCAN_NB * CHUNK
    bps = cpl // SCAN_NB
    n_pb = n_p * bps
    row_blk = lambda b: (b, 0)
    s_in, s_out, s_shapes = _state_specs(n_p, n_s, n_pb, bps, (GDN_HEADS, GDN_DK, GDN_DV))
    return pl.pallas_call(
        functools.partial(_gdn_scan_body, n_pb=n_pb, bps=bps),
        grid=(t // rows,),
        in_specs=[pl.BlockSpec((rows, D_MODEL), row_blk),
                  pl.BlockSpec((SCAN_NB, 2 * CHUNK, D_MODEL), lambda b: (b, 0, 0)),
                  pl.BlockSpec((rows, D_MODEL), row_blk),
                  pl.BlockSpec((rows, GDN_HEADS * CHUNK), row_blk),
                  pl.BlockSpec((SCAN_NB, GDN_HEADS, LANES), lambda b: (b, 0, 0)),
                  pl.BlockSpec((rows, D_MODEL), lambda b: (b, GDN_Z0 // D_MODEL)),
                  s_in, _resident((1, GDN_DV))],
        out_specs=[pl.BlockSpec((rows, D_MODEL), row_blk)] + s_out,
        out_shape=[jax.ShapeDtypeStruct((t, D_MODEL), BF16)] + s_shapes,
        scratch_shapes=[pltpu.VMEM((GDN_HEADS, GDN_DK, GDN_DV), F32)],
        compiler_params=_params(), name="gdn_scan")(u, wq, kd, attn, egl, p, s0, nw)


def _gla_exponent_matrix():
    i = np.arange(CHUNK)[:, None]
    t = np.arange(CHUNK)[None, :]
    blocks = [(t <= i)]
    for s in GLA_LEVELS:
        c = (i // (2 * s)) * (2 * s) + s - 1
        blocks.append(np.where(i > c, (t > c) & (t <= i), (t > i) & (t <= c)))
    w = np.concatenate(blocks, axis=0).astype(np.float32)
    return np.concatenate([w, w, w], axis=1)


def _split3(x):
    hi = x.astype(BF16)
    r1 = x - hi.astype(F32)
    mid = r1.astype(BF16)
    lo = (r1 - mid.astype(F32)).astype(BF16)
    return hi, mid, lo


def _gla_body(p_ref, s0_ref, wg2_ref, bg_ref, wexp_ref, nw_ref,
              o_ref, soutp_ref, souts_ref, st_scr, *, n_pb, bps):
    blk, is_prompt, first, last = _block_flags(n_pb, bps)

    @pl.when(first)
    def _():
        st_scr[...] = jnp.zeros_like(st_scr)

    row = lax.broadcasted_iota(jnp.int32, (CHUNK, CHUNK), 0)
    col = lax.broadcasted_iota(jnp.int32, (CHUNK, CHUNK), 1)
    eye = row == col
    masks = [_pair_mask(s) for s in GLA_LEVELS]
    heads = range(GLA_HEADS)
    ck = [slice(h * GLA_DK, (h + 1) * GLA_DK) for h in heads]
    cv = [slice(h * GLA_DV, (h + 1) * GLA_DV) for h in heads]

    prep = []
    for j in range(SCAN_NB):
        rs = slice(j * CHUNK, (j + 1) * CHUNK)
        x = _mm(p_ref[rs, GLA_G0:GLA_G0 + LANES], wg2_ref[...]) + bg_ref[...]
        glog = -_softplus(-x) * (1.0 / GLA_TAU)
        e_all = jnp.dot(wexp_ref[...], jnp.concatenate(_split3(glog), axis=0),
                        preferred_element_type=F32)
        q_all = p_ref[rs, 0:GLA_QK] * (GLA_DK ** -0.5)
        k_all = p_ref[rs, GLA_QK:2 * GLA_QK]
        qb, kb = q_all.astype(BF16), k_all.astype(BF16)
        attn = [jnp.where(eye, _mm_nt(qb[:, ck[h]], kb[:, ck[h]]), 0.0) for h in heads]
        for lvl, m in enumerate(masks):
            f = jnp.exp(e_all[(lvl + 1) * CHUNK:(lvl + 2) * CHUNK, :])
            qf, kf = (q_all * f).astype(BF16), (k_all * f).astype(BF16)
            attn = [attn[h] + jnp.where(m, _mm_nt(qf[:, ck[h]], kf[:, ck[h]]), 0.0) for h in heads]
        b = e_all[0:CHUNK, :]
        b_last = b[CHUNK - 1:CHUNK, :]
        vb = p_ref[rs, GLA_V0:GLA_V0 + D_MODEL].astype(BF16)
        prep.append(dict(qe=(q_all * jnp.exp(b)).astype(BF16),
                         kd=(k_all * jnp.exp(b_last - b)).astype(BF16),
                         eb_last=jnp.exp(b_last), vb=vb,
                         av=[_mm(attn[h], vb[:, cv[h]]) for h in heads]))

    st = [st_scr[h] for h in heads]
    for j, c in enumerate(prep):
        rs = slice(j * CHUNK, (j + 1) * CHUNK)
        st = [jnp.where(is_prompt, st[h], s0_ref[j, h]) for h in heads]
        o = [_mm_nt(c["qe"][:, ck[h]], st[h]) + c["av"][h] for h in heads]
        st = [st[h] * c["eb_last"][:, ck[h]] + _mm_tn(c["vb"][:, cv[h]], c["kd"][:, ck[h]])
              for h in heads]
        for h in heads:
            r = p_ref[rs, GLA_R0 + h * GLA_DV:GLA_R0 + (h + 1) * GLA_DV]
            o_ref[rs, cv[h]] = (_rms(o[h], nw_ref[h:h + 1, :]) * _silu(r)).astype(o_ref.dtype)

        @pl.when(jnp.logical_not(is_prompt))
        def _(j=j, st=st):
            for h in heads:
                souts_ref[j, h] = st[h]

    for h in heads:
        st_scr[h] = st[h]

    @pl.when(last)
    def _():
        for h in heads:
            soutp_ref[0, h] = st[h]


def _gla_scan(p, s0t, wg2, bg, wexp, nw, *, n_p, cpl, n_s):
    t = p.shape[0]
    rows = SCAN_NB * CHUNK
    bps = cpl // SCAN_NB
    n_pb = n_p * bps
    s_in, s_out, s_shapes = _state_specs(n_p, n_s, n_pb, bps, (GLA_HEADS, GLA_DV, GLA_DK))
    return pl.pallas_call(
        functools.partial(_gla_body, n_pb=n_pb, bps=bps),
        grid=(t // rows,),
        in_specs=[pl.BlockSpec((rows, GLA_P), lambda b: (b, 0)), s_in,
                  _resident((LANES, GLA_QK)), _resident((1, GLA_QK)),
                  _resident(((len(GLA_LEVELS) + 1) * CHUNK, 3 * CHUNK)),
                  _resident((GLA_HEADS, GLA_DV))],
        out_specs=[pl.BlockSpec((rows, D_MODEL), lambda b: (b, 0))] + s_out,
        out_shape=[jax.ShapeDtypeStruct((t, D_MODEL), BF16)] + s_shapes,
        scratch_shapes=[pltpu.VMEM((GLA_HEADS, GLA_DV, GLA_DK), F32)],
        compiler_params=_params(), name="gla_scan")(p, s0t, wg2, bg, wexp, nw)


CNV_NB = 4


def _cnv_body(x_ref, halo_ref, halo0_ref, wdw_ref, bdw_ref, o_ref, xc_scr, *, n_pb, bps, n_p):
    blk = pl.program_id(0)
    is_prompt = blk < n_pb
    win = CONV_HALO + CHUNK
    for j in range(CNV_NB):
        if j == 0:
            first = jnp.logical_or(jnp.logical_not(is_prompt), blk % bps == 0)
            prev = halo_ref[...]
        else:
            first = jnp.logical_not(is_prompt)
            prev = x_ref[j * CHUNK - CONV_HALO:j * CHUNK, :]
        seq = jnp.where(is_prompt, blk // bps, n_p + (blk - n_pb) * CNV_NB + j)
        xc_scr[j * win:j * win + CONV_HALO, :] = jnp.where(first, halo0_ref[seq], prev)
        xc_scr[j * win + CONV_HALO:(j + 1) * win, :] = x_ref[j * CHUNK:(j + 1) * CHUNK, :]
    taps = {}
    for w in range(CONV_W):
        s = CONV_HALO - (CONV_W - 1) + w
        taps.setdefault(-s % 8, []).append((w, s + (-s % 8)))
    for j in range(CNV_NB):
        for c0 in range(0, D_MODEL, LANES):
            cs = slice(c0, c0 + LANES)
            wv = xc_scr[j * win:(j + 1) * win, cs]
            y = None
            for r, group in sorted(taps.items()):
                rolled = wv if r == 0 else pltpu.roll(wv, r, 0)
                for w, a0 in group:
                    term = wdw_ref[w:w + 1, cs] * rolled[a0:a0 + CHUNK]
                    y = term if y is None else y + term
            o_ref[j * CHUNK:(j + 1) * CHUNK, cs] = y + bdw_ref[:, cs]


def _cnv_conv(glu, halo0, wdw, bdw, *, n_p, cpl, n_s):
    t = glu.shape[0]
    rows = CNV_NB * CHUNK
    bps = cpl // CNV_NB
    return pl.pallas_call(
        functools.partial(_cnv_body, n_pb=n_p * bps, bps=bps, n_p=n_p),
        grid=(t // rows,),
        in_specs=[pl.BlockSpec((rows, D_MODEL), lambda b: (b, 0)),
                  pl.BlockSpec((CONV_HALO, D_MODEL),
                               lambda b: (jnp.maximum(b * (rows // CONV_HALO) - 1, 0), 0)),
                  _resident((n_p + n_s, CONV_HALO, D_MODEL)),
                  _resident((CONV_HALO, D_MODEL)), _resident((1, D_MODEL))],
        out_specs=pl.BlockSpec((rows, D_MODEL), lambda b: (b, 0)),
        out_shape=jax.ShapeDtypeStruct((t, D_MODEL), F32),
        scratch_shapes=[pltpu.VMEM((CNV_NB * (CONV_HALO + CHUNK), D_MODEL), F32)],
        compiler_params=_params(), name="cnv_conv")(glu, glu, halo0, wdw, bdw)


def _pad_cols(w, n):
    return jnp.pad(w, ((0, 0), (0, n - w.shape[1])))


def _row(v):
    return v.reshape(1, -1).astype(F32)


def _tail_rows(a, lay, n):
    n_p, cpl, n_s = lay
    ends = np.concatenate([(np.arange(n_p) + 1) * cpl * CHUNK,
                           n_p * cpl * CHUNK + (np.arange(n_s) + 1) * CHUNK])
    idx = (ends[:, None] - n + np.arange(n)[None, :]).reshape(-1)
    return jnp.take(a, jnp.asarray(idx, jnp.int32), axis=0).reshape(n_p + n_s, n, a.shape[1])


def _gdn_mix(x, lay, dims, mix_nw, cache, state, w_in, conv_w, a_log, dt_bias, norm_w):
    n_p, cpl, n_s = lay
    p = _proj(x, _row(mix_nw), _pad_cols(w_in, GDN_P).astype(BF16), None,
              n_out=GDN_P, ck=384, name="gdn_in", **dims)
    halo0 = jnp.concatenate([jnp.zeros((n_p, 8, GDN_CONV_CH), F32),
                             jnp.pad(cache, ((0, 0), (8 - (GDN_CONV - 1), 0), (0, 0)))], axis=0)
    lane_pad = lambda v: jnp.pad(v.reshape(1, -1), ((0, 0), (GDN_HEADS, LANES - 2 * GDN_HEADS)))
    u, wq, kd, attn, egl = _gdn_pre(p, halo0, conv_w, lane_pad(a_log), lane_pad(dt_bias),
                                    n_p=n_p, cpl=cpl, n_s=n_s)
    o, s_p, s_s = _gdn_scan(u, wq, kd, attn, egl, p, state, _row(norm_w), n_p=n_p, cpl=cpl, n_s=n_s)
    conv_rows = _tail_rows(p, lay, GDN_CONV - 1)[:, :, :GDN_CONV_CH]
    return o, (conv_rows[:n_p], conv_rows[n_p:], s_p, s_s)


def _gla_mix(x, lay, dims, mix_nw, state, w_in, w_g2, b_g, norm_w):
    n_p, cpl, n_s = lay
    p = _proj(x, _row(mix_nw), _pad_cols(w_in, GLA_P).astype(BF16), None,
              n_out=GLA_P, ck=640, name="gla_in", **dims)
    wg2 = jnp.pad(w_g2, ((0, LANES - GLA_RANK), (0, 0))).astype(BF16)
    wexp = jnp.asarray(_gla_exponent_matrix(), BF16)
    o, st_p, st_s = _gla_scan(p, jnp.swapaxes(state, 2, 3), wg2, _row(b_g), wexp, norm_w,
                              n_p=n_p, cpl=cpl, n_s=n_s)
    return o, (jnp.swapaxes(st_p, 2, 3), jnp.swapaxes(st_s, 2, 3))


def _cnv_mix(x, lay, dims, mix_nw, cache, w_pw1, b_pw1, w_dw, b_dw):
    n_p, cpl, n_s = lay
    glu = _proj(x, _row(mix_nw), w_pw1.astype(BF16), _row(b_pw1),
                n_out=D_MODEL, ck=512, epilogue="glu", name="cnv_in", **dims)
    pad_rows = CONV_HALO - (CONV_W - 1)
    halo0 = jnp.concatenate([jnp.zeros((n_p, CONV_HALO, D_MODEL), F32),
                             jnp.pad(cache, ((0, 0), (pad_rows, 0), (0, 0)))], axis=0)
    wdw = jnp.pad(w_dw, ((0, CONV_HALO - CONV_W), (0, 0)))
    y = _cnv_conv(glu, halo0, wdw, _row(b_dw), n_p=n_p, cpl=cpl, n_s=n_s)
    rows = _tail_rows(glu, lay, CONV_W - 1)
    return y, (rows[:n_p], rows[n_p:])


def kernel(x_prompt, x_sample, cache_gdn_conv, state_gdn, state_gla, cache_conformer, mix_norm_w, ffn_norm_w, ffn_w1, ffn_w2, final_norm_w, gdn_w_in, gdn_conv_w, gdn_a_log, gdn_dt_bias, gdn_norm_w, gdn_w_out, gla_w_in, gla_w_g2, gla_b_g, gla_norm_w, gla_w_out, gmlp_w_in, gmlp_b_in, gmlp_ln_w, gmlp_ln_b, gmlp_w_s, gmlp_b_s, gmlp_w_out, gmlp_b_out, cnv_w_pw1, cnv_b_pw1, cnv_w_dw, cnv_b_dw, cnv_ln_w, cnv_ln_b, cnv_w_pw2, cnv_b_pw2):
    n_p, l_p, d = x_prompt.shape
    n_s, l_s, _ = x_sample.shape
    assert d == D_MODEL and l_s == CHUNK and l_p % (CNV_NB * CHUNK) == 0
    assert n_s % CNV_NB == 0
    t_p, t_s = n_p * l_p, n_s * l_s
    assert t_p % ROW_TILE == 0 and t_s % ROW_TILE == 0
    lay = (n_p, l_p // CHUNK, n_s)
    dims = dict(t=t_p + t_s, n_pt=t_p // ROW_TILE)
    depth = mix_norm_w.shape[0]

    x = (x_prompt.reshape(t_p, d), x_sample.reshape(t_s, d))
    outs = {k: [] for k in ("gdn_cp", "gdn_cs", "gdn_sp", "gdn_ss", "gla_sp", "gla_ss",
                            "gmlp_vs", "cnv_p", "cnv_s")}
    for i in range(depth):
        kind, j = i % 4, i // 4
        ffn = (_row(ffn_norm_w[i]), ffn_w1[i].astype(BF16), ffn_w2[i].astype(BF16),
               _row(final_norm_w) if i == depth - 1 else None)
        if kind == 0:
            o, (cp, cs, sp, ss) = _gdn_mix(x, lay, dims, mix_norm_w[i], cache_gdn_conv[j], state_gdn[j],
                                           gdn_w_in[j], gdn_conv_w[j], gdn_a_log[j], gdn_dt_bias[j],
                                           gdn_norm_w[j])
            outs["gdn_cp"].append(cp); outs["gdn_cs"].append(cs)
            outs["gdn_sp"].append(sp); outs["gdn_ss"].append(ss)
            res = _post([o], x, gdn_w_out[j].astype(BF16), None, *ffn,
                        prologue="plain", name=f"post_{i}", **dims)
        elif kind == 1:
            o, (sp, ss) = _gla_mix(x, lay, dims, mix_norm_w[i], state_gla[j], gla_w_in[j], gla_w_g2[j],
                                   gla_b_g[j], gla_norm_w[j])
            outs["gla_sp"].append(sp); outs["gla_ss"].append(ss)
            res = _post([o], x, gla_w_out[j].astype(BF16), None, *ffn,
                        prologue="plain", name=f"post_{i}", **dims)
        elif kind == 2:
            zz = _proj(x, _row(mix_norm_w[i]), gmlp_w_in[j].astype(BF16), _row(gmlp_b_in[j]),
                       n_out=2 * D_MODEL, ck=512, epilogue="gelu", name="gmlp_in", **dims)
            bs_full = jnp.repeat(gmlp_b_s[j].T, GMLP_GC, axis=1)
            res = _post([zz, _row(gmlp_ln_w[j]), _row(gmlp_ln_b[j]), gmlp_w_s[j], bs_full], x,
                        gmlp_w_out[j].astype(BF16), _row(gmlp_b_out[j]), *ffn,
                        prologue="gmlp", name=f"post_{i}", **dims)
            outs["gmlp_vs"].append(res[-1].reshape(n_s, l_s, d))
            res = res[:-1]
        else:
            y, (cp, cs) = _cnv_mix(x, lay, dims, mix_norm_w[i], cache_conformer[j], cnv_w_pw1[j],
                                   cnv_b_pw1[j], cnv_w_dw[j], cnv_b_dw[j])
            outs["cnv_p"].append(cp); outs["cnv_s"].append(cs)
            res = _post([y, _row(cnv_ln_w[j]), _row(cnv_ln_b[j])], x, cnv_w_pw2[j].astype(BF16),
                        _row(cnv_b_pw2[j]), *ffn, prologue="cnv", name=f"post_{i}", **dims)
        x = tuple(res) if i == depth - 1 else res[0]
    y_p, y_s = x
    return (y_p.reshape(n_p, l_p, d), y_s.reshape(n_s, l_s, d),
            jnp.stack(outs["gdn_cp"]), jnp.stack(outs["gdn_cs"]),
            jnp.stack(outs["gdn_sp"]), jnp.stack(outs["gdn_ss"]),
            jnp.stack(outs["gla_sp"]), jnp.stack(outs["gla_ss"]),
            jnp.stack(outs["gmlp_vs"]),
            jnp.stack(outs["cnv_p"]), jnp.stack(outs["cnv_s"]))
```

```python
import functools
import math

import numpy as np
import jax
import jax.numpy as jnp
from jax import lax
from jax.experimental import pallas as pl
from jax.experimental.pallas import tpu as pltpu

F32 = jnp.float32
BF16 = jnp.bfloat16

D_MODEL = 1024
D_FF = 4 * D_MODEL
EPS = 1e-6
CHUNK = 64
ROW_TILE = 512
SCAN_NB = 2
LANES = 128
VMEM_LIMIT = 56 * 1024 * 1024

GDN_HEADS, GDN_DK, GDN_DV, GDN_CONV = 8, 128, 128, 4
GDN_QK = GDN_HEADS * GDN_DK
GDN_CONV_CH = 2 * GDN_QK + GDN_HEADS * GDN_DV
GDN_Z0 = GDN_CONV_CH
GDN_BG0 = GDN_CONV_CH + GDN_HEADS * GDN_DV
GDN_P = GDN_BG0 + LANES

GLA_HEADS, GLA_DK, GLA_DV, GLA_RANK, GLA_TAU = 4, 128, 256, 16, 16.0
GLA_QK = GLA_HEADS * GLA_DK
GLA_V0, GLA_R0, GLA_G0 = 2 * GLA_QK, 2 * GLA_QK + D_MODEL, 2 * GLA_QK + 2 * D_MODEL
GLA_P = GLA_G0 + LANES
GLA_LEVELS = (32, 16, 8, 4, 2, 1)

GMLP_BLOCK, GMLP_GROUPS, GMLP_GC = 128, 4, 256
CONV_W = 31
CONV_HALO = 32


def _mm(a, b):
    return jnp.dot(a.astype(BF16), b.astype(BF16), preferred_element_type=F32)


def _mm_nt(a, b):
    return lax.dot_general(a.astype(BF16), b.astype(BF16), (((1,), (1,)), ((), ())),
                           preferred_element_type=F32)


def _mm_tn(a, b):
    return lax.dot_general(a.astype(BF16), b.astype(BF16), (((0,), (0,)), ((), ())),
                           preferred_element_type=F32)


def _rms(x, w):
    return x * lax.rsqrt(jnp.mean(x * x, axis=-1, keepdims=True) + EPS) * w


def _layernorm(x, w, b):
    xc = x - jnp.mean(x, axis=-1, keepdims=True)
    return xc * lax.rsqrt(jnp.mean(xc * xc, axis=-1, keepdims=True) + EPS) * w + b


def _sigmoid(x):
    return jax.nn.sigmoid(x)


def _silu(x):
    return x * _sigmoid(x)


def _softplus(x):
    return jnp.maximum(x, 0.0) + jnp.log1p(jnp.exp(-jnp.abs(x)))


def _gelu_tanh(x):
    return 0.5 * x * (1.0 + jnp.tanh(math.sqrt(2.0 / math.pi) * (x + 0.044715 * (x * x * x))))


def _cumsum_rows(x):
    row = lax.broadcasted_iota(jnp.int32, x.shape, 0)
    s = 1
    while s < x.shape[0]:
        x = x + jnp.where(row >= s, pltpu.roll(x, s, 0), 0.0)
        s *= 2
    return x


def _pair_mask(s):
    sh = int(math.log2(s))
    bi = lax.broadcasted_iota(jnp.int32, (CHUNK, CHUNK), 0) >> sh
    bj = lax.broadcasted_iota(jnp.int32, (CHUNK, CHUNK), 1) >> sh
    return jnp.logical_and((bi & 1) == 1, bj == bi - 1)


def _block_flags(n_pb, bps):
    blk = pl.program_id(0)
    is_prompt = blk < n_pb
    first = jnp.logical_and(is_prompt, blk % bps == 0)
    last = jnp.logical_and(is_prompt, blk % bps == bps - 1)
    return blk, is_prompt, first, last


def _state_specs(n_p, n_s, n_pb, bps, shape):
    zeros = (0,) * len(shape)
    sample = pl.BlockSpec((SCAN_NB,) + shape, lambda b: (jnp.maximum(b - n_pb, 0),) + zeros)
    prompt = pl.BlockSpec((1,) + shape, lambda b: (jnp.minimum(b // bps, n_p - 1),) + zeros)
    shapes = [jax.ShapeDtypeStruct((n_p,) + shape, F32), jax.ShapeDtypeStruct((n_s,) + shape, F32)]
    return sample, [prompt, sample], shapes


def _params(n_axes=1):
    return pltpu.CompilerParams(dimension_semantics=("arbitrary",) * n_axes,
                                vmem_limit_bytes=VMEM_LIMIT)


def _resident(shape):
    zeros = (0,) * len(shape)
    return pl.BlockSpec(shape, lambda i: zeros, pipeline_mode=pl.Buffered(1))


def _row_tile_specs(x, n_pt):
    if isinstance(x, tuple):
        return ([pl.BlockSpec((ROW_TILE, D_MODEL), lambda i: (jnp.minimum(i, n_pt - 1), 0)),
                 pl.BlockSpec((ROW_TILE, D_MODEL), lambda i: (jnp.maximum(i - n_pt, 0), 0))], list(x))
    return [pl.BlockSpec((ROW_TILE, D_MODEL), lambda i: (i, 0))], [x]


def _read_row_tile(refs, n_pt):
    if len(refs) == 2:
        return jnp.where(pl.program_id(0) < n_pt, refs[0][...], refs[1][...])
    return refs[0][...]


def _proj_body(*refs, n_x, n_pt, n_out, ck, epilogue, has_bias):
    x_refs, (nw_ref, w_ref), rest = refs[:n_x], refs[n_x:n_x + 2], refs[n_x + 2:]
    if has_bias:
        b_ref, o_ref = rest
    else:
        (o_ref,) = rest
    xn = _rms(_read_row_tile(x_refs, n_pt), nw_ref[...]).astype(BF16)
    for c0 in range(0, n_out, ck):
        c = c0 // ck
        sl = slice(c0, min(c0 + ck, n_out))
        y = jnp.dot(xn, w_ref[:, sl], preferred_element_type=F32)
        if has_bias:
            y = y + b_ref[:, sl]
        if epilogue == "gelu":
            y = _gelu_tanh(y)
        elif epilogue == "glu":
            sg = slice(n_out + c * ck, n_out + (c + 1) * ck)
            gate = jnp.dot(xn, w_ref[:, sg], preferred_element_type=F32) + b_ref[:, sg]
            y = y * _sigmoid(gate)
        o_ref[:, sl] = y.astype(o_ref.dtype)


def _proj(x, nw, w, b, *, t, n_pt, n_out, ck, epilogue="none", out_dtype=F32, name):
    n_w = w.shape[1]
    has_bias = b is not None
    in_specs, args = _row_tile_specs(x, n_pt)
    n_x = len(args)
    in_specs += [_resident((1, D_MODEL)), _resident((D_MODEL, n_w))]
    args += [nw, w]
    if has_bias:
        in_specs.append(_resident((1, n_w)))
        args.append(b)
    return pl.pallas_call(
        functools.partial(_proj_body, n_x=n_x, n_pt=n_pt, n_out=n_out, ck=ck, epilogue=epilogue,
                          has_bias=has_bias),
        grid=(t // ROW_TILE,),
        in_specs=in_specs,
        out_specs=pl.BlockSpec((ROW_TILE, n_out), lambda i: (i, 0)),
        out_shape=jax.ShapeDtypeStruct((t, n_out), out_dtype),
        compiler_params=_params(), name=name)(*args)


FFN_CK = 512


def _gmlp_gate(zz_ref, lnw_ref, lnb_ref, ws_ref, bs_ref, vs_ref, a_scr, is_sample):
    v = _layernorm(zz_ref[:, D_MODEL:2 * D_MODEL], lnw_ref[...], lnb_ref[...])

    @pl.when(is_sample)
    def _():
        vs_ref[...] = v

    vb = v.astype(BF16)
    row = lax.broadcasted_iota(jnp.int32, (GMLP_BLOCK, GMLP_BLOCK), 0)
    col = lax.broadcasted_iota(jnp.int32, (GMLP_BLOCK, GMLP_BLOCK), 1)
    tril = col <= row
    top_left = jnp.logical_and(row < CHUNK, col < CHUNK)
    brow = lax.broadcasted_iota(jnp.int32, (GMLP_BLOCK, D_MODEL), 0)
    bias = bs_ref[...]
    bias = jnp.where(jnp.logical_and(is_sample, brow >= CHUNK), pltpu.roll(bias, CHUNK, 0), bias)
    for grp in range(GMLP_GROUPS):
        cs = slice(grp * GMLP_GC, (grp + 1) * GMLP_GC)
        wt = jnp.where(tril, ws_ref[grp], 0.0)
        w11 = jnp.where(top_left, wt, 0.0)
        wd = w11 + pltpu.roll(pltpu.roll(w11, CHUNK, 0), CHUNK, 1)
        we = jnp.where(is_sample, wd, wt).astype(BF16)
        for blk in range(ROW_TILE // GMLP_BLOCK):
            rs = slice(blk * GMLP_BLOCK, (blk + 1) * GMLP_BLOCK)
            mixed = jnp.dot(we, vb[rs, cs], preferred_element_type=F32)
            a_scr[rs, cs] = (zz_ref[rs, cs] * (mixed + bias[:, cs])).astype(BF16)
    return a_scr[...]


def _post_body(*refs, prologue, n_res, has_bias, final, n_pt):
    n_in = {"plain": 1, "cnv": 3, "gmlp": 5}[prologue]
    pro, refs = refs[:n_in], refs[n_in:]
    res_refs, refs = refs[:n_res], refs[n_res:]
    wo_ref, refs = refs[0], refs[1:]
    if has_bias:
        bo_ref, refs = refs[0], refs[1:]
    fnw_ref, w1_ref, w2_ref, refs = refs[0], refs[1], refs[2], refs[3:]
    if final:
        finw_ref, refs = refs[0], refs[1:]
    n_main = 2 if final else 1
    outs, refs = refs[:n_main], refs[n_main:]
    is_sample = pl.program_id(0) >= n_pt

    if prologue == "plain":
        a = pro[0][...]
    elif prologue == "cnv":
        y_ref, lnw_ref, lnb_ref = pro
        a = _silu(_layernorm(y_ref[...], lnw_ref[...], lnb_ref[...])).astype(BF16)
    else:
        vs_ref, a_scr = refs
        a = _gmlp_gate(*pro, vs_ref, a_scr, is_sample)

    x = _read_row_tile(res_refs, n_pt) + jnp.dot(a, wo_ref[...], preferred_element_type=F32)
    if has_bias:
        x = x + bo_ref[...]
    xn = _rms(x, fnw_ref[...]).astype(BF16)
    for c in range(D_FF // FFN_CK):
        sl = slice(c * FFN_CK, (c + 1) * FFN_CK)
        h = jnp.dot(xn, w1_ref[:, sl], preferred_element_type=F32)
        h = jnp.square(jnp.maximum(h, 0.0)).astype(BF16)
        x = x + jnp.dot(h, w2_ref[sl, :], preferred_element_type=F32)
    if final:
        y = _rms(x, finw_ref[...])

        @pl.when(jnp.logical_not(is_sample))
        def _():
            outs[0][...] = y

        @pl.when(is_sample)
        def _():
            outs[1][...] = y
    else:
        outs[0][...] = x


def _post(pro_args, res, wo, bo, fnw, w1, w2, finw, *, layer, prologue, t, n_pt, name):
    has_bias = bo is not None
    final = finw is not None
    tile = lambda c: pl.BlockSpec((ROW_TILE, c), lambda i: (i, 0))
    if prologue == "plain":
        in_specs = [tile(D_MODEL)]
    elif prologue == "cnv":
        in_specs = [tile(D_MODEL), _resident((1, D_MODEL)), _resident((1, D_MODEL))]
    else:
        in_specs = [tile(2 * D_MODEL), _resident((1, D_MODEL)), _resident((1, D_MODEL)),
                    _resident((GMLP_GROUPS, GMLP_BLOCK, GMLP_BLOCK)),
                    _resident((GMLP_BLOCK, D_MODEL))]
    args = list(pro_args)
    res_specs, res_args = _row_tile_specs(res, n_pt)
    in_specs += res_specs + [_resident((D_MODEL, D_MODEL))]
    args += res_args + [wo]
    if has_bias:
        in_specs.append(_resident((1, D_MODEL)))
        args.append(bo)
    layer_block = lambda shape: pl.BlockSpec((None,) + shape, lambda i: (layer, 0, 0),
                                             pipeline_mode=pl.Buffered(1))
    in_specs += [_resident((1, D_MODEL)), layer_block((D_MODEL, D_FF)), layer_block((D_FF, D_MODEL))]
    args += [fnw, w1, w2]
    if final:
        in_specs.append(_resident((1, D_MODEL)))
        args.append(finw)
        out_specs = [pl.BlockSpec((ROW_TILE, D_MODEL), lambda i: (jnp.minimum(i, n_pt - 1), 0)),
                     pl.BlockSpec((ROW_TILE, D_MODEL), lambda i: (jnp.maximum(i - n_pt, 0), 0))]
        out_shape = [jax.ShapeDtypeStruct((n_pt * ROW_TILE, D_MODEL), F32),
                     jax.ShapeDtypeStruct((t - n_pt * ROW_TILE, D_MODEL), F32)]
    else:
        out_specs = [tile(D_MODEL)]
        out_shape = [jax.ShapeDtypeStruct((t, D_MODEL), F32)]
    scratch = []
    if prologue == "gmlp":
        out_specs.append(pl.BlockSpec((ROW_TILE, D_MODEL), lambda i: (jnp.maximum(i - n_pt, 0), 0)))
        out_shape.append(jax.ShapeDtypeStruct((t - n_pt * ROW_TILE, D_MODEL), F32))
        scratch.append(pltpu.VMEM((ROW_TILE, D_MODEL), BF16))
    return pl.pallas_call(
        functools.partial(_post_body, prologue=prologue, n_res=len(res_args), has_bias=has_bias,
                          final=final, n_pt=n_pt),
        grid=(t // ROW_TILE,),
        in_specs=in_specs, out_specs=out_specs, out_shape=out_shape, scratch_shapes=scratch,
        compiler_params=_params(), name=name)(*args)


GDN_NB = 2
GDN_WIN = CHUNK + 8
GDN_GROUPS = ROW_TILE // CHUNK


def _gdn_gates(bg, alog_ref, dtb_ref):
    glog = -jnp.exp(alog_ref[...]) * _softplus(bg + dtb_ref[...])
    return _sigmoid(bg), _cumsum_rows(glog)


def _gdn_in_body(*refs, n_x, n_pt, tps, n_p):
    x_refs = refs[:n_x]
    (nw_ref, w_ref, cw_ref, alog_ref, dtb_ref, halo0_ref,
     k_ref, kbq_ref, vk_ref, qe_ref, kd_ref, gate_ref, bg_ref, tails_ref,
     xc_scr, carry_scr) = refs[n_x:]
    i = pl.program_id(0)
    is_sample = i >= n_pt

    @pl.when(i == 0)
    def _():
        carry_scr[...] = jnp.zeros_like(carry_scr)

    xn = _rms(_read_row_tile(x_refs, n_pt), nw_ref[...]).astype(BF16)
    bg = jnp.dot(xn, w_ref[:, GDN_BG0:GDN_BG0 + LANES], preferred_element_type=F32)
    bg_ref[...] = bg
    beta, egc, e_rest = [], [], []
    for j in range(GDN_GROUPS):
        b_j, gc = _gdn_gates(bg[j * CHUNK:(j + 1) * CHUNK], alog_ref, dtb_ref)
        beta.append(b_j)
        egc.append(jnp.exp(gc))
        e_rest.append(jnp.exp(gc[CHUNK - 1:CHUNK, :] - gc))

    def project(c0, n):
        cols = slice(c0, c0 + n)
        y = jnp.dot(xn, w_ref[:, cols], preferred_element_type=F32)
        for j in range(GDN_GROUPS):
            r0 = j * GDN_WIN
            xc_scr[r0 + 8:r0 + GDN_WIN, cols] = y[j * CHUNK:(j + 1) * CHUNK]
            if j == 0:
                first = jnp.logical_or(is_sample, i % tps == 0)
                prev = carry_scr[:, cols]
            else:
                first = is_sample
                prev = xc_scr[r0 - 8:r0, cols]
            seq = jnp.where(is_sample, n_p + (i - n_pt) * GDN_GROUPS + j, i // tps)
            xc_scr[r0:r0 + 8, cols] = jnp.where(first, halo0_ref[seq, :, cols], prev)

    def conv_silu(j, c0):
        wv = xc_scr[j * GDN_WIN:(j + 1) * GDN_WIN, c0:c0 + LANES]
        y = cw_ref[GDN_CONV - 1:GDN_CONV, c0:c0 + LANES] * wv[8:GDN_WIN]
        for d in range(1, GDN_CONV):
            w = GDN_CONV - 1 - d
            y = y + cw_ref[w:w + 1, c0:c0 + LANES] * pltpu.roll(wv, d, 0)[8:GDN_WIN]
        return _silu(y)

    def epilogue(h):
        cs = slice(h * GDN_DK, (h + 1) * GDN_DK)
        for j in range(GDN_GROUPS):
            rs = slice(j * CHUNK, (j + 1) * CHUNK)
            q = conv_silu(j, h * GDN_DK)
            k = conv_silu(j, GDN_QK + h * GDN_DK)
            v = conv_silu(j, 2 * GDN_QK + h * GDN_DV)
            q = q * lax.rsqrt(jnp.sum(q * q, axis=-1, keepdims=True) + EPS) * (GDN_DK ** -0.5)
            k = k * lax.rsqrt(jnp.sum(k * k, axis=-1, keepdims=True) + EPS)
            beta_h = beta[j][:, h:h + 1]
            e_h = egc[j][:, 8 + h:9 + h]
            kb = k * beta_h
            k_ref[rs, cs] = k.astype(BF16)
            kbq_ref[j, 0:CHUNK, cs] = kb.astype(BF16)
            kbq_ref[j, CHUNK:2 * CHUNK, cs] = q.astype(BF16)
            vk_ref[rs, 2 * h * GDN_DV:(2 * h + 1) * GDN_DV] = (v * beta_h).astype(BF16)
            vk_ref[rs, (2 * h + 1) * GDN_DV:(2 * h + 2) * GDN_DV] = (kb * e_h).astype(BF16)
            qe_ref[rs, cs] = (q * e_h).astype(BF16)
            kd_ref[rs, cs] = (k * e_rest[j][:, 8 + h:9 + h]).astype(BF16)

    pair = 2 * GDN_DK
    n_pairs = GDN_HEADS // 2
    for m in range(n_pairs + 1):
        if m < n_pairs:
            for part in range(3):
                project(part * GDN_QK + m * pair, pair)
        if m > 0:
            epilogue(2 * m - 2)
            epilogue(2 * m - 1)
    for c in range(2):
        cols = slice(c * FFN_CK, (c + 1) * FFN_CK)
        z = jnp.dot(xn, w_ref[:, GDN_Z0 + c * FFN_CK:GDN_Z0 + (c + 1) * FFN_CK],
                    preferred_element_type=F32)
        gate_ref[:, cols] = _silu(z).astype(BF16)
    for j in range(GDN_GROUPS):
        tails_ref[j] = xc_scr[(j + 1) * GDN_WIN - 8:(j + 1) * GDN_WIN, :]
    carry_scr[...] = xc_scr[GDN_GROUPS * GDN_WIN - 8:GDN_GROUPS * GDN_WIN, :]


def _gdn_in(x, nw, w, cw, alog, dtb, halo0, *, t, n_pt, tps, n_p):
    in_specs, args = _row_tile_specs(x, n_pt)
    n_x = len(args)
    n_seq = halo0.shape[0]
    in_specs += [_resident((1, D_MODEL)), _resident((D_MODEL, GDN_P)),
                 _resident((GDN_CONV, GDN_CONV_CH)), _resident((1, LANES)), _resident((1, LANES)),
                 _resident((n_seq, 8, GDN_CONV_CH))]
    args += [nw, w, cw, alog, dtb, halo0]
    tile = lambda c: pl.BlockSpec((ROW_TILE, c), lambda i: (i, 0))
    n_chunks = t // CHUNK
    bf = lambda c: jax.ShapeDtypeStruct((t, c), BF16)
    return pl.pallas_call(
        functools.partial(_gdn_in_body, n_x=n_x, n_pt=n_pt, tps=tps, n_p=n_p),
        grid=(t // ROW_TILE,),
        in_specs=in_specs,
        out_specs=[tile(D_MODEL),
                   pl.BlockSpec((GDN_GROUPS, 2 * CHUNK, D_MODEL), lambda i: (i, 0, 0)),
                   tile(2 * D_MODEL), tile(D_MODEL), tile(D_MODEL), tile(D_MODEL), tile(LANES),
                   pl.BlockSpec((GDN_GROUPS, 8, GDN_CONV_CH), lambda i: (i, 0, 0))],
        out_shape=[bf(D_MODEL), jax.ShapeDtypeStruct((n_chunks, 2 * CHUNK, D_MODEL), BF16),
                   bf(2 * D_MODEL), bf(D_MODEL), bf(D_MODEL), bf(D_MODEL),
                   jax.ShapeDtypeStruct((t, LANES), F32),
                   jax.ShapeDtypeStruct((n_chunks, 8, GDN_CONV_CH), F32)],
        scratch_shapes=[pltpu.VMEM((GDN_GROUPS * GDN_WIN, GDN_CONV_CH), F32),
                        pltpu.VMEM((8, GDN_CONV_CH), F32)],
        compiler_params=_params(), name="gdn_in")(*args)


def _gdn_pre_body(k_ref, kbq_ref, vk_ref, qe_ref, bg_ref, alog_ref, dtb_ref,
                  u_ref, wq_ref, attn_ref, egl_ref):
    row = lax.broadcasted_iota(jnp.int32, (CHUNK, CHUNK), 0)
    col = lax.broadcasted_iota(jnp.int32, (CHUNK, CHUNK), 1)
    incl = row >= col
    strict = row > col
    eye = (row == col).astype(F32)
    masks = [_pair_mask(s) for s in (1, 2, 4, 8, 16, 32)]

    probs = [(j, h) for j in range(GDN_NB) for h in range(GDN_HEADS)]
    n = len(probs)
    rs = [slice(j * CHUNK, (j + 1) * CHUNK) for j, _ in probs]
    cs = [slice(h * GDN_DK, (h + 1) * GDN_DK) for _, h in probs]
    gcum, gcum_t = [], []
    for j in range(GDN_NB):
        _, gc = _gdn_gates(bg_ref[j * CHUNK:(j + 1) * CHUNK, :], alog_ref, dtb_ref)
        gcum.append(gc)
        gt = jnp.transpose(jnp.concatenate([gc, jnp.zeros_like(gc)], axis=0))
        gcum_t.append(gt)
        egl_ref[j] = jnp.broadcast_to(jnp.exp(gt[8:8 + GDN_HEADS, CHUNK - 1:CHUNK]),
                                      (GDN_HEADS, LANES))
    decay = [jnp.exp(jnp.minimum(gcum[j][:, 8 + h:9 + h] - gcum_t[j][8 + h:9 + h, 0:CHUNK], 0.0))
             for j, h in probs]
    qk = [lax.dot_general(kbq_ref[j, :, cs[i]], k_ref[rs[i], cs[i]], (((1,), (1,)), ((), ())),
                          preferred_element_type=F32) for i, (j, h) in enumerate(probs)]
    a = [jnp.where(strict, qk[i][0:CHUNK] * decay[i], 0.0) for i in range(n)]
    for i, (j, h) in enumerate(probs):
        attn = jnp.where(incl, qk[i][CHUNK:2 * CHUNK] * decay[i], 0.0)
        attn_ref[rs[i], h * CHUNK:(h + 1) * CHUNK] = attn.astype(attn_ref.dtype)
    t = [eye - jnp.where(masks[0], ai, 0.0) for ai in a]
    a_b = [ai.astype(BF16) for ai in a]
    for m in masks[1:]:
        t_b = [ti.astype(BF16) for ti in t]
        at = [jnp.dot(jnp.where(m, a_b[i], jnp.zeros_like(a_b[i])), t_b[i],
                      preferred_element_type=F32) for i in range(n)]
        t = [t[i] - jnp.dot(t_b[i], at[i].astype(BF16), preferred_element_type=F32)
             for i in range(n)]
    for i, (j, h) in enumerate(probs):
        uw = jnp.dot(t[i].astype(BF16), vk_ref[rs[i], 2 * h * GDN_DV:(2 * h + 2) * GDN_DV],
                     preferred_element_type=F32)
        u_ref[rs[i], cs[i]] = uw[:, 0:GDN_DV]
        wq_ref[j, 0:CHUNK, cs[i]] = uw[:, GDN_DV:].astype(wq_ref.dtype)
        wq_ref[j, CHUNK:2 * CHUNK, cs[i]] = qe_ref[rs[i], cs[i]]


def _gdn_pre(k, kbq, vk, qe, bgp, alog, dtb):
    t = k.shape[0]
    rows = GDN_NB * CHUNK
    n_chunks = t // CHUNK
    tile = lambda c: pl.BlockSpec((rows, c), lambda b: (b, 0))
    chunked = lambda r, c: pl.BlockSpec((GDN_NB, r, c), lambda b: (b, 0, 0))
    return pl.pallas_call(
        _gdn_pre_body,
        grid=(t // rows,),
        in_specs=[tile(D_MODEL), chunked(2 * CHUNK, D_MODEL), tile(2 * D_MODEL), tile(D_MODEL),
                  tile(LANES), _resident((1, LANES)), _resident((1, LANES))],
        out_specs=[tile(D_MODEL), chunked(2 * CHUNK, D_MODEL), tile(GDN_HEADS * CHUNK),
                   chunked(GDN_HEADS, LANES)],
        out_shape=[jax.ShapeDtypeStruct((t, D_MODEL), F32),
                   jax.ShapeDtypeStruct((n_chunks, 2 * CHUNK, D_MODEL), BF16),
                   jax.ShapeDtypeStruct((t, GDN_HEADS * CHUNK), BF16),
                   jax.ShapeDtypeStruct((n_chunks, GDN_HEADS, LANES), F32)],
        compiler_params=_params(), name="gdn_pre")(k, kbq, vk, qe, bgp, alog, dtb)


def _gdn_scan_body(u_ref, wq_ref, kd_ref, attn_ref, egl_ref, gate_ref, s0_ref, nw_ref,
                   o_ref, soutp_ref, souts_ref, s_scr, *, n_pb, bps):
    blk, is_prompt, first, last = _block_flags(n_pb, bps)

    @pl.when(first)
    def _():
        s_scr[...] = jnp.zeros_like(s_scr)

    heads = range(GDN_HEADS)
    cs = [slice(h * GDN_DK, (h + 1) * GDN_DK) for h in heads]
    s = [s_scr[h] for h in heads]
    for j in range(SCAN_NB):
        rs = slice(j * CHUNK, (j + 1) * CHUNK)
        s = [jnp.where(is_prompt, s[h], s0_ref[j, h]) for h in heads]
        ws_qs = [jnp.dot(wq_ref[j, :, cs[h]], s[h].astype(BF16), preferred_element_type=F32)
                 for h in heads]
        v_new = [(u_ref[rs, cs[h]] - ws_qs[h][0:CHUNK]).astype(BF16) for h in heads]
        o = [ws_qs[h][CHUNK:2 * CHUNK]
             + jnp.dot(attn_ref[rs, h * CHUNK:(h + 1) * CHUNK], v_new[h], preferred_element_type=F32)
             for h in heads]
        s = [s[h] * egl_ref[j, h:h + 1, :] + _mm_tn(kd_ref[rs, cs[h]], v_new[h]) for h in heads]
        for h in heads:
            o_ref[rs, cs[h]] = (_rms(o[h], nw_ref[...]) * gate_ref[rs, cs[h]]).astype(o_ref.dtype)

        @pl.when(jnp.logical_not(is_prompt))
        def _(j=j, s=s):
            for h in heads:
                souts_ref[j, h] = s[h]

    for h in heads:
        s_scr[h] = s[h]

    @pl.when(last)
    def _():
        for h in heads:
            soutp_ref[0, h] = s[h]


def _gdn_scan(u, wq, kd, attn, egl, gate, s0, nw, *, n_p, cpl, n_s):
    t = u.shape[0]
    rows = SCAN_NB * CHUNK
    bps = cpl // SCAN_NB
    n_pb = n_p * bps
    row_blk = lambda b: (b, 0)
    s_in, s_out, s_shapes = _state_specs(n_p, n_s, n_pb, bps, (GDN_HEADS, GDN_DK, GDN_DV))
    return pl.pallas_call(
        functools.partial(_gdn_scan_body, n_pb=n_pb, bps=bps),
        grid=(t // rows,),
        in_specs=[pl.BlockSpec((rows, D_MODEL), row_blk),
                  pl.BlockSpec((SCAN_NB, 2 * CHUNK, D_MODEL), lambda b: (b, 0, 0)),
                  pl.BlockSpec((rows, D_MODEL), row_blk),
                  pl.BlockSpec((rows, GDN_HEADS * CHUNK), row_blk),
                  pl.BlockSpec((SCAN_NB, GDN_HEADS, LANES), lambda b: (b, 0, 0)),
                  pl.BlockSpec((rows, D_MODEL), row_blk),
                  s_in, _resident((1, GDN_DV))],
        out_specs=[pl.BlockSpec((rows, D_MODEL), row_blk)] + s_out,
        out_shape=[jax.ShapeDtypeStruct((t, D_MODEL), BF16)] + s_shapes,
        scratch_shapes=[pltpu.VMEM((GDN_HEADS, GDN_DK, GDN_DV), F32)],
        compiler_params=_params(), name="gdn_scan")(u, wq, kd, attn, egl, gate, s0, nw)


def _gla_exponent_matrix():
    i = np.arange(CHUNK)[:, None]
    t = np.arange(CHUNK)[None, :]
    blocks = [(t <= i)]
    for s in GLA_LEVELS:
        c = (i // (2 * s)) * (2 * s) + s - 1
        blocks.append(np.where(i > c, (t > c) & (t <= i), (t > i) & (t <= c)))
    w = np.concatenate(blocks, axis=0).astype(np.float32)
    return np.concatenate([w, w, w], axis=1)


def _split3(x):
    hi = x.astype(BF16)
    r1 = x - hi.astype(F32)
    mid = r1.astype(BF16)
    lo = (r1 - mid.astype(F32)).astype(BF16)
    return hi, mid, lo


def _gla_body(p_ref, s0_ref, wg2_ref, bg_ref, wexp_ref, nw_ref,
              o_ref, soutp_ref, souts_ref, st_scr, *, n_pb, bps):
    blk, is_prompt, first, last = _block_flags(n_pb, bps)

    @pl.when(first)
    def _():
        st_scr[...] = jnp.zeros_like(st_scr)

    row = lax.broadcasted_iota(jnp.int32, (CHUNK, CHUNK), 0)
    col = lax.broadcasted_iota(jnp.int32, (CHUNK, CHUNK), 1)
    eye = row == col
    masks = [_pair_mask(s) for s in GLA_LEVELS]
    heads = range(GLA_HEADS)
    ck = [slice(h * GLA_DK, (h + 1) * GLA_DK) for h in heads]
    cv = [slice(h * GLA_DV, (h + 1) * GLA_DV) for h in heads]

    prep = []
    for j in range(SCAN_NB):
        rs = slice(j * CHUNK, (j + 1) * CHUNK)
        x = _mm(p_ref[rs, GLA_G0:GLA_G0 + LANES], wg2_ref[...]) + bg_ref[...]
        glog = -_softplus(-x) * (1.0 / GLA_TAU)
        e_all = jnp.dot(wexp_ref[...], jnp.concatenate(_split3(glog), axis=0),
                        preferred_element_type=F32)
        q_all = p_ref[rs, 0:GLA_QK] * (GLA_DK ** -0.5)
        k_all = p_ref[rs, GLA_QK:2 * GLA_QK]
        qb, kb = q_all.astype(BF16), k_all.astype(BF16)
        attn = [jnp.where(eye, _mm_nt(qb[:, ck[h]], kb[:, ck[h]]), 0.0) for h in heads]
        for lvl, m in enumerate(masks):
            f = jnp.exp(e_all[(lvl + 1) * CHUNK:(lvl + 2) * CHUNK, :])
            qf, kf = (q_all * f).astype(BF16), (k_all * f).astype(BF16)
            attn = [attn[h] + jnp.where(m, _mm_nt(qf[:, ck[h]], kf[:, ck[h]]), 0.0) for h in heads]
        b = e_all[0:CHUNK, :]
        b_last = b[CHUNK - 1:CHUNK, :]
        vb = p_ref[rs, GLA_V0:GLA_V0 + D_MODEL].astype(BF16)
        prep.append(dict(qe=(q_all * jnp.exp(b)).astype(BF16),
                         kd=(k_all * jnp.exp(b_last - b)).astype(BF16),
                         eb_last=jnp.exp(b_last), vb=vb,
                         av=[_mm(attn[h], vb[:, cv[h]]) for h in heads]))

    st = [st_scr[h] for h in heads]
    for j, c in enumerate(prep):
        rs = slice(j * CHUNK, (j + 1) * CHUNK)
        st = [jnp.where(is_prompt, st[h], s0_ref[j, h]) for h in heads]
        o = [_mm_nt(c["qe"][:, ck[h]], st[h]) + c["av"][h] for h in heads]
        st = [st[h] * c["eb_last"][:, ck[h]] + _mm_tn(c["vb"][:, cv[h]], c["kd"][:, ck[h]])
              for h in heads]
        for h in heads:
            r = p_ref[rs, GLA_R0 + h * GLA_DV:GLA_R0 + (h + 1) * GLA_DV]
            o_ref[rs, cv[h]] = (_rms(o[h], nw_ref[h:h + 1, :]) * _silu(r)).astype(o_ref.dtype)

        @pl.when(jnp.logical_not(is_prompt))
        def _(j=j, st=st):
            for h in heads:
                souts_ref[j, h] = st[h]

    for h in heads:
        st_scr[h] = st[h]

    @pl.when(last)
    def _():
        for h in heads:
            soutp_ref[0, h] = st[h]


def _gla_scan(p, s0t, wg2, bg, wexp, nw, *, n_p, cpl, n_s):
    t = p.shape[0]
    rows = SCAN_NB * CHUNK
    bps = cpl // SCAN_NB
    n_pb = n_p * bps
    s_in, s_out, s_shapes = _state_specs(n_p, n_s, n_pb, bps, (GLA_HEADS, GLA_DV, GLA_DK))
    return pl.pallas_call(
        functools.partial(_gla_body, n_pb=n_pb, bps=bps),
        grid=(t // rows,),
        in_specs=[pl.BlockSpec((rows, GLA_P), lambda b: (b, 0)), s_in,
                  _resident((LANES, GLA_QK)), _resident((1, GLA_QK)),
                  _resident(((len(GLA_LEVELS) + 1) * CHUNK, 3 * CHUNK)),
                  _resident((GLA_HEADS, GLA_DV))],
        out_specs=[pl.BlockSpec((rows, D_MODEL), lambda b: (b, 0))] + s_out,
        out_shape=[jax.ShapeDtypeStruct((t, D_MODEL), BF16)] + s_shapes,
        scratch_shapes=[pltpu.VMEM((GLA_HEADS, GLA_DV, GLA_DK), F32)],
        compiler_params=_params(), name="gla_scan")(p, s0t, wg2, bg, wexp, nw)


CNV_NB = 4


def _cnv_body(x_ref, halo_ref, halo0_ref, wdw_ref, bdw_ref, o_ref, xc_scr, *, n_pb, bps, n_p):
    blk = pl.program_id(0)
    is_prompt = blk < n_pb
    win = CONV_HALO + CHUNK
    for j in range(CNV_NB):
        if j == 0:
            first = jnp.logical_or(jnp.logical_not(is_prompt), blk % bps == 0)
            prev = halo_ref[...]
        else:
            first = jnp.logical_not(is_prompt)
            prev = x_ref[j * CHUNK - CONV_HALO:j * CHUNK, :]
        seq = jnp.where(is_prompt, blk // bps, n_p + (blk - n_pb) * CNV_NB + j)
        xc_scr[j * win:j * win + CONV_HALO, :] = jnp.where(first, halo0_ref[seq], prev)
        xc_scr[j * win + CONV_HALO:(j + 1) * win, :] = x_ref[j * CHUNK:(j + 1) * CHUNK, :]
    taps = {}
    for w in range(CONV_W):
        s = CONV_HALO - (CONV_W - 1) + w
        taps.setdefault(-s % 8, []).append((w, s + (-s % 8)))
    for j in range(CNV_NB):
        for c0 in range(0, D_MODEL, LANES):
            cs = slice(c0, c0 + LANES)
            wv = xc_scr[j * win:(j + 1) * win, cs]
            y = None
            for r, group in sorted(taps.items()):
                rolled = wv if r == 0 else pltpu.roll(wv, r, 0)
                for w, a0 in group:
                    term = wdw_ref[w:w + 1, cs] * rolled[a0:a0 + CHUNK]
                    y = term if y is None else y + term
            o_ref[j * CHUNK:(j + 1) * CHUNK, cs] = y + bdw_ref[:, cs]


def _cnv_conv(glu, halo0, wdw, bdw, *, n_p, cpl, n_s):
    t = glu.shape[0]
    rows = CNV_NB * CHUNK
    bps = cpl // CNV_NB
    return pl.pallas_call(
        functools.partial(_cnv_body, n_pb=n_p * bps, bps=bps, n_p=n_p),
        grid=(t // rows,),
        in_specs=[pl.BlockSpec((rows, D_MODEL), lambda b: (b, 0)),
                  pl.BlockSpec((CONV_HALO, D_MODEL),
                               lambda b: (jnp.maximum(b * (rows // CONV_HALO) - 1, 0), 0)),
                  _resident((n_p + n_s, CONV_HALO, D_MODEL)),
                  _resident((CONV_HALO, D_MODEL)), _resident((1, D_MODEL))],
        out_specs=pl.BlockSpec((rows, D_MODEL), lambda b: (b, 0)),
        out_shape=jax.ShapeDtypeStruct((t, D_MODEL), F32),
        scratch_shapes=[pltpu.VMEM((CNV_NB * (CONV_HALO + CHUNK), D_MODEL), F32)],
        compiler_params=_params(), name="cnv_conv")(glu, glu, halo0, wdw, bdw)


def _pad_cols(w, n):
    return jnp.pad(w, ((0, 0), (0, n - w.shape[1])))


def _row(v):
    return v.reshape(1, -1).astype(F32)


def _tail_rows(a, lay, n):
    n_p, cpl, n_s = lay
    ends = np.concatenate([(np.arange(n_p) + 1) * cpl * CHUNK,
                           n_p * cpl * CHUNK + (np.arange(n_s) + 1) * CHUNK])
    idx = (ends[:, None] - n + np.arange(n)[None, :]).reshape(-1)
    return jnp.take(a, jnp.asarray(idx, jnp.int32), axis=0).reshape(n_p + n_s, n, a.shape[1])


def _gdn_mix(x, lay, dims, mix_nw, cache, state, w_in, conv_w, a_log, dt_bias, norm_w):
    n_p, cpl, n_s = lay
    halo0 = jnp.concatenate([jnp.zeros((n_p, 8, GDN_CONV_CH), F32),
                             jnp.pad(cache, ((0, 0), (8 - (GDN_CONV - 1), 0), (0, 0)))], axis=0)
    lane_pad = lambda v: jnp.pad(v.reshape(1, -1), ((0, 0), (GDN_HEADS, LANES - 2 * GDN_HEADS)))
    alog, dtb = lane_pad(a_log), lane_pad(dt_bias)
    k, kbq, vk, qe, kd, gate, bgp, tails = _gdn_in(
        x, _row(mix_nw), _pad_cols(w_in, GDN_P).astype(BF16), conv_w, alog, dtb, halo0,
        tps=cpl * CHUNK // ROW_TILE, n_p=n_p, **dims)
    u, wq, attn, egl = _gdn_pre(k, kbq, vk, qe, bgp, alog, dtb)
    o, s_p, s_s = _gdn_scan(u, wq, kd, attn, egl, gate, state, _row(norm_w),
                            n_p=n_p, cpl=cpl, n_s=n_s)
    ends = np.concatenate([(np.arange(n_p) + 1) * cpl - 1, n_p * cpl + np.arange(n_s)])
    conv_rows = tails[jnp.asarray(ends, jnp.int32), 8 - (GDN_CONV - 1):, :]
    return o, (conv_rows[:n_p], conv_rows[n_p:], s_p, s_s)


def _gla_mix(x, lay, dims, mix_nw, state, w_in, w_g2, b_g, norm_w):
    n_p, cpl, n_s = lay
    p = _proj(x, _row(mix_nw), _pad_cols(w_in, GLA_P).astype(BF16), None,
              n_out=GLA_P, ck=512, name="gla_in", **dims)
    wg2 = jnp.pad(w_g2, ((0, LANES - GLA_RANK), (0, 0))).astype(BF16)
    wexp = jnp.asarray(_gla_exponent_matrix(), BF16)
    o, st_p, st_s = _gla_scan(p, jnp.swapaxes(state, 2, 3), wg2, _row(b_g), wexp, norm_w,
                              n_p=n_p, cpl=cpl, n_s=n_s)
    return o, (jnp.swapaxes(st_p, 2, 3), jnp.swapaxes(st_s, 2, 3))


def _cnv_mix(x, lay, dims, mix_nw, cache, w_pw1, b_pw1, w_dw, b_dw):
    n_p, cpl, n_s = lay
    glu = _proj(x, _row(mix_nw), w_pw1.astype(BF16), _row(b_pw1),
                n_out=D_MODEL, ck=512, epilogue="glu", name="cnv_in", **dims)
    pad_rows = CONV_HALO - (CONV_W - 1)
    halo0 = jnp.concatenate([jnp.zeros((n_p, CONV_HALO, D_MODEL), F32),
                             jnp.pad(cache, ((0, 0), (pad_rows, 0), (0, 0)))], axis=0)
    wdw = jnp.pad(w_dw, ((0, CONV_HALO - CONV_W), (0, 0)))
    y = _cnv_conv(glu, halo0, wdw, _row(b_dw), n_p=n_p, cpl=cpl, n_s=n_s)
    rows = _tail_rows(glu, lay, CONV_W - 1)
    return y, (rows[:n_p], rows[n_p:])


def kernel(x_prompt, x_sample, cache_gdn_conv, state_gdn, state_gla, cache_conformer, mix_norm_w, ffn_norm_w, ffn_w1, ffn_w2, final_norm_w, gdn_w_in, gdn_conv_w, gdn_a_log, gdn_dt_bias, gdn_norm_w, gdn_w_out, gla_w_in, gla_w_g2, gla_b_g, gla_norm_w, gla_w_out, gmlp_w_in, gmlp_b_in, gmlp_ln_w, gmlp_ln_b, gmlp_w_s, gmlp_b_s, gmlp_w_out, gmlp_b_out, cnv_w_pw1, cnv_b_pw1, cnv_w_dw, cnv_b_dw, cnv_ln_w, cnv_ln_b, cnv_w_pw2, cnv_b_pw2):
    n_p, l_p, d = x_prompt.shape
    n_s, l_s, _ = x_sample.shape
    assert d == D_MODEL and l_s == CHUNK and l_p % ROW_TILE == 0
    assert n_s % CNV_NB == 0
    t_p, t_s = n_p * l_p, n_s * l_s
    assert t_p % ROW_TILE == 0 and t_s % ROW_TILE == 0
    lay = (n_p, l_p // CHUNK, n_s)
    dims = dict(t=t_p + t_s, n_pt=t_p // ROW_TILE)
    depth = mix_norm_w.shape[0]

    x = (x_prompt.reshape(t_p, d), x_sample.reshape(t_s, d))
    outs = {k: [] for k in ("gdn_cp", "gdn_cs", "gdn_sp", "gdn_ss", "gla_sp", "gla_ss",
                            "gmlp_vs", "cnv_p", "cnv_s")}
    w1_all, w2_all = ffn_w1.astype(BF16), ffn_w2.astype(BF16)
    for i in range(depth):
        kind, j = i % 4, i // 4
        ffn = (_row(ffn_norm_w[i]), w1_all, w2_all, _row(final_norm_w) if i == depth - 1 else None)
        if kind == 0:
            o, (cp, cs, sp, ss) = _gdn_mix(x, lay, dims, mix_norm_w[i], cache_gdn_conv[j], state_gdn[j],
                                           gdn_w_in[j], gdn_conv_w[j], gdn_a_log[j], gdn_dt_bias[j],
                                           gdn_norm_w[j])
            outs["gdn_cp"].append(cp); outs["gdn_cs"].append(cs)
            outs["gdn_sp"].append(sp); outs["gdn_ss"].append(ss)
            res = _post([o], x, gdn_w_out[j].astype(BF16), None, *ffn,
                        prologue="plain", layer=i, name=f"post_{i}", **dims)
        elif kind == 1:
            o, (sp, ss) = _gla_mix(x, lay, dims, mix_norm_w[i], state_gla[j], gla_w_in[j], gla_w_g2[j],
                                   gla_b_g[j], gla_norm_w[j])
            outs["gla_sp"].append(sp); outs["gla_ss"].append(ss)
            res = _post([o], x, gla_w_out[j].astype(BF16), None, *ffn,
                        prologue="plain", layer=i, name=f"post_{i}", **dims)
        elif kind == 2:
            zz = _proj(x, _row(mix_norm_w[i]), gmlp_w_in[j].astype(BF16), _row(gmlp_b_in[j]),
                       n_out=2 * D_MODEL, ck=512, epilogue="gelu", name="gmlp_in", **dims)
            bs_full = jnp.repeat(gmlp_b_s[j].T, GMLP_GC, axis=1)
            res = _post([zz, _row(gmlp_ln_w[j]), _row(gmlp_ln_b[j]), gmlp_w_s[j], bs_full], x,
                        gmlp_w_out[j].astype(BF16), _row(gmlp_b_out[j]), *ffn,
                        prologue="gmlp", layer=i, name=f"post_{i}", **dims)
            outs["gmlp_vs"].append(res[-1].reshape(n_s, l_s, d))
            res = res[:-1]
        else:
            y, (cp, cs) = _cnv_mix(x, lay, dims, mix_norm_w[i], cache_conformer[j], cnv_w_pw1[j],
                                   cnv_b_pw1[j], cnv_w_dw[j], cnv_b_dw[j])
            outs["cnv_p"].append(cp); outs["cnv_s"].append(cs)
            res = _post([y, _row(cnv_ln_w[j]), _row(cnv_ln_b[j])], x, cnv_w_pw2[j].astype(BF16),
                        _row(cnv_b_pw2[j]), *ffn, prologue="cnv", layer=i, name=f"post_{i}", **dims)
        x = tuple(res) if i == depth - 1 else res[0]
    y_p, y_s = x
    return (y_p.reshape(n_p, l_p, d), y_s.reshape(n_s, l_s, d),
            jnp.stack(outs["gdn_cp"]), jnp.stack(outs["gdn_cs"]),
            jnp.stack(outs["gdn_sp"]), jnp.stack(outs["gdn_ss"]),
            jnp.stack(outs["gla_sp"]), jnp.stack(outs["gla_ss"]),
            jnp.stack(outs["gmlp_vs"]),
            jnp.stack(outs["cnv_p"]), jnp.stack(outs["cnv_s"]))
```

```python
import functools
import math

import numpy as np
import jax
import jax.numpy as jnp
from jax import lax
from jax.experimental import pallas as pl
from jax.experimental.pallas import tpu as pltpu

F32 = jnp.float32
BF16 = jnp.bfloat16

D_MODEL = 1024
D_FF = 4 * D_MODEL
EPS = 1e-6
CHUNK = 64
ROW_TILE = 512
SCAN_NB = 4
LANES = 128
VMEM_LIMIT = 56 * 1024 * 1024

GDN_HEADS, GDN_DK, GDN_DV, GDN_CONV = 8, 128, 128, 4
GDN_QK = GDN_HEADS * GDN_DK
GDN_CONV_CH = 2 * GDN_QK + GDN_HEADS * GDN_DV
GDN_Z0 = GDN_CONV_CH
GDN_BG0 = GDN_CONV_CH + GDN_HEADS * GDN_DV
GDN_P = GDN_BG0 + LANES

GLA_HEADS, GLA_DK, GLA_DV, GLA_RANK, GLA_TAU = 4, 128, 256, 16, 16.0
GLA_QK = GLA_HEADS * GLA_DK
GLA_V0, GLA_R0, GLA_G0 = 2 * GLA_QK, 2 * GLA_QK + D_MODEL, 2 * GLA_QK + 2 * D_MODEL
GLA_P = GLA_G0 + LANES
GLA_LEVELS = (32, 16, 8, 4, 2, 1)

GMLP_BLOCK, GMLP_GROUPS, GMLP_GC = 128, 4, 256
CONV_W = 31
CONV_HALO = 32


def _mm(a, b):
    return jnp.dot(a.astype(BF16), b.astype(BF16), preferred_element_type=F32)


def _mm_nt(a, b):
    return lax.dot_general(a.astype(BF16), b.astype(BF16), (((1,), (1,)), ((), ())),
                           preferred_element_type=F32)


def _mm_tn(a, b):
    return lax.dot_general(a.astype(BF16), b.astype(BF16), (((0,), (0,)), ((), ())),
                           preferred_element_type=F32)


def _rms(x, w):
    return x * lax.rsqrt(jnp.mean(x * x, axis=-1, keepdims=True) + EPS) * w


def _layernorm(x, w, b):
    xc = x - jnp.mean(x, axis=-1, keepdims=True)
    return xc * lax.rsqrt(jnp.mean(xc * xc, axis=-1, keepdims=True) + EPS) * w + b


def _sigmoid(x):
    return jax.nn.sigmoid(x)


def _silu(x):
    return x * _sigmoid(x)


def _softplus(x):
    return jnp.maximum(x, 0.0) + jnp.log1p(jnp.exp(-jnp.abs(x)))


def _gelu_tanh(x):
    return 0.5 * x * (1.0 + jnp.tanh(math.sqrt(2.0 / math.pi) * (x + 0.044715 * (x * x * x))))


def _cumsum_rows(x):
    row = lax.broadcasted_iota(jnp.int32, x.shape, 0)
    s = 1
    while s < x.shape[0]:
        x = x + jnp.where(row >= s, pltpu.roll(x, s, 0), 0.0)
        s *= 2
    return x


def _pair_mask(s):
    sh = int(math.log2(s))
    bi = lax.broadcasted_iota(jnp.int32, (CHUNK, CHUNK), 0) >> sh
    bj = lax.broadcasted_iota(jnp.int32, (CHUNK, CHUNK), 1) >> sh
    return jnp.logical_and((bi & 1) == 1, bj == bi - 1)


def _block_flags(n_pb, bps):
    blk = pl.program_id(0)
    is_prompt = blk < n_pb
    first = jnp.logical_and(is_prompt, blk % bps == 0)
    last = jnp.logical_and(is_prompt, blk % bps == bps - 1)
    return blk, is_prompt, first, last


def _state_specs(n_p, n_s, n_pb, bps, shape):
    zeros = (0,) * len(shape)
    sample = pl.BlockSpec((SCAN_NB,) + shape, lambda b: (jnp.maximum(b - n_pb, 0),) + zeros)
    prompt = pl.BlockSpec((1,) + shape, lambda b: (jnp.minimum(b // bps, n_p - 1),) + zeros)
    shapes = [jax.ShapeDtypeStruct((n_p,) + shape, F32), jax.ShapeDtypeStruct((n_s,) + shape, F32)]
    return sample, [prompt, sample], shapes


def _params(n_axes=1):
    return pltpu.CompilerParams(dimension_semantics=("arbitrary",) * n_axes,
                                vmem_limit_bytes=VMEM_LIMIT)


def _resident(shape):
    zeros = (0,) * len(shape)
    return pl.BlockSpec(shape, lambda i: zeros, pipeline_mode=pl.Buffered(1))


def _row_tile_specs(x, n_pt):
    if isinstance(x, tuple):
        return ([pl.BlockSpec((ROW_TILE, D_MODEL), lambda i: (jnp.minimum(i, n_pt - 1), 0)),
                 pl.BlockSpec((ROW_TILE, D_MODEL), lambda i: (jnp.maximum(i - n_pt, 0), 0))], list(x))
    return [pl.BlockSpec((ROW_TILE, D_MODEL), lambda i: (i, 0))], [x]


def _read_row_tile(refs, n_pt):
    if len(refs) == 2:
        return jnp.where(pl.program_id(0) < n_pt, refs[0][...], refs[1][...])
    return refs[0][...]


def _proj_body(*refs, n_x, n_pt, n_out, ck, epilogue, has_bias):
    x_refs, (nw_ref, w_ref), rest = refs[:n_x], refs[n_x:n_x + 2], refs[n_x + 2:]
    if has_bias:
        b_ref, o_ref = rest
    else:
        (o_ref,) = rest
    xn = _rms(_read_row_tile(x_refs, n_pt), nw_ref[...]).astype(BF16)
    for c0 in range(0, n_out, ck):
        c = c0 // ck
        sl = slice(c0, min(c0 + ck, n_out))
        y = jnp.dot(xn, w_ref[:, sl], preferred_element_type=F32)
        if has_bias:
            y = y + b_ref[:, sl]
        if epilogue == "gelu":
            y = _gelu_tanh(y)
        elif epilogue == "glu":
            sg = slice(n_out + c * ck, n_out + (c + 1) * ck)
            gate = jnp.dot(xn, w_ref[:, sg], preferred_element_type=F32) + b_ref[:, sg]
            y = y * _sigmoid(gate)
        o_ref[:, sl] = y.astype(o_ref.dtype)


def _proj(x, nw, w, b, *, t, n_pt, n_out, ck, epilogue="none", out_dtype=F32, name):
    n_w = w.shape[1]
    has_bias = b is not None
    in_specs, args = _row_tile_specs(x, n_pt)
    n_x = len(args)
    in_specs += [_resident((1, D_MODEL)), _resident((D_MODEL, n_w))]
    args += [nw, w]
    if has_bias:
        in_specs.append(_resident((1, n_w)))
        args.append(b)
    return pl.pallas_call(
        functools.partial(_proj_body, n_x=n_x, n_pt=n_pt, n_out=n_out, ck=ck, epilogue=epilogue,
                          has_bias=has_bias),
        grid=(t // ROW_TILE,),
        in_specs=in_specs,
        out_specs=pl.BlockSpec((ROW_TILE, n_out), lambda i: (i, 0)),
        out_shape=jax.ShapeDtypeStruct((t, n_out), out_dtype),
        compiler_params=_params(), name=name)(*args)


FFN_CK = 512


def _gmlp_gate(zz_ref, lnw_ref, lnb_ref, ws_ref, bs_ref, vs_ref, a_scr, is_sample):
    v = _layernorm(zz_ref[:, D_MODEL:2 * D_MODEL], lnw_ref[...], lnb_ref[...])

    @pl.when(is_sample)
    def _():
        vs_ref[...] = v

    vb = v.astype(BF16)
    row = lax.broadcasted_iota(jnp.int32, (GMLP_BLOCK, GMLP_BLOCK), 0)
    col = lax.broadcasted_iota(jnp.int32, (GMLP_BLOCK, GMLP_BLOCK), 1)
    tril = col <= row
    top_left = jnp.logical_and(row < CHUNK, col < CHUNK)
    brow = lax.broadcasted_iota(jnp.int32, (GMLP_BLOCK, D_MODEL), 0)
    bias = bs_ref[...]
    bias = jnp.where(jnp.logical_and(is_sample, brow >= CHUNK), pltpu.roll(bias, CHUNK, 0), bias)
    for grp in range(GMLP_GROUPS):
        cs = slice(grp * GMLP_GC, (grp + 1) * GMLP_GC)
        wt = jnp.where(tril, ws_ref[grp], 0.0)
        w11 = jnp.where(top_left, wt, 0.0)
        wd = w11 + pltpu.roll(pltpu.roll(w11, CHUNK, 0), CHUNK, 1)
        we = jnp.where(is_sample, wd, wt).astype(BF16)
        for blk in range(ROW_TILE // GMLP_BLOCK):
            rs = slice(blk * GMLP_BLOCK, (blk + 1) * GMLP_BLOCK)
            mixed = jnp.dot(we, vb[rs, cs], preferred_element_type=F32)
            a_scr[rs, cs] = (zz_ref[rs, cs] * (mixed + bias[:, cs])).astype(BF16)
    return a_scr[...]


def _post_body(*refs, prologue, n_res, has_bias, final, n_pt):
    n_in = {"plain": 1, "cnv": 3, "gmlp": 5}[prologue]
    pro, refs = refs[:n_in], refs[n_in:]
    res_refs, refs = refs[:n_res], refs[n_res:]
    wo_ref, refs = refs[0], refs[1:]
    if has_bias:
        bo_ref, refs = refs[0], refs[1:]
    fnw_ref, w1_ref, w2_ref, refs = refs[0], refs[1], refs[2], refs[3:]
    if final:
        finw_ref, refs = refs[0], refs[1:]
    n_main = 2 if final else 1
    outs, refs = refs[:n_main], refs[n_main:]
    is_sample = pl.program_id(0) >= n_pt

    if prologue == "plain":
        a = pro[0][...]
    elif prologue == "cnv":
        y_ref, lnw_ref, lnb_ref = pro
        a = _silu(_layernorm(y_ref[...], lnw_ref[...], lnb_ref[...])).astype(BF16)
    else:
        vs_ref, a_scr = refs
        a = _gmlp_gate(*pro, vs_ref, a_scr, is_sample)

    x = _read_row_tile(res_refs, n_pt) + jnp.dot(a, wo_ref[...], preferred_element_type=F32)
    if has_bias:
        x = x + bo_ref[...]
    xn = _rms(x, fnw_ref[...]).astype(BF16)
    for c in range(D_FF // FFN_CK):
        sl = slice(c * FFN_CK, (c + 1) * FFN_CK)
        h = jnp.dot(xn, w1_ref[:, sl], preferred_element_type=F32)
        h = jnp.square(jnp.maximum(h, 0.0)).astype(BF16)
        x = x + jnp.dot(h, w2_ref[sl, :], preferred_element_type=F32)
    if final:
        y = _rms(x, finw_ref[...])

        @pl.when(jnp.logical_not(is_sample))
        def _():
            outs[0][...] = y

        @pl.when(is_sample)
        def _():
            outs[1][...] = y
    else:
        outs[0][...] = x


def _post(pro_args, res, wo, bo, fnw, w1, w2, finw, *, layer, prologue, t, n_pt, name):
    has_bias = bo is not None
    final = finw is not None
    tile = lambda c: pl.BlockSpec((ROW_TILE, c), lambda i: (i, 0))
    if prologue == "plain":
        in_specs = [tile(D_MODEL)]
    elif prologue == "cnv":
        in_specs = [tile(D_MODEL), _resident((1, D_MODEL)), _resident((1, D_MODEL))]
    else:
        in_specs = [tile(2 * D_MODEL), _resident((1, D_MODEL)), _resident((1, D_MODEL)),
                    _resident((GMLP_GROUPS, GMLP_BLOCK, GMLP_BLOCK)),
                    _resident((GMLP_BLOCK, D_MODEL))]
    args = list(pro_args)
    res_specs, res_args = _row_tile_specs(res, n_pt)
    in_specs += res_specs + [_resident((D_MODEL, D_MODEL))]
    args += res_args + [wo]
    if has_bias:
        in_specs.append(_resident((1, D_MODEL)))
        args.append(bo)
    layer_block = lambda shape: pl.BlockSpec((None,) + shape, lambda i: (layer, 0, 0),
                                             pipeline_mode=pl.Buffered(1))
    in_specs += [_resident((1, D_MODEL)), layer_block((D_MODEL, D_FF)), layer_block((D_FF, D_MODEL))]
    args += [fnw, w1, w2]
    if final:
        in_specs.append(_resident((1, D_MODEL)))
        args.append(finw)
        out_specs = [pl.BlockSpec((ROW_TILE, D_MODEL), lambda i: (jnp.minimum(i, n_pt - 1), 0)),
                     pl.BlockSpec((ROW_TILE, D_MODEL), lambda i: (jnp.maximum(i - n_pt, 0), 0))]
        out_shape = [jax.ShapeDtypeStruct((n_pt * ROW_TILE, D_MODEL), F32),
                     jax.ShapeDtypeStruct((t - n_pt * ROW_TILE, D_MODEL), F32)]
    else:
        out_specs = [tile(D_MODEL)]
        out_shape = [jax.ShapeDtypeStruct((t, D_MODEL), F32)]
    scratch = []
    if prologue == "gmlp":
        out_specs.append(pl.BlockSpec((ROW_TILE, D_MODEL), lambda i: (jnp.maximum(i - n_pt, 0), 0)))
        out_shape.append(jax.ShapeDtypeStruct((t - n_pt * ROW_TILE, D_MODEL), F32))
        scratch.append(pltpu.VMEM((ROW_TILE, D_MODEL), BF16))
    return pl.pallas_call(
        functools.partial(_post_body, prologue=prologue, n_res=len(res_args), has_bias=has_bias,
                          final=final, n_pt=n_pt),
        grid=(t // ROW_TILE,),
        in_specs=in_specs, out_specs=out_specs, out_shape=out_shape, scratch_shapes=scratch,
        compiler_params=_params(), name=name)(*args)


GDN_NB = 4
GDN_WIN = CHUNK + 8
GDN_GROUPS = ROW_TILE // CHUNK


def _gdn_gates(bg, alog_ref, dtb_ref):
    glog = -jnp.exp(alog_ref[...]) * _softplus(bg + dtb_ref[...])
    return _sigmoid(bg), _cumsum_rows(glog)


def _gdn_in_body(*refs, n_x, n_pt, tps, n_p):
    x_refs = refs[:n_x]
    (nw_ref, w_ref, cw_ref, alog_ref, dtb_ref, halo0_ref,
     k_ref, kbq_ref, vk_ref, qe_ref, kd_ref, gate_ref, bg_ref, tails_ref,
     xc_scr, carry_scr) = refs[n_x:]
    i = pl.program_id(0)
    is_sample = i >= n_pt

    @pl.when(i == 0)
    def _():
        carry_scr[...] = jnp.zeros_like(carry_scr)

    xn = _rms(_read_row_tile(x_refs, n_pt), nw_ref[...]).astype(BF16)
    bg = jnp.dot(xn, w_ref[:, GDN_BG0:GDN_BG0 + LANES], preferred_element_type=F32)
    bg_ref[...] = bg
    beta, egc, e_rest = [], [], []
    for j in range(GDN_GROUPS):
        b_j, gc = _gdn_gates(bg[j * CHUNK:(j + 1) * CHUNK], alog_ref, dtb_ref)
        beta.append(b_j)
        egc.append(jnp.exp(gc))
        e_rest.append(jnp.exp(gc[CHUNK - 1:CHUNK, :] - gc))

    def project(c0, n):
        cols = slice(c0, c0 + n)
        y = jnp.dot(xn, w_ref[:, cols], preferred_element_type=F32)
        for j in range(GDN_GROUPS):
            r0 = j * GDN_WIN
            xc_scr[r0 + 8:r0 + GDN_WIN, cols] = y[j * CHUNK:(j + 1) * CHUNK]
            if j == 0:
                first = jnp.logical_or(is_sample, i % tps == 0)
                prev = carry_scr[:, cols]
            else:
                first = is_sample
                prev = xc_scr[r0 - 8:r0, cols]
            seq = jnp.where(is_sample, n_p + (i - n_pt) * GDN_GROUPS + j, i // tps)
            xc_scr[r0:r0 + 8, cols] = jnp.where(first, halo0_ref[seq, :, cols], prev)

    def conv_silu(j, c0):
        wv = xc_scr[j * GDN_WIN:(j + 1) * GDN_WIN, c0:c0 + LANES]
        y = cw_ref[GDN_CONV - 1:GDN_CONV, c0:c0 + LANES] * wv[8:GDN_WIN]
        for d in range(1, GDN_CONV):
            w = GDN_CONV - 1 - d
            y = y + cw_ref[w:w + 1, c0:c0 + LANES] * pltpu.roll(wv, d, 0)[8:GDN_WIN]
        return _silu(y)

    def epilogue(h):
        cs = slice(h * GDN_DK, (h + 1) * GDN_DK)
        for j in range(GDN_GROUPS):
            rs = slice(j * CHUNK, (j + 1) * CHUNK)
            q = conv_silu(j, h * GDN_DK)
            k = conv_silu(j, GDN_QK + h * GDN_DK)
            v = conv_silu(j, 2 * GDN_QK + h * GDN_DV)
            q = q * lax.rsqrt(jnp.sum(q * q, axis=-1, keepdims=True) + EPS) * (GDN_DK ** -0.5)
            k = k * lax.rsqrt(jnp.sum(k * k, axis=-1, keepdims=True) + EPS)
            beta_h = beta[j][:, h:h + 1]
            e_h = egc[j][:, 8 + h:9 + h]
            kb = k * beta_h
            k_ref[rs, cs] = k.astype(BF16)
            kbq_ref[j, 0:CHUNK, cs] = kb.astype(BF16)
            kbq_ref[j, CHUNK:2 * CHUNK, cs] = q.astype(BF16)
            vk_ref[rs, 2 * h * GDN_DV:(2 * h + 1) * GDN_DV] = (v * beta_h).astype(BF16)
            vk_ref[rs, (2 * h + 1) * GDN_DV:(2 * h + 2) * GDN_DV] = (kb * e_h).astype(BF16)
            qe_ref[rs, cs] = (q * e_h).astype(BF16)
            kd_ref[rs, cs] = (k * e_rest[j][:, 8 + h:9 + h]).astype(BF16)

    pair = 2 * GDN_DK
    n_pairs = GDN_HEADS // 2
    for m in range(n_pairs + 1):
        if m < n_pairs:
            for part in range(3):
                project(part * GDN_QK + m * pair, pair)
        if m > 0:
            epilogue(2 * m - 2)
            epilogue(2 * m - 1)
    for c in range(2):
        cols = slice(c * FFN_CK, (c + 1) * FFN_CK)
        z = jnp.dot(xn, w_ref[:, GDN_Z0 + c * FFN_CK:GDN_Z0 + (c + 1) * FFN_CK],
                    preferred_element_type=F32)
        gate_ref[:, cols] = _silu(z).astype(BF16)
    for j in range(GDN_GROUPS):
        tails_ref[j] = xc_scr[(j + 1) * GDN_WIN - 8:(j + 1) * GDN_WIN, :]
    carry_scr[...] = xc_scr[GDN_GROUPS * GDN_WIN - 8:GDN_GROUPS * GDN_WIN, :]


def _gdn_in(x, nw, w, cw, alog, dtb, halo0, *, t, n_pt, tps, n_p):
    in_specs, args = _row_tile_specs(x, n_pt)
    n_x = len(args)
    n_seq = halo0.shape[0]
    in_specs += [_resident((1, D_MODEL)), _resident((D_MODEL, GDN_P)),
                 _resident((GDN_CONV, GDN_CONV_CH)), _resident((1, LANES)), _resident((1, LANES)),
                 _resident((n_seq, 8, GDN_CONV_CH))]
    args += [nw, w, cw, alog, dtb, halo0]
    tile = lambda c: pl.BlockSpec((ROW_TILE, c), lambda i: (i, 0))
    n_chunks = t // CHUNK
    bf = lambda c: jax.ShapeDtypeStruct((t, c), BF16)
    return pl.pallas_call(
        functools.partial(_gdn_in_body, n_x=n_x, n_pt=n_pt, tps=tps, n_p=n_p),
        grid=(t // ROW_TILE,),
        in_specs=in_specs,
        out_specs=[tile(D_MODEL),
                   pl.BlockSpec((GDN_GROUPS, 2 * CHUNK, D_MODEL), lambda i: (i, 0, 0)),
                   tile(2 * D_MODEL), tile(D_MODEL), tile(D_MODEL), tile(D_MODEL), tile(LANES),
                   pl.BlockSpec((GDN_GROUPS, 8, GDN_CONV_CH), lambda i: (i, 0, 0))],
        out_shape=[bf(D_MODEL), jax.ShapeDtypeStruct((n_chunks, 2 * CHUNK, D_MODEL), BF16),
                   bf(2 * D_MODEL), bf(D_MODEL), bf(D_MODEL), bf(D_MODEL),
                   jax.ShapeDtypeStruct((t, LANES), F32),
                   jax.ShapeDtypeStruct((n_chunks, 8, GDN_CONV_CH), F32)],
        scratch_shapes=[pltpu.VMEM((GDN_GROUPS * GDN_WIN, GDN_CONV_CH), F32),
                        pltpu.VMEM((8, GDN_CONV_CH), F32)],
        compiler_params=_params(), name="gdn_in")(*args)


def _gdn_pre_body(k_ref, kbq_ref, vk_ref, qe_ref, bg_ref, alog_ref, dtb_ref,
                  u_ref, wq_ref, attn_ref, egl_ref):
    pw = 2 * CHUNK
    row = lax.broadcasted_iota(jnp.int32, (CHUNK, pw), 0)
    lane = lax.broadcasted_iota(jnp.int32, (CHUNK, pw), 1)
    col = lane & (CHUNK - 1)
    left = lane < CHUNK
    incl = row >= col
    strict = row > col
    eye = (row == col).astype(F32)

    def level_mask(s):
        sh = int(math.log2(s))
        bi, bj = row >> sh, col >> sh
        return jnp.logical_and((bi & 1) == 1, bj == bi - 1)

    masks = [level_mask(s) for s in (1, 2, 4, 8, 16, 32)]
    r2 = lax.broadcasted_iota(jnp.int32, (pw, pw), 0)
    l2 = lax.broadcasted_iota(jnp.int32, (pw, pw), 1)
    same_head = (r2 < CHUNK) == (l2 < CHUNK)

    def block_diag(x):
        return jnp.where(same_head, jnp.concatenate([x, x], axis=0), jnp.zeros((pw, pw), x.dtype))

    probs = [(j, m) for j in range(GDN_NB) for m in range(GDN_HEADS // 2)]
    n = len(probs)
    rs = [slice(j * CHUNK, (j + 1) * CHUNK) for j, _ in probs]
    hc = lambda h: slice(h * GDN_DK, (h + 1) * GDN_DK)
    gcum, gcum_t = [], []
    for j in range(GDN_NB):
        _, gc = _gdn_gates(bg_ref[j * CHUNK:(j + 1) * CHUNK, :], alog_ref, dtb_ref)
        gcum.append(gc)
        gt = jnp.transpose(jnp.concatenate([gc, jnp.zeros_like(gc)], axis=0))
        gcum_t.append(gt)
        egl_ref[j] = jnp.broadcast_to(jnp.exp(gt[8:8 + GDN_HEADS, CHUNK - 1:CHUNK]),
                                      (GDN_HEADS, LANES))
    decay = []
    for j, m in probs:
        h0, h1 = 8 + 2 * m, 9 + 2 * m
        g_col = jnp.where(left, gcum[j][:, h0:h0 + 1], gcum[j][:, h1:h1 + 1])
        g_row = jnp.concatenate([gcum_t[j][h0:h0 + 1, 0:CHUNK], gcum_t[j][h1:h1 + 1, 0:CHUNK]], axis=1)
        decay.append(jnp.exp(jnp.minimum(g_col - g_row, 0.0)))
    qk = []
    for i, (j, m) in enumerate(probs):
        k_pair = jnp.concatenate([k_ref[rs[i], hc(2 * m)], k_ref[rs[i], hc(2 * m + 1)]], axis=0)
        r0, r1 = (lax.dot_general(kbq_ref[j, :, hc(h)], k_pair, (((1,), (1,)), ((), ())),
                                  preferred_element_type=F32) for h in (2 * m, 2 * m + 1))
        qk.append(jnp.where(l2 < CHUNK, r0, r1))
    a = [jnp.where(strict, qk[i][0:CHUNK] * decay[i], 0.0) for i in range(n)]
    for i, (j, m) in enumerate(probs):
        attn = jnp.where(incl, qk[i][CHUNK:2 * CHUNK] * decay[i], 0.0)
        attn_ref[rs[i], m * pw:(m + 1) * pw] = attn.astype(attn_ref.dtype)
    t = [eye - jnp.where(masks[0], ai, 0.0) for ai in a]
    a_b = [ai.astype(BF16) for ai in a]
    for lm in masks[1:]:
        t_b = [ti.astype(BF16) for ti in t]
        at = [jnp.dot(jnp.where(lm, a_b[i], jnp.zeros_like(a_b[i])), block_diag(t_b[i]),
                      preferred_element_type=F32) for i in range(n)]
        t = [t[i] - jnp.dot(t_b[i], block_diag(at[i].astype(BF16)), preferred_element_type=F32)
             for i in range(n)]
    zeros = jnp.zeros((CHUNK, 2 * GDN_DV), BF16)
    for i, (j, m) in enumerate(probs):
        v0 = vk_ref[rs[i], 4 * m * GDN_DV:(4 * m + 2) * GDN_DV]
        v1 = vk_ref[rs[i], (4 * m + 2) * GDN_DV:(4 * m + 4) * GDN_DV]
        rhs = jnp.concatenate([jnp.concatenate([v0, zeros], axis=1),
                               jnp.concatenate([zeros, v1], axis=1)], axis=0)
        uw = jnp.dot(t[i].astype(BF16), rhs, preferred_element_type=F32)
        for q, h in enumerate((2 * m, 2 * m + 1)):
            u_ref[rs[i], hc(h)] = uw[:, 2 * q * GDN_DV:(2 * q + 1) * GDN_DV]
            wq_ref[j, 0:CHUNK, hc(h)] = uw[:, (2 * q + 1) * GDN_DV:(2 * q + 2) * GDN_DV].astype(wq_ref.dtype)
            wq_ref[j, CHUNK:2 * CHUNK, hc(h)] = qe_ref[rs[i], hc(h)]


def _gdn_pre(k, kbq, vk, qe, bgp, alog, dtb):
    t = k.shape[0]
    rows = GDN_NB * CHUNK
    n_chunks = t // CHUNK
    tile = lambda c: pl.BlockSpec((rows, c), lambda b: (b, 0))
    chunked = lambda r, c: pl.BlockSpec((GDN_NB, r, c), lambda b: (b, 0, 0))
    return pl.pallas_call(
        _gdn_pre_body,
        grid=(t // rows,),
        in_specs=[tile(D_MODEL), chunked(2 * CHUNK, D_MODEL), tile(2 * D_MODEL), tile(D_MODEL),
                  tile(LANES), _resident((1, LANES)), _resident((1, LANES))],
        out_specs=[tile(D_MODEL), chunked(2 * CHUNK, D_MODEL), tile(GDN_HEADS * CHUNK),
                   chunked(GDN_HEADS, LANES)],
        out_shape=[jax.ShapeDtypeStruct((t, D_MODEL), F32),
                   jax.ShapeDtypeStruct((n_chunks, 2 * CHUNK, D_MODEL), BF16),
                   jax.ShapeDtypeStruct((t, GDN_HEADS * CHUNK), BF16),
                   jax.ShapeDtypeStruct((n_chunks, GDN_HEADS, LANES), F32)],
        compiler_params=_params(), name="gdn_pre")(k, kbq, vk, qe, bgp, alog, dtb)


def _gdn_scan_body(u_ref, wq_ref, kd_ref, attn_ref, egl_ref, gate_ref, s0_ref, nw_ref,
                   o_ref, soutp_ref, souts_ref, s_scr, *, n_pb, bps):
    blk, is_prompt, first, last = _block_flags(n_pb, bps)

    @pl.when(first)
    def _():
        s_scr[...] = jnp.zeros_like(s_scr)

    heads = range(GDN_HEADS)
    cs = [slice(h * GDN_DK, (h + 1) * GDN_DK) for h in heads]
    s = [s_scr[h] for h in heads]
    for j in range(SCAN_NB):
        rs = slice(j * CHUNK, (j + 1) * CHUNK)
        s = [jnp.where(is_prompt, s[h], s0_ref[j, h]) for h in heads]
        ws_qs = [jnp.dot(wq_ref[j, :, cs[h]], s[h].astype(BF16), preferred_element_type=F32)
                 for h in heads]
        v_new = [(u_ref[rs, cs[h]] - ws_qs[h][0:CHUNK]).astype(BF16) for h in heads]
        o = [ws_qs[h][CHUNK:2 * CHUNK]
             + jnp.dot(attn_ref[rs, h * CHUNK:(h + 1) * CHUNK], v_new[h], preferred_element_type=F32)
             for h in heads]
        s = [s[h] * egl_ref[j, h:h + 1, :] + _mm_tn(kd_ref[rs, cs[h]], v_new[h]) for h in heads]
        for h in heads:
            o_ref[rs, cs[h]] = (_rms(o[h], nw_ref[...]) * gate_ref[rs, cs[h]]).astype(o_ref.dtype)

        @pl.when(jnp.logical_not(is_prompt))
        def _(j=j, s=s):
            for h in heads:
                souts_ref[j, h] = s[h]

    for h in heads:
        s_scr[h] = s[h]

    @pl.when(last)
    def _():
        for h in heads:
            soutp_ref[0, h] = s[h]


def _gdn_scan(u, wq, kd, attn, egl, gate, s0, nw, *, n_p, cpl, n_s):
    t = u.shape[0]
    rows = SCAN_NB * CHUNK
    bps = cpl // SCAN_NB
    n_pb = n_p * bps
    row_blk = lambda b: (b, 0)
    s_in, s_out, s_shapes = _state_specs(n_p, n_s, n_pb, bps, (GDN_HEADS, GDN_DK, GDN_DV))
    return pl.pallas_call(
        functools.partial(_gdn_scan_body, n_pb=n_pb, bps=bps),
        grid=(t // rows,),
        in_specs=[pl.BlockSpec((rows, D_MODEL), row_blk),
                  pl.BlockSpec((SCAN_NB, 2 * CHUNK, D_MODEL), lambda b: (b, 0, 0)),
                  pl.BlockSpec((rows, D_MODEL), row_blk),
                  pl.BlockSpec((rows, GDN_HEADS * CHUNK), row_blk),
                  pl.BlockSpec((SCAN_NB, GDN_HEADS, LANES), lambda b: (b, 0, 0)),
                  pl.BlockSpec((rows, D_MODEL), row_blk),
                  s_in, _resident((1, GDN_DV))],
        out_specs=[pl.BlockSpec((rows, D_MODEL), row_blk)] + s_out,
        out_shape=[jax.ShapeDtypeStruct((t, D_MODEL), BF16)] + s_shapes,
        scratch_shapes=[pltpu.VMEM((GDN_HEADS, GDN_DK, GDN_DV), F32)],
        compiler_params=_params(), name="gdn_scan")(u, wq, kd, attn, egl, gate, s0, nw)


def _gla_exponent_matrix():
    i = np.arange(CHUNK)[:, None]
    t = np.arange(CHUNK)[None, :]
    blocks = [(t <= i)]
    for s in GLA_LEVELS:
        c = (i // (2 * s)) * (2 * s) + s - 1
        blocks.append(np.where(i > c, (t > c) & (t <= i), (t > i) & (t <= c)))
    w = np.concatenate(blocks, axis=0).astype(np.float32)
    return np.concatenate([w, w, w], axis=1)


def _split3(x):
    hi = x.astype(BF16)
    r1 = x - hi.astype(F32)
    mid = r1.astype(BF16)
    lo = (r1 - mid.astype(F32)).astype(BF16)
    return hi, mid, lo


def _gla_body(p_ref, s0_ref, wg2_ref, bg_ref, wexp_ref, nw_ref,
              o_ref, soutp_ref, souts_ref, st_scr, *, n_pb, bps):
    blk, is_prompt, first, last = _block_flags(n_pb, bps)

    @pl.when(first)
    def _():
        st_scr[...] = jnp.zeros_like(st_scr)

    row = lax.broadcasted_iota(jnp.int32, (CHUNK, CHUNK), 0)
    col = lax.broadcasted_iota(jnp.int32, (CHUNK, CHUNK), 1)
    eye = row == col
    masks = [_pair_mask(s) for s in GLA_LEVELS]
    heads = range(GLA_HEADS)
    ck = [slice(h * GLA_DK, (h + 1) * GLA_DK) for h in heads]
    cv = [slice(h * GLA_DV, (h + 1) * GLA_DV) for h in heads]

    prep = []
    for j in range(SCAN_NB):
        rs = slice(j * CHUNK, (j + 1) * CHUNK)
        x = _mm(p_ref[rs, GLA_G0:GLA_G0 + LANES], wg2_ref[...]) + bg_ref[...]
        glog = -_softplus(-x) * (1.0 / GLA_TAU)
        e_all = jnp.dot(wexp_ref[...], jnp.concatenate(_split3(glog), axis=0),
                        preferred_element_type=F32)
        q_all = p_ref[rs, 0:GLA_QK] * (GLA_DK ** -0.5)
        k_all = p_ref[rs, GLA_QK:2 * GLA_QK]
        qb, kb = q_all.astype(BF16), k_all.astype(BF16)
        attn = [jnp.where(eye, _mm_nt(qb[:, ck[h]], kb[:, ck[h]]), 0.0) for h in heads]
        for lvl, m in enumerate(masks):
            f = jnp.exp(e_all[(lvl + 1) * CHUNK:(lvl + 2) * CHUNK, :])
            qf, kf = (q_all * f).astype(BF16), (k_all * f).astype(BF16)
            attn = [attn[h] + jnp.where(m, _mm_nt(qf[:, ck[h]], kf[:, ck[h]]), 0.0) for h in heads]
        b = e_all[0:CHUNK, :]
        b_last = b[CHUNK - 1:CHUNK, :]
        vb = p_ref[rs, GLA_V0:GLA_V0 + D_MODEL].astype(BF16)
        prep.append(dict(qe=(q_all * jnp.exp(b)).astype(BF16),
                         kd=(k_all * jnp.exp(b_last - b)).astype(BF16),
                         eb_last=jnp.exp(b_last), vb=vb,
                         av=[_mm(attn[h], vb[:, cv[h]]) for h in heads]))

    st = [st_scr[h] for h in heads]
    for j, c in enumerate(prep):
        rs = slice(j * CHUNK, (j + 1) * CHUNK)
        st = [jnp.where(is_prompt, st[h], s0_ref[j, h]) for h in heads]
        o = [_mm_nt(c["qe"][:, ck[h]], st[h]) + c["av"][h] for h in heads]
        st = [st[h] * c["eb_last"][:, ck[h]] + _mm_tn(c["vb"][:, cv[h]], c["kd"][:, ck[h]])
              for h in heads]
        for h in heads:
            r = p_ref[rs, GLA_R0 + h * GLA_DV:GLA_R0 + (h + 1) * GLA_DV]
            o_ref[rs, cv[h]] = (_rms(o[h], nw_ref[h:h + 1, :]) * _silu(r)).astype(o_ref.dtype)

        @pl.when(jnp.logical_not(is_prompt))
        def _(j=j, st=st):
            for h in heads:
                souts_ref[j, h] = st[h]

    for h in heads:
        st_scr[h] = st[h]

    @pl.when(last)
    def _():
        for h in heads:
            soutp_ref[0, h] = st[h]


def _gla_scan(p, s0t, wg2, bg, wexp, nw, *, n_p, cpl, n_s):
    t = p.shape[0]
    rows = SCAN_NB * CHUNK
    bps = cpl // SCAN_NB
    n_pb = n_p * bps
    s_in, s_out, s_shapes = _state_specs(n_p, n_s, n_pb, bps, (GLA_HEADS, GLA_DV, GLA_DK))
    return pl.pallas_call(
        functools.partial(_gla_body, n_pb=n_pb, bps=bps),
        grid=(t // rows,),
        in_specs=[pl.BlockSpec((rows, GLA_P), lambda b: (b, 0)), s_in,
                  _resident((LANES, GLA_QK)), _resident((1, GLA_QK)),
                  _resident(((len(GLA_LEVELS) + 1) * CHUNK, 3 * CHUNK)),
                  _resident((GLA_HEADS, GLA_DV))],
        out_specs=[pl.BlockSpec((rows, D_MODEL), lambda b: (b, 0))] + s_out,
        out_shape=[jax.ShapeDtypeStruct((t, D_MODEL), BF16)] + s_shapes,
        scratch_shapes=[pltpu.VMEM((GLA_HEADS, GLA_DV, GLA_DK), F32)],
        compiler_params=_params(), name="gla_scan")(p, s0t, wg2, bg, wexp, nw)


CNV_NB = 4


def _cnv_body(x_ref, halo_ref, halo0_ref, wdw_ref, bdw_ref, o_ref, xc_scr, *, n_pb, bps, n_p):
    blk = pl.program_id(0)
    is_prompt = blk < n_pb
    win = CONV_HALO + CHUNK
    for j in range(CNV_NB):
        if j == 0:
            first = jnp.logical_or(jnp.logical_not(is_prompt), blk % bps == 0)
            prev = halo_ref[...]
        else:
            first = jnp.logical_not(is_prompt)
            prev = x_ref[j * CHUNK - CONV_HALO:j * CHUNK, :]
        seq = jnp.where(is_prompt, blk // bps, n_p + (blk - n_pb) * CNV_NB + j)
        xc_scr[j * win:j * win + CONV_HALO, :] = jnp.where(first, halo0_ref[seq], prev)
        xc_scr[j * win + CONV_HALO:(j + 1) * win, :] = x_ref[j * CHUNK:(j + 1) * CHUNK, :]
    taps = {}
    for w in range(CONV_W):
        s = CONV_HALO - (CONV_W - 1) + w
        taps.setdefault(-s % 8, []).append((w, s + (-s % 8)))
    for j in range(CNV_NB):
        for c0 in range(0, D_MODEL, LANES):
            cs = slice(c0, c0 + LANES)
            wv = xc_scr[j * win:(j + 1) * win, cs]
            y = None
            for r, group in sorted(taps.items()):
                rolled = wv if r == 0 else pltpu.roll(wv, r, 0)
                for w, a0 in group:
                    term = wdw_ref[w:w + 1, cs] * rolled[a0:a0 + CHUNK]
                    y = term if y is None else y + term
            o_ref[j * CHUNK:(j + 1) * CHUNK, cs] = y + bdw_ref[:, cs]


def _cnv_conv(glu, halo0, wdw, bdw, *, n_p, cpl, n_s):
    t = glu.shape[0]
    rows = CNV_NB * CHUNK
    bps = cpl // CNV_NB
    return pl.pallas_call(
        functools.partial(_cnv_body, n_pb=n_p * bps, bps=bps, n_p=n_p),
        grid=(t // rows,),
        in_specs=[pl.BlockSpec((rows, D_MODEL), lambda b: (b, 0)),
                  pl.BlockSpec((CONV_HALO, D_MODEL),
                               lambda b: (jnp.maximum(b * (rows // CONV_HALO) - 1, 0), 0)),
                  _resident((n_p + n_s, CONV_HALO, D_MODEL)),
                  _resident((CONV_HALO, D_MODEL)), _resident((1, D_MODEL))],
        out_specs=pl.BlockSpec((rows, D_MODEL), lambda b: (b, 0)),
        out_shape=jax.ShapeDtypeStruct((t, D_MODEL), F32),
        scratch_shapes=[pltpu.VMEM((CNV_NB * (CONV_HALO + CHUNK), D_MODEL), F32)],
        compiler_params=_params(), name="cnv_conv")(glu, glu, halo0, wdw, bdw)


def _pad_cols(w, n):
    return jnp.pad(w, ((0, 0), (0, n - w.shape[1])))


def _row(v):
    return v.reshape(1, -1).astype(F32)


def _tail_rows(a, lay, n):
    n_p, cpl, n_s = lay
    ends = np.concatenate([(np.arange(n_p) + 1) * cpl * CHUNK,
                           n_p * cpl * CHUNK + (np.arange(n_s) + 1) * CHUNK])
    idx = (ends[:, None] - n + np.arange(n)[None, :]).reshape(-1)
    return jnp.take(a, jnp.asarray(idx, jnp.int32), axis=0).reshape(n_p + n_s, n, a.shape[1])


def _gdn_mix(x, lay, dims, mix_nw, cache, state, w_in, conv_w, a_log, dt_bias, norm_w):
    n_p, cpl, n_s = lay
    halo0 = jnp.concatenate([jnp.zeros((n_p, 8, GDN_CONV_CH), F32),
                             jnp.pad(cache, ((0, 0), (8 - (GDN_CONV - 1), 0), (0, 0)))], axis=0)
    lane_pad = lambda v: jnp.pad(v.reshape(1, -1), ((0, 0), (GDN_HEADS, LANES - 2 * GDN_HEADS)))
    alog, dtb = lane_pad(a_log), lane_pad(dt_bias)
    k, kbq, vk, qe, kd, gate, bgp, tails = _gdn_in(
        x, _row(mix_nw), _pad_cols(w_in, GDN_P).astype(BF16), conv_w, alog, dtb, halo0,
        tps=cpl * CHUNK // ROW_TILE, n_p=n_p, **dims)
    u, wq, attn, egl = _gdn_pre(k, kbq, vk, qe, bgp, alog, dtb)
    o, s_p, s_s = _gdn_scan(u, wq, kd, attn, egl, gate, state, _row(norm_w),
                            n_p=n_p, cpl=cpl, n_s=n_s)
    ends = np.concatenate([(np.arange(n_p) + 1) * cpl - 1, n_p * cpl + np.arange(n_s)])
    conv_rows = tails[jnp.asarray(ends, jnp.int32), 8 - (GDN_CONV - 1):, :]
    return o, (conv_rows[:n_p], conv_rows[n_p:], s_p, s_s)


def _gla_mix(x, lay, dims, mix_nw, state, w_in, w_g2, b_g, norm_w):
    n_p, cpl, n_s = lay
    p = _proj(x, _row(mix_nw), _pad_cols(w_in, GLA_P).astype(BF16), None,
              n_out=GLA_P, ck=512, name="gla_in", **dims)
    wg2 = jnp.pad(w_g2, ((0, LANES - GLA_RANK), (0, 0))).astype(BF16)
    wexp = jnp.asarray(_gla_exponent_matrix(), BF16)
    o, st_p, st_s = _gla_scan(p, jnp.swapaxes(state, 2, 3), wg2, _row(b_g), wexp, norm_w,
                              n_p=n_p, cpl=cpl, n_s=n_s)
    return o, (jnp.swapaxes(st_p, 2, 3), jnp.swapaxes(st_s, 2, 3))


def _cnv_mix(x, lay, dims, mix_nw, cache, w_pw1, b_pw1, w_dw, b_dw):
    n_p, cpl, n_s = lay
    glu = _proj(x, _row(mix_nw), w_pw1.astype(BF16), _row(b_pw1),
                n_out=D_MODEL, ck=512, epilogue="glu", name="cnv_in", **dims)
    pad_rows = CONV_HALO - (CONV_W - 1)
    halo0 = jnp.concatenate([jnp.zeros((n_p, CONV_HALO, D_MODEL), F32),
                             jnp.pad(cache, ((0, 0), (pad_rows, 0), (0, 0)))], axis=0)
    wdw = jnp.pad(w_dw, ((0, CONV_HALO - CONV_W), (0, 0)))
    y = _cnv_conv(glu, halo0, wdw, _row(b_dw), n_p=n_p, cpl=cpl, n_s=n_s)
    rows = _tail_rows(glu, lay, CONV_W - 1)
    return y, (rows[:n_p], rows[n_p:])


def kernel(x_prompt, x_sample, cache_gdn_conv, state_gdn, state_gla, cache_conformer, mix_norm_w, ffn_norm_w, ffn_w1, ffn_w2, final_norm_w, gdn_w_in, gdn_conv_w, gdn_a_log, gdn_dt_bias, gdn_norm_w, gdn_w_out, gla_w_in, gla_w_g2, gla_b_g, gla_norm_w, gla_w_out, gmlp_w_in, gmlp_b_in, gmlp_ln_w, gmlp_ln_b, gmlp_w_s, gmlp_b_s, gmlp_w_out, gmlp_b_out, cnv_w_pw1, cnv_b_pw1, cnv_w_dw, cnv_b_dw, cnv_ln_w, cnv_ln_b, cnv_w_pw2, cnv_b_pw2):
    n_p, l_p, d = x_prompt.shape
    n_s, l_s, _ = x_sample.shape
    assert d == D_MODEL and l_s == CHUNK and l_p % ROW_TILE == 0
    assert n_s % max(CNV_NB, SCAN_NB, GDN_NB) == 0
    t_p, t_s = n_p * l_p, n_s * l_s
    assert t_p % ROW_TILE == 0 and t_s % ROW_TILE == 0
    lay = (n_p, l_p // CHUNK, n_s)
    dims = dict(t=t_p + t_s, n_pt=t_p // ROW_TILE)
    depth = mix_norm_w.shape[0]

    x = (x_prompt.reshape(t_p, d), x_sample.reshape(t_s, d))
    outs = {k: [] for k in ("gdn_cp", "gdn_cs", "gdn_sp", "gdn_ss", "gla_sp", "gla_ss",
                            "gmlp_vs", "cnv_p", "cnv_s")}
    w1_all, w2_all = ffn_w1.astype(BF16), ffn_w2.astype(BF16)
    for i in range(depth):
        kind, j = i % 4, i // 4
        ffn = (_row(ffn_norm_w[i]), w1_all, w2_all, _row(final_norm_w) if i == depth - 1 else None)
        if kind == 0:
            o, (cp, cs, sp, ss) = _gdn_mix(x, lay, dims, mix_norm_w[i], cache_gdn_conv[j], state_gdn[j],
                                           gdn_w_in[j], gdn_conv_w[j], gdn_a_log[j], gdn_dt_bias[j],
                                           gdn_norm_w[j])
            outs["gdn_cp"].append(cp); outs["gdn_cs"].append(cs)
            outs["gdn_sp"].append(sp); outs["gdn_ss"].append(ss)
            res = _post([o], x, gdn_w_out[j].astype(BF16), None, *ffn,
                        prologue="plain", layer=i, name=f"post_{i}", **dims)
        elif kind == 1:
            o, (sp, ss) = _gla_mix(x, lay, dims, mix_norm_w[i], state_gla[j], gla_w_in[j], gla_w_g2[j],
                                   gla_b_g[j], gla_norm_w[j])
            outs["gla_sp"].append(sp); outs["gla_ss"].append(ss)
            res = _post([o], x, gla_w_out[j].astype(BF16), None, *ffn,
                        prologue="plain", layer=i, name=f"post_{i}", **dims)
        elif kind == 2:
            zz = _proj(x, _row(mix_norm_w[i]), gmlp_w_in[j].astype(BF16), _row(gmlp_b_in[j]),
                       n_out=2 * D_MODEL, ck=512, epilogue="gelu", name="gmlp_in", **dims)
            bs_full = jnp.repeat(gmlp_b_s[j].T, GMLP_GC, axis=1)
            res = _post([zz, _row(gmlp_ln_w[j]), _row(gmlp_ln_b[j]), gmlp_w_s[j], bs_full], x,
                        gmlp_w_out[j].astype(BF16), _row(gmlp_b_out[j]), *ffn,
                        prologue="gmlp", layer=i, name=f"post_{i}", **dims)
            outs["gmlp_vs"].append(res[-1].reshape(n_s, l_s, d))
            res = res[:-1]
        else:
            y, (cp, cs) = _cnv_mix(x, lay, dims, mix_norm_w[i], cache_conformer[j], cnv_w_pw1[j],
                                   cnv_b_pw1[j], cnv_w_dw[j], cnv_b_dw[j])
            outs["cnv_p"].append(cp); outs["cnv_s"].append(cs)
            res = _post([y, _row(cnv_ln_w[j]), _row(cnv_ln_b[j])], x, cnv_w_pw2[j].astype(BF16),
                        _row(cnv_b_pw2[j]), *ffn, prologue="cnv", layer=i, name=f"post_{i}", **dims)
        x = tuple(res) if i == depth - 1 else res[0]
    y_p, y_s = x
    return (y_p.reshape(n_p, l_p, d), y_s.reshape(n_s, l_s, d),
            jnp.stack(outs["gdn_cp"]), jnp.stack(outs["gdn_cs"]),
            jnp.stack(outs["gdn_sp"]), jnp.stack(outs["gdn_ss"]),
            jnp.stack(outs["gla_sp"]), jnp.stack(outs["gla_ss"]),
            jnp.stack(outs["gmlp_vs"]),
            jnp.stack(outs["cnv_p"]), jnp.stack(outs["cnv_s"]))
```

```python
import functools
import math

import numpy as np
import jax
import jax.numpy as jnp
from jax import lax
from jax.experimental import pallas as pl
from jax.experimental.pallas import tpu as pltpu

F32 = jnp.float32
BF16 = jnp.bfloat16

D_MODEL = 1024
D_FF = 4 * D_MODEL
EPS = 1e-6
CHUNK = 64
ROW_TILE = 512
SCAN_NB = 4
LANES = 128
VMEM_LIMIT = 56 * 1024 * 1024

GDN_HEADS, GDN_DK, GDN_DV, GDN_CONV = 8, 128, 128, 4
GDN_QK = GDN_HEADS * GDN_DK
GDN_CONV_CH = 2 * GDN_QK + GDN_HEADS * GDN_DV
GDN_Z0 = GDN_CONV_CH
GDN_BG0 = GDN_CONV_CH + GDN_HEADS * GDN_DV
GDN_P = GDN_BG0 + LANES

GLA_HEADS, GLA_DK, GLA_DV, GLA_RANK, GLA_TAU = 4, 128, 256, 16, 16.0
GLA_QK = GLA_HEADS * GLA_DK
GLA_V0, GLA_R0, GLA_G0 = 2 * GLA_QK, 2 * GLA_QK + D_MODEL, 2 * GLA_QK + 2 * D_MODEL
GLA_P = GLA_G0 + LANES
GLA_LEVELS = (32, 16, 8, 4, 2, 1)

GMLP_BLOCK, GMLP_GROUPS, GMLP_GC = 128, 4, 256
CONV_W = 31
CONV_HALO = 32


def _mm(a, b):
    return jnp.dot(a.astype(BF16), b.astype(BF16), preferred_element_type=F32)


def _mm_nt(a, b):
    return lax.dot_general(a.astype(BF16), b.astype(BF16), (((1,), (1,)), ((), ())),
                           preferred_element_type=F32)


def _mm_tn(a, b):
    return lax.dot_general(a.astype(BF16), b.astype(BF16), (((0,), (0,)), ((), ())),
                           preferred_element_type=F32)


def _rms(x, w):
    return x * lax.rsqrt(jnp.mean(x * x, axis=-1, keepdims=True) + EPS) * w


def _layernorm(x, w, b):
    xc = x - jnp.mean(x, axis=-1, keepdims=True)
    return xc * lax.rsqrt(jnp.mean(xc * xc, axis=-1, keepdims=True) + EPS) * w + b


def _sigmoid(x):
    return jax.nn.sigmoid(x)


def _silu(x):
    h = 0.5 * x
    return h + h * jnp.tanh(h)


def _softplus(x):
    return jnp.maximum(x, 0.0) + jnp.log1p(jnp.exp(-jnp.abs(x)))


def _gelu_tanh(x):
    return 0.5 * x * (1.0 + jnp.tanh(math.sqrt(2.0 / math.pi) * (x + 0.044715 * (x * x * x))))


def _cumsum_rows(x):
    row = lax.broadcasted_iota(jnp.int32, x.shape, 0)
    s = 1
    while s < x.shape[0]:
        x = x + jnp.where(row >= s, pltpu.roll(x, s, 0), 0.0)
        s *= 2
    return x


def _pair_mask(s):
    sh = int(math.log2(s))
    bi = lax.broadcasted_iota(jnp.int32, (CHUNK, CHUNK), 0) >> sh
    bj = lax.broadcasted_iota(jnp.int32, (CHUNK, CHUNK), 1) >> sh
    return jnp.logical_and((bi & 1) == 1, bj == bi - 1)


def _block_flags(n_pb, bps):
    blk = pl.program_id(0)
    is_prompt = blk < n_pb
    first = jnp.logical_and(is_prompt, blk % bps == 0)
    last = jnp.logical_and(is_prompt, blk % bps == bps - 1)
    return blk, is_prompt, first, last


def _state_specs(n_p, n_s, n_pb, bps, shape):
    zeros = (0,) * len(shape)
    sample = pl.BlockSpec((SCAN_NB,) + shape, lambda b: (jnp.maximum(b - n_pb, 0),) + zeros)
    prompt = pl.BlockSpec((1,) + shape, lambda b: (jnp.minimum(b // bps, n_p - 1),) + zeros)
    shapes = [jax.ShapeDtypeStruct((n_p,) + shape, F32), jax.ShapeDtypeStruct((n_s,) + shape, F32)]
    return sample, [prompt, sample], shapes


def _params(n_axes=1):
    return pltpu.CompilerParams(dimension_semantics=("arbitrary",) * n_axes,
                                vmem_limit_bytes=VMEM_LIMIT)


def _resident(shape):
    zeros = (0,) * len(shape)
    return pl.BlockSpec(shape, lambda i: zeros, pipeline_mode=pl.Buffered(1))


def _row_tile_specs(x, n_pt):
    if isinstance(x, tuple):
        return ([pl.BlockSpec((ROW_TILE, D_MODEL), lambda i: (jnp.minimum(i, n_pt - 1), 0)),
                 pl.BlockSpec((ROW_TILE, D_MODEL), lambda i: (jnp.maximum(i - n_pt, 0), 0))], list(x))
    return [pl.BlockSpec((ROW_TILE, D_MODEL), lambda i: (i, 0))], [x]


def _read_row_tile(refs, n_pt):
    if len(refs) == 2:
        return jnp.where(pl.program_id(0) < n_pt, refs[0][...], refs[1][...])
    return refs[0][...]


def _proj_body(*refs, n_x, n_pt, n_out, ck, epilogue, has_bias):
    x_refs, (nw_ref, w_ref), rest = refs[:n_x], refs[n_x:n_x + 2], refs[n_x + 2:]
    if has_bias:
        b_ref, o_ref = rest
    else:
        (o_ref,) = rest
    xn = _rms(_read_row_tile(x_refs, n_pt), nw_ref[...]).astype(BF16)
    for c0 in range(0, n_out, ck):
        sl = slice(c0, min(c0 + ck, n_out))
        y = jnp.dot(xn, w_ref[:, sl], preferred_element_type=F32)
        if has_bias:
            y = y + b_ref[:, sl]
        if epilogue == "gelu":
            y = _gelu_tanh(y)
        o_ref[:, sl] = y.astype(o_ref.dtype)


def _proj(x, nw, w, b, *, t, n_pt, n_out, ck, epilogue="none", out_dtype=F32, name):
    n_w = w.shape[1]
    has_bias = b is not None
    in_specs, args = _row_tile_specs(x, n_pt)
    n_x = len(args)
    in_specs += [_resident((1, D_MODEL)), _resident((D_MODEL, n_w))]
    args += [nw, w]
    if has_bias:
        in_specs.append(_resident((1, n_w)))
        args.append(b)
    return pl.pallas_call(
        functools.partial(_proj_body, n_x=n_x, n_pt=n_pt, n_out=n_out, ck=ck, epilogue=epilogue,
                          has_bias=has_bias),
        grid=(t // ROW_TILE,),
        in_specs=in_specs,
        out_specs=pl.BlockSpec((ROW_TILE, n_out), lambda i: (i, 0)),
        out_shape=jax.ShapeDtypeStruct((t, n_out), out_dtype),
        compiler_params=_params(), name=name)(*args)


FFN_CK = 512


def _gmlp_gate(zz_ref, lnw_ref, lnb_ref, ws_ref, bs_ref, vs_ref, a_scr, is_sample):
    v = _layernorm(zz_ref[:, D_MODEL:2 * D_MODEL], lnw_ref[...], lnb_ref[...])

    @pl.when(is_sample)
    def _():
        vs_ref[...] = v

    vb = v.astype(BF16)
    row = lax.broadcasted_iota(jnp.int32, (GMLP_BLOCK, GMLP_BLOCK), 0)
    col = lax.broadcasted_iota(jnp.int32, (GMLP_BLOCK, GMLP_BLOCK), 1)
    tril = col <= row
    top_left = jnp.logical_and(row < CHUNK, col < CHUNK)
    brow = lax.broadcasted_iota(jnp.int32, (GMLP_BLOCK, D_MODEL), 0)
    bias = bs_ref[...]
    bias = jnp.where(jnp.logical_and(is_sample, brow >= CHUNK), pltpu.roll(bias, CHUNK, 0), bias)
    for grp in range(GMLP_GROUPS):
        cs = slice(grp * GMLP_GC, (grp + 1) * GMLP_GC)
        wt = jnp.where(tril, ws_ref[grp], 0.0)
        w11 = jnp.where(top_left, wt, 0.0)
        wd = w11 + pltpu.roll(pltpu.roll(w11, CHUNK, 0), CHUNK, 1)
        we = jnp.where(is_sample, wd, wt).astype(BF16)
        for blk in range(ROW_TILE // GMLP_BLOCK):
            rs = slice(blk * GMLP_BLOCK, (blk + 1) * GMLP_BLOCK)
            mixed = jnp.dot(we, vb[rs, cs], preferred_element_type=F32)
            a_scr[rs, cs] = (zz_ref[rs, cs] * (mixed + bias[:, cs])).astype(BF16)
    return a_scr[...]


def _post_body(*refs, prologue, n_res, has_bias, final, n_pt):
    n_in = {"plain": 1, "gmlp": 5}[prologue]
    pro, refs = refs[:n_in], refs[n_in:]
    res_refs, refs = refs[:n_res], refs[n_res:]
    wo_ref, refs = refs[0], refs[1:]
    if has_bias:
        bo_ref, refs = refs[0], refs[1:]
    fnw_ref, w1_ref, w2_ref, refs = refs[0], refs[1], refs[2], refs[3:]
    if final:
        finw_ref, refs = refs[0], refs[1:]
    n_main = 2 if final else 1
    outs, refs = refs[:n_main], refs[n_main:]
    is_sample = pl.program_id(0) >= n_pt

    if prologue == "plain":
        a = pro[0][...]
    else:
        vs_ref, a_scr = refs
        a = _gmlp_gate(*pro, vs_ref, a_scr, is_sample)

    x = _read_row_tile(res_refs, n_pt) + jnp.dot(a, wo_ref[...], preferred_element_type=F32)
    if has_bias:
        x = x + bo_ref[...]
    xn = _rms(x, fnw_ref[...]).astype(BF16)
    for c in range(D_FF // FFN_CK):
        sl = slice(c * FFN_CK, (c + 1) * FFN_CK)
        h = jnp.dot(xn, w1_ref[:, sl], preferred_element_type=F32)
        h = jnp.square(jnp.maximum(h, 0.0)).astype(BF16)
        x = x + jnp.dot(h, w2_ref[sl, :], preferred_element_type=F32)
    if final:
        y = _rms(x, finw_ref[...])

        @pl.when(jnp.logical_not(is_sample))
        def _():
            outs[0][...] = y

        @pl.when(is_sample)
        def _():
            outs[1][...] = y
    else:
        outs[0][...] = x


def _post(pro_args, res, wo, bo, fnw, w1, w2, finw, *, layer, prologue, t, n_pt, name):
    has_bias = bo is not None
    final = finw is not None
    tile = lambda c: pl.BlockSpec((ROW_TILE, c), lambda i: (i, 0))
    if prologue == "plain":
        in_specs = [tile(D_MODEL)]
    else:
        in_specs = [tile(2 * D_MODEL), _resident((1, D_MODEL)), _resident((1, D_MODEL)),
                    _resident((GMLP_GROUPS, GMLP_BLOCK, GMLP_BLOCK)),
                    _resident((GMLP_BLOCK, D_MODEL))]
    args = list(pro_args)
    res_specs, res_args = _row_tile_specs(res, n_pt)
    in_specs += res_specs + [_resident((D_MODEL, D_MODEL))]
    args += res_args + [wo]
    if has_bias:
        in_specs.append(_resident((1, D_MODEL)))
        args.append(bo)
    layer_block = lambda shape: pl.BlockSpec((None,) + shape, lambda i: (layer, 0, 0),
                                             pipeline_mode=pl.Buffered(1))
    in_specs += [_resident((1, D_MODEL)), layer_block((D_MODEL, D_FF)), layer_block((D_FF, D_MODEL))]
    args += [fnw, w1, w2]
    if final:
        in_specs.append(_resident((1, D_MODEL)))
        args.append(finw)
        out_specs = [pl.BlockSpec((ROW_TILE, D_MODEL), lambda i: (jnp.minimum(i, n_pt - 1), 0)),
                     pl.BlockSpec((ROW_TILE, D_MODEL), lambda i: (jnp.maximum(i - n_pt, 0), 0))]
        out_shape = [jax.ShapeDtypeStruct((n_pt * ROW_TILE, D_MODEL), F32),
                     jax.ShapeDtypeStruct((t - n_pt * ROW_TILE, D_MODEL), F32)]
    else:
        out_specs = [tile(D_MODEL)]
        out_shape = [jax.ShapeDtypeStruct((t, D_MODEL), F32)]
    scratch = []
    if prologue == "gmlp":
        out_specs.append(pl.BlockSpec((ROW_TILE, D_MODEL), lambda i: (jnp.maximum(i - n_pt, 0), 0)))
        out_shape.append(jax.ShapeDtypeStruct((t - n_pt * ROW_TILE, D_MODEL), F32))
        scratch.append(pltpu.VMEM((ROW_TILE, D_MODEL), BF16))
    return pl.pallas_call(
        functools.partial(_post_body, prologue=prologue, n_res=len(res_args), has_bias=has_bias,
                          final=final, n_pt=n_pt),
        grid=(t // ROW_TILE,),
        in_specs=in_specs, out_specs=out_specs, out_shape=out_shape, scratch_shapes=scratch,
        compiler_params=_params(), name=name)(*args)


GDN_NB = 4
GDN_WIN = CHUNK + 8
GDN_GROUPS = ROW_TILE // CHUNK


def _gdn_gates(bg, alog_ref, dtb_ref):
    glog = -jnp.exp(alog_ref[...]) * _softplus(bg + dtb_ref[...])
    return _sigmoid(bg), _cumsum_rows(glog)


def _gdn_in_body(*refs, n_x, n_pt, tps, n_p):
    x_refs = refs[:n_x]
    (nw_ref, w_ref, cw_ref, alog_ref, dtb_ref, halo0_ref,
     k_ref, kbq_ref, vk_ref, qe_ref, kd_ref, gate_ref, bg_ref, tails_ref,
     xc_scr, carry_scr) = refs[n_x:]
    i = pl.program_id(0)
    is_sample = i >= n_pt

    @pl.when(i == 0)
    def _():
        carry_scr[...] = jnp.zeros_like(carry_scr)

    xn = _rms(_read_row_tile(x_refs, n_pt), nw_ref[...]).astype(BF16)
    bg = jnp.dot(xn, w_ref[:, GDN_BG0:GDN_BG0 + LANES], preferred_element_type=F32)
    bg_ref[...] = bg
    beta, egc, e_rest = [], [], []
    for j in range(GDN_GROUPS):
        b_j, gc = _gdn_gates(bg[j * CHUNK:(j + 1) * CHUNK], alog_ref, dtb_ref)
        beta.append(b_j)
        egc.append(jnp.exp(gc))
        e_rest.append(jnp.exp(gc[CHUNK - 1:CHUNK, :] - gc))

    def project(c0, n):
        cols = slice(c0, c0 + n)
        y = jnp.dot(xn, w_ref[:, cols], preferred_element_type=F32)
        for j in range(GDN_GROUPS):
            r0 = j * GDN_WIN
            xc_scr[r0 + 8:r0 + GDN_WIN, cols] = y[j * CHUNK:(j + 1) * CHUNK]
            if j == 0:
                first = jnp.logical_or(is_sample, i % tps == 0)
                prev = carry_scr[:, cols]
            else:
                first = is_sample
                prev = xc_scr[r0 - 8:r0, cols]
            seq = jnp.where(is_sample, n_p + (i - n_pt) * GDN_GROUPS + j, i // tps)
            xc_scr[r0:r0 + 8, cols] = jnp.where(first, halo0_ref[seq, :, cols], prev)

    def conv_silu(j, c0):
        cw = [cw_ref[w:w + 1, c0:c0 + LANES] for w in range(GDN_CONV)]
        wv = xc_scr[j * GDN_WIN:(j + 1) * GDN_WIN, c0:c0 + LANES]
        back2 = pltpu.roll(wv, 2, 0)
        odd = pltpu.roll(cw[2] * wv + cw[0] * back2, 1, 0)
        return _silu(cw[3] * wv[8:GDN_WIN] + cw[1] * back2[8:GDN_WIN] + odd[8:GDN_WIN])

    def epilogue(h):
        cs = slice(h * GDN_DK, (h + 1) * GDN_DK)
        for j in range(GDN_GROUPS):
            rs = slice(j * CHUNK, (j + 1) * CHUNK)
            q = conv_silu(j, h * GDN_DK)
            k = conv_silu(j, GDN_QK + h * GDN_DK)
            v = conv_silu(j, 2 * GDN_QK + h * GDN_DV)
            q = q * lax.rsqrt(jnp.sum(q * q, axis=-1, keepdims=True) + EPS) * (GDN_DK ** -0.5)
            k = k * lax.rsqrt(jnp.sum(k * k, axis=-1, keepdims=True) + EPS)
            beta_h = beta[j][:, h:h + 1]
            e_h = egc[j][:, 8 + h:9 + h]
            kb = k * beta_h
            k_ref[rs, cs] = k.astype(BF16)
            kbq_ref[j, 0:CHUNK, cs] = kb.astype(BF16)
            kbq_ref[j, CHUNK:2 * CHUNK, cs] = q.astype(BF16)
            vk_ref[rs, 2 * h * GDN_DV:(2 * h + 1) * GDN_DV] = (v * beta_h).astype(BF16)
            vk_ref[rs, (2 * h + 1) * GDN_DV:(2 * h + 2) * GDN_DV] = (kb * e_h).astype(BF16)
            qe_ref[rs, cs] = (q * e_h).astype(BF16)
            kd_ref[rs, cs] = (k * e_rest[j][:, 8 + h:9 + h]).astype(BF16)

    pair = 2 * GDN_DK
    n_pairs = GDN_HEADS // 2
    for m in range(n_pairs + 1):
        if m < n_pairs:
            for part in range(3):
                project(part * GDN_QK + m * pair, pair)
        if m > 0:
            epilogue(2 * m - 2)
            epilogue(2 * m - 1)
    for c in range(2):
        cols = slice(c * FFN_CK, (c + 1) * FFN_CK)
        z = jnp.dot(xn, w_ref[:, GDN_Z0 + c * FFN_CK:GDN_Z0 + (c + 1) * FFN_CK],
                    preferred_element_type=F32)
        gate_ref[:, cols] = _silu(z).astype(BF16)
    for j in range(GDN_GROUPS):
        tails_ref[j] = xc_scr[(j + 1) * GDN_WIN - 8:(j + 1) * GDN_WIN, :]
    carry_scr[...] = xc_scr[GDN_GROUPS * GDN_WIN - 8:GDN_GROUPS * GDN_WIN, :]


def _gdn_in(x, nw, w, cw, alog, dtb, halo0, *, t, n_pt, tps, n_p):
    in_specs, args = _row_tile_specs(x, n_pt)
    n_x = len(args)
    n_seq = halo0.shape[0]
    in_specs += [_resident((1, D_MODEL)), _resident((D_MODEL, GDN_P)),
                 _resident((GDN_CONV, GDN_CONV_CH)), _resident((1, LANES)), _resident((1, LANES)),
                 _resident((n_seq, 8, GDN_CONV_CH))]
    args += [nw, w, cw, alog, dtb, halo0]
    tile = lambda c: pl.BlockSpec((ROW_TILE, c), lambda i: (i, 0))
    n_chunks = t // CHUNK
    bf = lambda c: jax.ShapeDtypeStruct((t, c), BF16)
    return pl.pallas_call(
        functools.partial(_gdn_in_body, n_x=n_x, n_pt=n_pt, tps=tps, n_p=n_p),
        grid=(t // ROW_TILE,),
        in_specs=in_specs,
        out_specs=[tile(D_MODEL),
                   pl.BlockSpec((GDN_GROUPS, 2 * CHUNK, D_MODEL), lambda i: (i, 0, 0)),
                   tile(2 * D_MODEL), tile(D_MODEL), tile(D_MODEL), tile(D_MODEL), tile(LANES),
                   pl.BlockSpec((GDN_GROUPS, 8, GDN_CONV_CH), lambda i: (i, 0, 0))],
        out_shape=[bf(D_MODEL), jax.ShapeDtypeStruct((n_chunks, 2 * CHUNK, D_MODEL), BF16),
                   bf(2 * D_MODEL), bf(D_MODEL), bf(D_MODEL), bf(D_MODEL),
                   jax.ShapeDtypeStruct((t, LANES), F32),
                   jax.ShapeDtypeStruct((n_chunks, 8, GDN_CONV_CH), F32)],
        scratch_shapes=[pltpu.VMEM((GDN_GROUPS * GDN_WIN, GDN_CONV_CH), F32),
                        pltpu.VMEM((8, GDN_CONV_CH), F32)],
        compiler_params=_params(), name="gdn_in")(*args)


def _gdn_pre_body(k_ref, kbq_ref, vk_ref, qe_ref, bg_ref, alog_ref, dtb_ref,
                  u_ref, wq_ref, attn_ref, egl_ref):
    pw = 2 * CHUNK
    row = lax.broadcasted_iota(jnp.int32, (CHUNK, pw), 0)
    lane = lax.broadcasted_iota(jnp.int32, (CHUNK, pw), 1)
    col = lane & (CHUNK - 1)
    left = lane < CHUNK
    incl = row >= col
    strict = row > col
    eye = (row == col).astype(F32)

    def level_mask(s):
        sh = int(math.log2(s))
        bi, bj = row >> sh, col >> sh
        return jnp.logical_and((bi & 1) == 1, bj == bi - 1)

    masks = [level_mask(s) for s in (1, 2, 4, 8, 16, 32)]
    r2 = lax.broadcasted_iota(jnp.int32, (pw, pw), 0)
    l2 = lax.broadcasted_iota(jnp.int32, (pw, pw), 1)
    same_head = (r2 < CHUNK) == (l2 < CHUNK)

    def block_diag(x):
        return jnp.where(same_head, jnp.concatenate([x, x], axis=0), jnp.zeros((pw, pw), x.dtype))

    probs = [(j, m) for j in range(GDN_NB) for m in range(GDN_HEADS // 2)]
    n = len(probs)
    rs = [slice(j * CHUNK, (j + 1) * CHUNK) for j, _ in probs]
    hc = lambda h: slice(h * GDN_DK, (h + 1) * GDN_DK)
    gcum, gcum_t = [], []
    for j in range(GDN_NB):
        _, gc = _gdn_gates(bg_ref[j * CHUNK:(j + 1) * CHUNK, :], alog_ref, dtb_ref)
        gcum.append(gc)
        gt = jnp.transpose(jnp.concatenate([gc, jnp.zeros_like(gc)], axis=0))
        gcum_t.append(gt)
        egl_ref[j] = jnp.broadcast_to(jnp.exp(gt[8:8 + GDN_HEADS, CHUNK - 1:CHUNK]),
                                      (GDN_HEADS, LANES))
    decay = []
    for j, m in probs:
        h0, h1 = 8 + 2 * m, 9 + 2 * m
        g_col = jnp.where(left, gcum[j][:, h0:h0 + 1], gcum[j][:, h1:h1 + 1])
        g_row = jnp.concatenate([gcum_t[j][h0:h0 + 1, 0:CHUNK], gcum_t[j][h1:h1 + 1, 0:CHUNK]], axis=1)
        decay.append(jnp.exp(jnp.minimum(g_col - g_row, 0.0)))
    qk = []
    for i, (j, m) in enumerate(probs):
        k_pair = jnp.concatenate([k_ref[rs[i], hc(2 * m)], k_ref[rs[i], hc(2 * m + 1)]], axis=0)
        r0, r1 = (lax.dot_general(kbq_ref[j, :, hc(h)], k_pair, (((1,), (1,)), ((), ())),
                                  preferred_element_type=F32) for h in (2 * m, 2 * m + 1))
        qk.append(jnp.where(l2 < CHUNK, r0, r1))
    a = [jnp.where(strict, qk[i][0:CHUNK] * decay[i], 0.0) for i in range(n)]
    for i, (j, m) in enumerate(probs):
        attn = jnp.where(incl, qk[i][CHUNK:2 * CHUNK] * decay[i], 0.0)
        attn_ref[rs[i], m * pw:(m + 1) * pw] = attn.astype(attn_ref.dtype)
    t = [eye - jnp.where(masks[0], ai, 0.0) for ai in a]
    a_b = [ai.astype(BF16) for ai in a]
    for lm in masks[1:]:
        t_b = [ti.astype(BF16) for ti in t]
        at = [jnp.dot(jnp.where(lm, a_b[i], jnp.zeros_like(a_b[i])), block_diag(t_b[i]),
                      preferred_element_type=F32) for i in range(n)]
        t = [t[i] - jnp.dot(t_b[i], block_diag(at[i].astype(BF16)), preferred_element_type=F32)
             for i in range(n)]
    zeros = jnp.zeros((CHUNK, 2 * GDN_DV), BF16)
    for i, (j, m) in enumerate(probs):
        v0 = vk_ref[rs[i], 4 * m * GDN_DV:(4 * m + 2) * GDN_DV]
        v1 = vk_ref[rs[i], (4 * m + 2) * GDN_DV:(4 * m + 4) * GDN_DV]
        rhs = jnp.concatenate([jnp.concatenate([v0, zeros], axis=1),
                               jnp.concatenate([zeros, v1], axis=1)], axis=0)
        uw = jnp.dot(t[i].astype(BF16), rhs, preferred_element_type=F32)
        for q, h in enumerate((2 * m, 2 * m + 1)):
            u_ref[rs[i], hc(h)] = uw[:, 2 * q * GDN_DV:(2 * q + 1) * GDN_DV]
            wq_ref[j, 0:CHUNK, hc(h)] = uw[:, (2 * q + 1) * GDN_DV:(2 * q + 2) * GDN_DV].astype(wq_ref.dtype)
            wq_ref[j, CHUNK:2 * CHUNK, hc(h)] = qe_ref[rs[i], hc(h)]


def _gdn_pre(k, kbq, vk, qe, bgp, alog, dtb):
    t = k.shape[0]
    rows = GDN_NB * CHUNK
    n_chunks = t // CHUNK
    tile = lambda c: pl.BlockSpec((rows, c), lambda b: (b, 0))
    chunked = lambda r, c: pl.BlockSpec((GDN_NB, r, c), lambda b: (b, 0, 0))
    return pl.pallas_call(
        _gdn_pre_body,
        grid=(t // rows,),
        in_specs=[tile(D_MODEL), chunked(2 * CHUNK, D_MODEL), tile(2 * D_MODEL), tile(D_MODEL),
                  tile(LANES), _resident((1, LANES)), _resident((1, LANES))],
        out_specs=[tile(D_MODEL), chunked(2 * CHUNK, D_MODEL), tile(GDN_HEADS * CHUNK),
                   chunked(GDN_HEADS, LANES)],
        out_shape=[jax.ShapeDtypeStruct((t, D_MODEL), F32),
                   jax.ShapeDtypeStruct((n_chunks, 2 * CHUNK, D_MODEL), BF16),
                   jax.ShapeDtypeStruct((t, GDN_HEADS * CHUNK), BF16),
                   jax.ShapeDtypeStruct((n_chunks, GDN_HEADS, LANES), F32)],
        compiler_params=_params(), name="gdn_pre")(k, kbq, vk, qe, bgp, alog, dtb)


def _gdn_scan_body(u_ref, wq_ref, kd_ref, attn_ref, egl_ref, gate_ref, s0_ref, nw_ref,
                   o_ref, soutp_ref, souts_ref, s_scr, *, n_pb, bps):
    blk, is_prompt, first, last = _block_flags(n_pb, bps)

    @pl.when(first)
    def _():
        s_scr[...] = jnp.zeros_like(s_scr)

    heads = range(GDN_HEADS)
    cs = [slice(h * GDN_DK, (h + 1) * GDN_DK) for h in heads]
    s = [s_scr[h] for h in heads]
    for j in range(SCAN_NB):
        rs = slice(j * CHUNK, (j + 1) * CHUNK)
        s = [jnp.where(is_prompt, s[h], s0_ref[j, h]) for h in heads]
        ws_qs = [jnp.dot(wq_ref[j, :, cs[h]], s[h].astype(BF16), preferred_element_type=F32)
                 for h in heads]
        v_new = [(u_ref[rs, cs[h]] - ws_qs[h][0:CHUNK]).astype(BF16) for h in heads]
        o = [ws_qs[h][CHUNK:2 * CHUNK]
             + jnp.dot(attn_ref[rs, h * CHUNK:(h + 1) * CHUNK], v_new[h], preferred_element_type=F32)
             for h in heads]
        s = [s[h] * egl_ref[j, h:h + 1, :] + _mm_tn(kd_ref[rs, cs[h]], v_new[h]) for h in heads]
        for h in heads:
            o_ref[rs, cs[h]] = (_rms(o[h], nw_ref[...]) * gate_ref[rs, cs[h]]).astype(o_ref.dtype)

        @pl.when(jnp.logical_not(is_prompt))
        def _(j=j, s=s):
            for h in heads:
                souts_ref[j, h] = s[h]

    for h in heads:
        s_scr[h] = s[h]

    @pl.when(last)
    def _():
        for h in heads:
            soutp_ref[0, h] = s[h]


def _gdn_scan(u, wq, kd, attn, egl, gate, s0, nw, *, n_p, cpl, n_s):
    t = u.shape[0]
    rows = SCAN_NB * CHUNK
    bps = cpl // SCAN_NB
    n_pb = n_p * bps
    row_blk = lambda b: (b, 0)
    s_in, s_out, s_shapes = _state_specs(n_p, n_s, n_pb, bps, (GDN_HEADS, GDN_DK, GDN_DV))
    return pl.pallas_call(
        functools.partial(_gdn_scan_body, n_pb=n_pb, bps=bps),
        grid=(t // rows,),
        in_specs=[pl.BlockSpec((rows, D_MODEL), row_blk),
                  pl.BlockSpec((SCAN_NB, 2 * CHUNK, D_MODEL), lambda b: (b, 0, 0)),
                  pl.BlockSpec((rows, D_MODEL), row_blk),
                  pl.BlockSpec((rows, GDN_HEADS * CHUNK), row_blk),
                  pl.BlockSpec((SCAN_NB, GDN_HEADS, LANES), lambda b: (b, 0, 0)),
                  pl.BlockSpec((rows, D_MODEL), row_blk),
                  s_in, _resident((1, GDN_DV))],
        out_specs=[pl.BlockSpec((rows, D_MODEL), row_blk)] + s_out,
        out_shape=[jax.ShapeDtypeStruct((t, D_MODEL), BF16)] + s_shapes,
        scratch_shapes=[pltpu.VMEM((GDN_HEADS, GDN_DK, GDN_DV), F32)],
        compiler_params=_params(), name="gdn_scan")(u, wq, kd, attn, egl, gate, s0, nw)


def _gla_exponent_matrix():
    i = np.arange(CHUNK)[:, None]
    t = np.arange(CHUNK)[None, :]
    blocks = [(t <= i)]
    for s in GLA_LEVELS:
        c = (i // (2 * s)) * (2 * s) + s - 1
        blocks.append(np.where(i > c, (t > c) & (t <= i), (t > i) & (t <= c)))
    w = np.concatenate(blocks, axis=0).astype(np.float32)
    return np.concatenate([w, w, w], axis=1)


def _split3(x):
    hi = x.astype(BF16)
    r1 = x - hi.astype(F32)
    mid = r1.astype(BF16)
    lo = (r1 - mid.astype(F32)).astype(BF16)
    return hi, mid, lo


def _gla_body(p_ref, s0_ref, wg2_ref, bg_ref, wexp_ref, nw_ref,
              o_ref, soutp_ref, souts_ref, st_scr, *, n_pb, bps):
    blk, is_prompt, first, last = _block_flags(n_pb, bps)

    @pl.when(first)
    def _():
        st_scr[...] = jnp.zeros_like(st_scr)

    row = lax.broadcasted_iota(jnp.int32, (CHUNK, CHUNK), 0)
    col = lax.broadcasted_iota(jnp.int32, (CHUNK, CHUNK), 1)
    eye = row == col
    masks = [_pair_mask(s) for s in GLA_LEVELS]
    heads = range(GLA_HEADS)
    chunks = range(SCAN_NB)
    rs = [slice(j * CHUNK, (j + 1) * CHUNK) for j in chunks]
    ck = [slice(h * GLA_DK, (h + 1) * GLA_DK) for h in heads]
    cv = [slice(h * GLA_DV, (h + 1) * GLA_DV) for h in heads]

    x = [_mm(p_ref[rs[j], GLA_G0:GLA_G0 + LANES], wg2_ref[...]) + bg_ref[...] for j in chunks]
    glog = [-_softplus(-x[j]) * (1.0 / GLA_TAU) for j in chunks]
    e_all = [jnp.dot(wexp_ref[...], jnp.concatenate(_split3(glog[j]), axis=0),
                     preferred_element_type=F32) for j in chunks]
    q_all = [p_ref[rs[j], 0:GLA_QK] * (GLA_DK ** -0.5) for j in chunks]
    k_all = [p_ref[rs[j], GLA_QK:2 * GLA_QK] for j in chunks]
    qb = [q.astype(BF16) for q in q_all]
    kb = [k.astype(BF16) for k in k_all]
    attn = [[jnp.where(eye, _mm_nt(qb[j][:, ck[h]], kb[j][:, ck[h]]), 0.0) for h in heads]
            for j in chunks]
    for lvl, m in enumerate(masks):
        f = [jnp.exp(e_all[j][(lvl + 1) * CHUNK:(lvl + 2) * CHUNK, :]) for j in chunks]
        qf = [(q_all[j] * f[j]).astype(BF16) for j in chunks]
        kf = [(k_all[j] * f[j]).astype(BF16) for j in chunks]
        attn = [[attn[j][h] + jnp.where(m, _mm_nt(qf[j][:, ck[h]], kf[j][:, ck[h]]), 0.0)
                 for h in heads] for j in chunks]
    b = [e_all[j][0:CHUNK, :] for j in chunks]
    b_last = [b[j][CHUNK - 1:CHUNK, :] for j in chunks]
    qe = [(q_all[j] * jnp.exp(b[j])).astype(BF16) for j in chunks]
    kd = [(k_all[j] * jnp.exp(b_last[j] - b[j])).astype(BF16) for j in chunks]
    eb_last = [jnp.exp(b_last[j]) for j in chunks]
    vb = [p_ref[rs[j], GLA_V0:GLA_V0 + D_MODEL].astype(BF16) for j in chunks]
    av = [[_mm(attn[j][h], vb[j][:, cv[h]]) for h in heads] for j in chunks]
    kv = [[_mm_tn(vb[j][:, cv[h]], kd[j][:, ck[h]]) for h in heads] for j in chunks]

    st = [st_scr[h] for h in heads]
    for j in chunks:
        st = [jnp.where(is_prompt, st[h], s0_ref[j, h]) for h in heads]
        o = [_mm_nt(qe[j][:, ck[h]], st[h]) + av[j][h] for h in heads]
        st = [st[h] * eb_last[j][:, ck[h]] + kv[j][h] for h in heads]
        for h in heads:
            r = p_ref[rs[j], GLA_R0 + h * GLA_DV:GLA_R0 + (h + 1) * GLA_DV]
            o_ref[rs[j], cv[h]] = (_rms(o[h], nw_ref[h:h + 1, :]) * _silu(r)).astype(o_ref.dtype)

        @pl.when(jnp.logical_not(is_prompt))
        def _(j=j, st=st):
            for h in heads:
                souts_ref[j, h] = st[h]

    for h in heads:
        st_scr[h] = st[h]

    @pl.when(last)
    def _():
        for h in heads:
            soutp_ref[0, h] = st[h]


def _gla_scan(p, s0t, wg2, bg, wexp, nw, *, n_p, cpl, n_s):
    t = p.shape[0]
    rows = SCAN_NB * CHUNK
    bps = cpl // SCAN_NB
    n_pb = n_p * bps
    s_in, s_out, s_shapes = _state_specs(n_p, n_s, n_pb, bps, (GLA_HEADS, GLA_DV, GLA_DK))
    return pl.pallas_call(
        functools.partial(_gla_body, n_pb=n_pb, bps=bps),
        grid=(t // rows,),
        in_specs=[pl.BlockSpec((rows, GLA_P), lambda b: (b, 0)), s_in,
                  _resident((LANES, GLA_QK)), _resident((1, GLA_QK)),
                  _resident(((len(GLA_LEVELS) + 1) * CHUNK, 3 * CHUNK)),
                  _resident((GLA_HEADS, GLA_DV))],
        out_specs=[pl.BlockSpec((rows, D_MODEL), lambda b: (b, 0))] + s_out,
        out_shape=[jax.ShapeDtypeStruct((t, D_MODEL), BF16)] + s_shapes,
        scratch_shapes=[pltpu.VMEM((GLA_HEADS, GLA_DV, GLA_DK), F32)],
        compiler_params=_params(), name="gla_scan")(p, s0t, wg2, bg, wexp, nw)


CNV_WIN = CONV_HALO + CHUNK
TILE_CHUNKS = ROW_TILE // CHUNK
CNV_CK = 256


def _cnv_in_body(*refs, n_x, n_pt, tps, n_p):
    x_refs = refs[:n_x]
    (nw_ref, w_ref, b_ref, wdw_ref, bdw_ref, lnw_ref, lnb_ref, halo0_ref,
     a_ref, tails_ref, xc_scr, y_scr, carry_scr) = refs[n_x:]
    i = pl.program_id(0)
    is_sample = i >= n_pt

    @pl.when(i == 0)
    def _():
        carry_scr[...] = jnp.zeros_like(carry_scr)

    xn = _rms(_read_row_tile(x_refs, n_pt), nw_ref[...]).astype(BF16)
    taps = {}
    for w in range(CONV_W):
        s = CONV_HALO - (CONV_W - 1) + w
        taps.setdefault(-s % 8, []).append((w, s + (-s % 8)))

    def glu_slab(c0):
        cols = slice(c0, c0 + CNV_CK)
        gcols = slice(D_MODEL + c0, D_MODEL + c0 + CNV_CK)
        val = jnp.dot(xn, w_ref[:, cols], preferred_element_type=F32) + b_ref[:, cols]
        gate = jnp.dot(xn, w_ref[:, gcols], preferred_element_type=F32) + b_ref[:, gcols]
        glu = val * _sigmoid(gate)
        for j in range(TILE_CHUNKS):
            r0 = j * CNV_WIN
            xc_scr[r0 + CONV_HALO:r0 + CNV_WIN, cols] = glu[j * CHUNK:(j + 1) * CHUNK]
            if j == 0:
                first = jnp.logical_or(is_sample, i % tps == 0)
                prev = carry_scr[:, cols]
            else:
                first = is_sample
                prev = xc_scr[r0 - CONV_HALO:r0, cols]
            seq = jnp.where(is_sample, n_p + (i - n_pt) * TILE_CHUNKS + j, i // tps)
            xc_scr[r0:r0 + CONV_HALO, cols] = jnp.where(first, halo0_ref[seq, :, cols], prev)

    def conv_slab(c0):
        for j in range(TILE_CHUNKS):
            for b0 in range(c0, c0 + CNV_CK, LANES):
                cs = slice(b0, b0 + LANES)
                wv = xc_scr[j * CNV_WIN:(j + 1) * CNV_WIN, cs]
                y = None
                for r, group in sorted(taps.items()):
                    rolled = wv if r == 0 else pltpu.roll(wv, r, 0)
                    for w, a0 in group:
                        term = wdw_ref[w:w + 1, cs] * rolled[a0:a0 + CHUNK]
                        y = term if y is None else y + term
                y_scr[j * CHUNK:(j + 1) * CHUNK, cs] = y + bdw_ref[:, cs]

    slabs = list(range(0, D_MODEL, CNV_CK))
    for n, c0 in enumerate(slabs + [None]):
        if c0 is not None:
            glu_slab(c0)
        if n > 0:
            conv_slab(slabs[n - 1])
    for j in range(TILE_CHUNKS):
        rs = slice(j * CHUNK, (j + 1) * CHUNK)
        a_ref[rs, :] = _silu(_layernorm(y_scr[rs, :], lnw_ref[...], lnb_ref[...])).astype(BF16)
        tails_ref[j] = xc_scr[(j + 1) * CNV_WIN - CONV_HALO:(j + 1) * CNV_WIN, :]
    carry_scr[...] = xc_scr[TILE_CHUNKS * CNV_WIN - CONV_HALO:TILE_CHUNKS * CNV_WIN, :]


def _cnv_in(x, nw, w, b, wdw, bdw, lnw, lnb, halo0, *, t, n_pt, tps, n_p):
    in_specs, args = _row_tile_specs(x, n_pt)
    n_x = len(args)
    in_specs += [_resident((1, D_MODEL)), _resident((D_MODEL, 2 * D_MODEL)), _resident((1, 2 * D_MODEL)),
                 _resident((CONV_HALO, D_MODEL)), _resident((1, D_MODEL)),
                 _resident((1, D_MODEL)), _resident((1, D_MODEL)), _resident(halo0.shape)]
    args += [nw, w, b, wdw, bdw, lnw, lnb, halo0]
    return pl.pallas_call(
        functools.partial(_cnv_in_body, n_x=n_x, n_pt=n_pt, tps=tps, n_p=n_p),
        grid=(t // ROW_TILE,),
        in_specs=in_specs,
        out_specs=[pl.BlockSpec((ROW_TILE, D_MODEL), lambda i: (i, 0)),
                   pl.BlockSpec((TILE_CHUNKS, CONV_HALO, D_MODEL), lambda i: (i, 0, 0))],
        out_shape=[jax.ShapeDtypeStruct((t, D_MODEL), BF16),
                   jax.ShapeDtypeStruct((t // CHUNK, CONV_HALO, D_MODEL), F32)],
        scratch_shapes=[pltpu.VMEM((TILE_CHUNKS * CNV_WIN, D_MODEL), F32),
                        pltpu.VMEM((ROW_TILE, D_MODEL), F32),
                        pltpu.VMEM((CONV_HALO, D_MODEL), F32)],
        compiler_params=_params(), name="cnv_in")(*args)


def _pad_cols(w, n):
    return jnp.pad(w, ((0, 0), (0, n - w.shape[1])))


def _row(v):
    return v.reshape(1, -1).astype(F32)


def _tail_rows(a, lay, n):
    n_p, cpl, n_s = lay
    ends = np.concatenate([(np.arange(n_p) + 1) * cpl * CHUNK,
                           n_p * cpl * CHUNK + (np.arange(n_s) + 1) * CHUNK])
    idx = (ends[:, None] - n + np.arange(n)[None, :]).reshape(-1)
    return jnp.take(a, jnp.asarray(idx, jnp.int32), axis=0).reshape(n_p + n_s, n, a.shape[1])


def _gdn_mix(x, lay, dims, mix_nw, cache, state, w_in, conv_w, a_log, dt_bias, norm_w):
    n_p, cpl, n_s = lay
    halo0 = jnp.concatenate([jnp.zeros((n_p, 8, GDN_CONV_CH), F32),
                             jnp.pad(cache, ((0, 0), (8 - (GDN_CONV - 1), 0), (0, 0)))], axis=0)
    lane_pad = lambda v: jnp.pad(v.reshape(1, -1), ((0, 0), (GDN_HEADS, LANES - 2 * GDN_HEADS)))
    alog, dtb = lane_pad(a_log), lane_pad(dt_bias)
    k, kbq, vk, qe, kd, gate, bgp, tails = _gdn_in(
        x, _row(mix_nw), _pad_cols(w_in, GDN_P).astype(BF16), conv_w, alog, dtb, halo0,
        tps=cpl * CHUNK // ROW_TILE, n_p=n_p, **dims)
    u, wq, attn, egl = _gdn_pre(k, kbq, vk, qe, bgp, alog, dtb)
    o, s_p, s_s = _gdn_scan(u, wq, kd, attn, egl, gate, state, _row(norm_w),
                            n_p=n_p, cpl=cpl, n_s=n_s)
    ends = np.concatenate([(np.arange(n_p) + 1) * cpl - 1, n_p * cpl + np.arange(n_s)])
    conv_rows = tails[jnp.asarray(ends, jnp.int32), 8 - (GDN_CONV - 1):, :]
    return o, (conv_rows[:n_p], conv_rows[n_p:], s_p, s_s)


def _gla_mix(x, lay, dims, mix_nw, state, w_in, w_g2, b_g, norm_w):
    n_p, cpl, n_s = lay
    p = _proj(x, _row(mix_nw), _pad_cols(w_in, GLA_P).astype(BF16), None,
              n_out=GLA_P, ck=512, name="gla_in", **dims)
    wg2 = jnp.pad(w_g2, ((0, LANES - GLA_RANK), (0, 0))).astype(BF16)
    wexp = jnp.asarray(_gla_exponent_matrix(), BF16)
    o, st_p, st_s = _gla_scan(p, jnp.swapaxes(state, 2, 3), wg2, _row(b_g), wexp, norm_w,
                              n_p=n_p, cpl=cpl, n_s=n_s)
    return o, (jnp.swapaxes(st_p, 2, 3), jnp.swapaxes(st_s, 2, 3))


def _cnv_mix(x, lay, dims, mix_nw, cache, w_pw1, b_pw1, w_dw, b_dw, ln_w, ln_b):
    n_p, cpl, n_s = lay
    pad_rows = CONV_HALO - (CONV_W - 1)
    halo0 = jnp.concatenate([jnp.zeros((n_p, CONV_HALO, D_MODEL), F32),
                             jnp.pad(cache, ((0, 0), (pad_rows, 0), (0, 0)))], axis=0)
    wdw = jnp.pad(w_dw, ((0, CONV_HALO - CONV_W), (0, 0)))
    a, tails = _cnv_in(x, _row(mix_nw), w_pw1.astype(BF16), _row(b_pw1), wdw, _row(b_dw),
                       _row(ln_w), _row(ln_b), halo0, tps=cpl * CHUNK // ROW_TILE, n_p=n_p, **dims)
    ends = np.concatenate([(np.arange(n_p) + 1) * cpl - 1, n_p * cpl + np.arange(n_s)])
    rows = tails[jnp.asarray(ends, jnp.int32), pad_rows:, :]
    return a, (rows[:n_p], rows[n_p:])


def kernel(x_prompt, x_sample, cache_gdn_conv, state_gdn, state_gla, cache_conformer, mix_norm_w, ffn_norm_w, ffn_w1, ffn_w2, final_norm_w, gdn_w_in, gdn_conv_w, gdn_a_log, gdn_dt_bias, gdn_norm_w, gdn_w_out, gla_w_in, gla_w_g2, gla_b_g, gla_norm_w, gla_w_out, gmlp_w_in, gmlp_b_in, gmlp_ln_w, gmlp_ln_b, gmlp_w_s, gmlp_b_s, gmlp_w_out, gmlp_b_out, cnv_w_pw1, cnv_b_pw1, cnv_w_dw, cnv_b_dw, cnv_ln_w, cnv_ln_b, cnv_w_pw2, cnv_b_pw2):
    n_p, l_p, d = x_prompt.shape
    n_s, l_s, _ = x_sample.shape
    assert d == D_MODEL and l_s == CHUNK and l_p % ROW_TILE == 0
    assert n_s % max(SCAN_NB, GDN_NB) == 0
    t_p, t_s = n_p * l_p, n_s * l_s
    assert t_p % ROW_TILE == 0 and t_s % ROW_TILE == 0
    lay = (n_p, l_p // CHUNK, n_s)
    dims = dict(t=t_p + t_s, n_pt=t_p // ROW_TILE)
    depth = mix_norm_w.shape[0]

    x = (x_prompt.reshape(t_p, d), x_sample.reshape(t_s, d))
    outs = {k: [] for k in ("gdn_cp", "gdn_cs", "gdn_sp", "gdn_ss", "gla_sp", "gla_ss",
                            "gmlp_vs", "cnv_p", "cnv_s")}
    w1_all, w2_all = ffn_w1.astype(BF16), ffn_w2.astype(BF16)
    for i in range(depth):
        kind, j = i % 4, i // 4
        ffn = (_row(ffn_norm_w[i]), w1_all, w2_all, _row(final_norm_w) if i == depth - 1 else None)
        if kind == 0:
            o, (cp, cs, sp, ss) = _gdn_mix(x, lay, dims, mix_norm_w[i], cache_gdn_conv[j], state_gdn[j],
                                           gdn_w_in[j], gdn_conv_w[j], gdn_a_log[j], gdn_dt_bias[j],
                                           gdn_norm_w[j])
            outs["gdn_cp"].append(cp); outs["gdn_cs"].append(cs)
            outs["gdn_sp"].append(sp); outs["gdn_ss"].append(ss)
            res = _post([o], x, gdn_w_out[j].astype(BF16), None, *ffn,
                        prologue="plain", layer=i, name=f"post_{i}", **dims)
        elif kind == 1:
            o, (sp, ss) = _gla_mix(x, lay, dims, mix_norm_w[i], state_gla[j], gla_w_in[j], gla_w_g2[j],
                                   gla_b_g[j], gla_norm_w[j])
            outs["gla_sp"].append(sp); outs["gla_ss"].append(ss)
            res = _post([o], x, gla_w_out[j].astype(BF16), None, *ffn,
                        prologue="plain", layer=i, name=f"post_{i}", **dims)
        elif kind == 2:
            zz = _proj(x, _row(mix_norm_w[i]), gmlp_w_in[j].astype(BF16), _row(gmlp_b_in[j]),
                       n_out=2 * D_MODEL, ck=512, epilogue="gelu", name="gmlp_in", **dims)
            bs_full = jnp.repeat(gmlp_b_s[j].T, GMLP_GC, axis=1)
            res = _post([zz, _row(gmlp_ln_w[j]), _row(gmlp_ln_b[j]), gmlp_w_s[j], bs_full], x,
                        gmlp_w_out[j].astype(BF16), _row(gmlp_b_out[j]), *ffn,
                        prologue="gmlp", layer=i, name=f"post_{i}", **dims)
            outs["gmlp_vs"].append(res[-1].reshape(n_s, l_s, d))
            res = res[:-1]
        else:
            a, (cp, cs) = _cnv_mix(x, lay, dims, mix_norm_w[i], cache_conformer[j], cnv_w_pw1[j],
                                   cnv_b_pw1[j], cnv_w_dw[j], cnv_b_dw[j], cnv_ln_w[j], cnv_ln_b[j])
            outs["cnv_p"].append(cp); outs["cnv_s"].append(cs)
            res = _post([a], x, cnv_w_pw2[j].astype(BF16), _row(cnv_b_pw2[j]), *ffn,
                        prologue="plain", layer=i, name=f"post_{i}", **dims)
        x = tuple(res) if i == depth - 1 else res[0]
    y_p, y_s = x
    return (y_p.reshape(n_p, l_p, d), y_s.reshape(n_s, l_s, d),
            jnp.stack(outs["gdn_cp"]), jnp.stack(outs["gdn_cs"]),
            jnp.stack(outs["gdn_sp"]), jnp.stack(outs["gdn_ss"]),
            jnp.stack(outs["gla_sp"]), jnp.stack(outs["gla_ss"]),
            jnp.stack(outs["gmlp_vs"]),
            jnp.stack(outs["cnv_p"]), jnp.stack(outs["cnv_s"]))
```

```python
import functools
import math

import numpy as np
import jax
import jax.numpy as jnp
from jax import lax
from jax.experimental import pallas as pl
from jax.experimental.pallas import tpu as pltpu

F32 = jnp.float32
BF16 = jnp.bfloat16

D_MODEL = 1024
D_FF = 4 * D_MODEL
EPS = 1e-6
CHUNK = 64
ROW_TILE = 512
SCAN_NB = 4
LANES = 128
VMEM_LIMIT = 56 * 1024 * 1024

GDN_HEADS, GDN_DK, GDN_DV, GDN_CONV = 8, 128, 128, 4
GDN_QK = GDN_HEADS * GDN_DK
GDN_CONV_CH = 2 * GDN_QK + GDN_HEADS * GDN_DV
GDN_Z0 = GDN_CONV_CH
GDN_BG0 = GDN_CONV_CH + GDN_HEADS * GDN_DV
GDN_P = GDN_BG0 + LANES

GLA_HEADS, GLA_DK, GLA_DV, GLA_RANK, GLA_TAU = 4, 128, 256, 16, 16.0
GLA_QK = GLA_HEADS * GLA_DK
GLA_V0, GLA_R0, GLA_G0 = 2 * GLA_QK, 2 * GLA_QK + D_MODEL, 2 * GLA_QK + 2 * D_MODEL
GLA_P = GLA_G0 + LANES
GLA_LEVELS = (32, 16, 8, 4, 2, 1)

GMLP_BLOCK, GMLP_GROUPS, GMLP_GC = 128, 4, 256
CONV_W = 31
CONV_HALO = 32


def _mm(a, b):
    return jnp.dot(a.astype(BF16), b.astype(BF16), preferred_element_type=F32)


def _mm_nt(a, b):
    return lax.dot_general(a.astype(BF16), b.astype(BF16), (((1,), (1,)), ((), ())),
                           preferred_element_type=F32)


def _mm_tn(a, b):
    return lax.dot_general(a.astype(BF16), b.astype(BF16), (((0,), (0,)), ((), ())),
                           preferred_element_type=F32)


def _rms(x, w):
    return x * lax.rsqrt(jnp.mean(x * x, axis=-1, keepdims=True) + EPS) * w


def _layernorm(x, w, b):
    xc = x - jnp.mean(x, axis=-1, keepdims=True)
    return xc * lax.rsqrt(jnp.mean(xc * xc, axis=-1, keepdims=True) + EPS) * w + b


def _sigmoid(x):
    return jax.nn.sigmoid(x)


def _silu(x):
    h = 0.5 * x
    return h + h * jnp.tanh(h)


def _softplus(x):
    return jnp.maximum(x, 0.0) + jnp.log1p(jnp.exp(-jnp.abs(x)))


def _gelu_tanh(x):
    return 0.5 * x * (1.0 + jnp.tanh(math.sqrt(2.0 / math.pi) * (x + 0.044715 * (x * x * x))))


def _cumsum_rows(x):
    row = lax.broadcasted_iota(jnp.int32, x.shape, 0)
    s = 1
    while s < x.shape[0]:
        x = x + jnp.where(row >= s, pltpu.roll(x, s, 0), 0.0)
        s *= 2
    return x


def _pair_mask(s):
    sh = int(math.log2(s))
    bi = lax.broadcasted_iota(jnp.int32, (CHUNK, CHUNK), 0) >> sh
    bj = lax.broadcasted_iota(jnp.int32, (CHUNK, CHUNK), 1) >> sh
    return jnp.logical_and((bi & 1) == 1, bj == bi - 1)


def _block_flags(n_pb, bps):
    blk = pl.program_id(0)
    is_prompt = blk < n_pb
    first = jnp.logical_and(is_prompt, blk % bps == 0)
    last = jnp.logical_and(is_prompt, blk % bps == bps - 1)
    return blk, is_prompt, first, last


def _state_specs(n_p, n_s, n_pb, bps, shape):
    zeros = (0,) * len(shape)
    sample = pl.BlockSpec((SCAN_NB,) + shape, lambda b: (jnp.maximum(b - n_pb, 0),) + zeros)
    prompt = pl.BlockSpec((1,) + shape, lambda b: (jnp.minimum(b // bps, n_p - 1),) + zeros)
    shapes = [jax.ShapeDtypeStruct((n_p,) + shape, F32), jax.ShapeDtypeStruct((n_s,) + shape, F32)]
    return sample, [prompt, sample], shapes


def _params(n_axes=1):
    return pltpu.CompilerParams(dimension_semantics=("arbitrary",) * n_axes,
                                vmem_limit_bytes=VMEM_LIMIT)


def _resident(shape):
    zeros = (0,) * len(shape)
    return pl.BlockSpec(shape, lambda i: zeros, pipeline_mode=pl.Buffered(1))


def _row_tile_specs(x, n_pt):
    if isinstance(x, tuple):
        return ([pl.BlockSpec((ROW_TILE, D_MODEL), lambda i: (jnp.minimum(i, n_pt - 1), 0)),
                 pl.BlockSpec((ROW_TILE, D_MODEL), lambda i: (jnp.maximum(i - n_pt, 0), 0))], list(x))
    return [pl.BlockSpec((ROW_TILE, D_MODEL), lambda i: (i, 0))], [x]


def _read_row_tile(refs, n_pt):
    if len(refs) == 2:
        return jnp.where(pl.program_id(0) < n_pt, refs[0][...], refs[1][...])
    return refs[0][...]


def _proj_body(*refs, n_x, n_pt, n_out, ck, epilogue, has_bias):
    x_refs, (nw_ref, w_ref), rest = refs[:n_x], refs[n_x:n_x + 2], refs[n_x + 2:]
    if has_bias:
        b_ref, o_ref = rest
    else:
        (o_ref,) = rest
    xn = _rms(_read_row_tile(x_refs, n_pt), nw_ref[...]).astype(BF16)
    for c0 in range(0, n_out, ck):
        sl = slice(c0, min(c0 + ck, n_out))
        y = jnp.dot(xn, w_ref[:, sl], preferred_element_type=F32)
        if has_bias:
            y = y + b_ref[:, sl]
        if epilogue == "gelu":
            y = _gelu_tanh(y)
        o_ref[:, sl] = y.astype(o_ref.dtype)


def _proj(x, nw, w, b, *, t, n_pt, n_out, ck, epilogue="none", out_dtype=F32, name):
    n_w = w.shape[1]
    has_bias = b is not None
    in_specs, args = _row_tile_specs(x, n_pt)
    n_x = len(args)
    in_specs += [_resident((1, D_MODEL)), _resident((D_MODEL, n_w))]
    args += [nw, w]
    if has_bias:
        in_specs.append(_resident((1, n_w)))
        args.append(b)
    return pl.pallas_call(
        functools.partial(_proj_body, n_x=n_x, n_pt=n_pt, n_out=n_out, ck=ck, epilogue=epilogue,
                          has_bias=has_bias),
        grid=(t // ROW_TILE,),
        in_specs=in_specs,
        out_specs=pl.BlockSpec((ROW_TILE, n_out), lambda i: (i, 0)),
        out_shape=jax.ShapeDtypeStruct((t, n_out), out_dtype),
        compiler_params=_params(), name=name)(*args)


FFN_CK = 512


def _gmlp_gate(zz_ref, lnw_ref, lnb_ref, ws_ref, bs_ref, vs_ref, a_scr, is_sample):
    v = _layernorm(zz_ref[:, D_MODEL:2 * D_MODEL], lnw_ref[...], lnb_ref[...])

    @pl.when(is_sample)
    def _():
        vs_ref[...] = v

    vb = v.astype(BF16)
    row = lax.broadcasted_iota(jnp.int32, (GMLP_BLOCK, GMLP_BLOCK), 0)
    col = lax.broadcasted_iota(jnp.int32, (GMLP_BLOCK, GMLP_BLOCK), 1)
    tril = col <= row
    top_left = jnp.logical_and(row < CHUNK, col < CHUNK)
    brow = lax.broadcasted_iota(jnp.int32, (GMLP_BLOCK, D_MODEL), 0)
    bias = bs_ref[...]
    bias = jnp.where(jnp.logical_and(is_sample, brow >= CHUNK), pltpu.roll(bias, CHUNK, 0), bias)
    for grp in range(GMLP_GROUPS):
        cs = slice(grp * GMLP_GC, (grp + 1) * GMLP_GC)
        wt = jnp.where(tril, ws_ref[grp], 0.0)
        w11 = jnp.where(top_left, wt, 0.0)
        wd = w11 + pltpu.roll(pltpu.roll(w11, CHUNK, 0), CHUNK, 1)
        we = jnp.where(is_sample, wd, wt).astype(BF16)
        for blk in range(ROW_TILE // GMLP_BLOCK):
            rs = slice(blk * GMLP_BLOCK, (blk + 1) * GMLP_BLOCK)
            mixed = jnp.dot(we, vb[rs, cs], preferred_element_type=F32)
            a_scr[rs, cs] = (zz_ref[rs, cs] * (mixed + bias[:, cs])).astype(BF16)
    return a_scr[...]


def _post_body(*refs, prologue, n_res, has_bias, final, n_pt):
    n_in = {"plain": 1, "gmlp": 5}[prologue]
    pro, refs = refs[:n_in], refs[n_in:]
    res_refs, refs = refs[:n_res], refs[n_res:]
    wo_ref, refs = refs[0], refs[1:]
    if has_bias:
        bo_ref, refs = refs[0], refs[1:]
    fnw_ref, w1_ref, w2_ref, refs = refs[0], refs[1], refs[2], refs[3:]
    if final:
        finw_ref, refs = refs[0], refs[1:]
    n_main = 2 if final else 1
    outs, refs = refs[:n_main], refs[n_main:]
    is_sample = pl.program_id(0) >= n_pt

    if prologue == "plain":
        a = pro[0][...]
    else:
        vs_ref, a_scr = refs
        a = _gmlp_gate(*pro, vs_ref, a_scr, is_sample)

    x = _read_row_tile(res_refs, n_pt) + jnp.dot(a, wo_ref[...], preferred_element_type=F32)
    if has_bias:
        x = x + bo_ref[...]
    xn = _rms(x, fnw_ref[...]).astype(BF16)
    for c in range(D_FF // FFN_CK):
        sl = slice(c * FFN_CK, (c + 1) * FFN_CK)
        h = jnp.dot(xn, w1_ref[:, sl], preferred_element_type=F32)
        h = jnp.square(jnp.maximum(h, 0.0)).astype(BF16)
        x = x + jnp.dot(h, w2_ref[sl, :], preferred_element_type=F32)
    if final:
        y = _rms(x, finw_ref[...])

        @pl.when(jnp.logical_not(is_sample))
        def _():
            outs[0][...] = y

        @pl.when(is_sample)
        def _():
            outs[1][...] = y
    else:
        outs[0][...] = x


def _post(pro_args, res, wo, bo, fnw, w1, w2, finw, *, layer, prologue, t, n_pt, name):
    has_bias = bo is not None
    final = finw is not None
    tile = lambda c: pl.BlockSpec((ROW_TILE, c), lambda i: (i, 0))
    if prologue == "plain":
        in_specs = [tile(D_MODEL)]
    else:
        in_specs = [tile(2 * D_MODEL), _resident((1, D_MODEL)), _resident((1, D_MODEL)),
                    _resident((GMLP_GROUPS, GMLP_BLOCK, GMLP_BLOCK)),
                    _resident((GMLP_BLOCK, D_MODEL))]
    args = list(pro_args)
    res_specs, res_args = _row_tile_specs(res, n_pt)
    in_specs += res_specs + [_resident((D_MODEL, D_MODEL))]
    args += res_args + [wo]
    if has_bias:
        in_specs.append(_resident((1, D_MODEL)))
        args.append(bo)
    layer_block = lambda shape: pl.BlockSpec((None,) + shape, lambda i: (layer, 0, 0),
                                             pipeline_mode=pl.Buffered(1))
    in_specs += [_resident((1, D_MODEL)), layer_block((D_MODEL, D_FF)), layer_block((D_FF, D_MODEL))]
    args += [fnw, w1, w2]
    if final:
        in_specs.append(_resident((1, D_MODEL)))
        args.append(finw)
        out_specs = [pl.BlockSpec((ROW_TILE, D_MODEL), lambda i: (jnp.minimum(i, n_pt - 1), 0)),
                     pl.BlockSpec((ROW_TILE, D_MODEL), lambda i: (jnp.maximum(i - n_pt, 0), 0))]
        out_shape = [jax.ShapeDtypeStruct((n_pt * ROW_TILE, D_MODEL), F32),
                     jax.ShapeDtypeStruct((t - n_pt * ROW_TILE, D_MODEL), F32)]
    else:
        out_specs = [tile(D_MODEL)]
        out_shape = [jax.ShapeDtypeStruct((t, D_MODEL), F32)]
    scratch = []
    if prologue == "gmlp":
        out_specs.append(pl.BlockSpec((ROW_TILE, D_MODEL), lambda i: (jnp.maximum(i - n_pt, 0), 0)))
        out_shape.append(jax.ShapeDtypeStruct((t - n_pt * ROW_TILE, D_MODEL), F32))
        scratch.append(pltpu.VMEM((ROW_TILE, D_MODEL), BF16))
    return pl.pallas_call(
        functools.partial(_post_body, prologue=prologue, n_res=len(res_args), has_bias=has_bias,
                          final=final, n_pt=n_pt),
        grid=(t // ROW_TILE,),
        in_specs=in_specs, out_specs=out_specs, out_shape=out_shape, scratch_shapes=scratch,
        compiler_params=_params(), name=name)(*args)


GDN_NB = 4
GDN_WIN = CHUNK + 8
GDN_GROUPS = ROW_TILE // CHUNK


def _gdn_gates(bg, alog_ref, dtb_ref):
    glog = -jnp.exp(alog_ref[...]) * _softplus(bg + dtb_ref[...])
    return _sigmoid(bg), _cumsum_rows(glog)


def _gdn_in_body(*refs, n_x, n_pt, tps, n_p):
    x_refs = refs[:n_x]
    (nw_ref, w_ref, cw_ref, alog_ref, dtb_ref, halo0_ref,
     k_ref, kbq_ref, vk_ref, qe_ref, kd_ref, gate_ref, bg_ref, tails_ref,
     xc_scr, carry_scr) = refs[n_x:]
    i = pl.program_id(0)
    is_sample = i >= n_pt

    @pl.when(i == 0)
    def _():
        carry_scr[...] = jnp.zeros_like(carry_scr)

    xn = _rms(_read_row_tile(x_refs, n_pt), nw_ref[...]).astype(BF16)
    bg = jnp.dot(xn, w_ref[:, GDN_BG0:GDN_BG0 + LANES], preferred_element_type=F32)
    bg_ref[...] = bg
    beta, egc, e_rest = [], [], []
    for j in range(GDN_GROUPS):
        b_j, gc = _gdn_gates(bg[j * CHUNK:(j + 1) * CHUNK], alog_ref, dtb_ref)
        beta.append(b_j)
        egc.append(jnp.exp(gc))
        e_rest.append(jnp.exp(gc[CHUNK - 1:CHUNK, :] - gc))

    def project(c0, n):
        cols = slice(c0, c0 + n)
        y = jnp.dot(xn, w_ref[:, cols], preferred_element_type=F32)
        for j in range(GDN_GROUPS):
            r0 = j * GDN_WIN
            xc_scr[r0 + 8:r0 + GDN_WIN, cols] = y[j * CHUNK:(j + 1) * CHUNK]
            if j == 0:
                first = jnp.logical_or(is_sample, i % tps == 0)
                prev = carry_scr[:, cols]
            else:
                first = is_sample
                prev = xc_scr[r0 - 8:r0, cols]
            seq = jnp.where(is_sample, n_p + (i - n_pt) * GDN_GROUPS + j, i // tps)
            xc_scr[r0:r0 + 8, cols] = jnp.where(first, halo0_ref[seq, :, cols], prev)

    def conv_silu(j, c0):
        cw = [cw_ref[w:w + 1, c0:c0 + LANES] for w in range(GDN_CONV)]
        wv = xc_scr[j * GDN_WIN:(j + 1) * GDN_WIN, c0:c0 + LANES]
        back2 = pltpu.roll(wv, 2, 0)
        odd = pltpu.roll(cw[2] * wv + cw[0] * back2, 1, 0)
        return _silu(cw[3] * wv[8:GDN_WIN] + cw[1] * back2[8:GDN_WIN] + odd[8:GDN_WIN])

    def epilogue(h):
        cs = slice(h * GDN_DK, (h + 1) * GDN_DK)
        for j in range(GDN_GROUPS):
            rs = slice(j * CHUNK, (j + 1) * CHUNK)
            q = conv_silu(j, h * GDN_DK)
            k = conv_silu(j, GDN_QK + h * GDN_DK)
            v = conv_silu(j, 2 * GDN_QK + h * GDN_DV)
            q = q * lax.rsqrt(jnp.sum(q * q, axis=-1, keepdims=True) + EPS) * (GDN_DK ** -0.5)
            k = k * lax.rsqrt(jnp.sum(k * k, axis=-1, keepdims=True) + EPS)
            beta_h = beta[j][:, h:h + 1]
            e_h = egc[j][:, 8 + h:9 + h]
            kb = k * beta_h
            k_ref[rs, cs] = k.astype(BF16)
            kbq_ref[j, 0:CHUNK, cs] = kb.astype(BF16)
            kbq_ref[j, CHUNK:2 * CHUNK, cs] = q.astype(BF16)
            vk_ref[rs, 2 * h * GDN_DV:(2 * h + 1) * GDN_DV] = (v * beta_h).astype(BF16)
            vk_ref[rs, (2 * h + 1) * GDN_DV:(2 * h + 2) * GDN_DV] = (kb * e_h).astype(BF16)
            qe_ref[rs, cs] = (q * e_h).astype(BF16)
            kd_ref[rs, cs] = (k * e_rest[j][:, 8 + h:9 + h]).astype(BF16)

    pair = 2 * GDN_DK
    n_pairs = GDN_HEADS // 2
    for m in range(n_pairs + 1):
        if m < n_pairs:
            for part in range(3):
                project(part * GDN_QK + m * pair, pair)
        if m > 0:
            epilogue(2 * m - 2)
            epilogue(2 * m - 1)
    for c in range(2):
        cols = slice(c * FFN_CK, (c + 1) * FFN_CK)
        z = jnp.dot(xn, w_ref[:, GDN_Z0 + c * FFN_CK:GDN_Z0 + (c + 1) * FFN_CK],
                    preferred_element_type=F32)
        gate_ref[:, cols] = _silu(z).astype(BF16)
    for j in range(GDN_GROUPS):
        tails_ref[j] = xc_scr[(j + 1) * GDN_WIN - 8:(j + 1) * GDN_WIN, :]
    carry_scr[...] = xc_scr[GDN_GROUPS * GDN_WIN - 8:GDN_GROUPS * GDN_WIN, :]


def _gdn_in(x, nw, w, cw, alog, dtb, halo0, *, t, n_pt, tps, n_p):
    in_specs, args = _row_tile_specs(x, n_pt)
    n_x = len(args)
    n_seq = halo0.shape[0]
    in_specs += [_resident((1, D_MODEL)), _resident((D_MODEL, GDN_P)),
                 _resident((GDN_CONV, GDN_CONV_CH)), _resident((1, LANES)), _resident((1, LANES)),
                 _resident((n_seq, 8, GDN_CONV_CH))]
    args += [nw, w, cw, alog, dtb, halo0]
    tile = lambda c: pl.BlockSpec((ROW_TILE, c), lambda i: (i, 0))
    n_chunks = t // CHUNK
    bf = lambda c: jax.ShapeDtypeStruct((t, c), BF16)
    return pl.pallas_call(
        functools.partial(_gdn_in_body, n_x=n_x, n_pt=n_pt, tps=tps, n_p=n_p),
        grid=(t // ROW_TILE,),
        in_specs=in_specs,
        out_specs=[tile(D_MODEL),
                   pl.BlockSpec((GDN_GROUPS, 2 * CHUNK, D_MODEL), lambda i: (i, 0, 0)),
                   tile(2 * D_MODEL), tile(D_MODEL), tile(D_MODEL), tile(D_MODEL), tile(LANES),
                   pl.BlockSpec((GDN_GROUPS, 8, GDN_CONV_CH), lambda i: (i, 0, 0))],
        out_shape=[bf(D_MODEL), jax.ShapeDtypeStruct((n_chunks, 2 * CHUNK, D_MODEL), BF16),
                   bf(2 * D_MODEL), bf(D_MODEL), bf(D_MODEL), bf(D_MODEL),
                   jax.ShapeDtypeStruct((t, LANES), F32),
                   jax.ShapeDtypeStruct((n_chunks, 8, GDN_CONV_CH), F32)],
        scratch_shapes=[pltpu.VMEM((GDN_GROUPS * GDN_WIN, GDN_CONV_CH), F32),
                        pltpu.VMEM((8, GDN_CONV_CH), F32)],
        compiler_params=_params(), name="gdn_in")(*args)


def _gdn_pre_body(k_ref, kbq_ref, vk_ref, qe_ref, bg_ref, alog_ref, dtb_ref,
                  u_ref, wq_ref, attn_ref, egl_ref):
    pw = 2 * CHUNK
    row = lax.broadcasted_iota(jnp.int32, (CHUNK, pw), 0)
    lane = lax.broadcasted_iota(jnp.int32, (CHUNK, pw), 1)
    col = lane & (CHUNK - 1)
    left = lane < CHUNK
    incl = row >= col
    strict = row > col
    eye = (row == col).astype(F32)

    def level_mask(s):
        sh = int(math.log2(s))
        bi, bj = row >> sh, col >> sh
        return jnp.logical_and((bi & 1) == 1, bj == bi - 1)

    masks = [level_mask(s) for s in (1, 2, 4, 8, 16, 32)]
    r2 = lax.broadcasted_iota(jnp.int32, (pw, pw), 0)
    l2 = lax.broadcasted_iota(jnp.int32, (pw, pw), 1)
    same_head = (r2 < CHUNK) == (l2 < CHUNK)

    def block_diag(x):
        return jnp.where(same_head, jnp.concatenate([x, x], axis=0), jnp.zeros((pw, pw), x.dtype))

    probs = [(j, m) for j in range(GDN_NB) for m in range(GDN_HEADS // 2)]
    n = len(probs)
    rs = [slice(j * CHUNK, (j + 1) * CHUNK) for j, _ in probs]
    hc = lambda h: slice(h * GDN_DK, (h + 1) * GDN_DK)
    gcum, gcum_t = [], []
    for j in range(GDN_NB):
        _, gc = _gdn_gates(bg_ref[j * CHUNK:(j + 1) * CHUNK, :], alog_ref, dtb_ref)
        gcum.append(gc)
        gt = jnp.transpose(jnp.concatenate([gc, jnp.zeros_like(gc)], axis=0))
        gcum_t.append(gt)
        egl_ref[j] = jnp.broadcast_to(jnp.exp(gt[8:8 + GDN_HEADS, CHUNK - 1:CHUNK]),
                                      (GDN_HEADS, LANES))
    decay = []
    for j, m in probs:
        h0, h1 = 8 + 2 * m, 9 + 2 * m
        g_col = jnp.where(left, gcum[j][:, h0:h0 + 1], gcum[j][:, h1:h1 + 1])
        g_row = jnp.concatenate([gcum_t[j][h0:h0 + 1, 0:CHUNK], gcum_t[j][h1:h1 + 1, 0:CHUNK]], axis=1)
        decay.append(jnp.exp(jnp.minimum(g_col - g_row, 0.0)))
    qk = []
    for i, (j, m) in enumerate(probs):
        k_pair = jnp.concatenate([k_ref[rs[i], hc(2 * m)], k_ref[rs[i], hc(2 * m + 1)]], axis=0)
        r0, r1 = (lax.dot_general(kbq_ref[j, :, hc(h)], k_pair, (((1,), (1,)), ((), ())),
                                  preferred_element_type=F32) for h in (2 * m, 2 * m + 1))
        qk.append(jnp.where(l2 < CHUNK, r0, r1))
    a = [jnp.where(strict, qk[i][0:CHUNK] * decay[i], 0.0) for i in range(n)]
    for i, (j, m) in enumerate(probs):
        attn = jnp.where(incl, qk[i][CHUNK:2 * CHUNK] * decay[i], 0.0)
        attn_ref[rs[i], m * pw:(m + 1) * pw] = attn.astype(attn_ref.dtype)
    t = [eye - jnp.where(masks[0], ai, 0.0) for ai in a]
    a_b = [ai.astype(BF16) for ai in a]
    for lm in masks[1:]:
        t_b = [ti.astype(BF16) for ti in t]
        at = [jnp.dot(jnp.where(lm, a_b[i], jnp.zeros_like(a_b[i])), block_diag(t_b[i]),
                      preferred_element_type=F32) for i in range(n)]
        t = [t[i] - jnp.dot(t_b[i], block_diag(at[i].astype(BF16)), preferred_element_type=F32)
             for i in range(n)]
    zeros = jnp.zeros((CHUNK, 2 * GDN_DV), BF16)
    for i, (j, m) in enumerate(probs):
        v0 = vk_ref[rs[i], 4 * m * GDN_DV:(4 * m + 2) * GDN_DV]
        v1 = vk_ref[rs[i], (4 * m + 2) * GDN_DV:(4 * m + 4) * GDN_DV]
        rhs = jnp.concatenate([jnp.concatenate([v0, zeros], axis=1),
                               jnp.concatenate([zeros, v1], axis=1)], axis=0)
        uw = jnp.dot(t[i].astype(BF16), rhs, preferred_element_type=F32)
        for q, h in enumerate((2 * m, 2 * m + 1)):
            u_ref[rs[i], hc(h)] = uw[:, 2 * q * GDN_DV:(2 * q + 1) * GDN_DV]
            wq_ref[j, 0:CHUNK, hc(h)] = uw[:, (2 * q + 1) * GDN_DV:(2 * q + 2) * GDN_DV].astype(wq_ref.dtype)
            wq_ref[j, CHUNK:2 * CHUNK, hc(h)] = qe_ref[rs[i], hc(h)]


def _gdn_pre(k, kbq, vk, qe, bgp, alog, dtb):
    t = k.shape[0]
    rows = GDN_NB * CHUNK
    n_chunks = t // CHUNK
    tile = lambda c: pl.BlockSpec((rows, c), lambda b: (b, 0))
    chunked = lambda r, c: pl.BlockSpec((GDN_NB, r, c), lambda b: (b, 0, 0))
    return pl.pallas_call(
        _gdn_pre_body,
        grid=(t // rows,),
        in_specs=[tile(D_MODEL), chunked(2 * CHUNK, D_MODEL), tile(2 * D_MODEL), tile(D_MODEL),
                  tile(LANES), _resident((1, LANES)), _resident((1, LANES))],
        out_specs=[tile(D_MODEL), chunked(2 * CHUNK, D_MODEL), tile(GDN_HEADS * CHUNK),
                   chunked(GDN_HEADS, LANES)],
        out_shape=[jax.ShapeDtypeStruct((t, D_MODEL), F32),
                   jax.ShapeDtypeStruct((n_chunks, 2 * CHUNK, D_MODEL), BF16),
                   jax.ShapeDtypeStruct((t, GDN_HEADS * CHUNK), BF16),
                   jax.ShapeDtypeStruct((n_chunks, GDN_HEADS, LANES), F32)],
        compiler_params=_params(), name="gdn_pre")(k, kbq, vk, qe, bgp, alog, dtb)


def _gdn_scan_body(u_ref, wq_ref, kd_ref, attn_ref, egl_ref, gate_ref, s0_ref, nw_ref,
                   o_ref, soutp_ref, souts_ref, s_scr, *, n_pb, bps):
    blk, is_prompt, first, last = _block_flags(n_pb, bps)

    @pl.when(first)
    def _():
        s_scr[...] = jnp.zeros_like(s_scr)

    heads = range(GDN_HEADS)
    cs = [slice(h * GDN_DK, (h + 1) * GDN_DK) for h in heads]
    s = [s_scr[h] for h in heads]
    for j in range(SCAN_NB):
        rs = slice(j * CHUNK, (j + 1) * CHUNK)
        s = [jnp.where(is_prompt, s[h], s0_ref[j, h]) for h in heads]
        ws_qs = [jnp.dot(wq_ref[j, :, cs[h]], s[h].astype(BF16), preferred_element_type=F32)
                 for h in heads]
        v_new = [(u_ref[rs, cs[h]] - ws_qs[h][0:CHUNK]).astype(BF16) for h in heads]
        o = [ws_qs[h][CHUNK:2 * CHUNK]
             + jnp.dot(attn_ref[rs, h * CHUNK:(h + 1) * CHUNK], v_new[h], preferred_element_type=F32)
             for h in heads]
        s = [s[h] * egl_ref[j, h:h + 1, :] + _mm_tn(kd_ref[rs, cs[h]], v_new[h]) for h in heads]
        for h in heads:
            o_ref[rs, cs[h]] = (_rms(o[h], nw_ref[...]) * gate_ref[rs, cs[h]]).astype(o_ref.dtype)

        @pl.when(jnp.logical_not(is_prompt))
        def _(j=j, s=s):
            for h in heads:
                souts_ref[j, h] = s[h]

    for h in heads:
        s_scr[h] = s[h]

    @pl.when(last)
    def _():
        for h in heads:
            soutp_ref[0, h] = s[h]


def _gdn_scan(u, wq, kd, attn, egl, gate, s0, nw, *, n_p, cpl, n_s):
    t = u.shape[0]
    rows = SCAN_NB * CHUNK
    bps = cpl // SCAN_NB
    n_pb = n_p * bps
    row_blk = lambda b: (b, 0)
    s_in, s_out, s_shapes = _state_specs(n_p, n_s, n_pb, bps, (GDN_HEADS, GDN_DK, GDN_DV))
    return pl.pallas_call(
        functools.partial(_gdn_scan_body, n_pb=n_pb, bps=bps),
        grid=(t // rows,),
        in_specs=[pl.BlockSpec((rows, D_MODEL), row_blk),
                  pl.BlockSpec((SCAN_NB, 2 * CHUNK, D_MODEL), lambda b: (b, 0, 0)),
                  pl.BlockSpec((rows, D_MODEL), row_blk),
                  pl.BlockSpec((rows, GDN_HEADS * CHUNK), row_blk),
                  pl.BlockSpec((SCAN_NB, GDN_HEADS, LANES), lambda b: (b, 0, 0)),
                  pl.BlockSpec((rows, D_MODEL), row_blk),
                  s_in, _resident((1, GDN_DV))],
        out_specs=[pl.BlockSpec((rows, D_MODEL), row_blk)] + s_out,
        out_shape=[jax.ShapeDtypeStruct((t, D_MODEL), BF16)] + s_shapes,
        scratch_shapes=[pltpu.VMEM((GDN_HEADS, GDN_DK, GDN_DV), F32)],
        compiler_params=_params(), name="gdn_scan")(u, wq, kd, attn, egl, gate, s0, nw)


def _gla_exponent_matrix():
    i = np.arange(CHUNK)[:, None]
    t = np.arange(CHUNK)[None, :]
    blocks = [(t <= i)]
    for s in GLA_LEVELS:
        c = (i // (2 * s)) * (2 * s) + s - 1
        blocks.append(np.where(i > c, (t > c) & (t <= i), (t > i) & (t <= c)))
    w = np.concatenate(blocks, axis=0).astype(np.float32)
    return np.concatenate([w, w, w], axis=1)


def _split3(x):
    hi = x.astype(BF16)
    r1 = x - hi.astype(F32)
    mid = r1.astype(BF16)
    lo = (r1 - mid.astype(F32)).astype(BF16)
    return hi, mid, lo


def _gla_body(p_ref, s0_ref, wg2_ref, bg_ref, wexp_ref, nw_ref,
              o_ref, soutp_ref, souts_ref, st_scr, s0t_scr, *, n_pb, bps):
    blk, is_prompt, first, last = _block_flags(n_pb, bps)
    heads = range(GLA_HEADS)
    chunks = range(SCAN_NB)

    @pl.when(first)
    def _():
        st_scr[...] = jnp.zeros_like(st_scr)

    @pl.when(blk == 0)
    def _():
        s0t_scr[...] = jnp.zeros_like(s0t_scr)

    @pl.when(jnp.logical_not(is_prompt))
    def _():
        for j in chunks:
            for h in heads:
                s0t_scr[j, h] = s0_ref[j, h].T

    row = lax.broadcasted_iota(jnp.int32, (CHUNK, CHUNK), 0)
    col = lax.broadcasted_iota(jnp.int32, (CHUNK, CHUNK), 1)
    eye = row == col
    masks = [_pair_mask(s) for s in GLA_LEVELS]
    rs = [slice(j * CHUNK, (j + 1) * CHUNK) for j in chunks]
    ck = [slice(h * GLA_DK, (h + 1) * GLA_DK) for h in heads]
    cv = [slice(h * GLA_DV, (h + 1) * GLA_DV) for h in heads]

    x = [_mm(p_ref[rs[j], GLA_G0:GLA_G0 + LANES], wg2_ref[...]) + bg_ref[...] for j in chunks]
    glog = [-_softplus(-x[j]) * (1.0 / GLA_TAU) for j in chunks]
    e_all = [jnp.dot(wexp_ref[...], jnp.concatenate(_split3(glog[j]), axis=0),
                     preferred_element_type=F32) for j in chunks]
    q_all = [p_ref[rs[j], 0:GLA_QK] * (GLA_DK ** -0.5) for j in chunks]
    k_all = [p_ref[rs[j], GLA_QK:2 * GLA_QK] for j in chunks]
    qb = [q.astype(BF16) for q in q_all]
    kb = [k.astype(BF16) for k in k_all]
    attn = [[jnp.where(eye, _mm_nt(qb[j][:, ck[h]], kb[j][:, ck[h]]), 0.0) for h in heads]
            for j in chunks]
    for lvl, m in enumerate(masks):
        f = [jnp.exp(e_all[j][(lvl + 1) * CHUNK:(lvl + 2) * CHUNK, :]) for j in chunks]
        qf = [(q_all[j] * f[j]).astype(BF16) for j in chunks]
        kf = [(k_all[j] * f[j]).astype(BF16) for j in chunks]
        attn = [[attn[j][h] + jnp.where(m, _mm_nt(qf[j][:, ck[h]], kf[j][:, ck[h]]), 0.0)
                 for h in heads] for j in chunks]
    b = [e_all[j][0:CHUNK, :] for j in chunks]
    b_last = [b[j][CHUNK - 1:CHUNK, :] for j in chunks]
    qe = [(q_all[j] * jnp.exp(b[j])).astype(BF16) for j in chunks]
    kd = [(k_all[j] * jnp.exp(b_last[j] - b[j])).astype(BF16) for j in chunks]
    eb_last = [jnp.exp(b_last[j]) for j in chunks]
    vb = [p_ref[rs[j], GLA_V0:GLA_V0 + D_MODEL].astype(BF16) for j in chunks]
    av = [[_mm(attn[j][h], vb[j][:, cv[h]]) for h in heads] for j in chunks]
    kv = [[_mm_tn(vb[j][:, cv[h]], kd[j][:, ck[h]]) for h in heads] for j in chunks]

    st = [st_scr[h] for h in heads]
    for j in chunks:
        st = [jnp.where(is_prompt, st[h], s0t_scr[j, h]) for h in heads]
        o = [_mm_nt(qe[j][:, ck[h]], st[h]) + av[j][h] for h in heads]
        st = [st[h] * eb_last[j][:, ck[h]] + kv[j][h] for h in heads]
        for h in heads:
            r = p_ref[rs[j], GLA_R0 + h * GLA_DV:GLA_R0 + (h + 1) * GLA_DV]
            o_ref[rs[j], cv[h]] = (_rms(o[h], nw_ref[h:h + 1, :]) * _silu(r)).astype(o_ref.dtype)

        @pl.when(jnp.logical_not(is_prompt))
        def _(j=j, st=st):
            for h in heads:
                souts_ref[j, h] = st[h].T

    for h in heads:
        st_scr[h] = st[h]

    @pl.when(last)
    def _():
        for h in heads:
            soutp_ref[0, h] = st[h].T


def _gla_scan(p, s0, wg2, bg, wexp, nw, *, n_p, cpl, n_s):
    t = p.shape[0]
    rows = SCAN_NB * CHUNK
    bps = cpl // SCAN_NB
    n_pb = n_p * bps
    s_in, s_out, s_shapes = _state_specs(n_p, n_s, n_pb, bps, (GLA_HEADS, GLA_DK, GLA_DV))
    return pl.pallas_call(
        functools.partial(_gla_body, n_pb=n_pb, bps=bps),
        grid=(t // rows,),
        in_specs=[pl.BlockSpec((rows, GLA_P), lambda b: (b, 0)), s_in,
                  _resident((LANES, GLA_QK)), _resident((1, GLA_QK)),
                  _resident(((len(GLA_LEVELS) + 1) * CHUNK, 3 * CHUNK)),
                  _resident((GLA_HEADS, GLA_DV))],
        out_specs=[pl.BlockSpec((rows, D_MODEL), lambda b: (b, 0))] + s_out,
        out_shape=[jax.ShapeDtypeStruct((t, D_MODEL), BF16)] + s_shapes,
        scratch_shapes=[pltpu.VMEM((GLA_HEADS, GLA_DV, GLA_DK), F32),
                        pltpu.VMEM((SCAN_NB, GLA_HEADS, GLA_DV, GLA_DK), F32)],
        compiler_params=_params(), name="gla_scan")(p, s0, wg2, bg, wexp, nw)


CNV_WIN = CONV_HALO + CHUNK
TILE_CHUNKS = ROW_TILE // CHUNK
CNV_CK = 256


def _cnv_in_body(*refs, n_x, n_pt, tps, n_p):
    x_refs = refs[:n_x]
    (nw_ref, w_ref, b_ref, wdw_ref, bdw_ref, lnw_ref, lnb_ref, halo0_ref,
     a_ref, tails_ref, xc_scr, y_scr, carry_scr) = refs[n_x:]
    i = pl.program_id(0)
    is_sample = i >= n_pt

    @pl.when(i == 0)
    def _():
        carry_scr[...] = jnp.zeros_like(carry_scr)

    xn = _rms(_read_row_tile(x_refs, n_pt), nw_ref[...]).astype(BF16)
    taps = {}
    for w in range(CONV_W):
        s = CONV_HALO - (CONV_W - 1) + w
        taps.setdefault(-s % 8, []).append((w, s + (-s % 8)))

    def glu_slab(c0):
        cols = slice(c0, c0 + CNV_CK)
        gcols = slice(D_MODEL + c0, D_MODEL + c0 + CNV_CK)
        val = jnp.dot(xn, w_ref[:, cols], preferred_element_type=F32) + b_ref[:, cols]
        gate = jnp.dot(xn, w_ref[:, gcols], preferred_element_type=F32) + b_ref[:, gcols]
        glu = val * _sigmoid(gate)
        for j in range(TILE_CHUNKS):
            r0 = j * CNV_WIN
            xc_scr[r0 + CONV_HALO:r0 + CNV_WIN, cols] = glu[j * CHUNK:(j + 1) * CHUNK]
            if j == 0:
                first = jnp.logical_or(is_sample, i % tps == 0)
                prev = carry_scr[:, cols]
            else:
                first = is_sample
                prev = xc_scr[r0 - CONV_HALO:r0, cols]
            seq = jnp.where(is_sample, n_p + (i - n_pt) * TILE_CHUNKS + j, i // tps)
            xc_scr[r0:r0 + CONV_HALO, cols] = jnp.where(first, halo0_ref[seq, :, cols], prev)

    def conv_slab(c0):
        for j in range(TILE_CHUNKS):
            for b0 in range(c0, c0 + CNV_CK, LANES):
                cs = slice(b0, b0 + LANES)
                wv = xc_scr[j * CNV_WIN:(j + 1) * CNV_WIN, cs]
                y = None
                for r, group in sorted(taps.items()):
                    rolled = wv if r == 0 else pltpu.roll(wv, r, 0)
                    for w, a0 in group:
                        term = wdw_ref[w:w + 1, cs] * rolled[a0:a0 + CHUNK]
                        y = term if y is None else y + term
                y_scr[j * CHUNK:(j + 1) * CHUNK, cs] = y + bdw_ref[:, cs]

    slabs = list(range(0, D_MODEL, CNV_CK))
    for n, c0 in enumerate(slabs + [None]):
        if c0 is not None:
            glu_slab(c0)
        if n > 0:
            conv_slab(slabs[n - 1])
    for j in range(TILE_CHUNKS):
        rs = slice(j * CHUNK, (j + 1) * CHUNK)
        a_ref[rs, :] = _silu(_layernorm(y_scr[rs, :], lnw_ref[...], lnb_ref[...])).astype(BF16)
        tails_ref[j] = xc_scr[(j + 1) * CNV_WIN - CONV_HALO:(j + 1) * CNV_WIN, :]
    carry_scr[...] = xc_scr[TILE_CHUNKS * CNV_WIN - CONV_HALO:TILE_CHUNKS * CNV_WIN, :]


def _cnv_in(x, nw, w, b, wdw, bdw, lnw, lnb, halo0, *, t, n_pt, tps, n_p):
    in_specs, args = _row_tile_specs(x, n_pt)
    n_x = len(args)
    in_specs += [_resident((1, D_MODEL)), _resident((D_MODEL, 2 * D_MODEL)), _resident((1, 2 * D_MODEL)),
                 _resident((CONV_HALO, D_MODEL)), _resident((1, D_MODEL)),
                 _resident((1, D_MODEL)), _resident((1, D_MODEL)), _resident(halo0.shape)]
    args += [nw, w, b, wdw, bdw, lnw, lnb, halo0]
    return pl.pallas_call(
        functools.partial(_cnv_in_body, n_x=n_x, n_pt=n_pt, tps=tps, n_p=n_p),
        grid=(t // ROW_TILE,),
        in_specs=in_specs,
        out_specs=[pl.BlockSpec((ROW_TILE, D_MODEL), lambda i: (i, 0)),
                   pl.BlockSpec((TILE_CHUNKS, CONV_HALO, D_MODEL), lambda i: (i, 0, 0))],
        out_shape=[jax.ShapeDtypeStruct((t, D_MODEL), BF16),
                   jax.ShapeDtypeStruct((t // CHUNK, CONV_HALO, D_MODEL), F32)],
        scratch_shapes=[pltpu.VMEM((TILE_CHUNKS * CNV_WIN, D_MODEL), F32),
                        pltpu.VMEM((ROW_TILE, D_MODEL), F32),
                        pltpu.VMEM((CONV_HALO, D_MODEL), F32)],
        compiler_params=_params(), name="cnv_in")(*args)


def _pad_cols(w, n):
    return jnp.pad(w, ((0, 0), (0, n - w.shape[1])))


def _row(v):
    return v.reshape(1, -1).astype(F32)


def _tail_rows(a, lay, n):
    n_p, cpl, n_s = lay
    ends = np.concatenate([(np.arange(n_p) + 1) * cpl * CHUNK,
                           n_p * cpl * CHUNK + (np.arange(n_s) + 1) * CHUNK])
    idx = (ends[:, None] - n + np.arange(n)[None, :]).reshape(-1)
    return jnp.take(a, jnp.asarray(idx, jnp.int32), axis=0).reshape(n_p + n_s, n, a.shape[1])


def _gdn_mix(x, lay, dims, mix_nw, cache, state, w_in, conv_w, a_log, dt_bias, norm_w):
    n_p, cpl, n_s = lay
    halo0 = jnp.concatenate([jnp.zeros((n_p, 8, GDN_CONV_CH), F32),
                             jnp.pad(cache, ((0, 0), (8 - (GDN_CONV - 1), 0), (0, 0)))], axis=0)
    lane_pad = lambda v: jnp.pad(v.reshape(1, -1), ((0, 0), (GDN_HEADS, LANES - 2 * GDN_HEADS)))
    alog, dtb = lane_pad(a_log), lane_pad(dt_bias)
    k, kbq, vk, qe, kd, gate, bgp, tails = _gdn_in(
        x, _row(mix_nw), _pad_cols(w_in, GDN_P).astype(BF16), conv_w, alog, dtb, halo0,
        tps=cpl * CHUNK // ROW_TILE, n_p=n_p, **dims)
    u, wq, attn, egl = _gdn_pre(k, kbq, vk, qe, bgp, alog, dtb)
    o, s_p, s_s = _gdn_scan(u, wq, kd, attn, egl, gate, state, _row(norm_w),
                            n_p=n_p, cpl=cpl, n_s=n_s)
    ends = np.concatenate([(np.arange(n_p) + 1) * cpl - 1, n_p * cpl + np.arange(n_s)])
    conv_rows = tails[jnp.asarray(ends, jnp.int32), 8 - (GDN_CONV - 1):, :]
    return o, (conv_rows[:n_p], conv_rows[n_p:], s_p, s_s)


def _gla_mix(x, lay, dims, mix_nw, state, w_in, w_g2, b_g, norm_w):
    n_p, cpl, n_s = lay
    p = _proj(x, _row(mix_nw), _pad_cols(w_in, GLA_P).astype(BF16), None,
              n_out=GLA_P, ck=512, name="gla_in", **dims)
    wg2 = jnp.pad(w_g2, ((0, LANES - GLA_RANK), (0, 0))).astype(BF16)
    wexp = jnp.asarray(_gla_exponent_matrix(), BF16)
    o, s_p, s_s = _gla_scan(p, state, wg2, _row(b_g), wexp, norm_w, n_p=n_p, cpl=cpl, n_s=n_s)
    return o, (s_p, s_s)


def _cnv_mix(x, lay, dims, mix_nw, cache, w_pw1, b_pw1, w_dw, b_dw, ln_w, ln_b):
    n_p, cpl, n_s = lay
    pad_rows = CONV_HALO - (CONV_W - 1)
    halo0 = jnp.concatenate([jnp.zeros((n_p, CONV_HALO, D_MODEL), F32),
                             jnp.pad(cache, ((0, 0), (pad_rows, 0), (0, 0)))], axis=0)
    wdw = jnp.pad(w_dw, ((0, CONV_HALO - CONV_W), (0, 0)))
    a, tails = _cnv_in(x, _row(mix_nw), w_pw1.astype(BF16), _row(b_pw1), wdw, _row(b_dw),
                       _row(ln_w), _row(ln_b), halo0, tps=cpl * CHUNK // ROW_TILE, n_p=n_p, **dims)
    ends = np.concatenate([(np.arange(n_p) + 1) * cpl - 1, n_p * cpl + np.arange(n_s)])
    rows = tails[jnp.asarray(ends, jnp.int32), pad_rows:, :]
    return a, (rows[:n_p], rows[n_p:])


def kernel(x_prompt, x_sample, cache_gdn_conv, state_gdn, state_gla, cache_conformer, mix_norm_w, ffn_norm_w, ffn_w1, ffn_w2, final_norm_w, gdn_w_in, gdn_conv_w, gdn_a_log, gdn_dt_bias, gdn_norm_w, gdn_w_out, gla_w_in, gla_w_g2, gla_b_g, gla_norm_w, gla_w_out, gmlp_w_in, gmlp_b_in, gmlp_ln_w, gmlp_ln_b, gmlp_w_s, gmlp_b_s, gmlp_w_out, gmlp_b_out, cnv_w_pw1, cnv_b_pw1, cnv_w_dw, cnv_b_dw, cnv_ln_w, cnv_ln_b, cnv_w_pw2, cnv_b_pw2):
    n_p, l_p, d = x_prompt.shape
    n_s, l_s, _ = x_sample.shape
    assert d == D_MODEL and l_s == CHUNK and l_p % ROW_TILE == 0
    assert n_s % max(SCAN_NB, GDN_NB) == 0
    t_p, t_s = n_p * l_p, n_s * l_s
    assert t_p % ROW_TILE == 0 and t_s % ROW_TILE == 0
    lay = (n_p, l_p // CHUNK, n_s)
    dims = dict(t=t_p + t_s, n_pt=t_p // ROW_TILE)
    depth = mix_norm_w.shape[0]

    x = (x_prompt.reshape(t_p, d), x_sample.reshape(t_s, d))
    outs = {k: [] for k in ("gdn_cp", "gdn_cs", "gdn_sp", "gdn_ss", "gla_sp", "gla_ss",
                            "gmlp_vs", "cnv_p", "cnv_s")}
    w1_all, w2_all = ffn_w1.astype(BF16), ffn_w2.astype(BF16)
    for i in range(depth):
        kind, j = i % 4, i // 4
        ffn = (_row(ffn_norm_w[i]), w1_all, w2_all, _row(final_norm_w) if i == depth - 1 else None)
        if kind == 0:
            o, (cp, cs, sp, ss) = _gdn_mix(x, lay, dims, mix_norm_w[i], cache_gdn_conv[j], state_gdn[j],
                                           gdn_w_in[j], gdn_conv_w[j], gdn_a_log[j], gdn_dt_bias[j],
                                           gdn_norm_w[j])
            outs["gdn_cp"].append(cp); outs["gdn_cs"].append(cs)
            outs["gdn_sp"].append(sp); outs["gdn_ss"].append(ss)
            res = _post([o], x, gdn_w_out[j].astype(BF16), None, *ffn,
                        prologue="plain", layer=i, name=f"post_{i}", **dims)
        elif kind == 1:
            o, (sp, ss) = _gla_mix(x, lay, dims, mix_norm_w[i], state_gla[j], gla_w_in[j], gla_w_g2[j],
                                   gla_b_g[j], gla_norm_w[j])
            outs["gla_sp"].append(sp); outs["gla_ss"].append(ss)
            res = _post([o], x, gla_w_out[j].astype(BF16), None, *ffn,
                        prologue="plain", layer=i, name=f"post_{i}", **dims)
        elif kind == 2:
            zz = _proj(x, _row(mix_norm_w[i]), gmlp_w_in[j].astype(BF16), _row(gmlp_b_in[j]),
                       n_out=2 * D_MODEL, ck=512, epilogue="gelu", name="gmlp_in", **dims)
            bs_full = jnp.repeat(gmlp_b_s[j].T, GMLP_GC, axis=1)
            res = _post([zz, _row(gmlp_ln_w[j]), _row(gmlp_ln_b[j]), gmlp_w_s[j], bs_full], x,
                        gmlp_w_out[j].astype(BF16), _row(gmlp_b_out[j]), *ffn,
                        prologue="gmlp", layer=i, name=f"post_{i}", **dims)
            outs["gmlp_vs"].append(res[-1].reshape(n_s, l_s, d))
            res = res[:-1]
        else:
            a, (cp, cs) = _cnv_mix(x, lay, dims, mix_norm_w[i], cache_conformer[j], cnv_w_pw1[j],
                                   cnv_b_pw1[j], cnv_w_dw[j], cnv_b_dw[j], cnv_ln_w[j], cnv_ln_b[j])
            outs["cnv_p"].append(cp); outs["cnv_s"].append(cs)
            res = _post([a], x, cnv_w_pw2[j].astype(BF16), _row(cnv_b_pw2[j]), *ffn,
                        prologue="plain", layer=i, name=f"post_{i}", **dims)
        x = tuple(res) if i == depth - 1 else res[0]
    y_p, y_s = x
    stack = lambda xs: xs[0][None] if len(xs) == 1 else jnp.stack(xs)
    return (y_p.reshape(n_p, l_p, d), y_s.reshape(n_s, l_s, d),
            stack(outs["gdn_cp"]), stack(outs["gdn_cs"]),
            stack(outs["gdn_sp"]), stack(outs["gdn_ss"]),
            stack(outs["gla_sp"]), stack(outs["gla_ss"]),
            stack(outs["gmlp_vs"]),
            stack(outs["cnv_p"]), stack(outs["cnv_s"]))
```

```python
import functools
import math

import numpy as np
import jax
import jax.numpy as jnp
from jax import lax
from jax.experimental import pallas as pl
from jax.experimental.pallas import tpu as pltpu

F32 = jnp.float32
BF16 = jnp.bfloat16

D_MODEL = 1024
D_FF = 4 * D_MODEL
EPS = 1e-6
CHUNK = 64
ROW_TILE = 512
SCAN_NB = 4
LANES = 128
VMEM_LIMIT = 56 * 1024 * 1024

GDN_HEADS, GDN_DK, GDN_DV, GDN_CONV = 8, 128, 128, 4
GDN_QK = GDN_HEADS * GDN_DK
GDN_CONV_CH = 2 * GDN_QK + GDN_HEADS * GDN_DV
GDN_Z0 = GDN_CONV_CH
GDN_BG0 = GDN_CONV_CH + GDN_HEADS * GDN_DV
GDN_P = GDN_BG0 + LANES

GLA_HEADS, GLA_DK, GLA_DV, GLA_RANK, GLA_TAU = 4, 128, 256, 16, 16.0
GLA_QK = GLA_HEADS * GLA_DK
GLA_V0, GLA_R0, GLA_G0 = 2 * GLA_QK, 2 * GLA_QK + D_MODEL, 2 * GLA_QK + 2 * D_MODEL
GLA_P = GLA_G0 + LANES
GLA_LEVELS = (32, 16, 8, 4, 2, 1)

GMLP_BLOCK, GMLP_GROUPS, GMLP_GC = 128, 4, 256
CONV_W = 31
CONV_HALO = 32


def _mm(a, b):
    return jnp.dot(a.astype(BF16), b.astype(BF16), preferred_element_type=F32)


def _mm_nt(a, b):
    return lax.dot_general(a.astype(BF16), b.astype(BF16), (((1,), (1,)), ((), ())),
                           preferred_element_type=F32)


def _mm_tn(a, b):
    return lax.dot_general(a.astype(BF16), b.astype(BF16), (((0,), (0,)), ((), ())),
                           preferred_element_type=F32)


def _rms(x, w):
    return x * lax.rsqrt(jnp.mean(x * x, axis=-1, keepdims=True) + EPS) * w


def _layernorm(x, w, b):
    xc = x - jnp.mean(x, axis=-1, keepdims=True)
    return xc * lax.rsqrt(jnp.mean(xc * xc, axis=-1, keepdims=True) + EPS) * w + b


def _sigmoid(x):
    return jax.nn.sigmoid(x)


def _silu(x):
    h = 0.5 * x
    return h + h * jnp.tanh(h)


def _softplus(x):
    return jnp.maximum(x, 0.0) + jnp.log1p(jnp.exp(-jnp.abs(x)))


def _gelu_tanh(x):
    return 0.5 * x * (1.0 + jnp.tanh(math.sqrt(2.0 / math.pi) * (x + 0.044715 * (x * x * x))))


def _cumsum_rows(x):
    row = lax.broadcasted_iota(jnp.int32, x.shape, 0)
    s = 1
    while s < x.shape[0]:
        x = x + jnp.where(row >= s, pltpu.roll(x, s, 0), 0.0)
        s *= 2
    return x


def _pair_mask(s):
    sh = int(math.log2(s))
    bi = lax.broadcasted_iota(jnp.int32, (CHUNK, CHUNK), 0) >> sh
    bj = lax.broadcasted_iota(jnp.int32, (CHUNK, CHUNK), 1) >> sh
    return jnp.logical_and((bi & 1) == 1, bj == bi - 1)


def _block_flags(n_pb, bps):
    blk = pl.program_id(0)
    is_prompt = blk < n_pb
    first = jnp.logical_and(is_prompt, blk % bps == 0)
    last = jnp.logical_and(is_prompt, blk % bps == bps - 1)
    return blk, is_prompt, first, last


def _state_specs(n_p, n_s, n_pb, bps, shape):
    zeros = (0,) * len(shape)
    sample = pl.BlockSpec((SCAN_NB,) + shape, lambda b: (jnp.maximum(b - n_pb, 0),) + zeros)
    prompt = pl.BlockSpec((1,) + shape, lambda b: (jnp.minimum(b // bps, n_p - 1),) + zeros)
    shapes = [jax.ShapeDtypeStruct((n_p,) + shape, F32), jax.ShapeDtypeStruct((n_s,) + shape, F32)]
    return sample, [prompt, sample], shapes


def _params(n_axes=1):
    return pltpu.CompilerParams(dimension_semantics=("arbitrary",) * n_axes,
                                vmem_limit_bytes=VMEM_LIMIT)


def _resident(shape):
    zeros = (0,) * len(shape)
    return pl.BlockSpec(shape, lambda i: zeros, pipeline_mode=pl.Buffered(1))


def _row_tile_specs(x, n_pt):
    if isinstance(x, tuple):
        return ([pl.BlockSpec((ROW_TILE, D_MODEL), lambda i: (jnp.minimum(i, n_pt - 1), 0)),
                 pl.BlockSpec((ROW_TILE, D_MODEL), lambda i: (jnp.maximum(i - n_pt, 0), 0))], list(x))
    return [pl.BlockSpec((ROW_TILE, D_MODEL), lambda i: (i, 0))], [x]


def _read_row_tile(refs, n_pt):
    if len(refs) == 2:
        return jnp.where(pl.program_id(0) < n_pt, refs[0][...], refs[1][...])
    return refs[0][...]


def _proj_body(*refs, n_x, n_pt, n_out, ck, epilogue, has_bias):
    x_refs, (nw_ref, w_ref), rest = refs[:n_x], refs[n_x:n_x + 2], refs[n_x + 2:]
    if has_bias:
        b_ref, o_ref = rest
    else:
        (o_ref,) = rest
    xn = _rms(_read_row_tile(x_refs, n_pt), nw_ref[...]).astype(BF16)
    for c0 in range(0, n_out, ck):
        sl = slice(c0, min(c0 + ck, n_out))
        y = jnp.dot(xn, w_ref[:, sl], preferred_element_type=F32)
        if has_bias:
            y = y + b_ref[:, sl]
        if epilogue == "gelu":
            y = _gelu_tanh(y)
        o_ref[:, sl] = y.astype(o_ref.dtype)


def _proj(x, nw, w, b, *, t, n_pt, n_out, ck, epilogue="none", out_dtype=F32, name):
    n_w = w.shape[1]
    has_bias = b is not None
    in_specs, args = _row_tile_specs(x, n_pt)
    n_x = len(args)
    in_specs += [_resident((1, D_MODEL)), _resident((D_MODEL, n_w))]
    args += [nw, w]
    if has_bias:
        in_specs.append(_resident((1, n_w)))
        args.append(b)
    return pl.pallas_call(
        functools.partial(_proj_body, n_x=n_x, n_pt=n_pt, n_out=n_out, ck=ck, epilogue=epilogue,
                          has_bias=has_bias),
        grid=(t // ROW_TILE,),
        in_specs=in_specs,
        out_specs=pl.BlockSpec((ROW_TILE, n_out), lambda i: (i, 0)),
        out_shape=jax.ShapeDtypeStruct((t, n_out), out_dtype),
        compiler_params=_params(), name=name)(*args)


FFN_CK = 512


def _gmlp_gate(zz_ref, lnw_ref, lnb_ref, ws_ref, bs_ref, vs_ref, a_scr, is_sample):
    v = _layernorm(zz_ref[:, D_MODEL:2 * D_MODEL], lnw_ref[...], lnb_ref[...])

    @pl.when(is_sample)
    def _():
        vs_ref[...] = v

    vb = v.astype(BF16)
    row = lax.broadcasted_iota(jnp.int32, (GMLP_BLOCK, GMLP_BLOCK), 0)
    col = lax.broadcasted_iota(jnp.int32, (GMLP_BLOCK, GMLP_BLOCK), 1)
    tril = col <= row
    top_left = jnp.logical_and(row < CHUNK, col < CHUNK)
    brow = lax.broadcasted_iota(jnp.int32, (GMLP_BLOCK, D_MODEL), 0)
    bias = bs_ref[...]
    bias = jnp.where(jnp.logical_and(is_sample, brow >= CHUNK), pltpu.roll(bias, CHUNK, 0), bias)
    for grp in range(GMLP_GROUPS):
        cs = slice(grp * GMLP_GC, (grp + 1) * GMLP_GC)
        wt = jnp.where(tril, ws_ref[grp], 0.0)
        w11 = jnp.where(top_left, wt, 0.0)
        wd = w11 + pltpu.roll(pltpu.roll(w11, CHUNK, 0), CHUNK, 1)
        we = jnp.where(is_sample, wd, wt).astype(BF16)
        for blk in range(ROW_TILE // GMLP_BLOCK):
            rs = slice(blk * GMLP_BLOCK, (blk + 1) * GMLP_BLOCK)
            mixed = jnp.dot(we, vb[rs, cs], preferred_element_type=F32)
            a_scr[rs, cs] = (zz_ref[rs, cs] * (mixed + bias[:, cs])).astype(BF16)
    return a_scr[...]


def _post_body(*refs, prologue, n_res, has_bias, final, n_pt):
    n_in = {"plain": 1, "gmlp": 5}[prologue]
    pro, refs = refs[:n_in], refs[n_in:]
    res_refs, refs = refs[:n_res], refs[n_res:]
    wo_ref, refs = refs[0], refs[1:]
    if has_bias:
        bo_ref, refs = refs[0], refs[1:]
    fnw_ref, w1_ref, w2_ref, refs = refs[0], refs[1], refs[2], refs[3:]
    if final:
        finw_ref, refs = refs[0], refs[1:]
    n_main = 2 if final else 1
    outs, refs = refs[:n_main], refs[n_main:]
    is_sample = pl.program_id(0) >= n_pt

    if prologue == "plain":
        a = pro[0][...]
    else:
        vs_ref, a_scr = refs
        a = _gmlp_gate(*pro, vs_ref, a_scr, is_sample)

    x = _read_row_tile(res_refs, n_pt) + jnp.dot(a, wo_ref[...], preferred_element_type=F32)
    if has_bias:
        x = x + bo_ref[...]
    xn = _rms(x, fnw_ref[...]).astype(BF16)
    for c in range(D_FF // FFN_CK):
        sl = slice(c * FFN_CK, (c + 1) * FFN_CK)
        h = jnp.dot(xn, w1_ref[:, sl], preferred_element_type=F32)
        h = jnp.square(jnp.maximum(h, 0.0)).astype(BF16)
        x = x + jnp.dot(h, w2_ref[sl, :], preferred_element_type=F32)
    if final:
        y = _rms(x, finw_ref[...])

        @pl.when(jnp.logical_not(is_sample))
        def _():
            outs[0][...] = y

        @pl.when(is_sample)
        def _():
            outs[1][...] = y
    else:
        outs[0][...] = x


def _post(pro_args, res, wo, bo, fnw, w1, w2, finw, *, layer, prologue, t, n_pt, name):
    has_bias = bo is not None
    final = finw is not None
    tile = lambda c: pl.BlockSpec((ROW_TILE, c), lambda i: (i, 0))
    if prologue == "plain":
        in_specs = [tile(D_MODEL)]
    else:
        in_specs = [tile(2 * D_MODEL), _resident((1, D_MODEL)), _resident((1, D_MODEL)),
                    _resident((GMLP_GROUPS, GMLP_BLOCK, GMLP_BLOCK)),
                    _resident((GMLP_BLOCK, D_MODEL))]
    args = list(pro_args)
    res_specs, res_args = _row_tile_specs(res, n_pt)
    in_specs += res_specs + [_resident((D_MODEL, D_MODEL))]
    args += res_args + [wo]
    if has_bias:
        in_specs.append(_resident((1, D_MODEL)))
        args.append(bo)
    layer_block = lambda shape: pl.BlockSpec((None,) + shape, lambda i: (layer, 0, 0),
                                             pipeline_mode=pl.Buffered(1))
    in_specs += [_resident((1, D_MODEL)), layer_block((D_MODEL, D_FF)), layer_block((D_FF, D_MODEL))]
    args += [fnw, w1, w2]
    if final:
        in_specs.append(_resident((1, D_MODEL)))
        args.append(finw)
        out_specs = [pl.BlockSpec((ROW_TILE, D_MODEL), lambda i: (jnp.minimum(i, n_pt - 1), 0)),
                     pl.BlockSpec((ROW_TILE, D_MODEL), lambda i: (jnp.maximum(i - n_pt, 0), 0))]
        out_shape = [jax.ShapeDtypeStruct((n_pt * ROW_TILE, D_MODEL), F32),
                     jax.ShapeDtypeStruct((t - n_pt * ROW_TILE, D_MODEL), F32)]
    else:
        out_specs = [tile(D_MODEL)]
        out_shape = [jax.ShapeDtypeStruct((t, D_MODEL), F32)]
    scratch = []
    if prologue == "gmlp":
        out_specs.append(pl.BlockSpec((ROW_TILE, D_MODEL), lambda i: (jnp.maximum(i - n_pt, 0), 0)))
        out_shape.append(jax.ShapeDtypeStruct((t - n_pt * ROW_TILE, D_MODEL), F32))
        scratch.append(pltpu.VMEM((ROW_TILE, D_MODEL), BF16))
    return pl.pallas_call(
        functools.partial(_post_body, prologue=prologue, n_res=len(res_args), has_bias=has_bias,
                          final=final, n_pt=n_pt),
        grid=(t // ROW_TILE,),
        in_specs=in_specs, out_specs=out_specs, out_shape=out_shape, scratch_shapes=scratch,
        compiler_params=_params(), name=name)(*args)


GDN_NB = 4
GDN_WIN = CHUNK + 8
GDN_GROUPS = ROW_TILE // CHUNK
GDN_OP_K, GDN_OP_QE, GDN_OP_KD, GDN_OP_GATE, GDN_OP_VK, GDN_OPS = 0, 1, 2, 3, 4, 6


def _gdn_gates(bg, alog_ref, dtb_ref):
    glog = -jnp.exp(alog_ref[...]) * _softplus(bg + dtb_ref[...])
    return _sigmoid(bg), _cumsum_rows(glog)


def _gdn_in_body(*refs, n_x, n_pt, tps, n_p):
    x_refs = refs[:n_x]
    (nw_ref, w_ref, cw_ref, alog_ref, dtb_ref, halo0_ref,
     ops_ref, kbq_ref, bg_ref, tails_ref, xc_scr, carry_scr) = refs[n_x:]
    i = pl.program_id(0)
    is_sample = i >= n_pt

    @pl.when(i == 0)
    def _():
        carry_scr[...] = jnp.zeros_like(carry_scr)

    xn = _rms(_read_row_tile(x_refs, n_pt), nw_ref[...]).astype(BF16)
    bg = jnp.dot(xn, w_ref[:, GDN_BG0:GDN_BG0 + LANES], preferred_element_type=F32)
    bg_ref[...] = bg
    beta, egc, e_rest = [], [], []
    for j in range(GDN_GROUPS):
        b_j, gc = _gdn_gates(bg[j * CHUNK:(j + 1) * CHUNK], alog_ref, dtb_ref)
        beta.append(b_j)
        egc.append(jnp.exp(gc))
        e_rest.append(jnp.exp(gc[CHUNK - 1:CHUNK, :] - gc))

    def project(c0, n):
        cols = slice(c0, c0 + n)
        y = jnp.dot(xn, w_ref[:, cols], preferred_element_type=F32)
        for j in range(GDN_GROUPS):
            r0 = j * GDN_WIN
            xc_scr[r0 + 8:r0 + GDN_WIN, cols] = y[j * CHUNK:(j + 1) * CHUNK]
            if j == 0:
                first = jnp.logical_or(is_sample, i % tps == 0)
                prev = carry_scr[:, cols]
            else:
                first = is_sample
                prev = xc_scr[r0 - 8:r0, cols]
            seq = jnp.where(is_sample, n_p + (i - n_pt) * GDN_GROUPS + j, i // tps)
            xc_scr[r0:r0 + 8, cols] = jnp.where(first, halo0_ref[seq, :, cols], prev)

    def conv_silu(j, c0):
        cw = [cw_ref[w:w + 1, c0:c0 + LANES] for w in range(GDN_CONV)]
        wv = xc_scr[j * GDN_WIN:(j + 1) * GDN_WIN, c0:c0 + LANES]
        back2 = pltpu.roll(wv, 2, 0)
        odd = pltpu.roll(cw[2] * wv + cw[0] * back2, 1, 0)
        return _silu(cw[3] * wv[8:GDN_WIN] + cw[1] * back2[8:GDN_WIN] + odd[8:GDN_WIN])

    def epilogue(h):
        cs = slice(h * GDN_DK, (h + 1) * GDN_DK)
        for j in range(GDN_GROUPS):
            rs = slice(j * CHUNK, (j + 1) * CHUNK)
            q = conv_silu(j, h * GDN_DK)
            k = conv_silu(j, GDN_QK + h * GDN_DK)
            v = conv_silu(j, 2 * GDN_QK + h * GDN_DV)
            q = q * lax.rsqrt(jnp.sum(q * q, axis=-1, keepdims=True) + EPS) * (GDN_DK ** -0.5)
            k = k * lax.rsqrt(jnp.sum(k * k, axis=-1, keepdims=True) + EPS)
            beta_h = beta[j][:, h:h + 1]
            e_h = egc[j][:, 8 + h:9 + h]
            kb = k * beta_h
            col = lambda part, c0=h * GDN_DK: slice(part * D_MODEL + c0, part * D_MODEL + c0 + GDN_DK)
            ops_ref[rs, col(GDN_OP_K)] = k.astype(BF16)
            ops_ref[rs, col(GDN_OP_QE)] = (q * e_h).astype(BF16)
            ops_ref[rs, col(GDN_OP_KD)] = (k * e_rest[j][:, 8 + h:9 + h]).astype(BF16)
            vk0 = GDN_OP_VK * D_MODEL + 2 * h * GDN_DV
            ops_ref[rs, vk0:vk0 + GDN_DV] = (v * beta_h).astype(BF16)
            ops_ref[rs, vk0 + GDN_DV:vk0 + 2 * GDN_DV] = (kb * e_h).astype(BF16)
            kbq_ref[j, 0:CHUNK, cs] = kb.astype(BF16)
            kbq_ref[j, CHUNK:2 * CHUNK, cs] = q.astype(BF16)

    pair = 2 * GDN_DK
    n_pairs = GDN_HEADS // 2
    for m in range(n_pairs + 1):
        if m < n_pairs:
            for part in range(3):
                project(part * GDN_QK + m * pair, pair)
        if m > 0:
            epilogue(2 * m - 2)
            epilogue(2 * m - 1)
    for c in range(2):
        z = jnp.dot(xn, w_ref[:, GDN_Z0 + c * FFN_CK:GDN_Z0 + (c + 1) * FFN_CK],
                    preferred_element_type=F32)
        ops_ref[:, GDN_OP_GATE * D_MODEL + c * FFN_CK:GDN_OP_GATE * D_MODEL + (c + 1) * FFN_CK] = (
            _silu(z).astype(BF16))
    for j in range(GDN_GROUPS):
        tails_ref[j] = xc_scr[(j + 1) * GDN_WIN - 8:(j + 1) * GDN_WIN, :]
    carry_scr[...] = xc_scr[GDN_GROUPS * GDN_WIN - 8:GDN_GROUPS * GDN_WIN, :]


def _gdn_in(x, nw, w, cw, alog, dtb, halo0, *, t, n_pt, tps, n_p):
    in_specs, args = _row_tile_specs(x, n_pt)
    n_x = len(args)
    n_seq = halo0.shape[0]
    in_specs += [_resident((1, D_MODEL)), _resident((D_MODEL, GDN_P)),
                 _resident((GDN_CONV, GDN_CONV_CH)), _resident((1, LANES)), _resident((1, LANES)),
                 _resident((n_seq, 8, GDN_CONV_CH))]
    args += [nw, w, cw, alog, dtb, halo0]
    tile = lambda c: pl.BlockSpec((ROW_TILE, c), lambda i: (i, 0))
    n_chunks = t // CHUNK
    bf = lambda c: jax.ShapeDtypeStruct((t, c), BF16)
    return pl.pallas_call(
        functools.partial(_gdn_in_body, n_x=n_x, n_pt=n_pt, tps=tps, n_p=n_p),
        grid=(t // ROW_TILE,),
        in_specs=in_specs,
        out_specs=[tile(GDN_OPS * D_MODEL),
                   pl.BlockSpec((GDN_GROUPS, 2 * CHUNK, D_MODEL), lambda i: (i, 0, 0)),
                   tile(LANES),
                   pl.BlockSpec((GDN_GROUPS, 8, GDN_CONV_CH), lambda i: (i, 0, 0))],
        out_shape=[bf(GDN_OPS * D_MODEL), jax.ShapeDtypeStruct((n_chunks, 2 * CHUNK, D_MODEL), BF16),
                   jax.ShapeDtypeStruct((t, LANES), F32),
                   jax.ShapeDtypeStruct((n_chunks, 8, GDN_CONV_CH), F32)],
        scratch_shapes=[pltpu.VMEM((GDN_GROUPS * GDN_WIN, GDN_CONV_CH), F32),
                        pltpu.VMEM((8, GDN_CONV_CH), F32)],
        compiler_params=_params(), name="gdn_in")(*args)


def _gdn_pre_body(k_ref, kbq_ref, vk_ref, qe_ref, bg_ref, alog_ref, dtb_ref,
                  u_ref, wq_ref, attn_ref, egl_ref):
    pw = 2 * CHUNK
    row = lax.broadcasted_iota(jnp.int32, (CHUNK, pw), 0)
    lane = lax.broadcasted_iota(jnp.int32, (CHUNK, pw), 1)
    col = lane & (CHUNK - 1)
    left = lane < CHUNK
    incl = row >= col
    strict = row > col
    eye = (row == col).astype(F32)

    def level_mask(s):
        sh = int(math.log2(s))
        bi, bj = row >> sh, col >> sh
        return jnp.logical_and((bi & 1) == 1, bj == bi - 1)

    masks = [level_mask(s) for s in (1, 2, 4, 8, 16, 32)]
    r2 = lax.broadcasted_iota(jnp.int32, (pw, pw), 0)
    l2 = lax.broadcasted_iota(jnp.int32, (pw, pw), 1)
    same_head = (r2 < CHUNK) == (l2 < CHUNK)

    def block_diag(x):
        return jnp.where(same_head, jnp.concatenate([x, x], axis=0), jnp.zeros((pw, pw), x.dtype))

    probs = [(j, m) for j in range(GDN_NB) for m in range(GDN_HEADS // 2)]
    n = len(probs)
    rs = [slice(j * CHUNK, (j + 1) * CHUNK) for j, _ in probs]
    hc = lambda h: slice(h * GDN_DK, (h + 1) * GDN_DK)
    gcum, gcum_t = [], []
    for j in range(GDN_NB):
        _, gc = _gdn_gates(bg_ref[j * CHUNK:(j + 1) * CHUNK, :], alog_ref, dtb_ref)
        gcum.append(gc)
        gt = jnp.transpose(jnp.concatenate([gc, jnp.zeros_like(gc)], axis=0))
        gcum_t.append(gt)
        egl_ref[j] = jnp.broadcast_to(jnp.exp(gt[8:8 + GDN_HEADS, CHUNK - 1:CHUNK]),
                                      (GDN_HEADS, LANES))
    decay = []
    for j, m in probs:
        h0, h1 = 8 + 2 * m, 9 + 2 * m
        g_col = jnp.where(left, gcum[j][:, h0:h0 + 1], gcum[j][:, h1:h1 + 1])
        g_row = jnp.concatenate([gcum_t[j][h0:h0 + 1, 0:CHUNK], gcum_t[j][h1:h1 + 1, 0:CHUNK]], axis=1)
        decay.append(jnp.exp(jnp.minimum(g_col - g_row, 0.0)))
    qk = []
    for i, (j, m) in enumerate(probs):
        k_pair = jnp.concatenate([k_ref[rs[i], hc(2 * m)], k_ref[rs[i], hc(2 * m + 1)]], axis=0)
        r0, r1 = (lax.dot_general(kbq_ref[j, :, hc(h)], k_pair, (((1,), (1,)), ((), ())),
                                  preferred_element_type=F32) for h in (2 * m, 2 * m + 1))
        qk.append(jnp.where(l2 < CHUNK, r0, r1))
    a = [jnp.where(strict, qk[i][0:CHUNK] * decay[i], 0.0) for i in range(n)]
    for i, (j, m) in enumerate(probs):
        attn = jnp.where(incl, qk[i][CHUNK:2 * CHUNK] * decay[i], 0.0)
        attn_ref[rs[i], m * pw:(m + 1) * pw] = attn.astype(attn_ref.dtype)
    t = [eye - jnp.where(masks[0], ai, 0.0) for ai in a]
    a_b = [ai.astype(BF16) for ai in a]
    for lm in masks[1:]:
        t_b = [ti.astype(BF16) for ti in t]
        at = [jnp.dot(jnp.where(lm, a_b[i], jnp.zeros_like(a_b[i])), block_diag(t_b[i]),
                      preferred_element_type=F32) for i in range(n)]
        t = [t[i] - jnp.dot(t_b[i], block_diag(at[i].astype(BF16)), preferred_element_type=F32)
             for i in range(n)]
    zeros = jnp.zeros((CHUNK, 2 * GDN_DV), BF16)
    for i, (j, m) in enumerate(probs):
        v0 = vk_ref[rs[i], 4 * m * GDN_DV:(4 * m + 2) * GDN_DV]
        v1 = vk_ref[rs[i], (4 * m + 2) * GDN_DV:(4 * m + 4) * GDN_DV]
        rhs = jnp.concatenate([jnp.concatenate([v0, zeros], axis=1),
                               jnp.concatenate([zeros, v1], axis=1)], axis=0)
        uw = jnp.dot(t[i].astype(BF16), rhs, preferred_element_type=F32)
        for q, h in enumerate((2 * m, 2 * m + 1)):
            u_ref[rs[i], hc(h)] = uw[:, 2 * q * GDN_DV:(2 * q + 1) * GDN_DV]
            wq_ref[j, 0:CHUNK, hc(h)] = uw[:, (2 * q + 1) * GDN_DV:(2 * q + 2) * GDN_DV].astype(wq_ref.dtype)
            wq_ref[j, CHUNK:2 * CHUNK, hc(h)] = qe_ref[rs[i], hc(h)]


def _gdn_pre(ops, kbq, bgp, alog, dtb):
    t = ops.shape[0]
    rows = GDN_NB * CHUNK
    n_chunks = t // CHUNK
    tile = lambda c: pl.BlockSpec((rows, c), lambda b: (b, 0))
    chunked = lambda r, c: pl.BlockSpec((GDN_NB, r, c), lambda b: (b, 0, 0))
    op_block = lambda part, width: pl.BlockSpec((rows, width * D_MODEL), lambda b: (b, part // width))
    return pl.pallas_call(
        _gdn_pre_body,
        grid=(t // rows,),
        in_specs=[op_block(GDN_OP_K, 1), chunked(2 * CHUNK, D_MODEL), op_block(GDN_OP_VK, 2),
                  op_block(GDN_OP_QE, 1), tile(LANES), _resident((1, LANES)), _resident((1, LANES))],
        out_specs=[tile(D_MODEL), chunked(2 * CHUNK, D_MODEL), tile(GDN_HEADS * CHUNK),
                   chunked(GDN_HEADS, LANES)],
        out_shape=[jax.ShapeDtypeStruct((t, D_MODEL), F32),
                   jax.ShapeDtypeStruct((n_chunks, 2 * CHUNK, D_MODEL), BF16),
                   jax.ShapeDtypeStruct((t, GDN_HEADS * CHUNK), BF16),
                   jax.ShapeDtypeStruct((n_chunks, GDN_HEADS, LANES), F32)],
        compiler_params=_params(), name="gdn_pre")(ops, kbq, ops, ops, bgp, alog, dtb)


def _gdn_scan_body(u_ref, wq_ref, kd_ref, attn_ref, egl_ref, gate_ref, s0_ref, nw_ref,
                   o_ref, soutp_ref, souts_ref, s_scr, *, n_pb, bps):
    blk, is_prompt, first, last = _block_flags(n_pb, bps)

    @pl.when(first)
    def _():
        s_scr[...] = jnp.zeros_like(s_scr)

    heads = range(GDN_HEADS)
    cs = [slice(h * GDN_DK, (h + 1) * GDN_DK) for h in heads]
    s = [s_scr[h] for h in heads]
    for j in range(SCAN_NB):
        rs = slice(j * CHUNK, (j + 1) * CHUNK)
        s = [jnp.where(is_prompt, s[h], s0_ref[j, h]) for h in heads]
        ws_qs = [jnp.dot(wq_ref[j, :, cs[h]], s[h].astype(BF16), preferred_element_type=F32)
                 for h in heads]
        v_new = [(u_ref[rs, cs[h]] - ws_qs[h][0:CHUNK]).astype(BF16) for h in heads]
        o = [ws_qs[h][CHUNK:2 * CHUNK]
             + jnp.dot(attn_ref[rs, h * CHUNK:(h + 1) * CHUNK], v_new[h], preferred_element_type=F32)
             for h in heads]
        s = [s[h] * egl_ref[j, h:h + 1, :] + _mm_tn(kd_ref[rs, cs[h]], v_new[h]) for h in heads]
        for h in heads:
            o_ref[rs, cs[h]] = (_rms(o[h], nw_ref[...]) * gate_ref[rs, cs[h]]).astype(o_ref.dtype)

        @pl.when(jnp.logical_not(is_prompt))
        def _(j=j, s=s):
            for h in heads:
                souts_ref[j, h] = s[h]

    for h in heads:
        s_scr[h] = s[h]

    @pl.when(last)
    def _():
        for h in heads:
            soutp_ref[0, h] = s[h]


def _gdn_scan(u, wq, ops, attn, egl, s0, nw, *, n_p, cpl, n_s):
    t = u.shape[0]
    rows = SCAN_NB * CHUNK
    bps = cpl // SCAN_NB
    n_pb = n_p * bps
    row_blk = lambda b: (b, 0)
    s_in, s_out, s_shapes = _state_specs(n_p, n_s, n_pb, bps, (GDN_HEADS, GDN_DK, GDN_DV))
    return pl.pallas_call(
        functools.partial(_gdn_scan_body, n_pb=n_pb, bps=bps),
        grid=(t // rows,),
        in_specs=[pl.BlockSpec((rows, D_MODEL), row_blk),
                  pl.BlockSpec((SCAN_NB, 2 * CHUNK, D_MODEL), lambda b: (b, 0, 0)),
                  pl.BlockSpec((rows, D_MODEL), lambda b: (b, GDN_OP_KD)),
                  pl.BlockSpec((rows, GDN_HEADS * CHUNK), row_blk),
                  pl.BlockSpec((SCAN_NB, GDN_HEADS, LANES), lambda b: (b, 0, 0)),
                  pl.BlockSpec((rows, D_MODEL), lambda b: (b, GDN_OP_GATE)),
                  s_in, _resident((1, GDN_DV))],
        out_specs=[pl.BlockSpec((rows, D_MODEL), row_blk)] + s_out,
        out_shape=[jax.ShapeDtypeStruct((t, D_MODEL), BF16)] + s_shapes,
        scratch_shapes=[pltpu.VMEM((GDN_HEADS, GDN_DK, GDN_DV), F32)],
        compiler_params=_params(), name="gdn_scan")(u, wq, ops, attn, egl, ops, s0, nw)


def _gla_exponent_matrix():
    i = np.arange(CHUNK)[:, None]
    t = np.arange(CHUNK)[None, :]
    blocks = [(t <= i)]
    for s in GLA_LEVELS:
        c = (i // (2 * s)) * (2 * s) + s - 1
        blocks.append(np.where(i > c, (t > c) & (t <= i), (t > i) & (t <= c)))
    w = np.concatenate(blocks, axis=0).astype(np.float32)
    return np.concatenate([w, w, w], axis=1)


def _split3(x):
    hi = x.astype(BF16)
    r1 = x - hi.astype(F32)
    mid = r1.astype(BF16)
    lo = (r1 - mid.astype(F32)).astype(BF16)
    return hi, mid, lo


def _gla_body(p_ref, s0_ref, wg2_ref, bg_ref, wexp_ref, nw_ref,
              o_ref, soutp_ref, souts_ref, st_scr, s0t_scr, *, n_pb, bps):
    blk, is_prompt, first, last = _block_flags(n_pb, bps)
    heads = range(GLA_HEADS)
    chunks = range(SCAN_NB)

    @pl.when(first)
    def _():
        st_scr[...] = jnp.zeros_like(st_scr)

    @pl.when(blk == 0)
    def _():
        s0t_scr[...] = jnp.zeros_like(s0t_scr)

    @pl.when(jnp.logical_not(is_prompt))
    def _():
        for j in chunks:
            for h in heads:
                s0t_scr[j, h] = s0_ref[j, h].T

    row = lax.broadcasted_iota(jnp.int32, (CHUNK, CHUNK), 0)
    col = lax.broadcasted_iota(jnp.int32, (CHUNK, CHUNK), 1)
    eye = row == col
    masks = [_pair_mask(s) for s in GLA_LEVELS]
    rs = [slice(j * CHUNK, (j + 1) * CHUNK) for j in chunks]
    ck = [slice(h * GLA_DK, (h + 1) * GLA_DK) for h in heads]
    cv = [slice(h * GLA_DV, (h + 1) * GLA_DV) for h in heads]

    x = [_mm(p_ref[rs[j], GLA_G0:GLA_G0 + LANES], wg2_ref[...]) + bg_ref[...] for j in chunks]
    glog = [-_softplus(-x[j]) * (1.0 / GLA_TAU) for j in chunks]
    e_all = [jnp.dot(wexp_ref[...], jnp.concatenate(_split3(glog[j]), axis=0),
                     preferred_element_type=F32) for j in chunks]
    q_all = [p_ref[rs[j], 0:GLA_QK] * (GLA_DK ** -0.5) for j in chunks]
    k_all = [p_ref[rs[j], GLA_QK:2 * GLA_QK] for j in chunks]
    qb = [q.astype(BF16) for q in q_all]
    kb = [k.astype(BF16) for k in k_all]
    attn = [[jnp.where(eye, _mm_nt(qb[j][:, ck[h]], kb[j][:, ck[h]]), 0.0) for h in heads]
            for j in chunks]
    for lvl, m in enumerate(masks):
        f = [jnp.exp(e_all[j][(lvl + 1) * CHUNK:(lvl + 2) * CHUNK, :]) for j in chunks]
        qf = [(q_all[j] * f[j]).astype(BF16) for j in chunks]
        kf = [(k_all[j] * f[j]).astype(BF16) for j in chunks]
        attn = [[attn[j][h] + jnp.where(m, _mm_nt(qf[j][:, ck[h]], kf[j][:, ck[h]]), 0.0)
                 for h in heads] for j in chunks]
    b = [e_all[j][0:CHUNK, :] for j in chunks]
    b_last = [b[j][CHUNK - 1:CHUNK, :] for j in chunks]
    qe = [(q_all[j] * jnp.exp(b[j])).astype(BF16) for j in chunks]
    kd = [(k_all[j] * jnp.exp(b_last[j] - b[j])).astype(BF16) for j in chunks]
    eb_last = [jnp.exp(b_last[j]) for j in chunks]
    vb = [p_ref[rs[j], GLA_V0:GLA_V0 + D_MODEL].astype(BF16) for j in chunks]
    av = [[_mm(attn[j][h], vb[j][:, cv[h]]) for h in heads] for j in chunks]
    kv = [[_mm_tn(vb[j][:, cv[h]], kd[j][:, ck[h]]) for h in heads] for j in chunks]

    st = [st_scr[h] for h in heads]
    for j in chunks:
        st = [jnp.where(is_prompt, st[h], s0t_scr[j, h]) for h in heads]
        o = [_mm_nt(qe[j][:, ck[h]], st[h]) + av[j][h] for h in heads]
        st = [st[h] * eb_last[j][:, ck[h]] + kv[j][h] for h in heads]
        for h in heads:
            r = p_ref[rs[j], GLA_R0 + h * GLA_DV:GLA_R0 + (h + 1) * GLA_DV]
            o_ref[rs[j], cv[h]] = (_rms(o[h], nw_ref[h:h + 1, :]) * _silu(r)).astype(o_ref.dtype)

        @pl.when(jnp.logical_not(is_prompt))
        def _(j=j, st=st):
            for h in heads:
                souts_ref[j, h] = st[h].T

    for h in heads:
        st_scr[h] = st[h]

    @pl.when(last)
    def _():
        for h in heads:
            soutp_ref[0, h] = st[h].T


def _gla_scan(p, s0, wg2, bg, wexp, nw, *, n_p, cpl, n_s):
    t = p.shape[0]
    rows = SCAN_NB * CHUNK
    bps = cpl // SCAN_NB
    n_pb = n_p * bps
    s_in, s_out, s_shapes = _state_specs(n_p, n_s, n_pb, bps, (GLA_HEADS, GLA_DK, GLA_DV))
    return pl.pallas_call(
        functools.partial(_gla_body, n_pb=n_pb, bps=bps),
        grid=(t // rows,),
        in_specs=[pl.BlockSpec((rows, GLA_P), lambda b: (b, 0)), s_in,
                  _resident((LANES, GLA_QK)), _resident((1, GLA_QK)),
                  _resident(((len(GLA_LEVELS) + 1) * CHUNK, 3 * CHUNK)),
                  _resident((GLA_HEADS, GLA_DV))],
        out_specs=[pl.BlockSpec((rows, D_MODEL), lambda b: (b, 0))] + s_out,
        out_shape=[jax.ShapeDtypeStruct((t, D_MODEL), BF16)] + s_shapes,
        scratch_shapes=[pltpu.VMEM((GLA_HEADS, GLA_DV, GLA_DK), F32),
                        pltpu.VMEM((SCAN_NB, GLA_HEADS, GLA_DV, GLA_DK), F32)],
        compiler_params=_params(), name="gla_scan")(p, s0, wg2, bg, wexp, nw)


CNV_WIN = CONV_HALO + CHUNK
TILE_CHUNKS = ROW_TILE // CHUNK
CNV_CK = 256


def _cnv_in_body(*refs, n_x, n_pt, tps, n_p):
    x_refs = refs[:n_x]
    (nw_ref, w_ref, b_ref, wdw_ref, bdw_ref, lnw_ref, lnb_ref, halo0_ref,
     a_ref, tails_ref, xc_scr, y_scr, carry_scr) = refs[n_x:]
    i = pl.program_id(0)
    is_sample = i >= n_pt

    @pl.when(i == 0)
    def _():
        carry_scr[...] = jnp.zeros_like(carry_scr)

    xn = _rms(_read_row_tile(x_refs, n_pt), nw_ref[...]).astype(BF16)
    taps = {}
    for w in range(CONV_W):
        s = CONV_HALO - (CONV_W - 1) + w
        taps.setdefault(-s % 8, []).append((w, s + (-s % 8)))

    def glu_slab(c0):
        cols = slice(c0, c0 + CNV_CK)
        gcols = slice(D_MODEL + c0, D_MODEL + c0 + CNV_CK)
        val = jnp.dot(xn, w_ref[:, cols], preferred_element_type=F32) + b_ref[:, cols]
        gate = jnp.dot(xn, w_ref[:, gcols], preferred_element_type=F32) + b_ref[:, gcols]
        glu = val * _sigmoid(gate)
        for j in range(TILE_CHUNKS):
            r0 = j * CNV_WIN
            xc_scr[r0 + CONV_HALO:r0 + CNV_WIN, cols] = glu[j * CHUNK:(j + 1) * CHUNK]
            if j == 0:
                first = jnp.logical_or(is_sample, i % tps == 0)
                prev = carry_scr[:, cols]
            else:
                first = is_sample
                prev = xc_scr[r0 - CONV_HALO:r0, cols]
            seq = jnp.where(is_sample, n_p + (i - n_pt) * TILE_CHUNKS + j, i // tps)
            xc_scr[r0:r0 + CONV_HALO, cols] = jnp.where(first, halo0_ref[seq, :, cols], prev)

    def conv_slab(c0):
        for j in range(TILE_CHUNKS):
            for b0 in range(c0, c0 + CNV_CK, LANES):
                cs = slice(b0, b0 + LANES)
                wv = xc_scr[j * CNV_WIN:(j + 1) * CNV_WIN, cs]
                y = None
                for r, group in sorted(taps.items()):
                    rolled = wv if r == 0 else pltpu.roll(wv, r, 0)
                    for w, a0 in group:
                        term = wdw_ref[w:w + 1, cs] * rolled[a0:a0 + CHUNK]
                        y = term if y is None else y + term
                y_scr[j * CHUNK:(j + 1) * CHUNK, cs] = y + bdw_ref[:, cs]

    slabs = list(range(0, D_MODEL, CNV_CK))
    for n, c0 in enumerate(slabs + [None]):
        if c0 is not None:
            glu_slab(c0)
        if n > 0:
            conv_slab(slabs[n - 1])
    for j in range(TILE_CHUNKS):
        rs = slice(j * CHUNK, (j + 1) * CHUNK)
        a_ref[rs, :] = _silu(_layernorm(y_scr[rs, :], lnw_ref[...], lnb_ref[...])).astype(BF16)
        tails_ref[j] = xc_scr[(j + 1) * CNV_WIN - CONV_HALO:(j + 1) * CNV_WIN, :]
    carry_scr[...] = xc_scr[TILE_CHUNKS * CNV_WIN - CONV_HALO:TILE_CHUNKS * CNV_WIN, :]


def _cnv_in(x, nw, w, b, wdw, bdw, lnw, lnb, halo0, *, t, n_pt, tps, n_p):
    in_specs, args = _row_tile_specs(x, n_pt)
    n_x = len(args)
    in_specs += [_resident((1, D_MODEL)), _resident((D_MODEL, 2 * D_MODEL)), _resident((1, 2 * D_MODEL)),
                 _resident((CONV_HALO, D_MODEL)), _resident((1, D_MODEL)),
                 _resident((1, D_MODEL)), _resident((1, D_MODEL)), _resident(halo0.shape)]
    args += [nw, w, b, wdw, bdw, lnw, lnb, halo0]
    return pl.pallas_call(
        functools.partial(_cnv_in_body, n_x=n_x, n_pt=n_pt, tps=tps, n_p=n_p),
        grid=(t // ROW_TILE,),
        in_specs=in_specs,
        out_specs=[pl.BlockSpec((ROW_TILE, D_MODEL), lambda i: (i, 0)),
                   pl.BlockSpec((TILE_CHUNKS, CONV_HALO, D_MODEL), lambda i: (i, 0, 0))],
        out_shape=[jax.ShapeDtypeStruct((t, D_MODEL), BF16),
                   jax.ShapeDtypeStruct((t // CHUNK, CONV_HALO, D_MODEL), F32)],
        scratch_shapes=[pltpu.VMEM((TILE_CHUNKS * CNV_WIN, D_MODEL), F32),
                        pltpu.VMEM((ROW_TILE, D_MODEL), F32),
                        pltpu.VMEM((CONV_HALO, D_MODEL), F32)],
        compiler_params=_params(), name="cnv_in")(*args)


def _pad_cols(w, n):
    return jnp.pad(w, ((0, 0), (0, n - w.shape[1])))


def _row(v):
    return v.reshape(1, -1).astype(F32)


def _tail_rows(a, lay, n):
    n_p, cpl, n_s = lay
    ends = np.concatenate([(np.arange(n_p) + 1) * cpl * CHUNK,
                           n_p * cpl * CHUNK + (np.arange(n_s) + 1) * CHUNK])
    idx = (ends[:, None] - n + np.arange(n)[None, :]).reshape(-1)
    return jnp.take(a, jnp.asarray(idx, jnp.int32), axis=0).reshape(n_p + n_s, n, a.shape[1])


def _gdn_mix(x, lay, dims, mix_nw, cache, state, w_in, conv_w, a_log, dt_bias, norm_w):
    n_p, cpl, n_s = lay
    halo0 = jnp.concatenate([jnp.zeros((n_p, 8, GDN_CONV_CH), F32),
                             jnp.pad(cache, ((0, 0), (8 - (GDN_CONV - 1), 0), (0, 0)))], axis=0)
    lane_pad = lambda v: jnp.pad(v.reshape(1, -1), ((0, 0), (GDN_HEADS, LANES - 2 * GDN_HEADS)))
    alog, dtb = lane_pad(a_log), lane_pad(dt_bias)
    ops, kbq, bgp, tails = _gdn_in(
        x, _row(mix_nw), _pad_cols(w_in, GDN_P).astype(BF16), conv_w, alog, dtb, halo0,
        tps=cpl * CHUNK // ROW_TILE, n_p=n_p, **dims)
    u, wq, attn, egl = _gdn_pre(ops, kbq, bgp, alog, dtb)
    o, s_p, s_s = _gdn_scan(u, wq, ops, attn, egl, state, _row(norm_w), n_p=n_p, cpl=cpl, n_s=n_s)
    ends = np.concatenate([(np.arange(n_p) + 1) * cpl - 1, n_p * cpl + np.arange(n_s)])
    conv_rows = tails[jnp.asarray(ends, jnp.int32), 8 - (GDN_CONV - 1):, :]
    return o, (conv_rows[:n_p], conv_rows[n_p:], s_p, s_s)


def _gla_mix(x, lay, dims, mix_nw, state, w_in, w_g2, b_g, norm_w):
    n_p, cpl, n_s = lay
    p = _proj(x, _row(mix_nw), _pad_cols(w_in, GLA_P).astype(BF16), None,
              n_out=GLA_P, ck=512, name="gla_in", **dims)
    wg2 = jnp.pad(w_g2, ((0, LANES - GLA_RANK), (0, 0))).astype(BF16)
    wexp = jnp.asarray(_gla_exponent_matrix(), BF16)
    o, s_p, s_s = _gla_scan(p, state, wg2, _row(b_g), wexp, norm_w, n_p=n_p, cpl=cpl, n_s=n_s)
    return o, (s_p, s_s)


def _cnv_mix(x, lay, dims, mix_nw, cache, w_pw1, b_pw1, w_dw, b_dw, ln_w, ln_b):
    n_p, cpl, n_s = lay
    pad_rows = CONV_HALO - (CONV_W - 1)
    halo0 = jnp.concatenate([jnp.zeros((n_p, CONV_HALO, D_MODEL), F32),
                             jnp.pad(cache, ((0, 0), (pad_rows, 0), (0, 0)))], axis=0)
    wdw = jnp.pad(w_dw, ((0, CONV_HALO - CONV_W), (0, 0)))
    a, tails = _cnv_in(x, _row(mix_nw), w_pw1.astype(BF16), _row(b_pw1), wdw, _row(b_dw),
                       _row(ln_w), _row(ln_b), halo0, tps=cpl * CHUNK // ROW_TILE, n_p=n_p, **dims)
    ends = np.concatenate([(np.arange(n_p) + 1) * cpl - 1, n_p * cpl + np.arange(n_s)])
    rows = tails[jnp.asarray(ends, jnp.int32), pad_rows:, :]
    return a, (rows[:n_p], rows[n_p:])


def kernel(x_prompt, x_sample, cache_gdn_conv, state_gdn, state_gla, cache_conformer, mix_norm_w, ffn_norm_w, ffn_w1, ffn_w2, final_norm_w, gdn_w_in, gdn_conv_w, gdn_a_log, gdn_dt_bias, gdn_norm_w, gdn_w_out, gla_w_in, gla_w_g2, gla_b_g, gla_norm_w, gla_w_out, gmlp_w_in, gmlp_b_in, gmlp_ln_w, gmlp_ln_b, gmlp_w_s, gmlp_b_s, gmlp_w_out, gmlp_b_out, cnv_w_pw1, cnv_b_pw1, cnv_w_dw, cnv_b_dw, cnv_ln_w, cnv_ln_b, cnv_w_pw2, cnv_b_pw2):
    n_p, l_p, d = x_prompt.shape
    n_s, l_s, _ = x_sample.shape
    assert d == D_MODEL and l_s == CHUNK and l_p % ROW_TILE == 0
    assert n_s % max(SCAN_NB, GDN_NB) == 0
    t_p, t_s = n_p * l_p, n_s * l_s
    assert t_p % ROW_TILE == 0 and t_s % ROW_TILE == 0
    lay = (n_p, l_p // CHUNK, n_s)
    dims = dict(t=t_p + t_s, n_pt=t_p // ROW_TILE)
    depth = mix_norm_w.shape[0]

    x = (x_prompt.reshape(t_p, d), x_sample.reshape(t_s, d))
    outs = {k: [] for k in ("gdn_cp", "gdn_cs", "gdn_sp", "gdn_ss", "gla_sp", "gla_ss",
                            "gmlp_vs", "cnv_p", "cnv_s")}
    w1_all, w2_all = ffn_w1.astype(BF16), ffn_w2.astype(BF16)
    for i in range(depth):
        kind, j = i % 4, i // 4
        ffn = (_row(ffn_norm_w[i]), w1_all, w2_all, _row(final_norm_w) if i == depth - 1 else None)
        if kind == 0:
            o, (cp, cs, sp, ss) = _gdn_mix(x, lay, dims, mix_norm_w[i], cache_gdn_conv[j], state_gdn[j],
                                           gdn_w_in[j], gdn_conv_w[j], gdn_a_log[j], gdn_dt_bias[j],
                                           gdn_norm_w[j])
            outs["gdn_cp"].append(cp); outs["gdn_cs"].append(cs)
            outs["gdn_sp"].append(sp); outs["gdn_ss"].append(ss)
            res = _post([o], x, gdn_w_out[j].astype(BF16), None, *ffn,
                        prologue="plain", layer=i, name=f"post_{i}", **dims)
        elif kind == 1:
            o, (sp, ss) = _gla_mix(x, lay, dims, mix_norm_w[i], state_gla[j], gla_w_in[j], gla_w_g2[j],
                                   gla_b_g[j], gla_norm_w[j])
            outs["gla_sp"].append(sp); outs["gla_ss"].append(ss)
            res = _post([o], x, gla_w_out[j].astype(BF16), None, *ffn,
                        prologue="plain", layer=i, name=f"post_{i}", **dims)
        elif kind == 2:
            zz = _proj(x, _row(mix_norm_w[i]), gmlp_w_in[j].astype(BF16), _row(gmlp_b_in[j]),
                       n_out=2 * D_MODEL, ck=512, epilogue="gelu", name="gmlp_in", **dims)
            bs_full = jnp.repeat(gmlp_b_s[j].T, GMLP_GC, axis=1)
            res = _post([zz, _row(gmlp_ln_w[j]), _row(gmlp_ln_b[j]), gmlp_w_s[j], bs_full], x,
                        gmlp_w_out[j].astype(BF16), _row(gmlp_b_out[j]), *ffn,
                        prologue="gmlp", layer=i, name=f"post_{i}", **dims)
            outs["gmlp_vs"].append(res[-1].reshape(n_s, l_s, d))
            res = res[:-1]
        else:
            a, (cp, cs) = _cnv_mix(x, lay, dims, mix_norm_w[i], cache_conformer[j], cnv_w_pw1[j],
                                   cnv_b_pw1[j], cnv_w_dw[j], cnv_b_dw[j], cnv_ln_w[j], cnv_ln_b[j])
            outs["cnv_p"].append(cp); outs["cnv_s"].append(cs)
            res = _post([a], x, cnv_w_pw2[j].astype(BF16), _row(cnv_b_pw2[j]), *ffn,
                        prologue="plain", layer=i, name=f"post_{i}", **dims)
        x = tuple(res) if i == depth - 1 else res[0]
    y_p, y_s = x
    stack = lambda xs: xs[0][None] if len(xs) == 1 else jnp.stack(xs)
    return (y_p.reshape(n_p, l_p, d), y_s.reshape(n_s, l_s, d),
            stack(outs["gdn_cp"]), stack(outs["gdn_cs"]),
            stack(outs["gdn_sp"]), stack(outs["gdn_ss"]),
            stack(outs["gla_sp"]), stack(outs["gla_ss"]),
            stack(outs["gmlp_vs"]),
            stack(outs["cnv_p"]), stack(outs["cnv_s"]))
```

```python
import functools
import math

import numpy as np
import jax
import jax.numpy as jnp
from jax import lax
from jax.experimental import pallas as pl
from jax.experimental.pallas import tpu as pltpu

F32 = jnp.float32
BF16 = jnp.bfloat16

D_MODEL = 1024
D_FF = 4 * D_MODEL
EPS = 1e-6
CHUNK = 64
ROW_TILE = 512
SCAN_NB = 4
LANES = 128
VMEM_LIMIT = 56 * 1024 * 1024

GDN_HEADS, GDN_DK, GDN_DV, GDN_CONV = 8, 128, 128, 4
GDN_QK = GDN_HEADS * GDN_DK
GDN_CONV_CH = 2 * GDN_QK + GDN_HEADS * GDN_DV
GDN_Z0 = GDN_CONV_CH
GDN_BG0 = GDN_CONV_CH + GDN_HEADS * GDN_DV
GDN_P = GDN_BG0 + LANES

GLA_HEADS, GLA_DK, GLA_DV, GLA_RANK, GLA_TAU = 4, 128, 256, 16, 16.0
GLA_QK = GLA_HEADS * GLA_DK
GLA_V0, GLA_R0, GLA_G0 = 2 * GLA_QK, 2 * GLA_QK + D_MODEL, 2 * GLA_QK + 2 * D_MODEL
GLA_P = GLA_G0 + LANES
GLA_LEVELS = (32, 16, 8, 4, 2, 1)

GMLP_BLOCK, GMLP_GROUPS, GMLP_GC = 128, 4, 256
CONV_W = 31
CONV_HALO = 32


def _mm(a, b):
    return jnp.dot(a.astype(BF16), b.astype(BF16), preferred_element_type=F32)


def _mm_nt(a, b):
    return lax.dot_general(a.astype(BF16), b.astype(BF16), (((1,), (1,)), ((), ())),
                           preferred_element_type=F32)


def _mm_tn(a, b):
    return lax.dot_general(a.astype(BF16), b.astype(BF16), (((0,), (0,)), ((), ())),
                           preferred_element_type=F32)


def _rms(x, w):
    return x * lax.rsqrt(jnp.mean(x * x, axis=-1, keepdims=True) + EPS) * w


def _layernorm(x, w, b):
    xc = x - jnp.mean(x, axis=-1, keepdims=True)
    return xc * lax.rsqrt(jnp.mean(xc * xc, axis=-1, keepdims=True) + EPS) * w + b


def _sigmoid(x):
    return jax.nn.sigmoid(x)


def _silu(x):
    h = 0.5 * x
    return h + h * jnp.tanh(h)


def _softplus(x):
    return jnp.maximum(x, 0.0) + jnp.log1p(jnp.exp(-jnp.abs(x)))


def _gelu_tanh(x):
    return 0.5 * x * (1.0 + jnp.tanh(math.sqrt(2.0 / math.pi) * (x + 0.044715 * (x * x * x))))


def _cumsum_rows(x):
    row = lax.broadcasted_iota(jnp.int32, x.shape, 0)
    s = 1
    while s < x.shape[0]:
        x = x + jnp.where(row >= s, pltpu.roll(x, s, 0), 0.0)
        s *= 2
    return x


def _pair_mask(s):
    sh = int(math.log2(s))
    bi = lax.broadcasted_iota(jnp.int32, (CHUNK, CHUNK), 0) >> sh
    bj = lax.broadcasted_iota(jnp.int32, (CHUNK, CHUNK), 1) >> sh
    return jnp.logical_and((bi & 1) == 1, bj == bi - 1)


def _block_flags(n_pb, bps):
    blk = pl.program_id(0)
    is_prompt = blk < n_pb
    first = jnp.logical_and(is_prompt, blk % bps == 0)
    last = jnp.logical_and(is_prompt, blk % bps == bps - 1)
    return blk, is_prompt, first, last


def _state_specs(n_p, n_s, n_pb, bps, shape):
    zeros = (0,) * len(shape)
    sample = pl.BlockSpec((SCAN_NB,) + shape, lambda b: (jnp.maximum(b - n_pb, 0),) + zeros)
    prompt = pl.BlockSpec((1,) + shape, lambda b: (jnp.minimum(b // bps, n_p - 1),) + zeros)
    shapes = [jax.ShapeDtypeStruct((n_p,) + shape, F32), jax.ShapeDtypeStruct((n_s,) + shape, F32)]
    return sample, [prompt, sample], shapes


def _params(n_axes=1):
    return pltpu.CompilerParams(dimension_semantics=("arbitrary",) * n_axes,
                                vmem_limit_bytes=VMEM_LIMIT)


def _resident(shape):
    zeros = (0,) * len(shape)
    return pl.BlockSpec(shape, lambda i: zeros, pipeline_mode=pl.Buffered(1))


def _row_tile_specs(x, n_pt):
    if isinstance(x, tuple):
        return ([pl.BlockSpec((ROW_TILE, D_MODEL), lambda i: (jnp.minimum(i, n_pt - 1), 0)),
                 pl.BlockSpec((ROW_TILE, D_MODEL), lambda i: (jnp.maximum(i - n_pt, 0), 0))], list(x))
    return [pl.BlockSpec((ROW_TILE, D_MODEL), lambda i: (i, 0))], [x]


def _read_row_tile(refs, n_pt):
    if len(refs) == 2:
        return jnp.where(pl.program_id(0) < n_pt, refs[0][...], refs[1][...])
    return refs[0][...]


def _proj_body(*refs, n_x, n_pt, n_out, ck, epilogue, has_bias):
    x_refs, (nw_ref, w_ref), rest = refs[:n_x], refs[n_x:n_x + 2], refs[n_x + 2:]
    if has_bias:
        b_ref, o_ref = rest
    else:
        (o_ref,) = rest
    xn = _rms(_read_row_tile(x_refs, n_pt), nw_ref[...]).astype(BF16)
    for c0 in range(0, n_out, ck):
        sl = slice(c0, min(c0 + ck, n_out))
        y = jnp.dot(xn, w_ref[:, sl], preferred_element_type=F32)
        if has_bias:
            y = y + b_ref[:, sl]
        if epilogue == "gelu":
            y = _gelu_tanh(y)
        o_ref[:, sl] = y.astype(o_ref.dtype)


def _proj(x, nw, w, b, *, t, n_pt, n_out, ck, epilogue="none", out_dtype=F32, name):
    n_w = w.shape[1]
    has_bias = b is not None
    in_specs, args = _row_tile_specs(x, n_pt)
    n_x = len(args)
    in_specs += [_resident((1, D_MODEL)), _resident((D_MODEL, n_w))]
    args += [nw, w]
    if has_bias:
        in_specs.append(_resident((1, n_w)))
        args.append(b)
    return pl.pallas_call(
        functools.partial(_proj_body, n_x=n_x, n_pt=n_pt, n_out=n_out, ck=ck, epilogue=epilogue,
                          has_bias=has_bias),
        grid=(t // ROW_TILE,),
        in_specs=in_specs,
        out_specs=pl.BlockSpec((ROW_TILE, n_out), lambda i: (i, 0)),
        out_shape=jax.ShapeDtypeStruct((t, n_out), out_dtype),
        compiler_params=_params(), name=name)(*args)


FFN_CK = 512


def _gmlp_gate(zz_ref, lnw_ref, lnb_ref, ws_ref, bs_ref, vs_ref, a_scr, is_sample):
    v = _layernorm(zz_ref[:, D_MODEL:2 * D_MODEL], lnw_ref[...], lnb_ref[...])

    @pl.when(is_sample)
    def _():
        vs_ref[...] = v

    vb = v.astype(BF16)
    row = lax.broadcasted_iota(jnp.int32, (GMLP_BLOCK, GMLP_BLOCK), 0)
    col = lax.broadcasted_iota(jnp.int32, (GMLP_BLOCK, GMLP_BLOCK), 1)
    tril = col <= row
    top_left = jnp.logical_and(row < CHUNK, col < CHUNK)
    brow = lax.broadcasted_iota(jnp.int32, (GMLP_BLOCK, D_MODEL), 0)
    bias = bs_ref[...]
    bias = jnp.where(jnp.logical_and(is_sample, brow >= CHUNK), pltpu.roll(bias, CHUNK, 0), bias)
    for grp in range(GMLP_GROUPS):
        cs = slice(grp * GMLP_GC, (grp + 1) * GMLP_GC)
        wt = jnp.where(tril, ws_ref[grp], 0.0)
        w11 = jnp.where(top_left, wt, 0.0)
        wd = w11 + pltpu.roll(pltpu.roll(w11, CHUNK, 0), CHUNK, 1)
        we = jnp.where(is_sample, wd, wt).astype(BF16)
        for blk in range(ROW_TILE // GMLP_BLOCK):
            rs = slice(blk * GMLP_BLOCK, (blk + 1) * GMLP_BLOCK)
            mixed = jnp.dot(we, vb[rs, cs], preferred_element_type=F32)
            a_scr[rs, cs] = (zz_ref[rs, cs] * (mixed + bias[:, cs])).astype(BF16)
    return a_scr[...]


def _post_body(*refs, prologue, n_res, has_bias, final, n_pt):
    n_in = {"plain": 1, "gmlp": 5}[prologue]
    pro, refs = refs[:n_in], refs[n_in:]
    res_refs, refs = refs[:n_res], refs[n_res:]
    wo_ref, refs = refs[0], refs[1:]
    if has_bias:
        bo_ref, refs = refs[0], refs[1:]
    fnw_ref, w1_ref, w2_ref, refs = refs[0], refs[1], refs[2], refs[3:]
    if final:
        finw_ref, refs = refs[0], refs[1:]
    n_main = 2 if final else 1
    outs, refs = refs[:n_main], refs[n_main:]
    is_sample = pl.program_id(0) >= n_pt

    if prologue == "plain":
        a = pro[0][...]
    else:
        vs_ref, a_scr = refs
        a = _gmlp_gate(*pro, vs_ref, a_scr, is_sample)

    x = _read_row_tile(res_refs, n_pt) + jnp.dot(a, wo_ref[...], preferred_element_type=F32)
    if has_bias:
        x = x + bo_ref[...]
    xn = _rms(x, fnw_ref[...]).astype(BF16)
    for c in range(D_FF // FFN_CK):
        sl = slice(c * FFN_CK, (c + 1) * FFN_CK)
        h = jnp.dot(xn, w1_ref[:, sl], preferred_element_type=F32)
        h = jnp.square(jnp.maximum(h, 0.0)).astype(BF16)
        x = x + jnp.dot(h, w2_ref[sl, :], preferred_element_type=F32)
    if final:
        y = _rms(x, finw_ref[...])

        @pl.when(jnp.logical_not(is_sample))
        def _():
            outs[0][...] = y

        @pl.when(is_sample)
        def _():
            outs[1][...] = y
    else:
        outs[0][...] = x


def _post(pro_args, res, wo, bo, fnw, w1, w2, finw, *, layer, prologue, t, n_pt, name):
    has_bias = bo is not None
    final = finw is not None
    tile = lambda c: pl.BlockSpec((ROW_TILE, c), lambda i: (i, 0))
    if prologue == "plain":
        in_specs = [tile(D_MODEL)]
    else:
        in_specs = [tile(2 * D_MODEL), _resident((1, D_MODEL)), _resident((1, D_MODEL)),
                    _resident((GMLP_GROUPS, GMLP_BLOCK, GMLP_BLOCK)),
                    _resident((GMLP_BLOCK, D_MODEL))]
    args = list(pro_args)
    res_specs, res_args = _row_tile_specs(res, n_pt)
    in_specs += res_specs + [_resident((D_MODEL, D_MODEL))]
    args += res_args + [wo]
    if has_bias:
        in_specs.append(_resident((1, D_MODEL)))
        args.append(bo)
    layer_block = lambda shape: pl.BlockSpec((None,) + shape, lambda i: (layer, 0, 0),
                                             pipeline_mode=pl.Buffered(1))
    in_specs += [_resident((1, D_MODEL)), layer_block((D_MODEL, D_FF)), layer_block((D_FF, D_MODEL))]
    args += [fnw, w1, w2]
    if final:
        in_specs.append(_resident((1, D_MODEL)))
        args.append(finw)
        out_specs = [pl.BlockSpec((ROW_TILE, D_MODEL), lambda i: (jnp.minimum(i, n_pt - 1), 0)),
                     pl.BlockSpec((ROW_TILE, D_MODEL), lambda i: (jnp.maximum(i - n_pt, 0), 0))]
        out_shape = [jax.ShapeDtypeStruct((n_pt * ROW_TILE, D_MODEL), F32),
                     jax.ShapeDtypeStruct((t - n_pt * ROW_TILE, D_MODEL), F32)]
    else:
        out_specs = [tile(D_MODEL)]
        out_shape = [jax.ShapeDtypeStruct((t, D_MODEL), F32)]
    scratch = []
    if prologue == "gmlp":
        out_specs.append(pl.BlockSpec((ROW_TILE, D_MODEL), lambda i: (jnp.maximum(i - n_pt, 0), 0)))
        out_shape.append(jax.ShapeDtypeStruct((t - n_pt * ROW_TILE, D_MODEL), F32))
        scratch.append(pltpu.VMEM((ROW_TILE, D_MODEL), BF16))
    return pl.pallas_call(
        functools.partial(_post_body, prologue=prologue, n_res=len(res_args), has_bias=has_bias,
                          final=final, n_pt=n_pt),
        grid=(t // ROW_TILE,),
        in_specs=in_specs, out_specs=out_specs, out_shape=out_shape, scratch_shapes=scratch,
        compiler_params=_params(), name=name)(*args)


GDN_NB = 4
GDN_WIN = CHUNK + 8
GDN_GROUPS = ROW_TILE // CHUNK
GDN_OP_K, GDN_OP_QE, GDN_OP_KD, GDN_OP_GATE, GDN_OP_VK, GDN_OPS = 0, 1, 2, 3, 4, 6


def _gdn_gates(bg, alog_ref, dtb_ref):
    glog = -jnp.exp(alog_ref[...]) * _softplus(bg + dtb_ref[...])
    return _sigmoid(bg), _cumsum_rows(glog)


def _gdn_in_body(*refs, n_x, n_pt, tps, n_p):
    x_refs = refs[:n_x]
    (nw_ref, w_ref, cw_ref, alog_ref, dtb_ref, halo0_ref,
     ops_ref, kbq_ref, bg_ref, tails_ref, xc_scr, carry_scr) = refs[n_x:]
    i = pl.program_id(0)
    is_sample = i >= n_pt

    @pl.when(i == 0)
    def _():
        carry_scr[...] = jnp.zeros_like(carry_scr)

    xn = _rms(_read_row_tile(x_refs, n_pt), nw_ref[...]).astype(BF16)
    bg = jnp.dot(xn, w_ref[:, GDN_BG0:GDN_BG0 + LANES], preferred_element_type=F32)
    bg_ref[...] = bg
    beta, egc, e_rest = [], [], []
    for j in range(GDN_GROUPS):
        b_j, gc = _gdn_gates(bg[j * CHUNK:(j + 1) * CHUNK], alog_ref, dtb_ref)
        beta.append(b_j)
        egc.append(jnp.exp(gc))
        e_rest.append(jnp.exp(gc[CHUNK - 1:CHUNK, :] - gc))

    def project(c0, n):
        cols = slice(c0, c0 + n)
        y = jnp.dot(xn, w_ref[:, cols], preferred_element_type=F32)
        for j in range(GDN_GROUPS):
            r0 = j * GDN_WIN
            xc_scr[r0 + 8:r0 + GDN_WIN, cols] = y[j * CHUNK:(j + 1) * CHUNK]
            if j == 0:
                first = jnp.logical_or(is_sample, i % tps == 0)
                prev = carry_scr[:, cols]
            else:
                first = is_sample
                prev = y[j * CHUNK - 8:j * CHUNK]
            seq = jnp.where(is_sample, n_p + (i - n_pt) * GDN_GROUPS + j, i // tps)
            xc_scr[r0:r0 + 8, cols] = jnp.where(first, halo0_ref[seq, :, cols], prev)

    def conv_silu(j, c0):
        cw = [cw_ref[w:w + 1, c0:c0 + LANES] for w in range(GDN_CONV)]
        wv = xc_scr[j * GDN_WIN:(j + 1) * GDN_WIN, c0:c0 + LANES]
        back2 = pltpu.roll(wv, 2, 0)
        odd = pltpu.roll(cw[2] * wv + cw[0] * back2, 1, 0)
        return _silu(cw[3] * wv[8:GDN_WIN] + cw[1] * back2[8:GDN_WIN] + odd[8:GDN_WIN])

    def epilogue(h):
        cs = slice(h * GDN_DK, (h + 1) * GDN_DK)
        for j in range(GDN_GROUPS):
            rs = slice(j * CHUNK, (j + 1) * CHUNK)
            q = conv_silu(j, h * GDN_DK)
            k = conv_silu(j, GDN_QK + h * GDN_DK)
            v = conv_silu(j, 2 * GDN_QK + h * GDN_DV)
            q = q * lax.rsqrt(jnp.sum(q * q, axis=-1, keepdims=True) + EPS) * (GDN_DK ** -0.5)
            k = k * lax.rsqrt(jnp.sum(k * k, axis=-1, keepdims=True) + EPS)
            beta_h = beta[j][:, h:h + 1]
            e_h = egc[j][:, 8 + h:9 + h]
            kb = k * beta_h
            col = lambda part, c0=h * GDN_DK: slice(part * D_MODEL + c0, part * D_MODEL + c0 + GDN_DK)
            ops_ref[rs, col(GDN_OP_K)] = k.astype(BF16)
            ops_ref[rs, col(GDN_OP_QE)] = (q * e_h).astype(BF16)
            ops_ref[rs, col(GDN_OP_KD)] = (k * e_rest[j][:, 8 + h:9 + h]).astype(BF16)
            vk0 = GDN_OP_VK * D_MODEL + 2 * h * GDN_DV
            ops_ref[rs, vk0:vk0 + GDN_DV] = (v * beta_h).astype(BF16)
            ops_ref[rs, vk0 + GDN_DV:vk0 + 2 * GDN_DV] = (kb * e_h).astype(BF16)
            kbq_ref[j, 0:CHUNK, cs] = kb.astype(BF16)
            kbq_ref[j, CHUNK:2 * CHUNK, cs] = q.astype(BF16)

    pair = 2 * GDN_DK
    n_pairs = GDN_HEADS // 2
    for m in range(n_pairs + 1):
        if m < n_pairs:
            for part in range(3):
                project(part * GDN_QK + m * pair, pair)
        if m > 0:
            epilogue(2 * m - 2)
            epilogue(2 * m - 1)
    for c in range(2):
        z = jnp.dot(xn, w_ref[:, GDN_Z0 + c * FFN_CK:GDN_Z0 + (c + 1) * FFN_CK],
                    preferred_element_type=F32)
        ops_ref[:, GDN_OP_GATE * D_MODEL + c * FFN_CK:GDN_OP_GATE * D_MODEL + (c + 1) * FFN_CK] = (
            _silu(z).astype(BF16))
    for j in range(GDN_GROUPS):
        tails_ref[j] = xc_scr[(j + 1) * GDN_WIN - 8:(j + 1) * GDN_WIN, :]
    carry_scr[...] = xc_scr[GDN_GROUPS * GDN_WIN - 8:GDN_GROUPS * GDN_WIN, :]


def _gdn_in(x, nw, w, cw, alog, dtb, halo0, *, t, n_pt, tps, n_p):
    in_specs, args = _row_tile_specs(x, n_pt)
    n_x = len(args)
    n_seq = halo0.shape[0]
    in_specs += [_resident((1, D_MODEL)), _resident((D_MODEL, GDN_P)),
                 _resident((GDN_CONV, GDN_CONV_CH)), _resident((1, LANES)), _resident((1, LANES)),
                 _resident((n_seq, 8, GDN_CONV_CH))]
    args += [nw, w, cw, alog, dtb, halo0]
    tile = lambda c: pl.BlockSpec((ROW_TILE, c), lambda i: (i, 0))
    n_chunks = t // CHUNK
    bf = lambda c: jax.ShapeDtypeStruct((t, c), BF16)
    return pl.pallas_call(
        functools.partial(_gdn_in_body, n_x=n_x, n_pt=n_pt, tps=tps, n_p=n_p),
        grid=(t // ROW_TILE,),
        in_specs=in_specs,
        out_specs=[tile(GDN_OPS * D_MODEL),
                   pl.BlockSpec((GDN_GROUPS, 2 * CHUNK, D_MODEL), lambda i: (i, 0, 0)),
                   tile(LANES),
                   pl.BlockSpec((GDN_GROUPS, 8, GDN_CONV_CH), lambda i: (i, 0, 0))],
        out_shape=[bf(GDN_OPS * D_MODEL), jax.ShapeDtypeStruct((n_chunks, 2 * CHUNK, D_MODEL), BF16),
                   jax.ShapeDtypeStruct((t, LANES), F32),
                   jax.ShapeDtypeStruct((n_chunks, 8, GDN_CONV_CH), F32)],
        scratch_shapes=[pltpu.VMEM((GDN_GROUPS * GDN_WIN, GDN_CONV_CH), F32),
                        pltpu.VMEM((8, GDN_CONV_CH), F32)],
        compiler_params=_params(), name="gdn_in")(*args)


def _gdn_pre_body(k_ref, kbq_ref, vk_ref, qe_ref, bg_ref, alog_ref, dtb_ref,
                  u_ref, wq_ref, attn_ref, egl_ref):
    pw = 2 * CHUNK
    row = lax.broadcasted_iota(jnp.int32, (CHUNK, pw), 0)
    lane = lax.broadcasted_iota(jnp.int32, (CHUNK, pw), 1)
    col = lane & (CHUNK - 1)
    left = lane < CHUNK
    incl = row >= col
    strict = row > col
    eye = (row == col).astype(F32)

    def level_mask(s):
        sh = int(math.log2(s))
        bi, bj = row >> sh, col >> sh
        return jnp.logical_and((bi & 1) == 1, bj == bi - 1)

    masks = [level_mask(s) for s in (1, 2, 4, 8, 16, 32)]
    r2 = lax.broadcasted_iota(jnp.int32, (pw, pw), 0)
    l2 = lax.broadcasted_iota(jnp.int32, (pw, pw), 1)
    same_head = (r2 < CHUNK) == (l2 < CHUNK)

    def block_diag(x):
        return jnp.where(same_head, jnp.concatenate([x, x], axis=0), jnp.zeros((pw, pw), x.dtype))

    probs = [(j, m) for j in range(GDN_NB) for m in range(GDN_HEADS // 2)]
    n = len(probs)
    rs = [slice(j * CHUNK, (j + 1) * CHUNK) for j, _ in probs]
    hc = lambda h: slice(h * GDN_DK, (h + 1) * GDN_DK)
    gcum, gcum_t = [], []
    for j in range(GDN_NB):
        _, gc = _gdn_gates(bg_ref[j * CHUNK:(j + 1) * CHUNK, :], alog_ref, dtb_ref)
        gcum.append(gc)
        gt = jnp.transpose(jnp.concatenate([gc, jnp.zeros_like(gc)], axis=0))
        gcum_t.append(gt)
        egl_ref[j] = jnp.broadcast_to(jnp.exp(gt[8:8 + GDN_HEADS, CHUNK - 1:CHUNK]),
                                      (GDN_HEADS, LANES))
    decay = []
    for j, m in probs:
        h0, h1 = 8 + 2 * m, 9 + 2 * m
        g_col = jnp.where(left, gcum[j][:, h0:h0 + 1], gcum[j][:, h1:h1 + 1])
        g_row = jnp.concatenate([gcum_t[j][h0:h0 + 1, 0:CHUNK], gcum_t[j][h1:h1 + 1, 0:CHUNK]], axis=1)
        decay.append(jnp.exp(jnp.minimum(g_col - g_row, 0.0)))
    qk = []
    for i, (j, m) in enumerate(probs):
        k_pair = jnp.concatenate([k_ref[rs[i], hc(2 * m)], k_ref[rs[i], hc(2 * m + 1)]], axis=0)
        r0, r1 = (lax.dot_general(kbq_ref[j, :, hc(h)], k_pair, (((1,), (1,)), ((), ())),
                                  preferred_element_type=F32) for h in (2 * m, 2 * m + 1))
        qk.append(jnp.where(l2 < CHUNK, r0, r1))
    a = [jnp.where(strict, qk[i][0:CHUNK] * decay[i], 0.0) for i in range(n)]
    for i, (j, m) in enumerate(probs):
        attn = jnp.where(incl, qk[i][CHUNK:2 * CHUNK] * decay[i], 0.0)
        attn_ref[rs[i], m * pw:(m + 1) * pw] = attn.astype(attn_ref.dtype)
    t = [eye - jnp.where(masks[0], ai, 0.0) for ai in a]
    a_b = [ai.astype(BF16) for ai in a]
    for lm in masks[1:]:
        t_b = [ti.astype(BF16) for ti in t]
        at = [jnp.dot(jnp.where(lm, a_b[i], jnp.zeros_like(a_b[i])), block_diag(t_b[i]),
                      preferred_element_type=F32) for i in range(n)]
        t = [t[i] - jnp.dot(t_b[i], block_diag(at[i].astype(BF16)), preferred_element_type=F32)
             for i in range(n)]
    zeros = jnp.zeros((CHUNK, 2 * GDN_DV), BF16)
    for i, (j, m) in enumerate(probs):
        v0 = vk_ref[rs[i], 4 * m * GDN_DV:(4 * m + 2) * GDN_DV]
        v1 = vk_ref[rs[i], (4 * m + 2) * GDN_DV:(4 * m + 4) * GDN_DV]
        rhs = jnp.concatenate([jnp.concatenate([v0, zeros], axis=1),
                               jnp.concatenate([zeros, v1], axis=1)], axis=0)
        uw = jnp.dot(t[i].astype(BF16), rhs, preferred_element_type=F32)
        for q, h in enumerate((2 * m, 2 * m + 1)):
            u_ref[rs[i], hc(h)] = uw[:, 2 * q * GDN_DV:(2 * q + 1) * GDN_DV]
            wq_ref[j, 0:CHUNK, hc(h)] = uw[:, (2 * q + 1) * GDN_DV:(2 * q + 2) * GDN_DV].astype(wq_ref.dtype)
            wq_ref[j, CHUNK:2 * CHUNK, hc(h)] = qe_ref[rs[i], hc(h)]


def _gdn_pre(ops, kbq, bgp, alog, dtb):
    t = ops.shape[0]
    rows = GDN_NB * CHUNK
    n_chunks = t // CHUNK
    tile = lambda c: pl.BlockSpec((rows, c), lambda b: (b, 0))
    chunked = lambda r, c: pl.BlockSpec((GDN_NB, r, c), lambda b: (b, 0, 0))
    op_block = lambda part, width: pl.BlockSpec((rows, width * D_MODEL), lambda b: (b, part // width))
    return pl.pallas_call(
        _gdn_pre_body,
        grid=(t // rows,),
        in_specs=[op_block(GDN_OP_K, 1), chunked(2 * CHUNK, D_MODEL), op_block(GDN_OP_VK, 2),
                  op_block(GDN_OP_QE, 1), tile(LANES), _resident((1, LANES)), _resident((1, LANES))],
        out_specs=[tile(D_MODEL), chunked(2 * CHUNK, D_MODEL), tile(GDN_HEADS * CHUNK),
                   chunked(GDN_HEADS, LANES)],
        out_shape=[jax.ShapeDtypeStruct((t, D_MODEL), F32),
                   jax.ShapeDtypeStruct((n_chunks, 2 * CHUNK, D_MODEL), BF16),
                   jax.ShapeDtypeStruct((t, GDN_HEADS * CHUNK), BF16),
                   jax.ShapeDtypeStruct((n_chunks, GDN_HEADS, LANES), F32)],
        compiler_params=_params(), name="gdn_pre")(ops, kbq, ops, ops, bgp, alog, dtb)


def _gdn_scan_body(u_ref, wq_ref, kd_ref, attn_ref, egl_ref, gate_ref, s0_ref, nw_ref,
                   o_ref, soutp_ref, souts_ref, s_scr, *, n_pb, bps):
    blk, is_prompt, first, last = _block_flags(n_pb, bps)

    @pl.when(first)
    def _():
        s_scr[...] = jnp.zeros_like(s_scr)

    heads = range(GDN_HEADS)
    cs = [slice(h * GDN_DK, (h + 1) * GDN_DK) for h in heads]
    s = [s_scr[h] for h in heads]
    for j in range(SCAN_NB):
        rs = slice(j * CHUNK, (j + 1) * CHUNK)
        s = [jnp.where(is_prompt, s[h], s0_ref[j, h]) for h in heads]
        ws_qs = [jnp.dot(wq_ref[j, :, cs[h]], s[h].astype(BF16), preferred_element_type=F32)
                 for h in heads]
        v_new = [(u_ref[rs, cs[h]] - ws_qs[h][0:CHUNK]).astype(BF16) for h in heads]
        o = [ws_qs[h][CHUNK:2 * CHUNK]
             + jnp.dot(attn_ref[rs, h * CHUNK:(h + 1) * CHUNK], v_new[h], preferred_element_type=F32)
             for h in heads]
        s = [s[h] * egl_ref[j, h:h + 1, :] + _mm_tn(kd_ref[rs, cs[h]], v_new[h]) for h in heads]
        for h in heads:
            o_ref[rs, cs[h]] = (_rms(o[h], nw_ref[...]) * gate_ref[rs, cs[h]]).astype(o_ref.dtype)

        @pl.when(jnp.logical_not(is_prompt))
        def _(j=j, s=s):
            for h in heads:
                souts_ref[j, h] = s[h]

    for h in heads:
        s_scr[h] = s[h]

    @pl.when(last)
    def _():
        for h in heads:
            soutp_ref[0, h] = s[h]


def _gdn_scan(u, wq, ops, attn, egl, s0, nw, *, n_p, cpl, n_s):
    t = u.shape[0]
    rows = SCAN_NB * CHUNK
    bps = cpl // SCAN_NB
    n_pb = n_p * bps
    row_blk = lambda b: (b, 0)
    s_in, s_out, s_shapes = _state_specs(n_p, n_s, n_pb, bps, (GDN_HEADS, GDN_DK, GDN_DV))
    return pl.pallas_call(
        functools.partial(_gdn_scan_body, n_pb=n_pb, bps=bps),
        grid=(t // rows,),
        in_specs=[pl.BlockSpec((rows, D_MODEL), row_blk),
                  pl.BlockSpec((SCAN_NB, 2 * CHUNK, D_MODEL), lambda b: (b, 0, 0)),
                  pl.BlockSpec((rows, D_MODEL), lambda b: (b, GDN_OP_KD)),
                  pl.BlockSpec((rows, GDN_HEADS * CHUNK), row_blk),
                  pl.BlockSpec((SCAN_NB, GDN_HEADS, LANES), lambda b: (b, 0, 0)),
                  pl.BlockSpec((rows, D_MODEL), lambda b: (b, GDN_OP_GATE)),
                  s_in, _resident((1, GDN_DV))],
        out_specs=[pl.BlockSpec((rows, D_MODEL), row_blk)] + s_out,
        out_shape=[jax.ShapeDtypeStruct((t, D_MODEL), BF16)] + s_shapes,
        scratch_shapes=[pltpu.VMEM((GDN_HEADS, GDN_DK, GDN_DV), F32)],
        compiler_params=_params(), name="gdn_scan")(u, wq, ops, attn, egl, ops, s0, nw)


def _gla_exponent_matrix():
    i = np.arange(CHUNK)[:, None]
    t = np.arange(CHUNK)[None, :]
    blocks = [(t <= i)]
    for s in GLA_LEVELS:
        c = (i // (2 * s)) * (2 * s) + s - 1
        blocks.append(np.where(i > c, (t > c) & (t <= i), (t > i) & (t <= c)))
    w = np.concatenate(blocks, axis=0).astype(np.float32)
    return np.concatenate([w, w, w], axis=1)


def _split3(x):
    hi = x.astype(BF16)
    r1 = x - hi.astype(F32)
    mid = r1.astype(BF16)
    lo = (r1 - mid.astype(F32)).astype(BF16)
    return hi, mid, lo


def _gla_body(p_ref, s0_ref, wg2_ref, bg_ref, wexp_ref, nw_ref,
              o_ref, soutp_ref, souts_ref, st_scr, s0t_scr, *, n_pb, bps):
    blk, is_prompt, first, last = _block_flags(n_pb, bps)
    heads = range(GLA_HEADS)
    chunks = range(SCAN_NB)

    @pl.when(first)
    def _():
        st_scr[...] = jnp.zeros_like(st_scr)

    @pl.when(blk == 0)
    def _():
        s0t_scr[...] = jnp.zeros_like(s0t_scr)

    @pl.when(jnp.logical_not(is_prompt))
    def _():
        for j in chunks:
            for h in heads:
                s0t_scr[j, h] = s0_ref[j, h].T

    row = lax.broadcasted_iota(jnp.int32, (CHUNK, CHUNK), 0)
    col = lax.broadcasted_iota(jnp.int32, (CHUNK, CHUNK), 1)
    eye = row == col
    masks = [_pair_mask(s) for s in GLA_LEVELS]
    rs = [slice(j * CHUNK, (j + 1) * CHUNK) for j in chunks]
    ck = [slice(h * GLA_DK, (h + 1) * GLA_DK) for h in heads]
    cv = [slice(h * GLA_DV, (h + 1) * GLA_DV) for h in heads]

    x = [_mm(p_ref[rs[j], GLA_G0:GLA_G0 + LANES], wg2_ref[...]) + bg_ref[...] for j in chunks]
    glog = [-_softplus(-x[j]) * (1.0 / GLA_TAU) for j in chunks]
    e_all = [jnp.dot(wexp_ref[...], jnp.concatenate(_split3(glog[j]), axis=0),
                     preferred_element_type=F32) for j in chunks]
    q_all = [p_ref[rs[j], 0:GLA_QK] * (GLA_DK ** -0.5) for j in chunks]
    k_all = [p_ref[rs[j], GLA_QK:2 * GLA_QK] for j in chunks]
    qb = [q.astype(BF16) for q in q_all]
    kb = [k.astype(BF16) for k in k_all]
    attn = [[jnp.where(eye, _mm_nt(qb[j][:, ck[h]], kb[j][:, ck[h]]), 0.0) for h in heads]
            for j in chunks]
    for lvl, m in enumerate(masks):
        f = [jnp.exp(e_all[j][(lvl + 1) * CHUNK:(lvl + 2) * CHUNK, :]) for j in chunks]
        qf = [(q_all[j] * f[j]).astype(BF16) for j in chunks]
        kf = [(k_all[j] * f[j]).astype(BF16) for j in chunks]
        attn = [[attn[j][h] + jnp.where(m, _mm_nt(qf[j][:, ck[h]], kf[j][:, ck[h]]), 0.0)
                 for h in heads] for j in chunks]
    b = [e_all[j][0:CHUNK, :] for j in chunks]
    b_last = [b[j][CHUNK - 1:CHUNK, :] for j in chunks]
    qe = [(q_all[j] * jnp.exp(b[j])).astype(BF16) for j in chunks]
    kd = [(k_all[j] * jnp.exp(b_last[j] - b[j])).astype(BF16) for j in chunks]
    eb_last = [jnp.exp(b_last[j]) for j in chunks]
    vb = [p_ref[rs[j], GLA_V0:GLA_V0 + D_MODEL].astype(BF16) for j in chunks]
    av = [[_mm(attn[j][h], vb[j][:, cv[h]]) for h in heads] for j in chunks]
    kv = [[_mm_tn(vb[j][:, cv[h]], kd[j][:, ck[h]]) for h in heads] for j in chunks]

    st = [st_scr[h] for h in heads]
    for j in chunks:
        st = [jnp.where(is_prompt, st[h], s0t_scr[j, h]) for h in heads]
        o = [_mm_nt(qe[j][:, ck[h]], st[h]) + av[j][h] for h in heads]
        st = [st[h] * eb_last[j][:, ck[h]] + kv[j][h] for h in heads]
        for h in heads:
            r = p_ref[rs[j], GLA_R0 + h * GLA_DV:GLA_R0 + (h + 1) * GLA_DV]
            o_ref[rs[j], cv[h]] = (_rms(o[h], nw_ref[h:h + 1, :]) * _silu(r)).astype(o_ref.dtype)

        @pl.when(jnp.logical_not(is_prompt))
        def _(j=j, st=st):
            for h in heads:
                souts_ref[j, h] = st[h].T

    for h in heads:
        st_scr[h] = st[h]

    @pl.when(last)
    def _():
        for h in heads:
            soutp_ref[0, h] = st[h].T


def _gla_scan(p, s0, wg2, bg, wexp, nw, *, n_p, cpl, n_s):
    t = p.shape[0]
    rows = SCAN_NB * CHUNK
    bps = cpl // SCAN_NB
    n_pb = n_p * bps
    s_in, s_out, s_shapes = _state_specs(n_p, n_s, n_pb, bps, (GLA_HEADS, GLA_DK, GLA_DV))
    return pl.pallas_call(
        functools.partial(_gla_body, n_pb=n_pb, bps=bps),
        grid=(t // rows,),
        in_specs=[pl.BlockSpec((rows, GLA_P), lambda b: (b, 0)), s_in,
                  _resident((LANES, GLA_QK)), _resident((1, GLA_QK)),
                  _resident(((len(GLA_LEVELS) + 1) * CHUNK, 3 * CHUNK)),
                  _resident((GLA_HEADS, GLA_DV))],
        out_specs=[pl.BlockSpec((rows, D_MODEL), lambda b: (b, 0))] + s_out,
        out_shape=[jax.ShapeDtypeStruct((t, D_MODEL), BF16)] + s_shapes,
        scratch_shapes=[pltpu.VMEM((GLA_HEADS, GLA_DV, GLA_DK), F32),
                        pltpu.VMEM((SCAN_NB, GLA_HEADS, GLA_DV, GLA_DK), F32)],
        compiler_params=_params(), name="gla_scan")(p, s0, wg2, bg, wexp, nw)


CNV_WIN = CONV_HALO + CHUNK
TILE_CHUNKS = ROW_TILE // CHUNK
CNV_CK = 256


def _cnv_in_body(*refs, n_x, n_pt, tps, n_p):
    x_refs = refs[:n_x]
    (nw_ref, w_ref, b_ref, wdw_ref, bdw_ref, lnw_ref, lnb_ref, halo0_ref,
     a_ref, tails_ref, xc_scr, y_scr, carry_scr) = refs[n_x:]
    i = pl.program_id(0)
    is_sample = i >= n_pt

    @pl.when(i == 0)
    def _():
        carry_scr[...] = jnp.zeros_like(carry_scr)

    xn = _rms(_read_row_tile(x_refs, n_pt), nw_ref[...]).astype(BF16)
    taps = {}
    for w in range(CONV_W):
        s = CONV_HALO - (CONV_W - 1) + w
        taps.setdefault(-s % 8, []).append((w, s + (-s % 8)))

    def glu_slab(c0):
        cols = slice(c0, c0 + CNV_CK)
        gcols = slice(D_MODEL + c0, D_MODEL + c0 + CNV_CK)
        val = jnp.dot(xn, w_ref[:, cols], preferred_element_type=F32) + b_ref[:, cols]
        gate = jnp.dot(xn, w_ref[:, gcols], preferred_element_type=F32) + b_ref[:, gcols]
        glu = val * _sigmoid(gate)
        for j in range(TILE_CHUNKS):
            r0 = j * CNV_WIN
            xc_scr[r0 + CONV_HALO:r0 + CNV_WIN, cols] = glu[j * CHUNK:(j + 1) * CHUNK]
            if j == 0:
                first = jnp.logical_or(is_sample, i % tps == 0)
                prev = carry_scr[:, cols]
            else:
                first = is_sample
                prev = glu[j * CHUNK - CONV_HALO:j * CHUNK]
            seq = jnp.where(is_sample, n_p + (i - n_pt) * TILE_CHUNKS + j, i // tps)
            xc_scr[r0:r0 + CONV_HALO, cols] = jnp.where(first, halo0_ref[seq, :, cols], prev)

    def conv_slab(c0):
        for j in range(TILE_CHUNKS):
            for b0 in range(c0, c0 + CNV_CK, LANES):
                cs = slice(b0, b0 + LANES)
                wv = xc_scr[j * CNV_WIN:(j + 1) * CNV_WIN, cs]
                y = None
                for r, group in sorted(taps.items()):
                    rolled = wv if r == 0 else pltpu.roll(wv, r, 0)
                    for w, a0 in group:
                        term = wdw_ref[w:w + 1, cs] * rolled[a0:a0 + CHUNK]
                        y = term if y is None else y + term
                y_scr[j * CHUNK:(j + 1) * CHUNK, cs] = y + bdw_ref[:, cs]

    slabs = list(range(0, D_MODEL, CNV_CK))
    for n, c0 in enumerate(slabs + [None]):
        if c0 is not None:
            glu_slab(c0)
        if n > 0:
            conv_slab(slabs[n - 1])
    for j in range(TILE_CHUNKS):
        rs = slice(j * CHUNK, (j + 1) * CHUNK)
        a_ref[rs, :] = _silu(_layernorm(y_scr[rs, :], lnw_ref[...], lnb_ref[...])).astype(BF16)
        tails_ref[j] = xc_scr[(j + 1) * CNV_WIN - CONV_HALO:(j + 1) * CNV_WIN, :]
    carry_scr[...] = xc_scr[TILE_CHUNKS * CNV_WIN - CONV_HALO:TILE_CHUNKS * CNV_WIN, :]


def _cnv_in(x, nw, w, b, wdw, bdw, lnw, lnb, halo0, *, t, n_pt, tps, n_p):
    in_specs, args = _row_tile_specs(x, n_pt)
    n_x = len(args)
    in_specs += [_resident((1, D_MODEL)), _resident((D_MODEL, 2 * D_MODEL)), _resident((1, 2 * D_MODEL)),
                 _resident((CONV_HALO, D_MODEL)), _resident((1, D_MODEL)),
                 _resident((1, D_MODEL)), _resident((1, D_MODEL)), _resident(halo0.shape)]
    args += [nw, w, b, wdw, bdw, lnw, lnb, halo0]
    return pl.pallas_call(
        functools.partial(_cnv_in_body, n_x=n_x, n_pt=n_pt, tps=tps, n_p=n_p),
        grid=(t // ROW_TILE,),
        in_specs=in_specs,
        out_specs=[pl.BlockSpec((ROW_TILE, D_MODEL), lambda i: (i, 0)),
                   pl.BlockSpec((TILE_CHUNKS, CONV_HALO, D_MODEL), lambda i: (i, 0, 0))],
        out_shape=[jax.ShapeDtypeStruct((t, D_MODEL), BF16),
                   jax.ShapeDtypeStruct((t // CHUNK, CONV_HALO, D_MODEL), F32)],
        scratch_shapes=[pltpu.VMEM((TILE_CHUNKS * CNV_WIN, D_MODEL), F32),
                        pltpu.VMEM((ROW_TILE, D_MODEL), F32),
                        pltpu.VMEM((CONV_HALO, D_MODEL), F32)],
        compiler_params=_params(), name="cnv_in")(*args)


def _pad_cols(w, n):
    return jnp.pad(w, ((0, 0), (0, n - w.shape[1])))


def _row(v):
    return v.reshape(1, -1).astype(F32)


def _tail_rows(a, lay, n):
    n_p, cpl, n_s = lay
    ends = np.concatenate([(np.arange(n_p) + 1) * cpl * CHUNK,
                           n_p * cpl * CHUNK + (np.arange(n_s) + 1) * CHUNK])
    idx = (ends[:, None] - n + np.arange(n)[None, :]).reshape(-1)
    return jnp.take(a, jnp.asarray(idx, jnp.int32), axis=0).reshape(n_p + n_s, n, a.shape[1])


def _gdn_mix(x, lay, dims, mix_nw, cache, state, w_in, conv_w, a_log, dt_bias, norm_w):
    n_p, cpl, n_s = lay
    halo0 = jnp.concatenate([jnp.zeros((n_p, 8, GDN_CONV_CH), F32),
                             jnp.pad(cache, ((0, 0), (8 - (GDN_CONV - 1), 0), (0, 0)))], axis=0)
    lane_pad = lambda v: jnp.pad(v.reshape(1, -1), ((0, 0), (GDN_HEADS, LANES - 2 * GDN_HEADS)))
    alog, dtb = lane_pad(a_log), lane_pad(dt_bias)
    ops, kbq, bgp, tails = _gdn_in(
        x, _row(mix_nw), _pad_cols(w_in, GDN_P).astype(BF16), conv_w, alog, dtb, halo0,
        tps=cpl * CHUNK // ROW_TILE, n_p=n_p, **dims)
    u, wq, attn, egl = _gdn_pre(ops, kbq, bgp, alog, dtb)
    o, s_p, s_s = _gdn_scan(u, wq, ops, attn, egl, state, _row(norm_w), n_p=n_p, cpl=cpl, n_s=n_s)
    ends = np.concatenate([(np.arange(n_p) + 1) * cpl - 1, n_p * cpl + np.arange(n_s)])
    conv_rows = tails[jnp.asarray(ends, jnp.int32), 8 - (GDN_CONV - 1):, :]
    return o, (conv_rows[:n_p], conv_rows[n_p:], s_p, s_s)


def _gla_mix(x, lay, dims, mix_nw, state, w_in, w_g2, b_g, norm_w):
    n_p, cpl, n_s = lay
    p = _proj(x, _row(mix_nw), _pad_cols(w_in, GLA_P).astype(BF16), None,
              n_out=GLA_P, ck=512, name="gla_in", **dims)
    wg2 = jnp.pad(w_g2, ((0, LANES - GLA_RANK), (0, 0))).astype(BF16)
    wexp = jnp.asarray(_gla_exponent_matrix(), BF16)
    o, s_p, s_s = _gla_scan(p, state, wg2, _row(b_g), wexp, norm_w, n_p=n_p, cpl=cpl, n_s=n_s)
    return o, (s_p, s_s)


def _cnv_mix(x, lay, dims, mix_nw, cache, w_pw1, b_pw1, w_dw, b_dw, ln_w, ln_b):
    n_p, cpl, n_s = lay
    pad_rows = CONV_HALO - (CONV_W - 1)
    halo0 = jnp.concatenate([jnp.zeros((n_p, CONV_HALO, D_MODEL), F32),
                             jnp.pad(cache, ((0, 0), (pad_rows, 0), (0, 0)))], axis=0)
    wdw = jnp.pad(w_dw, ((0, CONV_HALO - CONV_W), (0, 0)))
    a, tails = _cnv_in(x, _row(mix_nw), w_pw1.astype(BF16), _row(b_pw1), wdw, _row(b_dw),
                       _row(ln_w), _row(ln_b), halo0, tps=cpl * CHUNK // ROW_TILE, n_p=n_p, **dims)
    ends = np.concatenate([(np.arange(n_p) + 1) * cpl - 1, n_p * cpl + np.arange(n_s)])
    rows = tails[jnp.asarray(ends, jnp.int32), pad_rows:, :]
    return a, (rows[:n_p], rows[n_p:])


def kernel(x_prompt, x_sample, cache_gdn_conv, state_gdn, state_gla, cache_conformer, mix_norm_w, ffn_norm_w, ffn_w1, ffn_w2, final_norm_w, gdn_w_in, gdn_conv_w, gdn_a_log, gdn_dt_bias, gdn_norm_w, gdn_w_out, gla_w_in, gla_w_g2, gla_b_g, gla_norm_w, gla_w_out, gmlp_w_in, gmlp_b_in, gmlp_ln_w, gmlp_ln_b, gmlp_w_s, gmlp_b_s, gmlp_w_out, gmlp_b_out, cnv_w_pw1, cnv_b_pw1, cnv_w_dw, cnv_b_dw, cnv_ln_w, cnv_ln_b, cnv_w_pw2, cnv_b_pw2):
    n_p, l_p, d = x_prompt.shape
    n_s, l_s, _ = x_sample.shape
    assert d == D_MODEL and l_s == CHUNK and l_p % ROW_TILE == 0
    assert n_s % max(SCAN_NB, GDN_NB) == 0
    t_p, t_s = n_p * l_p, n_s * l_s
    assert t_p % ROW_TILE == 0 and t_s % ROW_TILE == 0
    lay = (n_p, l_p // CHUNK, n_s)
    dims = dict(t=t_p + t_s, n_pt=t_p // ROW_TILE)
    depth = mix_norm_w.shape[0]

    x = (x_prompt.reshape(t_p, d), x_sample.reshape(t_s, d))
    outs = {k: [] for k in ("gdn_cp", "gdn_cs", "gdn_sp", "gdn_ss", "gla_sp", "gla_ss",
                            "gmlp_vs", "cnv_p", "cnv_s")}
    w1_all, w2_all = ffn_w1.astype(BF16), ffn_w2.astype(BF16)
    for i in range(depth):
        kind, j = i % 4, i // 4
        ffn = (_row(ffn_norm_w[i]), w1_all, w2_all, _row(final_norm_w) if i == depth - 1 else None)
        if kind == 0:
            o, (cp, cs, sp, ss) = _gdn_mix(x, lay, dims, mix_norm_w[i], cache_gdn_conv[j], state_gdn[j],
                                           gdn_w_in[j], gdn_conv_w[j], gdn_a_log[j], gdn_dt_bias[j],
                                           gdn_norm_w[j])
            outs["gdn_cp"].append(cp); outs["gdn_cs"].append(cs)
            outs["gdn_sp"].append(sp); outs["gdn_ss"].append(ss)
            res = _post([o], x, gdn_w_out[j].astype(BF16), None, *ffn,
                        prologue="plain", layer=i, name=f"post_{i}", **dims)
        elif kind == 1:
            o, (sp, ss) = _gla_mix(x, lay, dims, mix_norm_w[i], state_gla[j], gla_w_in[j], gla_w_g2[j],
                                   gla_b_g[j], gla_norm_w[j])
            outs["gla_sp"].append(sp); outs["gla_ss"].append(ss)
            res = _post([o], x, gla_w_out[j].astype(BF16), None, *ffn,
                        prologue="plain", layer=i, name=f"post_{i}", **dims)
        elif kind == 2:
            zz = _proj(x, _row(mix_norm_w[i]), gmlp_w_in[j].astype(BF16), _row(gmlp_b_in[j]),
                       n_out=2 * D_MODEL, ck=512, epilogue="gelu", name="gmlp_in", **dims)
            bs_full = jnp.repeat(gmlp_b_s[j].T, GMLP_GC, axis=1)
            res = _post([zz, _row(gmlp_ln_w[j]), _row(gmlp_ln_b[j]), gmlp_w_s[j], bs_full], x,
                        gmlp_w_out[j].astype(BF16), _row(gmlp_b_out[j]), *ffn,
                        prologue="gmlp", layer=i, name=f"post_{i}", **dims)
            outs["gmlp_vs"].append(res[-1].reshape(n_s, l_s, d))
            res = res[:-1]
        else:
            a, (cp, cs) = _cnv_mix(x, lay, dims, mix_norm_w[i], cache_conformer[j], cnv_w_pw1[j],
                                   cnv_b_pw1[j], cnv_w_dw[j], cnv_b_dw[j], cnv_ln_w[j], cnv_ln_b[j])
            outs["cnv_p"].append(cp); outs["cnv_s"].append(cs)
            res = _post([a], x, cnv_w_pw2[j].astype(BF16), _row(cnv_b_pw2[j]), *ffn,
                        prologue="plain", layer=i, name=f"post_{i}", **dims)
        x = tuple(res) if i == depth - 1 else res[0]
    y_p, y_s = x
    stack = lambda xs: xs[0][None] if len(xs) == 1 else jnp.stack(xs)
    return (y_p.reshape(n_p, l_p, d), y_s.reshape(n_s, l_s, d),
            stack(outs["gdn_cp"]), stack(outs["gdn_cs"]),
            stack(outs["gdn_sp"]), stack(outs["gdn_ss"]),
            stack(outs["gla_sp"]), stack(outs["gla_ss"]),
            stack(outs["gmlp_vs"]),
            stack(outs["cnv_p"]), stack(outs["cnv_s"]))
```

```python
import functools
import math

import numpy as np
import jax
import jax.numpy as jnp
from jax import lax
from jax.experimental import pallas as pl
from jax.experimental.pallas import tpu as pltpu

F32 = jnp.float32
BF16 = jnp.bfloat16

D_MODEL = 1024
D_FF = 4 * D_MODEL
EPS = 1e-6
CHUNK = 64
ROW_TILE = 512
SCAN_NB = 4
LANES = 128
VMEM_LIMIT = 56 * 1024 * 1024

GDN_HEADS, GDN_DK, GDN_DV, GDN_CONV = 8, 128, 128, 4
GDN_QK = GDN_HEADS * GDN_DK
GDN_CONV_CH = 2 * GDN_QK + GDN_HEADS * GDN_DV
GDN_Z0 = GDN_CONV_CH
GDN_BG0 = GDN_CONV_CH + GDN_HEADS * GDN_DV
GDN_P = GDN_BG0 + LANES

GLA_HEADS, GLA_DK, GLA_DV, GLA_RANK, GLA_TAU = 4, 128, 256, 16, 16.0
GLA_QK = GLA_HEADS * GLA_DK
GLA_V0, GLA_R0, GLA_G0 = 2 * GLA_QK, 2 * GLA_QK + D_MODEL, 2 * GLA_QK + 2 * D_MODEL
GLA_P = GLA_G0 + LANES
GLA_LEVELS = (32, 16, 8, 4, 2, 1)

GMLP_BLOCK, GMLP_GROUPS, GMLP_GC = 128, 4, 256
CONV_W = 31
CONV_HALO = 32


def _mm(a, b):
    return jnp.dot(a.astype(BF16), b.astype(BF16), preferred_element_type=F32)


def _mm_nt(a, b):
    return lax.dot_general(a.astype(BF16), b.astype(BF16), (((1,), (1,)), ((), ())),
                           preferred_element_type=F32)


def _mm_tn(a, b):
    return lax.dot_general(a.astype(BF16), b.astype(BF16), (((0,), (0,)), ((), ())),
                           preferred_element_type=F32)


def _rms(x, w):
    return x * lax.rsqrt(jnp.mean(x * x, axis=-1, keepdims=True) + EPS) * w


def _layernorm(x, w, b):
    xc = x - jnp.mean(x, axis=-1, keepdims=True)
    return xc * lax.rsqrt(jnp.mean(xc * xc, axis=-1, keepdims=True) + EPS) * w + b


def _sigmoid(x):
    return jax.nn.sigmoid(x)


def _silu(x):
    h = 0.5 * x
    return h + h * jnp.tanh(h)


def _softplus(x):
    return jnp.maximum(x, 0.0) + jnp.log1p(jnp.exp(-jnp.abs(x)))


def _gelu_tanh(x):
    return 0.5 * x * (1.0 + jnp.tanh(math.sqrt(2.0 / math.pi) * (x + 0.044715 * (x * x * x))))


def _cumsum_rows(x):
    row = lax.broadcasted_iota(jnp.int32, x.shape, 0)
    s = 1
    while s < x.shape[0]:
        x = x + jnp.where(row >= s, pltpu.roll(x, s, 0), 0.0)
        s *= 2
    return x


def _pair_mask(s):
    sh = int(math.log2(s))
    bi = lax.broadcasted_iota(jnp.int32, (CHUNK, CHUNK), 0) >> sh
    bj = lax.broadcasted_iota(jnp.int32, (CHUNK, CHUNK), 1) >> sh
    return jnp.logical_and((bi & 1) == 1, bj == bi - 1)


def _block_flags(n_pb, bps):
    blk = pl.program_id(0)
    is_prompt = blk < n_pb
    first = jnp.logical_and(is_prompt, blk % bps == 0)
    last = jnp.logical_and(is_prompt, blk % bps == bps - 1)
    return blk, is_prompt, first, last


def _state_specs(n_p, n_s, n_pb, bps, shape):
    zeros = (0,) * len(shape)
    sample = pl.BlockSpec((SCAN_NB,) + shape, lambda b: (jnp.maximum(b - n_pb, 0),) + zeros)
    prompt = pl.BlockSpec((1,) + shape, lambda b: (jnp.minimum(b // bps, n_p - 1),) + zeros)
    shapes = [jax.ShapeDtypeStruct((n_p,) + shape, F32), jax.ShapeDtypeStruct((n_s,) + shape, F32)]
    return sample, [prompt, sample], shapes


def _params(n_axes=1):
    return pltpu.CompilerParams(dimension_semantics=("arbitrary",) * n_axes,
                                vmem_limit_bytes=VMEM_LIMIT)


def _resident(shape):
    zeros = (0,) * len(shape)
    return pl.BlockSpec(shape, lambda i: zeros, pipeline_mode=pl.Buffered(1))


def _row_tile_specs(x, n_pt):
    if isinstance(x, tuple):
        return ([pl.BlockSpec((ROW_TILE, D_MODEL), lambda i: (jnp.minimum(i, n_pt - 1), 0)),
                 pl.BlockSpec((ROW_TILE, D_MODEL), lambda i: (jnp.maximum(i - n_pt, 0), 0))], list(x))
    return [pl.BlockSpec((ROW_TILE, D_MODEL), lambda i: (i, 0))], [x]


def _read_row_tile(refs, n_pt):
    if len(refs) == 2:
        return jnp.where(pl.program_id(0) < n_pt, refs[0][...], refs[1][...])
    return refs[0][...]


def _proj_body(*refs, n_x, n_pt, n_out, ck, epilogue, has_bias):
    x_refs, (nw_ref, w_ref), rest = refs[:n_x], refs[n_x:n_x + 2], refs[n_x + 2:]
    if has_bias:
        b_ref, o_ref = rest
    else:
        (o_ref,) = rest
    xn = _rms(_read_row_tile(x_refs, n_pt), nw_ref[...]).astype(BF16)
    for c0 in range(0, n_out, ck):
        sl = slice(c0, min(c0 + ck, n_out))
        y = jnp.dot(xn, w_ref[:, sl], preferred_element_type=F32)
        if has_bias:
            y = y + b_ref[:, sl]
        if epilogue == "gelu":
            y = _gelu_tanh(y)
        o_ref[:, sl] = y.astype(o_ref.dtype)


def _proj(x, nw, w, b, *, t, n_pt, n_out, ck, epilogue="none", out_dtype=F32, name):
    n_w = w.shape[1]
    has_bias = b is not None
    in_specs, args = _row_tile_specs(x, n_pt)
    n_x = len(args)
    in_specs += [_resident((1, D_MODEL)), _resident((D_MODEL, n_w))]
    args += [nw, w]
    if has_bias:
        in_specs.append(_resident((1, n_w)))
        args.append(b)
    return pl.pallas_call(
        functools.partial(_proj_body, n_x=n_x, n_pt=n_pt, n_out=n_out, ck=ck, epilogue=epilogue,
                          has_bias=has_bias),
        grid=(t // ROW_TILE,),
        in_specs=in_specs,
        out_specs=pl.BlockSpec((ROW_TILE, n_out), lambda i: (i, 0)),
        out_shape=jax.ShapeDtypeStruct((t, n_out), out_dtype),
        compiler_params=_params(), name=name)(*args)


FFN_CK = 512


def _gmlp_gate(zz_ref, lnw_ref, lnb_ref, ws_ref, bs_ref, vs_ref, a_scr, is_sample):
    v = _layernorm(zz_ref[:, D_MODEL:2 * D_MODEL], lnw_ref[...], lnb_ref[...])

    @pl.when(is_sample)
    def _():
        vs_ref[...] = v

    vb = v.astype(BF16)
    row = lax.broadcasted_iota(jnp.int32, (GMLP_BLOCK, GMLP_BLOCK), 0)
    col = lax.broadcasted_iota(jnp.int32, (GMLP_BLOCK, GMLP_BLOCK), 1)
    tril = col <= row
    top_left = jnp.logical_and(row < CHUNK, col < CHUNK)
    brow = lax.broadcasted_iota(jnp.int32, (GMLP_BLOCK, D_MODEL), 0)
    bias = bs_ref[...]
    bias = jnp.where(jnp.logical_and(is_sample, brow >= CHUNK), pltpu.roll(bias, CHUNK, 0), bias)
    for grp in range(GMLP_GROUPS):
        cs = slice(grp * GMLP_GC, (grp + 1) * GMLP_GC)
        wt = jnp.where(tril, ws_ref[grp], 0.0)
        w11 = jnp.where(top_left, wt, 0.0)
        wd = w11 + pltpu.roll(pltpu.roll(w11, CHUNK, 0), CHUNK, 1)
        we = jnp.where(is_sample, wd, wt).astype(BF16)
        for blk in range(ROW_TILE // GMLP_BLOCK):
            rs = slice(blk * GMLP_BLOCK, (blk + 1) * GMLP_BLOCK)
            mixed = jnp.dot(we, vb[rs, cs], preferred_element_type=F32)
            a_scr[rs, cs] = (zz_ref[rs, cs] * (mixed + bias[:, cs])).astype(BF16)
    return a_scr[...]


def _post_body(*refs, prologue, n_res, has_bias, final, n_pt):
    n_in = {"plain": 1, "gmlp": 5}[prologue]
    pro, refs = refs[:n_in], refs[n_in:]
    res_refs, refs = refs[:n_res], refs[n_res:]
    wo_ref, refs = refs[0], refs[1:]
    if has_bias:
        bo_ref, refs = refs[0], refs[1:]
    fnw_ref, w1_ref, w2_ref, refs = refs[0], refs[1], refs[2], refs[3:]
    if final:
        finw_ref, refs = refs[0], refs[1:]
    n_main = 2 if final else 1
    outs, refs = refs[:n_main], refs[n_main:]
    is_sample = pl.program_id(0) >= n_pt

    if prologue == "plain":
        a = pro[0][...]
    else:
        vs_ref, a_scr = refs
        a = _gmlp_gate(*pro, vs_ref, a_scr, is_sample)

    x = _read_row_tile(res_refs, n_pt) + jnp.dot(a, wo_ref[...], preferred_element_type=F32)
    if has_bias:
        x = x + bo_ref[...]
    xn = _rms(x, fnw_ref[...]).astype(BF16)
    for c in range(D_FF // FFN_CK):
        sl = slice(c * FFN_CK, (c + 1) * FFN_CK)
        h = jnp.dot(xn, w1_ref[:, sl], preferred_element_type=F32)
        h = jnp.square(jnp.maximum(h, 0.0)).astype(BF16)
        x = x + jnp.dot(h, w2_ref[sl, :], preferred_element_type=F32)
    if final:
        y = _rms(x, finw_ref[...])

        @pl.when(jnp.logical_not(is_sample))
        def _():
            outs[0][...] = y

        @pl.when(is_sample)
        def _():
            outs[1][...] = y
    else:
        outs[0][...] = x


def _post(pro_args, res, wo, bo, fnw, w1, w2, finw, *, layer, prologue, t, n_pt, name):
    has_bias = bo is not None
    final = finw is not None
    tile = lambda c: pl.BlockSpec((ROW_TILE, c), lambda i: (i, 0))
    if prologue == "plain":
        in_specs = [tile(D_MODEL)]
    else:
        in_specs = [tile(2 * D_MODEL), _resident((1, D_MODEL)), _resident((1, D_MODEL)),
                    _resident((GMLP_GROUPS, GMLP_BLOCK, GMLP_BLOCK)),
                    _resident((GMLP_BLOCK, D_MODEL))]
    args = list(pro_args)
    res_specs, res_args = _row_tile_specs(res, n_pt)
    in_specs += res_specs + [_resident((D_MODEL, D_MODEL))]
    args += res_args + [wo]
    if has_bias:
        in_specs.append(_resident((1, D_MODEL)))
        args.append(bo)
    layer_block = lambda shape: pl.BlockSpec((None,) + shape, lambda i: (layer, 0, 0),
                                             pipeline_mode=pl.Buffered(1))
    in_specs += [_resident((1, D_MODEL)), layer_block((D_MODEL, D_FF)), layer_block((D_FF, D_MODEL))]
    args += [fnw, w1, w2]
    if final:
        in_specs.append(_resident((1, D_MODEL)))
        args.append(finw)
        out_specs = [pl.BlockSpec((ROW_TILE, D_MODEL), lambda i: (jnp.minimum(i, n_pt - 1), 0)),
                     pl.BlockSpec((ROW_TILE, D_MODEL), lambda i: (jnp.maximum(i - n_pt, 0), 0))]
        out_shape = [jax.ShapeDtypeStruct((n_pt * ROW_TILE, D_MODEL), F32),
                     jax.ShapeDtypeStruct((t - n_pt * ROW_TILE, D_MODEL), F32)]
    else:
        out_specs = [tile(D_MODEL)]
        out_shape = [jax.ShapeDtypeStruct((t, D_MODEL), F32)]
    scratch = []
    if prologue == "gmlp":
        out_specs.append(pl.BlockSpec((ROW_TILE, D_MODEL), lambda i: (jnp.maximum(i - n_pt, 0), 0)))
        out_shape.append(jax.ShapeDtypeStruct((t - n_pt * ROW_TILE, D_MODEL), F32))
        scratch.append(pltpu.VMEM((ROW_TILE, D_MODEL), BF16))
    return pl.pallas_call(
        functools.partial(_post_body, prologue=prologue, n_res=len(res_args), has_bias=has_bias,
                          final=final, n_pt=n_pt),
        grid=(t // ROW_TILE,),
        in_specs=in_specs, out_specs=out_specs, out_shape=out_shape, scratch_shapes=scratch,
        compiler_params=_params(), name=name)(*args)


GDN_NB = 4
GDN_WIN = CHUNK + 8
GDN_GROUPS = ROW_TILE // CHUNK
GDN_OP_K, GDN_OP_QE, GDN_OP_KD, GDN_OP_GATE, GDN_OP_VK, GDN_OPS = 0, 1, 2, 3, 4, 6


def _gdn_gates(bg, alog_ref, dtb_ref):
    glog = -jnp.exp(alog_ref[...]) * _softplus(bg + dtb_ref[...])
    return _sigmoid(bg), _cumsum_rows(glog)


def _gdn_in_body(*refs, n_x, n_pt, tps, n_p):
    x_refs = refs[:n_x]
    (nw_ref, w_ref, cw_ref, alog_ref, dtb_ref, halo0_ref,
     ops_ref, kbq_ref, bg_ref, tails_ref, xc_scr, carry_scr) = refs[n_x:]
    i = pl.program_id(0)
    is_sample = i >= n_pt

    @pl.when(i == 0)
    def _():
        carry_scr[...] = jnp.zeros_like(carry_scr)

    xn = _rms(_read_row_tile(x_refs, n_pt), nw_ref[...]).astype(BF16)
    bg = jnp.dot(xn, w_ref[:, GDN_BG0:GDN_BG0 + LANES], preferred_element_type=F32)
    bg_ref[...] = bg
    beta, egc, e_rest = [], [], []
    for j in range(GDN_GROUPS):
        b_j, gc = _gdn_gates(bg[j * CHUNK:(j + 1) * CHUNK], alog_ref, dtb_ref)
        beta.append(b_j)
        egc.append(jnp.exp(gc))
        e_rest.append(jnp.exp(gc[CHUNK - 1:CHUNK, :] - gc))

    def project(c0, n):
        cols = slice(c0, c0 + n)
        y = jnp.dot(xn, w_ref[:, cols], preferred_element_type=F32)
        for j in range(GDN_GROUPS):
            r0 = j * GDN_WIN
            xc_scr[r0 + 8:r0 + GDN_WIN, cols] = y[j * CHUNK:(j + 1) * CHUNK]
            if j == 0:
                first = jnp.logical_or(is_sample, i % tps == 0)
                prev = carry_scr[:, cols]
            else:
                first = is_sample
                prev = y[j * CHUNK - 8:j * CHUNK]
            seq = jnp.where(is_sample, n_p + (i - n_pt) * GDN_GROUPS + j, i // tps)
            xc_scr[r0:r0 + 8, cols] = jnp.where(first, halo0_ref[seq, :, cols], prev)

    def conv_silu(j, c0):
        cw = [cw_ref[w:w + 1, c0:c0 + LANES] for w in range(GDN_CONV)]
        wv = xc_scr[j * GDN_WIN:(j + 1) * GDN_WIN, c0:c0 + LANES]
        back2 = pltpu.roll(wv, 2, 0)
        odd = pltpu.roll(cw[2] * wv + cw[0] * back2, 1, 0)
        return _silu(cw[3] * wv[8:GDN_WIN] + cw[1] * back2[8:GDN_WIN] + odd[8:GDN_WIN])

    def epilogue(h):
        cs = slice(h * GDN_DK, (h + 1) * GDN_DK)
        for j in range(GDN_GROUPS):
            rs = slice(j * CHUNK, (j + 1) * CHUNK)
            q = conv_silu(j, h * GDN_DK)
            k = conv_silu(j, GDN_QK + h * GDN_DK)
            v = conv_silu(j, 2 * GDN_QK + h * GDN_DV)
            q = q * lax.rsqrt(jnp.sum(q * q, axis=-1, keepdims=True) + EPS) * (GDN_DK ** -0.5)
            k = k * lax.rsqrt(jnp.sum(k * k, axis=-1, keepdims=True) + EPS)
            beta_h = beta[j][:, h:h + 1]
            e_h = egc[j][:, 8 + h:9 + h]
            kb = k * beta_h
            col = lambda part, c0=h * GDN_DK: slice(part * D_MODEL + c0, part * D_MODEL + c0 + GDN_DK)
            ops_ref[rs, col(GDN_OP_K)] = k.astype(BF16)
            ops_ref[rs, col(GDN_OP_QE)] = (q * e_h).astype(BF16)
            ops_ref[rs, col(GDN_OP_KD)] = (k * e_rest[j][:, 8 + h:9 + h]).astype(BF16)
            vk0 = GDN_OP_VK * D_MODEL + 2 * h * GDN_DV
            ops_ref[rs, vk0:vk0 + GDN_DV] = (v * beta_h).astype(BF16)
            ops_ref[rs, vk0 + GDN_DV:vk0 + 2 * GDN_DV] = (kb * e_h).astype(BF16)
            kbq_ref[j, 0:CHUNK, cs] = kb.astype(BF16)
            kbq_ref[j, CHUNK:2 * CHUNK, cs] = q.astype(BF16)

    pair = 2 * GDN_DK
    n_pairs = GDN_HEADS // 2
    for m in range(n_pairs + 1):
        if m < n_pairs:
            for part in range(3):
                project(part * GDN_QK + m * pair, pair)
        if m > 0:
            epilogue(2 * m - 2)
            epilogue(2 * m - 1)
    for c in range(2):
        z = jnp.dot(xn, w_ref[:, GDN_Z0 + c * FFN_CK:GDN_Z0 + (c + 1) * FFN_CK],
                    preferred_element_type=F32)
        ops_ref[:, GDN_OP_GATE * D_MODEL + c * FFN_CK:GDN_OP_GATE * D_MODEL + (c + 1) * FFN_CK] = (
            _silu(z).astype(BF16))
    for j in range(GDN_GROUPS):
        tails_ref[j] = xc_scr[(j + 1) * GDN_WIN - 8:(j + 1) * GDN_WIN, :]
    carry_scr[...] = xc_scr[GDN_GROUPS * GDN_WIN - 8:GDN_GROUPS * GDN_WIN, :]


def _gdn_in(x, nw, w, cw, alog, dtb, halo0, *, t, n_pt, tps, n_p):
    in_specs, args = _row_tile_specs(x, n_pt)
    n_x = len(args)
    n_seq = halo0.shape[0]
    in_specs += [_resident((1, D_MODEL)), _resident((D_MODEL, GDN_P)),
                 _resident((GDN_CONV, GDN_CONV_CH)), _resident((1, LANES)), _resident((1, LANES)),
                 _resident((n_seq, 8, GDN_CONV_CH))]
    args += [nw, w, cw, alog, dtb, halo0]
    tile = lambda c: pl.BlockSpec((ROW_TILE, c), lambda i: (i, 0))
    n_chunks = t // CHUNK
    bf = lambda c: jax.ShapeDtypeStruct((t, c), BF16)
    return pl.pallas_call(
        functools.partial(_gdn_in_body, n_x=n_x, n_pt=n_pt, tps=tps, n_p=n_p),
        grid=(t // ROW_TILE,),
        in_specs=in_specs,
        out_specs=[tile(GDN_OPS * D_MODEL),
                   pl.BlockSpec((GDN_GROUPS, 2 * CHUNK, D_MODEL), lambda i: (i, 0, 0)),
                   tile(LANES),
                   pl.BlockSpec((GDN_GROUPS, 8, GDN_CONV_CH), lambda i: (i, 0, 0))],
        out_shape=[bf(GDN_OPS * D_MODEL), jax.ShapeDtypeStruct((n_chunks, 2 * CHUNK, D_MODEL), BF16),
                   jax.ShapeDtypeStruct((t, LANES), F32),
                   jax.ShapeDtypeStruct((n_chunks, 8, GDN_CONV_CH), F32)],
        scratch_shapes=[pltpu.VMEM((GDN_GROUPS * GDN_WIN, GDN_CONV_CH), F32),
                        pltpu.VMEM((8, GDN_CONV_CH), F32)],
        compiler_params=_params(), name="gdn_in")(*args)


def _gdn_pre_body(k_ref, kbq_ref, vk_ref, qe_ref, bg_ref, alog_ref, dtb_ref,
                  u_ref, wq_ref, attn_ref, egl_ref):
    pw = 2 * CHUNK
    row = lax.broadcasted_iota(jnp.int32, (CHUNK, pw), 0)
    lane = lax.broadcasted_iota(jnp.int32, (CHUNK, pw), 1)
    col = lane & (CHUNK - 1)
    left = lane < CHUNK
    incl = row >= col
    strict = row > col
    eye = (row == col).astype(F32)

    def level_mask(s):
        sh = int(math.log2(s))
        bi, bj = row >> sh, col >> sh
        return jnp.logical_and((bi & 1) == 1, bj == bi - 1)

    masks = [level_mask(s) for s in (1, 2, 4, 8, 16, 32)]
    r2 = lax.broadcasted_iota(jnp.int32, (pw, pw), 0)
    l2 = lax.broadcasted_iota(jnp.int32, (pw, pw), 1)
    same_head = (r2 < CHUNK) == (l2 < CHUNK)

    def block_diag(x):
        return jnp.where(same_head, jnp.concatenate([x, x], axis=0), jnp.zeros((pw, pw), x.dtype))

    probs = [(j, m) for j in range(GDN_NB) for m in range(GDN_HEADS // 2)]
    n = len(probs)
    rs = [slice(j * CHUNK, (j + 1) * CHUNK) for j, _ in probs]
    hc = lambda h: slice(h * GDN_DK, (h + 1) * GDN_DK)
    gcum, gcum_t = [], []
    for j in range(GDN_NB):
        _, gc = _gdn_gates(bg_ref[j * CHUNK:(j + 1) * CHUNK, :], alog_ref, dtb_ref)
        gcum.append(gc)
        gt = jnp.transpose(jnp.concatenate([gc, jnp.zeros_like(gc)], axis=0))
        gcum_t.append(gt)
        egl_ref[j] = jnp.broadcast_to(jnp.exp(gt[8:8 + GDN_HEADS, CHUNK - 1:CHUNK]),
                                      (GDN_HEADS, LANES))
    decay = []
    for j, m in probs:
        h0, h1 = 8 + 2 * m, 9 + 2 * m
        g_col = jnp.where(left, gcum[j][:, h0:h0 + 1], gcum[j][:, h1:h1 + 1])
        g_row = jnp.concatenate([gcum_t[j][h0:h0 + 1, 0:CHUNK], gcum_t[j][h1:h1 + 1, 0:CHUNK]], axis=1)
        decay.append(jnp.exp(jnp.minimum(g_col - g_row, 0.0)))
    qk = []
    for i, (j, m) in enumerate(probs):
        k_pair = jnp.concatenate([k_ref[rs[i], hc(2 * m)], k_ref[rs[i], hc(2 * m + 1)]], axis=0)
        r0, r1 = (lax.dot_general(kbq_ref[j, :, hc(h)], k_pair, (((1,), (1,)), ((), ())),
                                  preferred_element_type=F32) for h in (2 * m, 2 * m + 1))
        qk.append(jnp.where(l2 < CHUNK, r0, r1))
    a = [jnp.where(strict, qk[i][0:CHUNK] * decay[i], 0.0) for i in range(n)]
    for i, (j, m) in enumerate(probs):
        attn = jnp.where(incl, qk[i][CHUNK:2 * CHUNK] * decay[i], 0.0)
        attn_ref[rs[i], m * pw:(m + 1) * pw] = attn.astype(attn_ref.dtype)
    t = [eye - jnp.where(masks[0], ai, 0.0) for ai in a]
    a_b = [ai.astype(BF16) for ai in a]
    for lm in masks[1:]:
        t_b = [ti.astype(BF16) for ti in t]
        at = [jnp.dot(jnp.where(lm, a_b[i], jnp.zeros_like(a_b[i])), block_diag(t_b[i]),
                      preferred_element_type=F32) for i in range(n)]
        t = [t[i] - jnp.dot(t_b[i], block_diag(at[i].astype(BF16)), preferred_element_type=F32)
             for i in range(n)]
    zeros = jnp.zeros((CHUNK, 2 * GDN_DV), BF16)
    for i, (j, m) in enumerate(probs):
        v0 = vk_ref[rs[i], 4 * m * GDN_DV:(4 * m + 2) * GDN_DV]
        v1 = vk_ref[rs[i], (4 * m + 2) * GDN_DV:(4 * m + 4) * GDN_DV]
        rhs = jnp.concatenate([jnp.concatenate([v0, zeros], axis=1),
                               jnp.concatenate([zeros, v1], axis=1)], axis=0)
        uw = jnp.dot(t[i].astype(BF16), rhs, preferred_element_type=F32)
        for q, h in enumerate((2 * m, 2 * m + 1)):
            u_ref[rs[i], hc(h)] = uw[:, 2 * q * GDN_DV:(2 * q + 1) * GDN_DV]
            wq_ref[j, 0:CHUNK, hc(h)] = uw[:, (2 * q + 1) * GDN_DV:(2 * q + 2) * GDN_DV].astype(wq_ref.dtype)
            wq_ref[j, CHUNK:2 * CHUNK, hc(h)] = qe_ref[rs[i], hc(h)]


def _gdn_pre(ops, kbq, bgp, alog, dtb):
    t = ops.shape[0]
    rows = GDN_NB * CHUNK
    n_chunks = t // CHUNK
    tile = lambda c: pl.BlockSpec((rows, c), lambda b: (b, 0))
    chunked = lambda r, c: pl.BlockSpec((GDN_NB, r, c), lambda b: (b, 0, 0))
    op_block = lambda part, width: pl.BlockSpec((rows, width * D_MODEL), lambda b: (b, part // width))
    return pl.pallas_call(
        _gdn_pre_body,
        grid=(t // rows,),
        in_specs=[op_block(GDN_OP_K, 1), chunked(2 * CHUNK, D_MODEL), op_block(GDN_OP_VK, 2),
                  op_block(GDN_OP_QE, 1), tile(LANES), _resident((1, LANES)), _resident((1, LANES))],
        out_specs=[tile(D_MODEL), chunked(2 * CHUNK, D_MODEL), tile(GDN_HEADS * CHUNK),
                   chunked(GDN_HEADS, LANES)],
        out_shape=[jax.ShapeDtypeStruct((t, D_MODEL), F32),
                   jax.ShapeDtypeStruct((n_chunks, 2 * CHUNK, D_MODEL), BF16),
                   jax.ShapeDtypeStruct((t, GDN_HEADS * CHUNK), BF16),
                   jax.ShapeDtypeStruct((n_chunks, GDN_HEADS, LANES), F32)],
        compiler_params=_params(), name="gdn_pre")(ops, kbq, ops, ops, bgp, alog, dtb)


def _gdn_scan_body(u_ref, wq_ref, kd_ref, attn_ref, egl_ref, gate_ref, s0_ref, nw_ref,
                   o_ref, soutp_ref, souts_ref, s_scr, *, n_pb, bps):
    blk, is_prompt, first, last = _block_flags(n_pb, bps)

    @pl.when(first)
    def _():
        s_scr[...] = jnp.zeros_like(s_scr)

    heads = range(GDN_HEADS)
    cs = [slice(h * GDN_DK, (h + 1) * GDN_DK) for h in heads]
    s = [s_scr[h] for h in heads]
    for j in range(SCAN_NB):
        rs = slice(j * CHUNK, (j + 1) * CHUNK)
        s = [jnp.where(is_prompt, s[h], s0_ref[j, h]) for h in heads]
        ws_qs = [jnp.dot(wq_ref[j, :, cs[h]], s[h].astype(BF16), preferred_element_type=F32)
                 for h in heads]
        v_new = [(u_ref[rs, cs[h]] - ws_qs[h][0:CHUNK]).astype(BF16) for h in heads]
        o = [ws_qs[h][CHUNK:2 * CHUNK]
             + jnp.dot(attn_ref[rs, h * CHUNK:(h + 1) * CHUNK], v_new[h], preferred_element_type=F32)
             for h in heads]
        s = [s[h] * egl_ref[j, h:h + 1, :] + _mm_tn(kd_ref[rs, cs[h]], v_new[h]) for h in heads]
        for h in heads:
            o_ref[rs, cs[h]] = (_rms(o[h], nw_ref[...]) * gate_ref[rs, cs[h]]).astype(o_ref.dtype)

        @pl.when(jnp.logical_not(is_prompt))
        def _(j=j, s=s):
            for h in heads:
                souts_ref[j, h] = s[h]

    for h in heads:
        s_scr[h] = s[h]

    @pl.when(last)
    def _():
        for h in heads:
            soutp_ref[0, h] = s[h]


def _gdn_scan(u, wq, ops, attn, egl, s0, nw, *, n_p, cpl, n_s):
    t = u.shape[0]
    rows = SCAN_NB * CHUNK
    bps = cpl // SCAN_NB
    n_pb = n_p * bps
    row_blk = lambda b: (b, 0)
    s_in, s_out, s_shapes = _state_specs(n_p, n_s, n_pb, bps, (GDN_HEADS, GDN_DK, GDN_DV))
    return pl.pallas_call(
        functools.partial(_gdn_scan_body, n_pb=n_pb, bps=bps),
        grid=(t // rows,),
        in_specs=[pl.BlockSpec((rows, D_MODEL), row_blk),
                  pl.BlockSpec((SCAN_NB, 2 * CHUNK, D_MODEL), lambda b: (b, 0, 0)),
                  pl.BlockSpec((rows, D_MODEL), lambda b: (b, GDN_OP_KD)),
                  pl.BlockSpec((rows, GDN_HEADS * CHUNK), row_blk),
                  pl.BlockSpec((SCAN_NB, GDN_HEADS, LANES), lambda b: (b, 0, 0)),
                  pl.BlockSpec((rows, D_MODEL), lambda b: (b, GDN_OP_GATE)),
                  s_in, _resident((1, GDN_DV))],
        out_specs=[pl.BlockSpec((rows, D_MODEL), row_blk)] + s_out,
        out_shape=[jax.ShapeDtypeStruct((t, D_MODEL), BF16)] + s_shapes,
        scratch_shapes=[pltpu.VMEM((GDN_HEADS, GDN_DK, GDN_DV), F32)],
        compiler_params=_params(), name="gdn_scan")(u, wq, ops, attn, egl, ops, s0, nw)


GLA_NLV = len(GLA_LEVELS) + 1
GLA_SC = 2 * GLA_QK
GLA_OP_QE, GLA_OP_KD, GLA_OP_V, GLA_OP_GATE = 0, GLA_QK, 2 * GLA_QK, 2 * GLA_QK + D_MODEL
GLA_OPS = GLA_OP_GATE + D_MODEL


def _gla_exponent_matrix():
    i = np.arange(CHUNK)[:, None]
    t = np.arange(CHUNK)[None, :]
    blocks = [(t <= i)]
    for s in GLA_LEVELS:
        c = (i // (2 * s)) * (2 * s) + s - 1
        blocks.append(np.where(i > c, (t > c) & (t <= i), (t > i) & (t <= c)))
    w = np.concatenate(blocks, axis=0).astype(np.float32)
    return np.concatenate([w, w, w], axis=1)


def _split3(x):
    hi = x.astype(BF16)
    r1 = x - hi.astype(F32)
    mid = r1.astype(BF16)
    lo = (r1 - mid.astype(F32)).astype(BF16)
    return hi, mid, lo


def _gla_in_body(*refs, n_x, n_pt):
    x_refs = refs[:n_x]
    nw_ref, w_ref, wg2_ref, bg_ref, wexp_ref, sc_ref, ops_ref, ebl_ref = refs[n_x:]
    xn = _rms(_read_row_tile(x_refs, n_pt), nw_ref[...]).astype(BF16)
    chunks = range(ROW_TILE // CHUNK)
    rs = [slice(j * CHUNK, (j + 1) * CHUNK) for j in chunks]

    g_low = jnp.dot(xn, w_ref[:, GLA_G0:GLA_G0 + LANES], preferred_element_type=F32)
    q_all = jnp.dot(xn, w_ref[:, 0:GLA_QK], preferred_element_type=F32) * (GLA_DK ** -0.5)
    k_all = jnp.dot(xn, w_ref[:, GLA_QK:2 * GLA_QK], preferred_element_type=F32)
    sc_ref[:, 0:GLA_QK] = q_all.astype(BF16)
    sc_ref[:, GLA_QK:GLA_SC] = k_all.astype(BF16)
    x = [_mm(g_low[rs[j]], wg2_ref[...]) + bg_ref[...] for j in chunks]
    glog = [-_softplus(-x[j]) * (1.0 / GLA_TAU) for j in chunks]
    e_all = [jnp.dot(wexp_ref[...], jnp.concatenate(_split3(glog[j]), axis=0),
                     preferred_element_type=F32) for j in chunks]
    for lvl in range(1, GLA_NLV):
        for j in chunks:
            f = jnp.exp(e_all[j][lvl * CHUNK:(lvl + 1) * CHUNK, :])
            sc_ref[rs[j], lvl * GLA_SC:lvl * GLA_SC + GLA_QK] = (q_all[rs[j]] * f).astype(BF16)
            sc_ref[rs[j], lvl * GLA_SC + GLA_QK:(lvl + 1) * GLA_SC] = (k_all[rs[j]] * f).astype(BF16)
    for j in chunks:
        b = e_all[j][0:CHUNK, :]
        b_last = b[CHUNK - 1:CHUNK, :]
        ops_ref[rs[j], GLA_OP_QE:GLA_OP_QE + GLA_QK] = (q_all[rs[j]] * jnp.exp(b)).astype(BF16)
        ops_ref[rs[j], GLA_OP_KD:GLA_OP_KD + GLA_QK] = (k_all[rs[j]] * jnp.exp(b_last - b)).astype(BF16)
        ebl_ref[j] = jnp.broadcast_to(jnp.exp(b_last), (8, GLA_QK))
    for c in range(D_MODEL // FFN_CK):
        cols = slice(c * FFN_CK, (c + 1) * FFN_CK)
        v = jnp.dot(xn, w_ref[:, GLA_V0 + c * FFN_CK:GLA_V0 + (c + 1) * FFN_CK],
                    preferred_element_type=F32)
        ops_ref[:, GLA_OP_V + c * FFN_CK:GLA_OP_V + (c + 1) * FFN_CK] = v.astype(BF16)
        r = jnp.dot(xn, w_ref[:, GLA_R0 + c * FFN_CK:GLA_R0 + (c + 1) * FFN_CK],
                    preferred_element_type=F32)
        ops_ref[:, GLA_OP_GATE + c * FFN_CK:GLA_OP_GATE + (c + 1) * FFN_CK] = _silu(r).astype(BF16)


def _gla_in(x, nw, w, wg2, bg, wexp, *, t, n_pt):
    in_specs, args = _row_tile_specs(x, n_pt)
    n_x = len(args)
    in_specs += [_resident((1, D_MODEL)), _resident((D_MODEL, GLA_P)), _resident((LANES, GLA_QK)),
                 _resident((1, GLA_QK)), _resident((GLA_NLV * CHUNK, 3 * CHUNK))]
    args += [nw, w, wg2, bg, wexp]
    tile = lambda c: pl.BlockSpec((ROW_TILE, c), lambda i: (i, 0))
    return pl.pallas_call(
        functools.partial(_gla_in_body, n_x=n_x, n_pt=n_pt),
        grid=(t // ROW_TILE,),
        in_specs=in_specs,
        out_specs=[tile(GLA_NLV * GLA_SC), tile(GLA_OPS),
                   pl.BlockSpec((ROW_TILE // CHUNK, 8, GLA_QK), lambda i: (i, 0, 0))],
        out_shape=[jax.ShapeDtypeStruct((t, GLA_NLV * GLA_SC), BF16),
                   jax.ShapeDtypeStruct((t, GLA_OPS), BF16),
                   jax.ShapeDtypeStruct((t // CHUNK, 8, GLA_QK), F32)],
        compiler_params=_params(), name="gla_in")(*args)


def _gla_body(sc_ref, ops_ref, ebl_ref, s0_ref, nw_ref,
              o_ref, soutp_ref, souts_ref, st_scr, s0t_scr, *, n_pb, bps):
    blk, is_prompt, first, last = _block_flags(n_pb, bps)
    heads = range(GLA_HEADS)
    chunks = range(SCAN_NB)

    @pl.when(first)
    def _():
        st_scr[...] = jnp.zeros_like(st_scr)

    @pl.when(blk == 0)
    def _():
        s0t_scr[...] = jnp.zeros_like(s0t_scr)

    @pl.when(jnp.logical_not(is_prompt))
    def _():
        for j in chunks:
            for h in heads:
                s0t_scr[j, h] = s0_ref[j, h].T

    row = lax.broadcasted_iota(jnp.int32, (CHUNK, CHUNK), 0)
    col = lax.broadcasted_iota(jnp.int32, (CHUNK, CHUNK), 1)
    masks = [row == col] + [_pair_mask(s) for s in GLA_LEVELS]
    rs = [slice(j * CHUNK, (j + 1) * CHUNK) for j in chunks]
    cv = [slice(h * GLA_DV, (h + 1) * GLA_DV) for h in heads]

    def op(j, base, h, width):
        return ops_ref[rs[j], base + h * width:base + (h + 1) * width]

    def scores(j, h, lvl):
        q0 = lvl * GLA_SC + h * GLA_DK
        k0 = q0 + GLA_QK
        return lax.dot_general(sc_ref[rs[j], q0:q0 + GLA_DK], sc_ref[rs[j], k0:k0 + GLA_DK],
                               (((1,), (1,)), ((), ())), preferred_element_type=F32)

    attn = [[jnp.where(masks[0], scores(j, h, 0), 0.0) for h in heads] for j in chunks]
    for lvl in range(1, GLA_NLV):
        attn = [[attn[j][h] + jnp.where(masks[lvl], scores(j, h, lvl), 0.0) for h in heads]
                for j in chunks]
    av = [[jnp.dot(attn[j][h].astype(BF16), op(j, GLA_OP_V, h, GLA_DV), preferred_element_type=F32)
           for h in heads] for j in chunks]
    kv = [[lax.dot_general(op(j, GLA_OP_V, h, GLA_DV), op(j, GLA_OP_KD, h, GLA_DK),
                           (((0,), (0,)), ((), ())), preferred_element_type=F32)
           for h in heads] for j in chunks]

    st = [st_scr[h] for h in heads]
    for j in chunks:
        st = [jnp.where(is_prompt, st[h], s0t_scr[j, h]) for h in heads]
        o = [_mm_nt(op(j, GLA_OP_QE, h, GLA_DK), st[h]) + av[j][h] for h in heads]
        st = [st[h] * ebl_ref[j, 0:1, h * GLA_DK:(h + 1) * GLA_DK] + kv[j][h] for h in heads]
        for h in heads:
            gate = op(j, GLA_OP_GATE, h, GLA_DV)
            o_ref[rs[j], cv[h]] = (_rms(o[h], nw_ref[h:h + 1, :]) * gate).astype(o_ref.dtype)

        @pl.when(jnp.logical_not(is_prompt))
        def _(j=j, st=st):
            for h in heads:
                souts_ref[j, h] = st[h].T

    for h in heads:
        st_scr[h] = st[h]

    @pl.when(last)
    def _():
        for h in heads:
            soutp_ref[0, h] = st[h].T


def _gla_scan(sc, ops, ebl, s0, nw, *, n_p, cpl, n_s):
    t = sc.shape[0]
    rows = SCAN_NB * CHUNK
    bps = cpl // SCAN_NB
    n_pb = n_p * bps
    s_in, s_out, s_shapes = _state_specs(n_p, n_s, n_pb, bps, (GLA_HEADS, GLA_DK, GLA_DV))
    return pl.pallas_call(
        functools.partial(_gla_body, n_pb=n_pb, bps=bps),
        grid=(t // rows,),
        in_specs=[pl.BlockSpec((rows, GLA_NLV * GLA_SC), lambda b: (b, 0)),
                  pl.BlockSpec((rows, GLA_OPS), lambda b: (b, 0)),
                  pl.BlockSpec((SCAN_NB, 8, GLA_QK), lambda b: (b, 0, 0)),
                  s_in, _resident((GLA_HEADS, GLA_DV))],
        out_specs=[pl.BlockSpec((rows, D_MODEL), lambda b: (b, 0))] + s_out,
        out_shape=[jax.ShapeDtypeStruct((t, D_MODEL), BF16)] + s_shapes,
        scratch_shapes=[pltpu.VMEM((GLA_HEADS, GLA_DV, GLA_DK), F32),
                        pltpu.VMEM((SCAN_NB, GLA_HEADS, GLA_DV, GLA_DK), F32)],
        compiler_params=_params(), name="gla_scan")(sc, ops, ebl, s0, nw)


CNV_WIN = CONV_HALO + CHUNK
TILE_CHUNKS = ROW_TILE // CHUNK
CNV_CK = 256


def _cnv_in_body(*refs, n_x, n_pt, tps, n_p):
    x_refs = refs[:n_x]
    (nw_ref, w_ref, b_ref, wdw_ref, bdw_ref, lnw_ref, lnb_ref, halo0_ref,
     a_ref, tails_ref, xc_scr, y_scr, carry_scr) = refs[n_x:]
    i = pl.program_id(0)
    is_sample = i >= n_pt

    @pl.when(i == 0)
    def _():
        carry_scr[...] = jnp.zeros_like(carry_scr)

    xn = _rms(_read_row_tile(x_refs, n_pt), nw_ref[...]).astype(BF16)
    taps = {}
    for w in range(CONV_W):
        s = CONV_HALO - (CONV_W - 1) + w
        taps.setdefault(-s % 8, []).append((w, s + (-s % 8)))

    def glu_slab(c0):
        cols = slice(c0, c0 + CNV_CK)
        gcols = slice(D_MODEL + c0, D_MODEL + c0 + CNV_CK)
        val = jnp.dot(xn, w_ref[:, cols], preferred_element_type=F32) + b_ref[:, cols]
        gate = jnp.dot(xn, w_ref[:, gcols], preferred_element_type=F32) + b_ref[:, gcols]
        glu = val * _sigmoid(gate)
        for j in range(TILE_CHUNKS):
            r0 = j * CNV_WIN
            xc_scr[r0 + CONV_HALO:r0 + CNV_WIN, cols] = glu[j * CHUNK:(j + 1) * CHUNK]
            if j == 0:
                first = jnp.logical_or(is_sample, i % tps == 0)
                prev = carry_scr[:, cols]
            else:
                first = is_sample
                prev = glu[j * CHUNK - CONV_HALO:j * CHUNK]
            seq = jnp.where(is_sample, n_p + (i - n_pt) * TILE_CHUNKS + j, i // tps)
            xc_scr[r0:r0 + CONV_HALO, cols] = jnp.where(first, halo0_ref[seq, :, cols], prev)

    def conv_slab(c0):
        for j in range(TILE_CHUNKS):
            for b0 in range(c0, c0 + CNV_CK, LANES):
                cs = slice(b0, b0 + LANES)
                wv = xc_scr[j * CNV_WIN:(j + 1) * CNV_WIN, cs]
                y = None
                for r, group in sorted(taps.items()):
                    rolled = wv if r == 0 else pltpu.roll(wv, r, 0)
                    for w, a0 in group:
                        term = wdw_ref[w:w + 1, cs] * rolled[a0:a0 + CHUNK]
                        y = term if y is None else y + term
                y_scr[j * CHUNK:(j + 1) * CHUNK, cs] = y + bdw_ref[:, cs]

    slabs = list(range(0, D_MODEL, CNV_CK))
    for n, c0 in enumerate(slabs + [None]):
        if c0 is not None:
            glu_slab(c0)
        if n > 0:
            conv_slab(slabs[n - 1])
    for j in range(TILE_CHUNKS):
        rs = slice(j * CHUNK, (j + 1) * CHUNK)
        a_ref[rs, :] = _silu(_layernorm(y_scr[rs, :], lnw_ref[...], lnb_ref[...])).astype(BF16)
        tails_ref[j] = xc_scr[(j + 1) * CNV_WIN - CONV_HALO:(j + 1) * CNV_WIN, :]
    carry_scr[...] = xc_scr[TILE_CHUNKS * CNV_WIN - CONV_HALO:TILE_CHUNKS * CNV_WIN, :]


def _cnv_in(x, nw, w, b, wdw, bdw, lnw, lnb, halo0, *, t, n_pt, tps, n_p):
    in_specs, args = _row_tile_specs(x, n_pt)
    n_x = len(args)
    in_specs += [_resident((1, D_MODEL)), _resident((D_MODEL, 2 * D_MODEL)), _resident((1, 2 * D_MODEL)),
                 _resident((CONV_HALO, D_MODEL)), _resident((1, D_MODEL)),
                 _resident((1, D_MODEL)), _resident((1, D_MODEL)), _resident(halo0.shape)]
    args += [nw, w, b, wdw, bdw, lnw, lnb, halo0]
    return pl.pallas_call(
        functools.partial(_cnv_in_body, n_x=n_x, n_pt=n_pt, tps=tps, n_p=n_p),
        grid=(t // ROW_TILE,),
        in_specs=in_specs,
        out_specs=[pl.BlockSpec((ROW_TILE, D_MODEL), lambda i: (i, 0)),
                   pl.BlockSpec((TILE_CHUNKS, CONV_HALO, D_MODEL), lambda i: (i, 0, 0))],
        out_shape=[jax.ShapeDtypeStruct((t, D_MODEL), BF16),
                   jax.ShapeDtypeStruct((t // CHUNK, CONV_HALO, D_MODEL), F32)],
        scratch_shapes=[pltpu.VMEM((TILE_CHUNKS * CNV_WIN, D_MODEL), F32),
                        pltpu.VMEM((ROW_TILE, D_MODEL), F32),
                        pltpu.VMEM((CONV_HALO, D_MODEL), F32)],
        compiler_params=_params(), name="cnv_in")(*args)


def _pad_cols(w, n):
    return jnp.pad(w, ((0, 0), (0, n - w.shape[1])))


def _row(v):
    return v.reshape(1, -1).astype(F32)


def _tail_rows(a, lay, n):
    n_p, cpl, n_s = lay
    ends = np.concatenate([(np.arange(n_p) + 1) * cpl * CHUNK,
                           n_p * cpl * CHUNK + (np.arange(n_s) + 1) * CHUNK])
    idx = (ends[:, None] - n + np.arange(n)[None, :]).reshape(-1)
    return jnp.take(a, jnp.asarray(idx, jnp.int32), axis=0).reshape(n_p + n_s, n, a.shape[1])


def _gdn_mix(x, lay, dims, mix_nw, cache, state, w_in, conv_w, a_log, dt_bias, norm_w):
    n_p, cpl, n_s = lay
    halo0 = jnp.concatenate([jnp.zeros((n_p, 8, GDN_CONV_CH), F32),
                             jnp.pad(cache, ((0, 0), (8 - (GDN_CONV - 1), 0), (0, 0)))], axis=0)
    lane_pad = lambda v: jnp.pad(v.reshape(1, -1), ((0, 0), (GDN_HEADS, LANES - 2 * GDN_HEADS)))
    alog, dtb = lane_pad(a_log), lane_pad(dt_bias)
    ops, kbq, bgp, tails = _gdn_in(
        x, _row(mix_nw), _pad_cols(w_in, GDN_P).astype(BF16), conv_w, alog, dtb, halo0,
        tps=cpl * CHUNK // ROW_TILE, n_p=n_p, **dims)
    u, wq, attn, egl = _gdn_pre(ops, kbq, bgp, alog, dtb)
    o, s_p, s_s = _gdn_scan(u, wq, ops, attn, egl, state, _row(norm_w), n_p=n_p, cpl=cpl, n_s=n_s)
    ends = np.concatenate([(np.arange(n_p) + 1) * cpl - 1, n_p * cpl + np.arange(n_s)])
    conv_rows = tails[jnp.asarray(ends, jnp.int32), 8 - (GDN_CONV - 1):, :]
    return o, (conv_rows[:n_p], conv_rows[n_p:], s_p, s_s)


def _gla_mix(x, lay, dims, mix_nw, state, w_in, w_g2, b_g, norm_w):
    n_p, cpl, n_s = lay
    wg2 = jnp.pad(w_g2, ((0, LANES - GLA_RANK), (0, 0))).astype(BF16)
    wexp = jnp.asarray(_gla_exponent_matrix(), BF16)
    sc, ops, ebl = _gla_in(x, _row(mix_nw), _pad_cols(w_in, GLA_P).astype(BF16), wg2, _row(b_g), wexp,
                           **dims)
    o, s_p, s_s = _gla_scan(sc, ops, ebl, state, norm_w, n_p=n_p, cpl=cpl, n_s=n_s)
    return o, (s_p, s_s)


def _cnv_mix(x, lay, dims, mix_nw, cache, w_pw1, b_pw1, w_dw, b_dw, ln_w, ln_b):
    n_p, cpl, n_s = lay
    pad_rows = CONV_HALO - (CONV_W - 1)
    halo0 = jnp.concatenate([jnp.zeros((n_p, CONV_HALO, D_MODEL), F32),
                             jnp.pad(cache, ((0, 0), (pad_rows, 0), (0, 0)))], axis=0)
    wdw = jnp.pad(w_dw, ((0, CONV_HALO - CONV_W), (0, 0)))
    a, tails = _cnv_in(x, _row(mix_nw), w_pw1.astype(BF16), _row(b_pw1), wdw, _row(b_dw),
                       _row(ln_w), _row(ln_b), halo0, tps=cpl * CHUNK // ROW_TILE, n_p=n_p, **dims)
    ends = np.concatenate([(np.arange(n_p) + 1) * cpl - 1, n_p * cpl + np.arange(n_s)])
    rows = tails[jnp.asarray(ends, jnp.int32), pad_rows:, :]
    return a, (rows[:n_p], rows[n_p:])


def kernel(x_prompt, x_sample, cache_gdn_conv, state_gdn, state_gla, cache_conformer, mix_norm_w, ffn_norm_w, ffn_w1, ffn_w2, final_norm_w, gdn_w_in, gdn_conv_w, gdn_a_log, gdn_dt_bias, gdn_norm_w, gdn_w_out, gla_w_in, gla_w_g2, gla_b_g, gla_norm_w, gla_w_out, gmlp_w_in, gmlp_b_in, gmlp_ln_w, gmlp_ln_b, gmlp_w_s, gmlp_b_s, gmlp_w_out, gmlp_b_out, cnv_w_pw1, cnv_b_pw1, cnv_w_dw, cnv_b_dw, cnv_ln_w, cnv_ln_b, cnv_w_pw2, cnv_b_pw2):
    n_p, l_p, d = x_prompt.shape
    n_s, l_s, _ = x_sample.shape
    assert d == D_MODEL and l_s == CHUNK and l_p % ROW_TILE == 0
    assert n_s % max(SCAN_NB, GDN_NB) == 0
    t_p, t_s = n_p * l_p, n_s * l_s
    assert t_p % ROW_TILE == 0 and t_s % ROW_TILE == 0
    lay = (n_p, l_p // CHUNK, n_s)
    dims = dict(t=t_p + t_s, n_pt=t_p // ROW_TILE)
    depth = mix_norm_w.shape[0]

    x = (x_prompt.reshape(t_p, d), x_sample.reshape(t_s, d))
    outs = {k: [] for k in ("gdn_cp", "gdn_cs", "gdn_sp", "gdn_ss", "gla_sp", "gla_ss",
                            "gmlp_vs", "cnv_p", "cnv_s")}
    w1_all, w2_all = ffn_w1.astype(BF16), ffn_w2.astype(BF16)
    for i in range(depth):
        kind, j = i % 4, i // 4
        ffn = (_row(ffn_norm_w[i]), w1_all, w2_all, _row(final_norm_w) if i == depth - 1 else None)
        if kind == 0:
            o, (cp, cs, sp, ss) = _gdn_mix(x, lay, dims, mix_norm_w[i], cache_gdn_conv[j], state_gdn[j],
                                           gdn_w_in[j], gdn_conv_w[j], gdn_a_log[j], gdn_dt_bias[j],
                                           gdn_norm_w[j])
            outs["gdn_cp"].append(cp); outs["gdn_cs"].append(cs)
            outs["gdn_sp"].append(sp); outs["gdn_ss"].append(ss)
            res = _post([o], x, gdn_w_out[j].astype(BF16), None, *ffn,
                        prologue="plain", layer=i, name=f"post_{i}", **dims)
        elif kind == 1:
            o, (sp, ss) = _gla_mix(x, lay, dims, mix_norm_w[i], state_gla[j], gla_w_in[j], gla_w_g2[j],
                                   gla_b_g[j], gla_norm_w[j])
            outs["gla_sp"].append(sp); outs["gla_ss"].append(ss)
            res = _post([o], x, gla_w_out[j].astype(BF16), None, *ffn,
                        prologue="plain", layer=i, name=f"post_{i}", **dims)
        elif kind == 2:
            zz = _proj(x, _row(mix_norm_w[i]), gmlp_w_in[j].astype(BF16), _row(gmlp_b_in[j]),
                       n_out=2 * D_MODEL, ck=512, epilogue="gelu", name="gmlp_in", **dims)
            bs_full = jnp.repeat(gmlp_b_s[j].T, GMLP_GC, axis=1)
            res = _post([zz, _row(gmlp_ln_w[j]), _row(gmlp_ln_b[j]), gmlp_w_s[j], bs_full], x,
                        gmlp_w_out[j].astype(BF16), _row(gmlp_b_out[j]), *ffn,
                        prologue="gmlp", layer=i, name=f"post_{i}", **dims)
            outs["gmlp_vs"].append(res[-1].reshape(n_s, l_s, d))
            res = res[:-1]
        else:
            a, (cp, cs) = _cnv_mix(x, lay, dims, mix_norm_w[i], cache_conformer[j], cnv_w_pw1[j],
                                   cnv_b_pw1[j], cnv_w_dw[j], cnv_b_dw[j], cnv_ln_w[j], cnv_ln_b[j])
            outs["cnv_p"].append(cp); outs["cnv_s"].append(cs)
            res = _post([a], x, cnv_w_pw2[j].astype(BF16), _row(cnv_b_pw2[j]), *ffn,
                        prologue="plain", layer=i, name=f"post_{i}", **dims)
        x = tuple(res) if i == depth - 1 else res[0]
    y_p, y_s = x
    stack = lambda xs: xs[0][None] if len(xs) == 1 else jnp.stack(xs)
    return (y_p.reshape(n_p, l_p, d), y_s.reshape(n_s, l_s, d),
            stack(outs["gdn_cp"]), stack(outs["gdn_cs"]),
            stack(outs["gdn_sp"]), stack(outs["gdn_ss"]),
            stack(outs["gla_sp"]), stack(outs["gla_ss"]),
            stack(outs["gmlp_vs"]),
            stack(outs["cnv_p"]), stack(outs["cnv_s"]))
```

```python
import functools
import math

import numpy as np
import jax
import jax.numpy as jnp
from jax import lax
from jax.experimental import pallas as pl
from jax.experimental.pallas import tpu as pltpu

F32 = jnp.float32
BF16 = jnp.bfloat16

D_MODEL = 1024
D_FF = 4 * D_MODEL
EPS = 1e-6
CHUNK = 64
ROW_TILE = 512
SCAN_NB = 4
LANES = 128
VMEM_LIMIT = 56 * 1024 * 1024

GDN_HEADS, GDN_DK, GDN_DV, GDN_CONV = 8, 128, 128, 4
GDN_QK = GDN_HEADS * GDN_DK
GDN_CONV_CH = 2 * GDN_QK + GDN_HEADS * GDN_DV
GDN_Z0 = GDN_CONV_CH
GDN_BG0 = GDN_CONV_CH + GDN_HEADS * GDN_DV
GDN_P = GDN_BG0 + LANES

GLA_HEADS, GLA_DK, GLA_DV, GLA_RANK, GLA_TAU = 4, 128, 256, 16, 16.0
GLA_QK = GLA_HEADS * GLA_DK
GLA_V0, GLA_R0, GLA_G0 = 2 * GLA_QK, 2 * GLA_QK + D_MODEL, 2 * GLA_QK + 2 * D_MODEL
GLA_P = GLA_G0 + LANES
GLA_LEVELS = (32, 16, 8, 4, 2, 1)

GMLP_BLOCK, GMLP_GROUPS, GMLP_GC = 128, 4, 256
CONV_W = 31
CONV_HALO = 32


def _mm(a, b):
    return jnp.dot(a.astype(BF16), b.astype(BF16), preferred_element_type=F32)


def _mm_nt(a, b):
    return lax.dot_general(a.astype(BF16), b.astype(BF16), (((1,), (1,)), ((), ())),
                           preferred_element_type=F32)


def _mm_tn(a, b):
    return lax.dot_general(a.astype(BF16), b.astype(BF16), (((0,), (0,)), ((), ())),
                           preferred_element_type=F32)


def _rms(x, w):
    return x * lax.rsqrt(jnp.mean(x * x, axis=-1, keepdims=True) + EPS) * w


def _layernorm(x, w, b):
    xc = x - jnp.mean(x, axis=-1, keepdims=True)
    return xc * lax.rsqrt(jnp.mean(xc * xc, axis=-1, keepdims=True) + EPS) * w + b


def _sigmoid(x):
    return jax.nn.sigmoid(x)


def _silu(x):
    h = 0.5 * x
    return h + h * jnp.tanh(h)


def _softplus(x):
    return jnp.maximum(x, 0.0) + jnp.log1p(jnp.exp(-jnp.abs(x)))


def _gelu_tanh(x):
    return 0.5 * x * (1.0 + jnp.tanh(math.sqrt(2.0 / math.pi) * (x + 0.044715 * (x * x * x))))


def _cumsum_rows(x):
    row = lax.broadcasted_iota(jnp.int32, x.shape, 0)
    s = 1
    while s < x.shape[0]:
        x = x + jnp.where(row >= s, pltpu.roll(x, s, 0), 0.0)
        s *= 2
    return x


def _pair_mask(s):
    sh = int(math.log2(s))
    bi = lax.broadcasted_iota(jnp.int32, (CHUNK, CHUNK), 0) >> sh
    bj = lax.broadcasted_iota(jnp.int32, (CHUNK, CHUNK), 1) >> sh
    return jnp.logical_and((bi & 1) == 1, bj == bi - 1)


def _block_flags(n_pb, bps):
    blk = pl.program_id(0)
    is_prompt = blk < n_pb
    first = jnp.logical_and(is_prompt, blk % bps == 0)
    last = jnp.logical_and(is_prompt, blk % bps == bps - 1)
    return blk, is_prompt, first, last


def _state_specs(n_p, n_s, n_pb, bps, nb, shape):
    zeros = (0,) * len(shape)
    sample = pl.BlockSpec((nb,) + shape, lambda b: (jnp.maximum(b - n_pb, 0),) + zeros)
    prompt = pl.BlockSpec((1,) + shape, lambda b: (jnp.minimum(b // bps, n_p - 1),) + zeros)
    shapes = [jax.ShapeDtypeStruct((n_p,) + shape, F32), jax.ShapeDtypeStruct((n_s,) + shape, F32)]
    return sample, [prompt, sample], shapes


def _params(n_axes=1):
    return pltpu.CompilerParams(dimension_semantics=("arbitrary",) * n_axes,
                                vmem_limit_bytes=VMEM_LIMIT)


def _resident(shape):
    zeros = (0,) * len(shape)
    return pl.BlockSpec(shape, lambda i: zeros, pipeline_mode=pl.Buffered(1))


def _row_tile_specs(x, n_pt):
    if isinstance(x, tuple):
        return ([pl.BlockSpec((ROW_TILE, D_MODEL), lambda i: (jnp.minimum(i, n_pt - 1), 0)),
                 pl.BlockSpec((ROW_TILE, D_MODEL), lambda i: (jnp.maximum(i - n_pt, 0), 0))], list(x))
    return [pl.BlockSpec((ROW_TILE, D_MODEL), lambda i: (i, 0))], [x]


def _read_row_tile(refs, n_pt):
    if len(refs) == 2:
        return jnp.where(pl.program_id(0) < n_pt, refs[0][...], refs[1][...])
    return refs[0][...]


def _proj_body(*refs, n_x, n_pt, n_out, ck, epilogue, has_bias):
    x_refs, (nw_ref, w_ref), rest = refs[:n_x], refs[n_x:n_x + 2], refs[n_x + 2:]
    if has_bias:
        b_ref, o_ref = rest
    else:
        (o_ref,) = rest
    xn = _rms(_read_row_tile(x_refs, n_pt), nw_ref[...]).astype(BF16)
    for c0 in range(0, n_out, ck):
        sl = slice(c0, min(c0 + ck, n_out))
        y = jnp.dot(xn, w_ref[:, sl], preferred_element_type=F32)
        if has_bias:
            y = y + b_ref[:, sl]
        if epilogue == "gelu":
            y = _gelu_tanh(y)
        o_ref[:, sl] = y.astype(o_ref.dtype)


def _proj(x, nw, w, b, *, t, n_pt, n_out, ck, epilogue="none", out_dtype=F32, name):
    n_w = w.shape[1]
    has_bias = b is not None
    in_specs, args = _row_tile_specs(x, n_pt)
    n_x = len(args)
    in_specs += [_resident((1, D_MODEL)), _resident((D_MODEL, n_w))]
    args += [nw, w]
    if has_bias:
        in_specs.append(_resident((1, n_w)))
        args.append(b)
    return pl.pallas_call(
        functools.partial(_proj_body, n_x=n_x, n_pt=n_pt, n_out=n_out, ck=ck, epilogue=epilogue,
                          has_bias=has_bias),
        grid=(t // ROW_TILE,),
        in_specs=in_specs,
        out_specs=pl.BlockSpec((ROW_TILE, n_out), lambda i: (i, 0)),
        out_shape=jax.ShapeDtypeStruct((t, n_out), out_dtype),
        compiler_params=_params(), name=name)(*args)


FFN_CK = 512


def _gmlp_gate(zz_ref, lnw_ref, lnb_ref, ws_ref, bs_ref, vs_ref, a_scr, is_sample):
    v = _layernorm(zz_ref[:, D_MODEL:2 * D_MODEL], lnw_ref[...], lnb_ref[...])

    @pl.when(is_sample)
    def _():
        vs_ref[...] = v

    vb = v.astype(BF16)
    row = lax.broadcasted_iota(jnp.int32, (GMLP_BLOCK, GMLP_BLOCK), 0)
    col = lax.broadcasted_iota(jnp.int32, (GMLP_BLOCK, GMLP_BLOCK), 1)
    tril = col <= row
    top_left = jnp.logical_and(row < CHUNK, col < CHUNK)
    brow = lax.broadcasted_iota(jnp.int32, (GMLP_BLOCK, D_MODEL), 0)
    bias = bs_ref[...]
    bias = jnp.where(jnp.logical_and(is_sample, brow >= CHUNK), pltpu.roll(bias, CHUNK, 0), bias)
    for grp in range(GMLP_GROUPS):
        cs = slice(grp * GMLP_GC, (grp + 1) * GMLP_GC)
        wt = jnp.where(tril, ws_ref[grp], 0.0)
        w11 = jnp.where(top_left, wt, 0.0)
        wd = w11 + pltpu.roll(pltpu.roll(w11, CHUNK, 0), CHUNK, 1)
        we = jnp.where(is_sample, wd, wt).astype(BF16)
        for blk in range(ROW_TILE // GMLP_BLOCK):
            rs = slice(blk * GMLP_BLOCK, (blk + 1) * GMLP_BLOCK)
            mixed = jnp.dot(we, vb[rs, cs], preferred_element_type=F32)
            a_scr[rs, cs] = (zz_ref[rs, cs] * (mixed + bias[:, cs])).astype(BF16)
    return a_scr[...]


def _post_body(*refs, prologue, n_res, has_bias, final, n_pt):
    n_in = {"plain": 1, "gmlp": 5}[prologue]
    pro, refs = refs[:n_in], refs[n_in:]
    res_refs, refs = refs[:n_res], refs[n_res:]
    wo_ref, refs = refs[0], refs[1:]
    if has_bias:
        bo_ref, refs = refs[0], refs[1:]
    fnw_ref, w1_ref, w2_ref, refs = refs[0], refs[1], refs[2], refs[3:]
    if final:
        finw_ref, refs = refs[0], refs[1:]
    n_main = 2 if final else 1
    outs, refs = refs[:n_main], refs[n_main:]
    is_sample = pl.program_id(0) >= n_pt

    if prologue == "plain":
        a = pro[0][...]
    else:
        vs_ref, a_scr = refs
        a = _gmlp_gate(*pro, vs_ref, a_scr, is_sample)

    x = _read_row_tile(res_refs, n_pt) + jnp.dot(a, wo_ref[...], preferred_element_type=F32)
    if has_bias:
        x = x + bo_ref[...]
    xn = _rms(x, fnw_ref[...]).astype(BF16)
    for c in range(D_FF // FFN_CK):
        sl = slice(c * FFN_CK, (c + 1) * FFN_CK)
        h = jnp.dot(xn, w1_ref[:, sl], preferred_element_type=F32)
        h = jnp.square(jnp.maximum(h, 0.0)).astype(BF16)
        x = x + jnp.dot(h, w2_ref[sl, :], preferred_element_type=F32)
    if final:
        y = _rms(x, finw_ref[...])

        @pl.when(jnp.logical_not(is_sample))
        def _():
            outs[0][...] = y

        @pl.when(is_sample)
        def _():
            outs[1][...] = y
    else:
        outs[0][...] = x


def _post(pro_args, res, wo, bo, fnw, w1, w2, finw, *, layer, prologue, t, n_pt, name):
    has_bias = bo is not None
    final = finw is not None
    tile = lambda c: pl.BlockSpec((ROW_TILE, c), lambda i: (i, 0))
    if prologue == "plain":
        in_specs = [tile(D_MODEL)]
    else:
        in_specs = [tile(2 * D_MODEL), _resident((1, D_MODEL)), _resident((1, D_MODEL)),
                    _resident((GMLP_GROUPS, GMLP_BLOCK, GMLP_BLOCK)),
                    _resident((GMLP_BLOCK, D_MODEL))]
    args = list(pro_args)
    res_specs, res_args = _row_tile_specs(res, n_pt)
    in_specs += res_specs + [_resident((D_MODEL, D_MODEL))]
    args += res_args + [wo]
    if has_bias:
        in_specs.append(_resident((1, D_MODEL)))
        args.append(bo)
    layer_block = lambda shape: pl.BlockSpec((None,) + shape, lambda i: (layer, 0, 0),
                                             pipeline_mode=pl.Buffered(1))
    in_specs += [_resident((1, D_MODEL)), layer_block((D_MODEL, D_FF)), layer_block((D_FF, D_MODEL))]
    args += [fnw, w1, w2]
    if final:
        in_specs.append(_resident((1, D_MODEL)))
        args.append(finw)
        out_specs = [pl.BlockSpec((ROW_TILE, D_MODEL), lambda i: (jnp.minimum(i, n_pt - 1), 0)),
                     pl.BlockSpec((ROW_TILE, D_MODEL), lambda i: (jnp.maximum(i - n_pt, 0), 0))]
        out_shape = [jax.ShapeDtypeStruct((n_pt * ROW_TILE, D_MODEL), F32),
                     jax.ShapeDtypeStruct((t - n_pt * ROW_TILE, D_MODEL), F32)]
    else:
        out_specs = [tile(D_MODEL)]
        out_shape = [jax.ShapeDtypeStruct((t, D_MODEL), F32)]
    scratch = []
    if prologue == "gmlp":
        out_specs.append(pl.BlockSpec((ROW_TILE, D_MODEL), lambda i: (jnp.maximum(i - n_pt, 0), 0)))
        out_shape.append(jax.ShapeDtypeStruct((t - n_pt * ROW_TILE, D_MODEL), F32))
        scratch.append(pltpu.VMEM((ROW_TILE, D_MODEL), BF16))
    return pl.pallas_call(
        functools.partial(_post_body, prologue=prologue, n_res=len(res_args), has_bias=has_bias,
                          final=final, n_pt=n_pt),
        grid=(t // ROW_TILE,),
        in_specs=in_specs, out_specs=out_specs, out_shape=out_shape, scratch_shapes=scratch,
        compiler_params=_params(), name=name)(*args)


GDN_NB = 4
GDN_WIN = CHUNK + 8
GDN_GROUPS = ROW_TILE // CHUNK
GDN_OP_K, GDN_OP_QE, GDN_OP_KD, GDN_OP_GATE, GDN_OP_VK, GDN_OPS = 0, 1, 2, 3, 4, 6


def _gdn_gates(bg, alog_ref, dtb_ref):
    glog = -jnp.exp(alog_ref[...]) * _softplus(bg + dtb_ref[...])
    return _sigmoid(bg), _cumsum_rows(glog)


def _gdn_in_body(*refs, n_x, n_pt, tps, n_p):
    x_refs = refs[:n_x]
    (nw_ref, w_ref, cw_ref, alog_ref, dtb_ref, halo0_ref,
     ops_ref, kbq_ref, bg_ref, tails_ref, xc_scr, carry_scr) = refs[n_x:]
    i = pl.program_id(0)
    is_sample = i >= n_pt

    @pl.when(i == 0)
    def _():
        carry_scr[...] = jnp.zeros_like(carry_scr)

    xn = _rms(_read_row_tile(x_refs, n_pt), nw_ref[...]).astype(BF16)
    bg = jnp.dot(xn, w_ref[:, GDN_BG0:GDN_BG0 + LANES], preferred_element_type=F32)
    bg_ref[...] = bg
    beta, egc, e_rest = [], [], []
    for j in range(GDN_GROUPS):
        b_j, gc = _gdn_gates(bg[j * CHUNK:(j + 1) * CHUNK], alog_ref, dtb_ref)
        beta.append(b_j)
        egc.append(jnp.exp(gc))
        e_rest.append(jnp.exp(gc[CHUNK - 1:CHUNK, :] - gc))

    def project(c0, n):
        cols = slice(c0, c0 + n)
        y = jnp.dot(xn, w_ref[:, cols], preferred_element_type=F32)
        for j in range(GDN_GROUPS):
            r0 = j * GDN_WIN
            xc_scr[r0 + 8:r0 + GDN_WIN, cols] = y[j * CHUNK:(j + 1) * CHUNK]
            if j == 0:
                first = jnp.logical_or(is_sample, i % tps == 0)
                prev = carry_scr[:, cols]
            else:
                first = is_sample
                prev = y[j * CHUNK - 8:j * CHUNK]
            seq = jnp.where(is_sample, n_p + (i - n_pt) * GDN_GROUPS + j, i // tps)
            xc_scr[r0:r0 + 8, cols] = jnp.where(first, halo0_ref[seq, :, cols], prev)

    def conv_silu(j, c0):
        cw = [cw_ref[w:w + 1, c0:c0 + LANES] for w in range(GDN_CONV)]
        wv = xc_scr[j * GDN_WIN:(j + 1) * GDN_WIN, c0:c0 + LANES]
        back2 = pltpu.roll(wv, 2, 0)
        odd = pltpu.roll(cw[2] * wv + cw[0] * back2, 1, 0)
        return _silu(cw[3] * wv[8:GDN_WIN] + cw[1] * back2[8:GDN_WIN] + odd[8:GDN_WIN])

    def epilogue(h):
        cs = slice(h * GDN_DK, (h + 1) * GDN_DK)
        for j in range(GDN_GROUPS):
            rs = slice(j * CHUNK, (j + 1) * CHUNK)
            q = conv_silu(j, h * GDN_DK)
            k = conv_silu(j, GDN_QK + h * GDN_DK)
            v = conv_silu(j, 2 * GDN_QK + h * GDN_DV)
            q = q * lax.rsqrt(jnp.sum(q * q, axis=-1, keepdims=True) + EPS) * (GDN_DK ** -0.5)
            k = k * lax.rsqrt(jnp.sum(k * k, axis=-1, keepdims=True) + EPS)
            beta_h = beta[j][:, h:h + 1]
            e_h = egc[j][:, 8 + h:9 + h]
            kb = k * beta_h
            col = lambda part, c0=h * GDN_DK: slice(part * D_MODEL + c0, part * D_MODEL + c0 + GDN_DK)
            ops_ref[rs, col(GDN_OP_K)] = k.astype(BF16)
            ops_ref[rs, col(GDN_OP_QE)] = (q * e_h).astype(BF16)
            ops_ref[rs, col(GDN_OP_KD)] = (k * e_rest[j][:, 8 + h:9 + h]).astype(BF16)
            vk0 = GDN_OP_VK * D_MODEL + 2 * h * GDN_DV
            ops_ref[rs, vk0:vk0 + GDN_DV] = (v * beta_h).astype(BF16)
            ops_ref[rs, vk0 + GDN_DV:vk0 + 2 * GDN_DV] = (kb * e_h).astype(BF16)
            kbq_ref[j, 0:CHUNK, cs] = kb.astype(BF16)
            kbq_ref[j, CHUNK:2 * CHUNK, cs] = q.astype(BF16)

    pair = 2 * GDN_DK
    n_pairs = GDN_HEADS // 2
    for m in range(n_pairs + 1):
        if m < n_pairs:
            for part in range(3):
                project(part * GDN_QK + m * pair, pair)
        if m > 0:
            epilogue(2 * m - 2)
            epilogue(2 * m - 1)
    for c in range(2):
        z = jnp.dot(xn, w_ref[:, GDN_Z0 + c * FFN_CK:GDN_Z0 + (c + 1) * FFN_CK],
                    preferred_element_type=F32)
        ops_ref[:, GDN_OP_GATE * D_MODEL + c * FFN_CK:GDN_OP_GATE * D_MODEL + (c + 1) * FFN_CK] = (
            _silu(z).astype(BF16))
    for j in range(GDN_GROUPS):
        tails_ref[j] = xc_scr[(j + 1) * GDN_WIN - 8:(j + 1) * GDN_WIN, :]
    carry_scr[...] = xc_scr[GDN_GROUPS * GDN_WIN - 8:GDN_GROUPS * GDN_WIN, :]


def _gdn_in(x, nw, w, cw, alog, dtb, halo0, *, t, n_pt, tps, n_p):
    in_specs, args = _row_tile_specs(x, n_pt)
    n_x = len(args)
    n_seq = halo0.shape[0]
    in_specs += [_resident((1, D_MODEL)), _resident((D_MODEL, GDN_P)),
                 _resident((GDN_CONV, GDN_CONV_CH)), _resident((1, LANES)), _resident((1, LANES)),
                 _resident((n_seq, 8, GDN_CONV_CH))]
    args += [nw, w, cw, alog, dtb, halo0]
    tile = lambda c: pl.BlockSpec((ROW_TILE, c), lambda i: (i, 0))
    n_chunks = t // CHUNK
    bf = lambda c: jax.ShapeDtypeStruct((t, c), BF16)
    return pl.pallas_call(
        functools.partial(_gdn_in_body, n_x=n_x, n_pt=n_pt, tps=tps, n_p=n_p),
        grid=(t // ROW_TILE,),
        in_specs=in_specs,
        out_specs=[tile(GDN_OPS * D_MODEL),
                   pl.BlockSpec((GDN_GROUPS, 2 * CHUNK, D_MODEL), lambda i: (i, 0, 0)),
                   tile(LANES),
                   pl.BlockSpec((GDN_GROUPS, 8, GDN_CONV_CH), lambda i: (i, 0, 0))],
        out_shape=[bf(GDN_OPS * D_MODEL), jax.ShapeDtypeStruct((n_chunks, 2 * CHUNK, D_MODEL), BF16),
                   jax.ShapeDtypeStruct((t, LANES), F32),
                   jax.ShapeDtypeStruct((n_chunks, 8, GDN_CONV_CH), F32)],
        scratch_shapes=[pltpu.VMEM((GDN_GROUPS * GDN_WIN, GDN_CONV_CH), F32),
                        pltpu.VMEM((8, GDN_CONV_CH), F32)],
        compiler_params=_params(), name="gdn_in")(*args)


def _gdn_pre_body(k_ref, kbq_ref, vk_ref, qe_ref, bg_ref, alog_ref, dtb_ref,
                  u_ref, wq_ref, attn_ref, egl_ref):
    pw = 2 * CHUNK
    row = lax.broadcasted_iota(jnp.int32, (CHUNK, pw), 0)
    lane = lax.broadcasted_iota(jnp.int32, (CHUNK, pw), 1)
    col = lane & (CHUNK - 1)
    left = lane < CHUNK
    incl = row >= col
    strict = row > col
    eye = (row == col).astype(F32)

    def level_mask(s):
        sh = int(math.log2(s))
        bi, bj = row >> sh, col >> sh
        return jnp.logical_and((bi & 1) == 1, bj == bi - 1)

    masks = [level_mask(s) for s in (1, 2, 4, 8, 16, 32)]
    r2 = lax.broadcasted_iota(jnp.int32, (pw, pw), 0)
    l2 = lax.broadcasted_iota(jnp.int32, (pw, pw), 1)
    same_head = (r2 < CHUNK) == (l2 < CHUNK)

    def block_diag(x):
        return jnp.where(same_head, jnp.concatenate([x, x], axis=0), jnp.zeros((pw, pw), x.dtype))

    probs = [(j, m) for j in range(GDN_NB) for m in range(GDN_HEADS // 2)]
    n = len(probs)
    rs = [slice(j * CHUNK, (j + 1) * CHUNK) for j, _ in probs]
    hc = lambda h: slice(h * GDN_DK, (h + 1) * GDN_DK)
    gcum, gcum_t = [], []
    for j in range(GDN_NB):
        _, gc = _gdn_gates(bg_ref[j * CHUNK:(j + 1) * CHUNK, :], alog_ref, dtb_ref)
        gcum.append(gc)
        gt = jnp.transpose(jnp.concatenate([gc, jnp.zeros_like(gc)], axis=0))
        gcum_t.append(gt)
        egl_ref[j] = jnp.broadcast_to(jnp.exp(gt[8:8 + GDN_HEADS, CHUNK - 1:CHUNK]),
                                      (GDN_HEADS, LANES))
    decay = []
    for j, m in probs:
        h0, h1 = 8 + 2 * m, 9 + 2 * m
        g_col = jnp.where(left, gcum[j][:, h0:h0 + 1], gcum[j][:, h1:h1 + 1])
        g_row = jnp.concatenate([gcum_t[j][h0:h0 + 1, 0:CHUNK], gcum_t[j][h1:h1 + 1, 0:CHUNK]], axis=1)
        decay.append(jnp.exp(jnp.minimum(g_col - g_row, 0.0)))
    qk = []
    for i, (j, m) in enumerate(probs):
        k_pair = jnp.concatenate([k_ref[rs[i], hc(2 * m)], k_ref[rs[i], hc(2 * m + 1)]], axis=0)
        r0, r1 = (lax.dot_general(kbq_ref[j, :, hc(h)], k_pair, (((1,), (1,)), ((), ())),
                                  preferred_element_type=F32) for h in (2 * m, 2 * m + 1))
        qk.append(jnp.where(l2 < CHUNK, r0, r1))
    a = [jnp.where(strict, qk[i][0:CHUNK] * decay[i], 0.0) for i in range(n)]
    for i, (j, m) in enumerate(probs):
        attn = jnp.where(incl, qk[i][CHUNK:2 * CHUNK] * decay[i], 0.0)
        attn_ref[rs[i], m * pw:(m + 1) * pw] = attn.astype(attn_ref.dtype)
    t = [eye - jnp.where(masks[0], ai, 0.0) for ai in a]
    a_b = [ai.astype(BF16) for ai in a]
    for lm in masks[1:]:
        t_b = [ti.astype(BF16) for ti in t]
        at = [jnp.dot(jnp.where(lm, a_b[i], jnp.zeros_like(a_b[i])), block_diag(t_b[i]),
                      preferred_element_type=F32) for i in range(n)]
        t = [t[i] - jnp.dot(t_b[i], block_diag(at[i].astype(BF16)), preferred_element_type=F32)
             for i in range(n)]
    zeros = jnp.zeros((CHUNK, 2 * GDN_DV), BF16)
    for i, (j, m) in enumerate(probs):
        v0 = vk_ref[rs[i], 4 * m * GDN_DV:(4 * m + 2) * GDN_DV]
        v1 = vk_ref[rs[i], (4 * m + 2) * GDN_DV:(4 * m + 4) * GDN_DV]
        rhs = jnp.concatenate([jnp.concatenate([v0, zeros], axis=1),
                               jnp.concatenate([zeros, v1], axis=1)], axis=0)
        uw = jnp.dot(t[i].astype(BF16), rhs, preferred_element_type=F32)
        for q, h in enumerate((2 * m, 2 * m + 1)):
            u_ref[rs[i], hc(h)] = uw[:, 2 * q * GDN_DV:(2 * q + 1) * GDN_DV]
            wq_ref[j, 0:CHUNK, hc(h)] = uw[:, (2 * q + 1) * GDN_DV:(2 * q + 2) * GDN_DV].astype(wq_ref.dtype)
            wq_ref[j, CHUNK:2 * CHUNK, hc(h)] = qe_ref[rs[i], hc(h)]


def _gdn_pre(ops, kbq, bgp, alog, dtb):
    t = ops.shape[0]
    rows = GDN_NB * CHUNK
    n_chunks = t // CHUNK
    tile = lambda c: pl.BlockSpec((rows, c), lambda b: (b, 0))
    chunked = lambda r, c: pl.BlockSpec((GDN_NB, r, c), lambda b: (b, 0, 0))
    op_block = lambda part, width: pl.BlockSpec((rows, width * D_MODEL), lambda b: (b, part // width))
    return pl.pallas_call(
        _gdn_pre_body,
        grid=(t // rows,),
        in_specs=[op_block(GDN_OP_K, 1), chunked(2 * CHUNK, D_MODEL), op_block(GDN_OP_VK, 2),
                  op_block(GDN_OP_QE, 1), tile(LANES), _resident((1, LANES)), _resident((1, LANES))],
        out_specs=[tile(D_MODEL), chunked(2 * CHUNK, D_MODEL), tile(GDN_HEADS * CHUNK),
                   chunked(GDN_HEADS, LANES)],
        out_shape=[jax.ShapeDtypeStruct((t, D_MODEL), F32),
                   jax.ShapeDtypeStruct((n_chunks, 2 * CHUNK, D_MODEL), BF16),
                   jax.ShapeDtypeStruct((t, GDN_HEADS * CHUNK), BF16),
                   jax.ShapeDtypeStruct((n_chunks, GDN_HEADS, LANES), F32)],
        compiler_params=_params(), name="gdn_pre")(ops, kbq, ops, ops, bgp, alog, dtb)


def _gdn_scan_body(u_ref, wq_ref, kd_ref, attn_ref, egl_ref, gate_ref, s0_ref, nw_ref,
                   o_ref, soutp_ref, souts_ref, s_scr, *, n_pb, bps):
    blk, is_prompt, first, last = _block_flags(n_pb, bps)

    @pl.when(first)
    def _():
        s_scr[...] = jnp.zeros_like(s_scr)

    heads = range(GDN_HEADS)
    cs = [slice(h * GDN_DK, (h + 1) * GDN_DK) for h in heads]
    s = [s_scr[h] for h in heads]
    for j in range(SCAN_NB):
        rs = slice(j * CHUNK, (j + 1) * CHUNK)
        s = [jnp.where(is_prompt, s[h], s0_ref[j, h]) for h in heads]
        ws_qs = [jnp.dot(wq_ref[j, :, cs[h]], s[h].astype(BF16), preferred_element_type=F32)
                 for h in heads]
        v_new = [(u_ref[rs, cs[h]] - ws_qs[h][0:CHUNK]).astype(BF16) for h in heads]
        o = [ws_qs[h][CHUNK:2 * CHUNK]
             + jnp.dot(attn_ref[rs, h * CHUNK:(h + 1) * CHUNK], v_new[h], preferred_element_type=F32)
             for h in heads]
        s = [s[h] * egl_ref[j, h:h + 1, :] + _mm_tn(kd_ref[rs, cs[h]], v_new[h]) for h in heads]
        for h in heads:
            o_ref[rs, cs[h]] = (_rms(o[h], nw_ref[...]) * gate_ref[rs, cs[h]]).astype(o_ref.dtype)

        @pl.when(jnp.logical_not(is_prompt))
        def _(j=j, s=s):
            for h in heads:
                souts_ref[j, h] = s[h]

    for h in heads:
        s_scr[h] = s[h]

    @pl.when(last)
    def _():
        for h in heads:
            soutp_ref[0, h] = s[h]


def _gdn_scan(u, wq, ops, attn, egl, s0, nw, *, n_p, cpl, n_s):
    t = u.shape[0]
    rows = SCAN_NB * CHUNK
    bps = cpl // SCAN_NB
    n_pb = n_p * bps
    row_blk = lambda b: (b, 0)
    s_in, s_out, s_shapes = _state_specs(n_p, n_s, n_pb, bps, SCAN_NB, (GDN_HEADS, GDN_DK, GDN_DV))
    return pl.pallas_call(
        functools.partial(_gdn_scan_body, n_pb=n_pb, bps=bps),
        grid=(t // rows,),
        in_specs=[pl.BlockSpec((rows, D_MODEL), row_blk),
                  pl.BlockSpec((SCAN_NB, 2 * CHUNK, D_MODEL), lambda b: (b, 0, 0)),
                  pl.BlockSpec((rows, D_MODEL), lambda b: (b, GDN_OP_KD)),
                  pl.BlockSpec((rows, GDN_HEADS * CHUNK), row_blk),
                  pl.BlockSpec((SCAN_NB, GDN_HEADS, LANES), lambda b: (b, 0, 0)),
                  pl.BlockSpec((rows, D_MODEL), lambda b: (b, GDN_OP_GATE)),
                  s_in, _resident((1, GDN_DV))],
        out_specs=[pl.BlockSpec((rows, D_MODEL), row_blk)] + s_out,
        out_shape=[jax.ShapeDtypeStruct((t, D_MODEL), BF16)] + s_shapes,
        scratch_shapes=[pltpu.VMEM((GDN_HEADS, GDN_DK, GDN_DV), F32)],
        compiler_params=_params(), name="gdn_scan")(u, wq, ops, attn, egl, ops, s0, nw)


GLA_NLV = len(GLA_LEVELS) + 1
GLA_SC = 2 * GLA_QK
GLA_OP_QE, GLA_OP_KD, GLA_OP_V, GLA_OP_GATE = 0, GLA_QK, 2 * GLA_QK, 2 * GLA_QK + D_MODEL
GLA_OPS = GLA_OP_GATE + D_MODEL
GLA_NB = 8


def _gla_exponent_matrix():
    i = np.arange(CHUNK)[:, None]
    t = np.arange(CHUNK)[None, :]
    blocks = [(t <= i)]
    for s in GLA_LEVELS:
        c = (i // (2 * s)) * (2 * s) + s - 1
        blocks.append(np.where(i > c, (t > c) & (t <= i), (t > i) & (t <= c)))
    w = np.concatenate(blocks, axis=0).astype(np.float32)
    return np.concatenate([w, w, w], axis=1)


def _split3(x):
    hi = x.astype(BF16)
    r1 = x - hi.astype(F32)
    mid = r1.astype(BF16)
    lo = (r1 - mid.astype(F32)).astype(BF16)
    return hi, mid, lo


def _gla_in_body(*refs, n_x, n_pt):
    x_refs = refs[:n_x]
    nw_ref, w_ref, wg2_ref, bg_ref, wexp_ref, sc_ref, ops_ref, ebl_ref = refs[n_x:]
    xn = _rms(_read_row_tile(x_refs, n_pt), nw_ref[...]).astype(BF16)
    chunks = range(ROW_TILE // CHUNK)
    rs = [slice(j * CHUNK, (j + 1) * CHUNK) for j in chunks]

    g_low = jnp.dot(xn, w_ref[:, GLA_G0:GLA_G0 + LANES], preferred_element_type=F32)
    q_all = jnp.dot(xn, w_ref[:, 0:GLA_QK], preferred_element_type=F32) * (GLA_DK ** -0.5)
    k_all = jnp.dot(xn, w_ref[:, GLA_QK:2 * GLA_QK], preferred_element_type=F32)
    sc_ref[:, 0:GLA_QK] = q_all.astype(BF16)
    sc_ref[:, GLA_QK:GLA_SC] = k_all.astype(BF16)
    x = [_mm(g_low[rs[j]], wg2_ref[...]) + bg_ref[...] for j in chunks]
    glog = [-_softplus(-x[j]) * (1.0 / GLA_TAU) for j in chunks]
    e_all = [jnp.dot(wexp_ref[...], jnp.concatenate(_split3(glog[j]), axis=0),
                     preferred_element_type=F32) for j in chunks]
    for lvl in range(1, GLA_NLV):
        for j in chunks:
            f = jnp.exp(e_all[j][lvl * CHUNK:(lvl + 1) * CHUNK, :])
            sc_ref[rs[j], lvl * GLA_SC:lvl * GLA_SC + GLA_QK] = (q_all[rs[j]] * f).astype(BF16)
            sc_ref[rs[j], lvl * GLA_SC + GLA_QK:(lvl + 1) * GLA_SC] = (k_all[rs[j]] * f).astype(BF16)
    for j in chunks:
        b = e_all[j][0:CHUNK, :]
        b_last = b[CHUNK - 1:CHUNK, :]
        ops_ref[rs[j], GLA_OP_QE:GLA_OP_QE + GLA_QK] = (q_all[rs[j]] * jnp.exp(b)).astype(BF16)
        ops_ref[rs[j], GLA_OP_KD:GLA_OP_KD + GLA_QK] = (k_all[rs[j]] * jnp.exp(b_last - b)).astype(BF16)
        ebl_ref[j] = jnp.broadcast_to(jnp.exp(b_last), (8, GLA_QK))
    for c in range(D_MODEL // FFN_CK):
        cols = slice(c * FFN_CK, (c + 1) * FFN_CK)
        v = jnp.dot(xn, w_ref[:, GLA_V0 + c * FFN_CK:GLA_V0 + (c + 1) * FFN_CK],
                    preferred_element_type=F32)
        ops_ref[:, GLA_OP_V + c * FFN_CK:GLA_OP_V + (c + 1) * FFN_CK] = v.astype(BF16)
        r = jnp.dot(xn, w_ref[:, GLA_R0 + c * FFN_CK:GLA_R0 + (c + 1) * FFN_CK],
                    preferred_element_type=F32)
        ops_ref[:, GLA_OP_GATE + c * FFN_CK:GLA_OP_GATE + (c + 1) * FFN_CK] = _silu(r).astype(BF16)


def _gla_in(x, nw, w, wg2, bg, wexp, *, t, n_pt):
    in_specs, args = _row_tile_specs(x, n_pt)
    n_x = len(args)
    in_specs += [_resident((1, D_MODEL)), _resident((D_MODEL, GLA_P)), _resident((LANES, GLA_QK)),
                 _resident((1, GLA_QK)), _resident((GLA_NLV * CHUNK, 3 * CHUNK))]
    args += [nw, w, wg2, bg, wexp]
    tile = lambda c: pl.BlockSpec((ROW_TILE, c), lambda i: (i, 0))
    return pl.pallas_call(
        functools.partial(_gla_in_body, n_x=n_x, n_pt=n_pt),
        grid=(t // ROW_TILE,),
        in_specs=in_specs,
        out_specs=[tile(GLA_NLV * GLA_SC), tile(GLA_OPS),
                   pl.BlockSpec((ROW_TILE // CHUNK, 8, GLA_QK), lambda i: (i, 0, 0))],
        out_shape=[jax.ShapeDtypeStruct((t, GLA_NLV * GLA_SC), BF16),
                   jax.ShapeDtypeStruct((t, GLA_OPS), BF16),
                   jax.ShapeDtypeStruct((t // CHUNK, 8, GLA_QK), F32)],
        compiler_params=_params(), name="gla_in")(*args)


def _gla_body(sc_ref, ops_ref, ebl_ref, s0_ref, nw_ref,
              o_ref, soutp_ref, souts_ref, st_scr, s0t_scr, *, n_pb, bps):
    blk, is_prompt, first, last = _block_flags(n_pb, bps)
    heads = range(GLA_HEADS)
    chunks = range(GLA_NB)

    @pl.when(first)
    def _():
        st_scr[...] = jnp.zeros_like(st_scr)

    @pl.when(blk == 0)
    def _():
        s0t_scr[...] = jnp.zeros_like(s0t_scr)

    @pl.when(jnp.logical_not(is_prompt))
    def _():
        for j in chunks:
            for h in heads:
                s0t_scr[j, h] = s0_ref[j, h].T

    row = lax.broadcasted_iota(jnp.int32, (CHUNK, CHUNK), 0)
    col = lax.broadcasted_iota(jnp.int32, (CHUNK, CHUNK), 1)
    masks = [row == col] + [_pair_mask(s) for s in GLA_LEVELS]
    rs = [slice(j * CHUNK, (j + 1) * CHUNK) for j in chunks]
    cv = [slice(h * GLA_DV, (h + 1) * GLA_DV) for h in heads]

    def op(j, base, h, width):
        return ops_ref[rs[j], base + h * width:base + (h + 1) * width]

    def scores(j, h, lvl):
        q0 = lvl * GLA_SC + h * GLA_DK
        k0 = q0 + GLA_QK
        return lax.dot_general(sc_ref[rs[j], q0:q0 + GLA_DK], sc_ref[rs[j], k0:k0 + GLA_DK],
                               (((1,), (1,)), ((), ())), preferred_element_type=F32)

    attn = [[jnp.where(masks[0], scores(j, h, 0), 0.0) for h in heads] for j in chunks]
    for lvl in range(1, GLA_NLV):
        attn = [[attn[j][h] + jnp.where(masks[lvl], scores(j, h, lvl), 0.0) for h in heads]
                for j in chunks]
    av = [[jnp.dot(attn[j][h].astype(BF16), op(j, GLA_OP_V, h, GLA_DV), preferred_element_type=F32)
           for h in heads] for j in chunks]
    kv = [[lax.dot_general(op(j, GLA_OP_V, h, GLA_DV), op(j, GLA_OP_KD, h, GLA_DK),
                           (((0,), (0,)), ((), ())), preferred_element_type=F32)
           for h in heads] for j in chunks]

    st = [st_scr[h] for h in heads]
    for j in chunks:
        st = [jnp.where(is_prompt, st[h], s0t_scr[j, h]) for h in heads]
        o = [_mm_nt(op(j, GLA_OP_QE, h, GLA_DK), st[h]) + av[j][h] for h in heads]
        st = [st[h] * ebl_ref[j, 0:1, h * GLA_DK:(h + 1) * GLA_DK] + kv[j][h] for h in heads]
        for h in heads:
            gate = op(j, GLA_OP_GATE, h, GLA_DV)
            o_ref[rs[j], cv[h]] = (_rms(o[h], nw_ref[h:h + 1, :]) * gate).astype(o_ref.dtype)

        @pl.when(jnp.logical_not(is_prompt))
        def _(j=j, st=st):
            for h in heads:
                souts_ref[j, h] = st[h].T

    for h in heads:
        st_scr[h] = st[h]

    @pl.when(last)
    def _():
        for h in heads:
            soutp_ref[0, h] = st[h].T


def _gla_scan(sc, ops, ebl, s0, nw, *, n_p, cpl, n_s):
    t = sc.shape[0]
    rows = GLA_NB * CHUNK
    bps = cpl // GLA_NB
    n_pb = n_p * bps
    s_in, s_out, s_shapes = _state_specs(n_p, n_s, n_pb, bps, GLA_NB, (GLA_HEADS, GLA_DK, GLA_DV))
    return pl.pallas_call(
        functools.partial(_gla_body, n_pb=n_pb, bps=bps),
        grid=(t // rows,),
        in_specs=[pl.BlockSpec((rows, GLA_NLV * GLA_SC), lambda b: (b, 0)),
                  pl.BlockSpec((rows, GLA_OPS), lambda b: (b, 0)),
                  pl.BlockSpec((GLA_NB, 8, GLA_QK), lambda b: (b, 0, 0)),
                  s_in, _resident((GLA_HEADS, GLA_DV))],
        out_specs=[pl.BlockSpec((rows, D_MODEL), lambda b: (b, 0))] + s_out,
        out_shape=[jax.ShapeDtypeStruct((t, D_MODEL), BF16)] + s_shapes,
        scratch_shapes=[pltpu.VMEM((GLA_HEADS, GLA_DV, GLA_DK), F32),
                        pltpu.VMEM((GLA_NB, GLA_HEADS, GLA_DV, GLA_DK), F32)],
        compiler_params=_params(), name="gla_scan")(sc, ops, ebl, s0, nw)


CNV_WIN = CONV_HALO + CHUNK
TILE_CHUNKS = ROW_TILE // CHUNK
CNV_CK = 256


def _cnv_in_body(*refs, n_x, n_pt, tps, n_p):
    x_refs = refs[:n_x]
    (nw_ref, w_ref, b_ref, wdw_ref, bdw_ref, lnw_ref, lnb_ref, halo0_ref,
     a_ref, tails_ref, xc_scr, y_scr, carry_scr) = refs[n_x:]
    i = pl.program_id(0)
    is_sample = i >= n_pt

    @pl.when(i == 0)
    def _():
        carry_scr[...] = jnp.zeros_like(carry_scr)

    xn = _rms(_read_row_tile(x_refs, n_pt), nw_ref[...]).astype(BF16)
    taps = {}
    for w in range(CONV_W):
        s = CONV_HALO - (CONV_W - 1) + w
        taps.setdefault(-s % 8, []).append((w, s + (-s % 8)))

    def glu_slab(c0):
        cols = slice(c0, c0 + CNV_CK)
        gcols = slice(D_MODEL + c0, D_MODEL + c0 + CNV_CK)
        val = jnp.dot(xn, w_ref[:, cols], preferred_element_type=F32) + b_ref[:, cols]
        gate = jnp.dot(xn, w_ref[:, gcols], preferred_element_type=F32) + b_ref[:, gcols]
        glu = val * _sigmoid(gate)
        for j in range(TILE_CHUNKS):
            r0 = j * CNV_WIN
            xc_scr[r0 + CONV_HALO:r0 + CNV_WIN, cols] = glu[j * CHUNK:(j + 1) * CHUNK]
            if j == 0:
                first = jnp.logical_or(is_sample, i % tps == 0)
                prev = carry_scr[:, cols]
            else:
                first = is_sample
                prev = glu[j * CHUNK - CONV_HALO:j * CHUNK]
            seq = jnp.where(is_sample, n_p + (i - n_pt) * TILE_CHUNKS + j, i // tps)
            xc_scr[r0:r0 + CONV_HALO, cols] = jnp.where(first, halo0_ref[seq, :, cols], prev)

    def conv_slab(c0):
        for j in range(TILE_CHUNKS):
            for b0 in range(c0, c0 + CNV_CK, LANES):
                cs = slice(b0, b0 + LANES)
                wv = xc_scr[j * CNV_WIN:(j + 1) * CNV_WIN, cs]
                y = None
                for r, group in sorted(taps.items()):
                    rolled = wv if r == 0 else pltpu.roll(wv, r, 0)
                    for w, a0 in group:
                        term = wdw_ref[w:w + 1, cs] * rolled[a0:a0 + CHUNK]
                        y = term if y is None else y + term
                y_scr[j * CHUNK:(j + 1) * CHUNK, cs] = y + bdw_ref[:, cs]

    slabs = list(range(0, D_MODEL, CNV_CK))
    for n, c0 in enumerate(slabs + [None]):
        if c0 is not None:
            glu_slab(c0)
        if n > 0:
            conv_slab(slabs[n - 1])
    for j in range(TILE_CHUNKS):
        rs = slice(j * CHUNK, (j + 1) * CHUNK)
        a_ref[rs, :] = _silu(_layernorm(y_scr[rs, :], lnw_ref[...], lnb_ref[...])).astype(BF16)
        tails_ref[j] = xc_scr[(j + 1) * CNV_WIN - CONV_HALO:(j + 1) * CNV_WIN, :]
    carry_scr[...] = xc_scr[TILE_CHUNKS * CNV_WIN - CONV_HALO:TILE_CHUNKS * CNV_WIN, :]


def _cnv_in(x, nw, w, b, wdw, bdw, lnw, lnb, halo0, *, t, n_pt, tps, n_p):
    in_specs, args = _row_tile_specs(x, n_pt)
    n_x = len(args)
    in_specs += [_resident((1, D_MODEL)), _resident((D_MODEL, 2 * D_MODEL)), _resident((1, 2 * D_MODEL)),
                 _resident((CONV_HALO, D_MODEL)), _resident((1, D_MODEL)),
                 _resident((1, D_MODEL)), _resident((1, D_MODEL)), _resident(halo0.shape)]
    args += [nw, w, b, wdw, bdw, lnw, lnb, halo0]
    return pl.pallas_call(
        functools.partial(_cnv_in_body, n_x=n_x, n_pt=n_pt, tps=tps, n_p=n_p),
        grid=(t // ROW_TILE,),
        in_specs=in_specs,
        out_specs=[pl.BlockSpec((ROW_TILE, D_MODEL), lambda i: (i, 0)),
                   pl.BlockSpec((TILE_CHUNKS, CONV_HALO, D_MODEL), lambda i: (i, 0, 0))],
        out_shape=[jax.ShapeDtypeStruct((t, D_MODEL), BF16),
                   jax.ShapeDtypeStruct((t // CHUNK, CONV_HALO, D_MODEL), F32)],
        scratch_shapes=[pltpu.VMEM((TILE_CHUNKS * CNV_WIN, D_MODEL), F32),
                        pltpu.VMEM((ROW_TILE, D_MODEL), F32),
                        pltpu.VMEM((CONV_HALO, D_MODEL), F32)],
        compiler_params=_params(), name="cnv_in")(*args)


def _pad_cols(w, n):
    return jnp.pad(w, ((0, 0), (0, n - w.shape[1])))


def _row(v):
    return v.reshape(1, -1).astype(F32)


def _tail_rows(a, lay, n):
    n_p, cpl, n_s = lay
    ends = np.concatenate([(np.arange(n_p) + 1) * cpl * CHUNK,
                           n_p * cpl * CHUNK + (np.arange(n_s) + 1) * CHUNK])
    idx = (ends[:, None] - n + np.arange(n)[None, :]).reshape(-1)
    return jnp.take(a, jnp.asarray(idx, jnp.int32), axis=0).reshape(n_p + n_s, n, a.shape[1])


def _gdn_mix(x, lay, dims, mix_nw, cache, state, w_in, conv_w, a_log, dt_bias, norm_w):
    n_p, cpl, n_s = lay
    halo0 = jnp.concatenate([jnp.zeros((n_p, 8, GDN_CONV_CH), F32),
                             jnp.pad(cache, ((0, 0), (8 - (GDN_CONV - 1), 0), (0, 0)))], axis=0)
    lane_pad = lambda v: jnp.pad(v.reshape(1, -1), ((0, 0), (GDN_HEADS, LANES - 2 * GDN_HEADS)))
    alog, dtb = lane_pad(a_log), lane_pad(dt_bias)
    ops, kbq, bgp, tails = _gdn_in(
        x, _row(mix_nw), _pad_cols(w_in, GDN_P).astype(BF16), conv_w, alog, dtb, halo0,
        tps=cpl * CHUNK // ROW_TILE, n_p=n_p, **dims)
    u, wq, attn, egl = _gdn_pre(ops, kbq, bgp, alog, dtb)
    o, s_p, s_s = _gdn_scan(u, wq, ops, attn, egl, state, _row(norm_w), n_p=n_p, cpl=cpl, n_s=n_s)
    ends = np.concatenate([(np.arange(n_p) + 1) * cpl - 1, n_p * cpl + np.arange(n_s)])
    conv_rows = tails[jnp.asarray(ends, jnp.int32), 8 - (GDN_CONV - 1):, :]
    return o, (conv_rows[:n_p], conv_rows[n_p:], s_p, s_s)


def _gla_mix(x, lay, dims, mix_nw, state, w_in, w_g2, b_g, norm_w):
    n_p, cpl, n_s = lay
    wg2 = jnp.pad(w_g2, ((0, LANES - GLA_RANK), (0, 0))).astype(BF16)
    wexp = jnp.asarray(_gla_exponent_matrix(), BF16)
    sc, ops, ebl = _gla_in(x, _row(mix_nw), _pad_cols(w_in, GLA_P).astype(BF16), wg2, _row(b_g), wexp,
                           **dims)
    o, s_p, s_s = _gla_scan(sc, ops, ebl, state, norm_w, n_p=n_p, cpl=cpl, n_s=n_s)
    return o, (s_p, s_s)


def _cnv_mix(x, lay, dims, mix_nw, cache, w_pw1, b_pw1, w_dw, b_dw, ln_w, ln_b):
    n_p, cpl, n_s = lay
    pad_rows = CONV_HALO - (CONV_W - 1)
    halo0 = jnp.concatenate([jnp.zeros((n_p, CONV_HALO, D_MODEL), F32),
                             jnp.pad(cache, ((0, 0), (pad_rows, 0), (0, 0)))], axis=0)
    wdw = jnp.pad(w_dw, ((0, CONV_HALO - CONV_W), (0, 0)))
    a, tails = _cnv_in(x, _row(mix_nw), w_pw1.astype(BF16), _row(b_pw1), wdw, _row(b_dw),
                       _row(ln_w), _row(ln_b), halo0, tps=cpl * CHUNK // ROW_TILE, n_p=n_p, **dims)
    ends = np.concatenate([(np.arange(n_p) + 1) * cpl - 1, n_p * cpl + np.arange(n_s)])
    rows = tails[jnp.asarray(ends, jnp.int32), pad_rows:, :]
    return a, (rows[:n_p], rows[n_p:])


def kernel(x_prompt, x_sample, cache_gdn_conv, state_gdn, state_gla, cache_conformer, mix_norm_w, ffn_norm_w, ffn_w1, ffn_w2, final_norm_w, gdn_w_in, gdn_conv_w, gdn_a_log, gdn_dt_bias, gdn_norm_w, gdn_w_out, gla_w_in, gla_w_g2, gla_b_g, gla_norm_w, gla_w_out, gmlp_w_in, gmlp_b_in, gmlp_ln_w, gmlp_ln_b, gmlp_w_s, gmlp_b_s, gmlp_w_out, gmlp_b_out, cnv_w_pw1, cnv_b_pw1, cnv_w_dw, cnv_b_dw, cnv_ln_w, cnv_ln_b, cnv_w_pw2, cnv_b_pw2):
    n_p, l_p, d = x_prompt.shape
    n_s, l_s, _ = x_sample.shape
    assert d == D_MODEL and l_s == CHUNK and l_p % ROW_TILE == 0
    assert n_s % max(SCAN_NB, GDN_NB, GLA_NB) == 0
    t_p, t_s = n_p * l_p, n_s * l_s
    assert t_p % ROW_TILE == 0 and t_s % ROW_TILE == 0
    lay = (n_p, l_p // CHUNK, n_s)
    dims = dict(t=t_p + t_s, n_pt=t_p // ROW_TILE)
    depth = mix_norm_w.shape[0]

    x = (x_prompt.reshape(t_p, d), x_sample.reshape(t_s, d))
    outs = {k: [] for k in ("gdn_cp", "gdn_cs", "gdn_sp", "gdn_ss", "gla_sp", "gla_ss",
                            "gmlp_vs", "cnv_p", "cnv_s")}
    w1_all, w2_all = ffn_w1.astype(BF16), ffn_w2.astype(BF16)
    for i in range(depth):
        kind, j = i % 4, i // 4
        ffn = (_row(ffn_norm_w[i]), w1_all, w2_all, _row(final_norm_w) if i == depth - 1 else None)
        if kind == 0:
            o, (cp, cs, sp, ss) = _gdn_mix(x, lay, dims, mix_norm_w[i], cache_gdn_conv[j], state_gdn[j],
                                           gdn_w_in[j], gdn_conv_w[j], gdn_a_log[j], gdn_dt_bias[j],
                                           gdn_norm_w[j])
            outs["gdn_cp"].append(cp); outs["gdn_cs"].append(cs)
            outs["gdn_sp"].append(sp); outs["gdn_ss"].append(ss)
            res = _post([o], x, gdn_w_out[j].astype(BF16), None, *ffn,
                        prologue="plain", layer=i, name=f"post_{i}", **dims)
        elif kind == 1:
            o, (sp, ss) = _gla_mix(x, lay, dims, mix_norm_w[i], state_gla[j], gla_w_in[j], gla_w_g2[j],
                                   gla_b_g[j], gla_norm_w[j])
            outs["gla_sp"].append(sp); outs["gla_ss"].append(ss)
            res = _post([o], x, gla_w_out[j].astype(BF16), None, *ffn,
                        prologue="plain", layer=i, name=f"post_{i}", **dims)
        elif kind == 2:
            zz = _proj(x, _row(mix_norm_w[i]), gmlp_w_in[j].astype(BF16), _row(gmlp_b_in[j]),
                       n_out=2 * D_MODEL, ck=512, epilogue="gelu", name="gmlp_in", **dims)
            bs_full = jnp.repeat(gmlp_b_s[j].T, GMLP_GC, axis=1)
            res = _post([zz, _row(gmlp_ln_w[j]), _row(gmlp_ln_b[j]), gmlp_w_s[j], bs_full], x,
                        gmlp_w_out[j].astype(BF16), _row(gmlp_b_out[j]), *ffn,
                        prologue="gmlp", layer=i, name=f"post_{i}", **dims)
            outs["gmlp_vs"].append(res[-1].reshape(n_s, l_s, d))
            res = res[:-1]
        else:
            a, (cp, cs) = _cnv_mix(x, lay, dims, mix_norm_w[i], cache_conformer[j], cnv_w_pw1[j],
                                   cnv_b_pw1[j], cnv_w_dw[j], cnv_b_dw[j], cnv_ln_w[j], cnv_ln_b[j])
            outs["cnv_p"].append(cp); outs["cnv_s"].append(cs)
            res = _post([a], x, cnv_w_pw2[j].astype(BF16), _row(cnv_b_pw2[j]), *ffn,
                        prologue="plain", layer=i, name=f"post_{i}", **dims)
        x = tuple(res) if i == depth - 1 else res[0]
    y_p, y_s = x
    stack = lambda xs: xs[0][None] if len(xs) == 1 else jnp.stack(xs)
    return (y_p.reshape(n_p, l_p, d), y_s.reshape(n_s, l_s, d),
            stack(outs["gdn_cp"]), stack(outs["gdn_cs"]),
            stack(outs["gdn_sp"]), stack(outs["gdn_ss"]),
            stack(outs["gla_sp"]), stack(outs["gla_ss"]),
            stack(outs["gmlp_vs"]),
            stack(outs["cnv_p"]), stack(outs["cnv_s"]))
```

```python
import functools
import math

import numpy as np
import jax
import jax.numpy as jnp
from jax import lax
from jax.experimental import pallas as pl
from jax.experimental.pallas import tpu as pltpu

F32 = jnp.float32
BF16 = jnp.bfloat16

D_MODEL = 1024
D_FF = 4 * D_MODEL
EPS = 1e-6
CHUNK = 64
ROW_TILE = 512
SCAN_NB = 8
LANES = 128
VMEM_LIMIT = 56 * 1024 * 1024

GDN_HEADS, GDN_DK, GDN_DV, GDN_CONV = 8, 128, 128, 4
GDN_QK = GDN_HEADS * GDN_DK
GDN_CONV_CH = 2 * GDN_QK + GDN_HEADS * GDN_DV
GDN_Z0 = GDN_CONV_CH
GDN_BG0 = GDN_CONV_CH + GDN_HEADS * GDN_DV
GDN_P = GDN_BG0 + LANES

GLA_HEADS, GLA_DK, GLA_DV, GLA_RANK, GLA_TAU = 4, 128, 256, 16, 16.0
GLA_QK = GLA_HEADS * GLA_DK
GLA_V0, GLA_R0, GLA_G0 = 2 * GLA_QK, 2 * GLA_QK + D_MODEL, 2 * GLA_QK + 2 * D_MODEL
GLA_P = GLA_G0 + LANES
GLA_LEVELS = (32, 16, 8, 4, 2, 1)

GMLP_BLOCK, GMLP_GROUPS, GMLP_GC = 128, 4, 256
CONV_W = 31
CONV_HALO = 32


def _mm(a, b):
    return jnp.dot(a.astype(BF16), b.astype(BF16), preferred_element_type=F32)


def _mm_nt(a, b):
    return lax.dot_general(a.astype(BF16), b.astype(BF16), (((1,), (1,)), ((), ())),
                           preferred_element_type=F32)


def _mm_tn(a, b):
    return lax.dot_general(a.astype(BF16), b.astype(BF16), (((0,), (0,)), ((), ())),
                           preferred_element_type=F32)


def _rms(x, w):
    return x * lax.rsqrt(jnp.mean(x * x, axis=-1, keepdims=True) + EPS) * w


def _layernorm(x, w, b):
    xc = x - jnp.mean(x, axis=-1, keepdims=True)
    return xc * lax.rsqrt(jnp.mean(xc * xc, axis=-1, keepdims=True) + EPS) * w + b


def _sigmoid(x):
    return jax.nn.sigmoid(x)


def _silu(x):
    h = 0.5 * x
    return h + h * jnp.tanh(h)


def _softplus(x):
    return jnp.maximum(x, 0.0) + jnp.log1p(jnp.exp(-jnp.abs(x)))


def _gelu_tanh(x):
    return 0.5 * x * (1.0 + jnp.tanh(math.sqrt(2.0 / math.pi) * (x + 0.044715 * (x * x * x))))


def _cumsum_rows(x):
    row = lax.broadcasted_iota(jnp.int32, x.shape, 0)
    s = 1
    while s < x.shape[0]:
        x = x + jnp.where(row >= s, pltpu.roll(x, s, 0), 0.0)
        s *= 2
    return x


def _pair_mask(s):
    sh = int(math.log2(s))
    bi = lax.broadcasted_iota(jnp.int32, (CHUNK, CHUNK), 0) >> sh
    bj = lax.broadcasted_iota(jnp.int32, (CHUNK, CHUNK), 1) >> sh
    return jnp.logical_and((bi & 1) == 1, bj == bi - 1)


def _block_flags(n_pb, bps):
    blk = pl.program_id(0)
    is_prompt = blk < n_pb
    first = jnp.logical_and(is_prompt, blk % bps == 0)
    last = jnp.logical_and(is_prompt, blk % bps == bps - 1)
    return blk, is_prompt, first, last


def _state_specs(n_p, n_s, n_pb, bps, nb, shape):
    zeros = (0,) * len(shape)
    sample = pl.BlockSpec((nb,) + shape, lambda b: (jnp.maximum(b - n_pb, 0),) + zeros)
    prompt = pl.BlockSpec((1,) + shape, lambda b: (jnp.minimum(b // bps, n_p - 1),) + zeros)
    shapes = [jax.ShapeDtypeStruct((n_p,) + shape, F32), jax.ShapeDtypeStruct((n_s,) + shape, F32)]
    return sample, [prompt, sample], shapes


def _params(n_axes=1):
    return pltpu.CompilerParams(dimension_semantics=("arbitrary",) * n_axes,
                                vmem_limit_bytes=VMEM_LIMIT)


def _resident(shape):
    zeros = (0,) * len(shape)
    return pl.BlockSpec(shape, lambda i: zeros, pipeline_mode=pl.Buffered(1))


def _row_tile_specs(x, n_pt):
    if isinstance(x, tuple):
        return ([pl.BlockSpec((ROW_TILE, D_MODEL), lambda i: (jnp.minimum(i, n_pt - 1), 0)),
                 pl.BlockSpec((ROW_TILE, D_MODEL), lambda i: (jnp.maximum(i - n_pt, 0), 0))], list(x))
    return [pl.BlockSpec((ROW_TILE, D_MODEL), lambda i: (i, 0))], [x]


def _read_row_tile(refs, n_pt):
    if len(refs) == 2:
        return jnp.where(pl.program_id(0) < n_pt, refs[0][...], refs[1][...])
    return refs[0][...]


def _proj_body(*refs, n_x, n_pt, n_out, ck, epilogue, has_bias):
    x_refs, (nw_ref, w_ref), rest = refs[:n_x], refs[n_x:n_x + 2], refs[n_x + 2:]
    if has_bias:
        b_ref, o_ref = rest
    else:
        (o_ref,) = rest
    xn = _rms(_read_row_tile(x_refs, n_pt), nw_ref[...]).astype(BF16)
    for c0 in range(0, n_out, ck):
        sl = slice(c0, min(c0 + ck, n_out))
        y = jnp.dot(xn, w_ref[:, sl], preferred_element_type=F32)
        if has_bias:
            y = y + b_ref[:, sl]
        if epilogue == "gelu":
            y = _gelu_tanh(y)
        o_ref[:, sl] = y.astype(o_ref.dtype)


def _proj(x, nw, w, b, *, t, n_pt, n_out, ck, epilogue="none", out_dtype=F32, name):
    n_w = w.shape[1]
    has_bias = b is not None
    in_specs, args = _row_tile_specs(x, n_pt)
    n_x = len(args)
    in_specs += [_resident((1, D_MODEL)), _resident((D_MODEL, n_w))]
    args += [nw, w]
    if has_bias:
        in_specs.append(_resident((1, n_w)))
        args.append(b)
    return pl.pallas_call(
        functools.partial(_proj_body, n_x=n_x, n_pt=n_pt, n_out=n_out, ck=ck, epilogue=epilogue,
                          has_bias=has_bias),
        grid=(t // ROW_TILE,),
        in_specs=in_specs,
        out_specs=pl.BlockSpec((ROW_TILE, n_out), lambda i: (i, 0)),
        out_shape=jax.ShapeDtypeStruct((t, n_out), out_dtype),
        compiler_params=_params(), name=name)(*args)


FFN_CK = 512


def _gmlp_gate(zz_ref, lnw_ref, lnb_ref, ws_ref, bs_ref, vs_ref, a_scr, is_sample):
    v = _layernorm(zz_ref[:, D_MODEL:2 * D_MODEL], lnw_ref[...], lnb_ref[...])

    @pl.when(is_sample)
    def _():
        vs_ref[...] = v

    vb = v.astype(BF16)
    row = lax.broadcasted_iota(jnp.int32, (GMLP_BLOCK, GMLP_BLOCK), 0)
    col = lax.broadcasted_iota(jnp.int32, (GMLP_BLOCK, GMLP_BLOCK), 1)
    tril = col <= row
    top_left = jnp.logical_and(row < CHUNK, col < CHUNK)
    brow = lax.broadcasted_iota(jnp.int32, (GMLP_BLOCK, D_MODEL), 0)
    bias = bs_ref[...]
    bias = jnp.where(jnp.logical_and(is_sample, brow >= CHUNK), pltpu.roll(bias, CHUNK, 0), bias)
    for grp in range(GMLP_GROUPS):
        cs = slice(grp * GMLP_GC, (grp + 1) * GMLP_GC)
        wt = jnp.where(tril, ws_ref[grp], 0.0)
        w11 = jnp.where(top_left, wt, 0.0)
        wd = w11 + pltpu.roll(pltpu.roll(w11, CHUNK, 0), CHUNK, 1)
        we = jnp.where(is_sample, wd, wt).astype(BF16)
        for blk in range(ROW_TILE // GMLP_BLOCK):
            rs = slice(blk * GMLP_BLOCK, (blk + 1) * GMLP_BLOCK)
            mixed = jnp.dot(we, vb[rs, cs], preferred_element_type=F32)
            a_scr[rs, cs] = (zz_ref[rs, cs] * (mixed + bias[:, cs])).astype(BF16)
    return a_scr[...]


def _post_body(*refs, prologue, n_res, has_bias, final, n_pt):
    n_in = {"plain": 1, "gmlp": 5}[prologue]
    pro, refs = refs[:n_in], refs[n_in:]
    res_refs, refs = refs[:n_res], refs[n_res:]
    wo_ref, refs = refs[0], refs[1:]
    if has_bias:
        bo_ref, refs = refs[0], refs[1:]
    fnw_ref, w1_ref, w2_ref, refs = refs[0], refs[1], refs[2], refs[3:]
    if final:
        finw_ref, refs = refs[0], refs[1:]
    n_main = 2 if final else 1
    outs, refs = refs[:n_main], refs[n_main:]
    is_sample = pl.program_id(0) >= n_pt

    if prologue == "plain":
        a = pro[0][...]
    else:
        vs_ref, a_scr = refs
        a = _gmlp_gate(*pro, vs_ref, a_scr, is_sample)

    x = _read_row_tile(res_refs, n_pt) + jnp.dot(a, wo_ref[...], preferred_element_type=F32)
    if has_bias:
        x = x + bo_ref[...]
    xn = _rms(x, fnw_ref[...]).astype(BF16)
    for c in range(D_FF // FFN_CK):
        sl = slice(c * FFN_CK, (c + 1) * FFN_CK)
        h = jnp.dot(xn, w1_ref[:, sl], preferred_element_type=F32)
        h = jnp.square(jnp.maximum(h, 0.0)).astype(BF16)
        x = x + jnp.dot(h, w2_ref[sl, :], preferred_element_type=F32)
    if final:
        y = _rms(x, finw_ref[...])

        @pl.when(jnp.logical_not(is_sample))
        def _():
            outs[0][...] = y

        @pl.when(is_sample)
        def _():
            outs[1][...] = y
    else:
        outs[0][...] = x


def _post(pro_args, res, wo, bo, fnw, w1, w2, finw, *, layer, prologue, t, n_pt, name):
    has_bias = bo is not None
    final = finw is not None
    tile = lambda c: pl.BlockSpec((ROW_TILE, c), lambda i: (i, 0))
    if prologue == "plain":
        in_specs = [tile(D_MODEL)]
    else:
        in_specs = [tile(2 * D_MODEL), _resident((1, D_MODEL)), _resident((1, D_MODEL)),
                    _resident((GMLP_GROUPS, GMLP_BLOCK, GMLP_BLOCK)),
                    _resident((GMLP_BLOCK, D_MODEL))]
    args = list(pro_args)
    res_specs, res_args = _row_tile_specs(res, n_pt)
    in_specs += res_specs + [_resident((D_MODEL, D_MODEL))]
    args += res_args + [wo]
    if has_bias:
        in_specs.append(_resident((1, D_MODEL)))
        args.append(bo)
    layer_block = lambda shape: pl.BlockSpec((None,) + shape, lambda i: (layer, 0, 0),
                                             pipeline_mode=pl.Buffered(1))
    in_specs += [_resident((1, D_MODEL)), layer_block((D_MODEL, D_FF)), layer_block((D_FF, D_MODEL))]
    args += [fnw, w1, w2]
    if final:
        in_specs.append(_resident((1, D_MODEL)))
        args.append(finw)
        out_specs = [pl.BlockSpec((ROW_TILE, D_MODEL), lambda i: (jnp.minimum(i, n_pt - 1), 0)),
                     pl.BlockSpec((ROW_TILE, D_MODEL), lambda i: (jnp.maximum(i - n_pt, 0), 0))]
        out_shape = [jax.ShapeDtypeStruct((n_pt * ROW_TILE, D_MODEL), F32),
                     jax.ShapeDtypeStruct((t - n_pt * ROW_TILE, D_MODEL), F32)]
    else:
        out_specs = [tile(D_MODEL)]
        out_shape = [jax.ShapeDtypeStruct((t, D_MODEL), F32)]
    scratch = []
    if prologue == "gmlp":
        out_specs.append(pl.BlockSpec((ROW_TILE, D_MODEL), lambda i: (jnp.maximum(i - n_pt, 0), 0)))
        out_shape.append(jax.ShapeDtypeStruct((t - n_pt * ROW_TILE, D_MODEL), F32))
        scratch.append(pltpu.VMEM((ROW_TILE, D_MODEL), BF16))
    return pl.pallas_call(
        functools.partial(_post_body, prologue=prologue, n_res=len(res_args), has_bias=has_bias,
                          final=final, n_pt=n_pt),
        grid=(t // ROW_TILE,),
        in_specs=in_specs, out_specs=out_specs, out_shape=out_shape, scratch_shapes=scratch,
        compiler_params=_params(), name=name)(*args)


GDN_NB = 8
GDN_WIN = CHUNK + 8
GDN_GROUPS = ROW_TILE // CHUNK
GDN_OP_K, GDN_OP_QE, GDN_OP_KD, GDN_OP_GATE, GDN_OP_VK, GDN_OPS = 0, 1, 2, 3, 4, 6


def _gdn_gates(bg, alog_ref, dtb_ref):
    glog = -jnp.exp(alog_ref[...]) * _softplus(bg + dtb_ref[...])
    return _sigmoid(bg), _cumsum_rows(glog)


def _gdn_in_body(*refs, n_x, n_pt, tps, n_p):
    x_refs = refs[:n_x]
    (nw_ref, w_ref, cw_ref, alog_ref, dtb_ref, halo0_ref,
     ops_ref, kbq_ref, bg_ref, tails_ref, xc_scr, carry_scr) = refs[n_x:]
    i = pl.program_id(0)
    is_sample = i >= n_pt

    @pl.when(i == 0)
    def _():
        carry_scr[...] = jnp.zeros_like(carry_scr)

    xn = _rms(_read_row_tile(x_refs, n_pt), nw_ref[...]).astype(BF16)
    bg = jnp.dot(xn, w_ref[:, GDN_BG0:GDN_BG0 + LANES], preferred_element_type=F32)
    bg_ref[...] = bg
    beta, egc, e_rest = [], [], []
    for j in range(GDN_GROUPS):
        b_j, gc = _gdn_gates(bg[j * CHUNK:(j + 1) * CHUNK], alog_ref, dtb_ref)
        beta.append(b_j)
        egc.append(jnp.exp(gc))
        e_rest.append(jnp.exp(gc[CHUNK - 1:CHUNK, :] - gc))

    def project(c0, n):
        cols = slice(c0, c0 + n)
        y = jnp.dot(xn, w_ref[:, cols], preferred_element_type=F32)
        for j in range(GDN_GROUPS):
            r0 = j * GDN_WIN
            xc_scr[r0 + 8:r0 + GDN_WIN, cols] = y[j * CHUNK:(j + 1) * CHUNK]
            if j == 0:
                first = jnp.logical_or(is_sample, i % tps == 0)
                prev = carry_scr[:, cols]
            else:
                first = is_sample
                prev = y[j * CHUNK - 8:j * CHUNK]
            seq = jnp.where(is_sample, n_p + (i - n_pt) * GDN_GROUPS + j, i // tps)
            xc_scr[r0:r0 + 8, cols] = jnp.where(first, halo0_ref[seq, :, cols], prev)

    def conv_silu(j, c0):
        cw = [cw_ref[w:w + 1, c0:c0 + LANES] for w in range(GDN_CONV)]
        wv = xc_scr[j * GDN_WIN:(j + 1) * GDN_WIN, c0:c0 + LANES]
        back2 = pltpu.roll(wv, 2, 0)
        odd = pltpu.roll(cw[2] * wv + cw[0] * back2, 1, 0)
        return _silu(cw[3] * wv[8:GDN_WIN] + cw[1] * back2[8:GDN_WIN] + odd[8:GDN_WIN])

    def epilogue(h):
        cs = slice(h * GDN_DK, (h + 1) * GDN_DK)
        for j in range(GDN_GROUPS):
            rs = slice(j * CHUNK, (j + 1) * CHUNK)
            q = conv_silu(j, h * GDN_DK)
            k = conv_silu(j, GDN_QK + h * GDN_DK)
            v = conv_silu(j, 2 * GDN_QK + h * GDN_DV)
            q = q * lax.rsqrt(jnp.sum(q * q, axis=-1, keepdims=True) + EPS) * (GDN_DK ** -0.5)
            k = k * lax.rsqrt(jnp.sum(k * k, axis=-1, keepdims=True) + EPS)
            beta_h = beta[j][:, h:h + 1]
            e_h = egc[j][:, 8 + h:9 + h]
            kb = k * beta_h
            col = lambda part, c0=h * GDN_DK: slice(part * D_MODEL + c0, part * D_MODEL + c0 + GDN_DK)
            ops_ref[rs, col(GDN_OP_K)] = k.astype(BF16)
            ops_ref[rs, col(GDN_OP_QE)] = (q * e_h).astype(BF16)
            ops_ref[rs, col(GDN_OP_KD)] = (k * e_rest[j][:, 8 + h:9 + h]).astype(BF16)
            vk0 = GDN_OP_VK * D_MODEL + 2 * h * GDN_DV
            ops_ref[rs, vk0:vk0 + GDN_DV] = (v * beta_h).astype(BF16)
            ops_ref[rs, vk0 + GDN_DV:vk0 + 2 * GDN_DV] = (kb * e_h).astype(BF16)
            kbq_ref[j, 0:CHUNK, cs] = kb.astype(BF16)
            kbq_ref[j, CHUNK:2 * CHUNK, cs] = q.astype(BF16)

    pair = 2 * GDN_DK
    n_pairs = GDN_HEADS // 2
    for m in range(n_pairs + 1):
        if m < n_pairs:
            for part in range(3):
                project(part * GDN_QK + m * pair, pair)
        if m > 0:
            epilogue(2 * m - 2)
            epilogue(2 * m - 1)
    for c in range(2):
        z = jnp.dot(xn, w_ref[:, GDN_Z0 + c * FFN_CK:GDN_Z0 + (c + 1) * FFN_CK],
                    preferred_element_type=F32)
        ops_ref[:, GDN_OP_GATE * D_MODEL + c * FFN_CK:GDN_OP_GATE * D_MODEL + (c + 1) * FFN_CK] = (
            _silu(z).astype(BF16))
    for j in range(GDN_GROUPS):
        tails_ref[j] = xc_scr[(j + 1) * GDN_WIN - 8:(j + 1) * GDN_WIN, :]
    carry_scr[...] = xc_scr[GDN_GROUPS * GDN_WIN - 8:GDN_GROUPS * GDN_WIN, :]


def _gdn_in(x, nw, w, cw, alog, dtb, halo0, *, t, n_pt, tps, n_p):
    in_specs, args = _row_tile_specs(x, n_pt)
    n_x = len(args)
    n_seq = halo0.shape[0]
    in_specs += [_resident((1, D_MODEL)), _resident((D_MODEL, GDN_P)),
                 _resident((GDN_CONV, GDN_CONV_CH)), _resident((1, LANES)), _resident((1, LANES)),
                 _resident((n_seq, 8, GDN_CONV_CH))]
    args += [nw, w, cw, alog, dtb, halo0]
    tile = lambda c: pl.BlockSpec((ROW_TILE, c), lambda i: (i, 0))
    n_chunks = t // CHUNK
    bf = lambda c: jax.ShapeDtypeStruct((t, c), BF16)
    return pl.pallas_call(
        functools.partial(_gdn_in_body, n_x=n_x, n_pt=n_pt, tps=tps, n_p=n_p),
        grid=(t // ROW_TILE,),
        in_specs=in_specs,
        out_specs=[tile(GDN_OPS * D_MODEL),
                   pl.BlockSpec((GDN_GROUPS, 2 * CHUNK, D_MODEL), lambda i: (i, 0, 0)),
                   tile(LANES),
                   pl.BlockSpec((GDN_GROUPS, 8, GDN_CONV_CH), lambda i: (i, 0, 0))],
        out_shape=[bf(GDN_OPS * D_MODEL), jax.ShapeDtypeStruct((n_chunks, 2 * CHUNK, D_MODEL), BF16),
                   jax.ShapeDtypeStruct((t, LANES), F32),
                   jax.ShapeDtypeStruct((n_chunks, 8, GDN_CONV_CH), F32)],
        scratch_shapes=[pltpu.VMEM((GDN_GROUPS * GDN_WIN, GDN_CONV_CH), F32),
                        pltpu.VMEM((8, GDN_CONV_CH), F32)],
        compiler_params=_params(), name="gdn_in")(*args)


def _gdn_pre_body(k_ref, kbq_ref, vk_ref, qe_ref, bg_ref, alog_ref, dtb_ref,
                  u_ref, wq_ref, attn_ref, egl_ref):
    pw = 2 * CHUNK
    row = lax.broadcasted_iota(jnp.int32, (CHUNK, pw), 0)
    lane = lax.broadcasted_iota(jnp.int32, (CHUNK, pw), 1)
    col = lane & (CHUNK - 1)
    left = lane < CHUNK
    incl = row >= col
    strict = row > col
    eye = (row == col).astype(F32)

    def level_mask(s):
        sh = int(math.log2(s))
        bi, bj = row >> sh, col >> sh
        return jnp.logical_and((bi & 1) == 1, bj == bi - 1)

    masks = [level_mask(s) for s in (1, 2, 4, 8, 16, 32)]
    r2 = lax.broadcasted_iota(jnp.int32, (pw, pw), 0)
    l2 = lax.broadcasted_iota(jnp.int32, (pw, pw), 1)
    same_head = (r2 < CHUNK) == (l2 < CHUNK)

    def block_diag(x):
        return jnp.where(same_head, jnp.concatenate([x, x], axis=0), jnp.zeros((pw, pw), x.dtype))

    probs = [(j, m) for j in range(GDN_NB) for m in range(GDN_HEADS // 2)]
    n = len(probs)
    rs = [slice(j * CHUNK, (j + 1) * CHUNK) for j, _ in probs]
    hc = lambda h: slice(h * GDN_DK, (h + 1) * GDN_DK)
    gcum, gcum_t = [], []
    for j in range(GDN_NB):
        _, gc = _gdn_gates(bg_ref[j * CHUNK:(j + 1) * CHUNK, :], alog_ref, dtb_ref)
        gcum.append(gc)
        gt = jnp.transpose(jnp.concatenate([gc, jnp.zeros_like(gc)], axis=0))
        gcum_t.append(gt)
        egl_ref[j] = jnp.broadcast_to(jnp.exp(gt[8:8 + GDN_HEADS, CHUNK - 1:CHUNK]),
                                      (GDN_HEADS, LANES))
    decay = []
    for j, m in probs:
        h0, h1 = 8 + 2 * m, 9 + 2 * m
        g_col = jnp.where(left, gcum[j][:, h0:h0 + 1], gcum[j][:, h1:h1 + 1])
        g_row = jnp.concatenate([gcum_t[j][h0:h0 + 1, 0:CHUNK], gcum_t[j][h1:h1 + 1, 0:CHUNK]], axis=1)
        decay.append(jnp.exp(jnp.minimum(g_col - g_row, 0.0)))
    qk = []
    for i, (j, m) in enumerate(probs):
        k_pair = jnp.concatenate([k_ref[rs[i], hc(2 * m)], k_ref[rs[i], hc(2 * m + 1)]], axis=0)
        r0, r1 = (lax.dot_general(kbq_ref[j, :, hc(h)], k_pair, (((1,), (1,)), ((), ())),
                                  preferred_element_type=F32) for h in (2 * m, 2 * m + 1))
        qk.append(jnp.where(l2 < CHUNK, r0, r1))
    a = [jnp.where(strict, qk[i][0:CHUNK] * decay[i], 0.0) for i in range(n)]
    for i, (j, m) in enumerate(probs):
        attn = jnp.where(incl, qk[i][CHUNK:2 * CHUNK] * decay[i], 0.0)
        attn_ref[rs[i], m * pw:(m + 1) * pw] = attn.astype(attn_ref.dtype)
    t = [eye - jnp.where(masks[0], ai, 0.0) for ai in a]
    a_b = [ai.astype(BF16) for ai in a]
    for lm in masks[1:]:
        t_b = [ti.astype(BF16) for ti in t]
        at = [jnp.dot(jnp.where(lm, a_b[i], jnp.zeros_like(a_b[i])), block_diag(t_b[i]),
                      preferred_element_type=F32) for i in range(n)]
        t = [t[i] - jnp.dot(t_b[i], block_diag(at[i].astype(BF16)), preferred_element_type=F32)
             for i in range(n)]
    zeros = jnp.zeros((CHUNK, 2 * GDN_DV), BF16)
    for i, (j, m) in enumerate(probs):
        v0 = vk_ref[rs[i], 4 * m * GDN_DV:(4 * m + 2) * GDN_DV]
        v1 = vk_ref[rs[i], (4 * m + 2) * GDN_DV:(4 * m + 4) * GDN_DV]
        rhs = jnp.concatenate([jnp.concatenate([v0, zeros], axis=1),
                               jnp.concatenate([zeros, v1], axis=1)], axis=0)
        uw = jnp.dot(t[i].astype(BF16), rhs, preferred_element_type=F32)
        for q, h in enumerate((2 * m, 2 * m + 1)):
            u_ref[rs[i], hc(h)] = uw[:, 2 * q * GDN_DV:(2 * q + 1) * GDN_DV]
            wq_ref[j, 0:CHUNK, hc(h)] = uw[:, (2 * q + 1) * GDN_DV:(2 * q + 2) * GDN_DV].astype(wq_ref.dtype)
            wq_ref[j, CHUNK:2 * CHUNK, hc(h)] = qe_ref[rs[i], hc(h)]


def _gdn_pre(ops, kbq, bgp, alog, dtb):
    t = ops.shape[0]
    rows = GDN_NB * CHUNK
    n_chunks = t // CHUNK
    tile = lambda c: pl.BlockSpec((rows, c), lambda b: (b, 0))
    chunked = lambda r, c: pl.BlockSpec((GDN_NB, r, c), lambda b: (b, 0, 0))
    op_block = lambda part, width: pl.BlockSpec((rows, width * D_MODEL), lambda b: (b, part // width))
    return pl.pallas_call(
        _gdn_pre_body,
        grid=(t // rows,),
        in_specs=[op_block(GDN_OP_K, 1), chunked(2 * CHUNK, D_MODEL), op_block(GDN_OP_VK, 2),
                  op_block(GDN_OP_QE, 1), tile(LANES), _resident((1, LANES)), _resident((1, LANES))],
        out_specs=[tile(D_MODEL), chunked(2 * CHUNK, D_MODEL), tile(GDN_HEADS * CHUNK),
                   chunked(GDN_HEADS, LANES)],
        out_shape=[jax.ShapeDtypeStruct((t, D_MODEL), F32),
                   jax.ShapeDtypeStruct((n_chunks, 2 * CHUNK, D_MODEL), BF16),
                   jax.ShapeDtypeStruct((t, GDN_HEADS * CHUNK), BF16),
                   jax.ShapeDtypeStruct((n_chunks, GDN_HEADS, LANES), F32)],
        compiler_params=_params(), name="gdn_pre")(ops, kbq, ops, ops, bgp, alog, dtb)


def _gdn_scan_body(u_ref, wq_ref, kd_ref, attn_ref, egl_ref, gate_ref, s0_ref, nw_ref,
                   o_ref, soutp_ref, souts_ref, s_scr, *, n_pb, bps):
    blk, is_prompt, first, last = _block_flags(n_pb, bps)

    @pl.when(first)
    def _():
        s_scr[...] = jnp.zeros_like(s_scr)

    heads = range(GDN_HEADS)
    cs = [slice(h * GDN_DK, (h + 1) * GDN_DK) for h in heads]
    s = [s_scr[h] for h in heads]
    for j in range(SCAN_NB):
        rs = slice(j * CHUNK, (j + 1) * CHUNK)
        s = [jnp.where(is_prompt, s[h], s0_ref[j, h]) for h in heads]
        ws_qs = [jnp.dot(wq_ref[j, :, cs[h]], s[h].astype(BF16), preferred_element_type=F32)
                 for h in heads]
        v_new = [(u_ref[rs, cs[h]] - ws_qs[h][0:CHUNK]).astype(BF16) for h in heads]
        o = [ws_qs[h][CHUNK:2 * CHUNK]
             + jnp.dot(attn_ref[rs, h * CHUNK:(h + 1) * CHUNK], v_new[h], preferred_element_type=F32)
             for h in heads]
        s = [s[h] * egl_ref[j, h:h + 1, :] + _mm_tn(kd_ref[rs, cs[h]], v_new[h]) for h in heads]
        for h in heads:
            o_ref[rs, cs[h]] = (_rms(o[h], nw_ref[...]) * gate_ref[rs, cs[h]]).astype(o_ref.dtype)

        @pl.when(jnp.logical_not(is_prompt))
        def _(j=j, s=s):
            for h in heads:
                souts_ref[j, h] = s[h]

    for h in heads:
        s_scr[h] = s[h]

    @pl.when(last)
    def _():
        for h in heads:
            soutp_ref[0, h] = s[h]


def _gdn_scan(u, wq, ops, attn, egl, s0, nw, *, n_p, cpl, n_s):
    t = u.shape[0]
    rows = SCAN_NB * CHUNK
    bps = cpl // SCAN_NB
    n_pb = n_p * bps
    row_blk = lambda b: (b, 0)
    s_in, s_out, s_shapes = _state_specs(n_p, n_s, n_pb, bps, SCAN_NB, (GDN_HEADS, GDN_DK, GDN_DV))
    return pl.pallas_call(
        functools.partial(_gdn_scan_body, n_pb=n_pb, bps=bps),
        grid=(t // rows,),
        in_specs=[pl.BlockSpec((rows, D_MODEL), row_blk),
                  pl.BlockSpec((SCAN_NB, 2 * CHUNK, D_MODEL), lambda b: (b, 0, 0)),
                  pl.BlockSpec((rows, D_MODEL), lambda b: (b, GDN_OP_KD)),
                  pl.BlockSpec((rows, GDN_HEADS * CHUNK), row_blk),
                  pl.BlockSpec((SCAN_NB, GDN_HEADS, LANES), lambda b: (b, 0, 0)),
                  pl.BlockSpec((rows, D_MODEL), lambda b: (b, GDN_OP_GATE)),
                  s_in, _resident((1, GDN_DV))],
        out_specs=[pl.BlockSpec((rows, D_MODEL), row_blk)] + s_out,
        out_shape=[jax.ShapeDtypeStruct((t, D_MODEL), BF16)] + s_shapes,
        scratch_shapes=[pltpu.VMEM((GDN_HEADS, GDN_DK, GDN_DV), F32)],
        compiler_params=_params(), name="gdn_scan")(u, wq, ops, attn, egl, ops, s0, nw)


GLA_NLV = len(GLA_LEVELS) + 1
GLA_SC = 2 * GLA_QK
GLA_OP_QE, GLA_OP_KD, GLA_OP_V, GLA_OP_GATE = 0, GLA_QK, 2 * GLA_QK, 2 * GLA_QK + D_MODEL
GLA_OPS = GLA_OP_GATE + D_MODEL
GLA_NB = 8


def _gla_exponent_matrix():
    i = np.arange(CHUNK)[:, None]
    t = np.arange(CHUNK)[None, :]
    blocks = [(t <= i)]
    for s in GLA_LEVELS:
        c = (i // (2 * s)) * (2 * s) + s - 1
        blocks.append(np.where(i > c, (t > c) & (t <= i), (t > i) & (t <= c)))
    w = np.concatenate(blocks, axis=0).astype(np.float32)
    return np.concatenate([w, w, w], axis=1)


def _split3(x):
    hi = x.astype(BF16)
    r1 = x - hi.astype(F32)
    mid = r1.astype(BF16)
    lo = (r1 - mid.astype(F32)).astype(BF16)
    return hi, mid, lo


def _gla_in_body(*refs, n_x, n_pt):
    x_refs = refs[:n_x]
    nw_ref, w_ref, wg2_ref, bg_ref, wexp_ref, sc_ref, ops_ref, ebl_ref = refs[n_x:]
    xn = _rms(_read_row_tile(x_refs, n_pt), nw_ref[...]).astype(BF16)
    chunks = range(ROW_TILE // CHUNK)
    rs = [slice(j * CHUNK, (j + 1) * CHUNK) for j in chunks]

    g_low = jnp.dot(xn, w_ref[:, GLA_G0:GLA_G0 + LANES], preferred_element_type=F32)
    q_all = jnp.dot(xn, w_ref[:, 0:GLA_QK], preferred_element_type=F32) * (GLA_DK ** -0.5)
    k_all = jnp.dot(xn, w_ref[:, GLA_QK:2 * GLA_QK], preferred_element_type=F32)
    sc_ref[:, 0:GLA_QK] = q_all.astype(BF16)
    sc_ref[:, GLA_QK:GLA_SC] = k_all.astype(BF16)
    x = [_mm(g_low[rs[j]], wg2_ref[...]) + bg_ref[...] for j in chunks]
    glog = [-_softplus(-x[j]) * (1.0 / GLA_TAU) for j in chunks]
    e_all = [jnp.dot(wexp_ref[...], jnp.concatenate(_split3(glog[j]), axis=0),
                     preferred_element_type=F32) for j in chunks]
    for lvl in range(1, GLA_NLV):
        for j in chunks:
            f = jnp.exp(e_all[j][lvl * CHUNK:(lvl + 1) * CHUNK, :])
            sc_ref[rs[j], lvl * GLA_SC:lvl * GLA_SC + GLA_QK] = (q_all[rs[j]] * f).astype(BF16)
            sc_ref[rs[j], lvl * GLA_SC + GLA_QK:(lvl + 1) * GLA_SC] = (k_all[rs[j]] * f).astype(BF16)
    for j in chunks:
        b = e_all[j][0:CHUNK, :]
        b_last = b[CHUNK - 1:CHUNK, :]
        ops_ref[rs[j], GLA_OP_QE:GLA_OP_QE + GLA_QK] = (q_all[rs[j]] * jnp.exp(b)).astype(BF16)
        ops_ref[rs[j], GLA_OP_KD:GLA_OP_KD + GLA_QK] = (k_all[rs[j]] * jnp.exp(b_last - b)).astype(BF16)
        ebl_ref[j] = jnp.broadcast_to(jnp.exp(b_last), (8, GLA_QK))
    for c in range(D_MODEL // FFN_CK):
        cols = slice(c * FFN_CK, (c + 1) * FFN_CK)
        v = jnp.dot(xn, w_ref[:, GLA_V0 + c * FFN_CK:GLA_V0 + (c + 1) * FFN_CK],
                    preferred_element_type=F32)
        ops_ref[:, GLA_OP_V + c * FFN_CK:GLA_OP_V + (c + 1) * FFN_CK] = v.astype(BF16)
        r = jnp.dot(xn, w_ref[:, GLA_R0 + c * FFN_CK:GLA_R0 + (c + 1) * FFN_CK],
                    preferred_element_type=F32)
        ops_ref[:, GLA_OP_GATE + c * FFN_CK:GLA_OP_GATE + (c + 1) * FFN_CK] = _silu(r).astype(BF16)


def _gla_in(x, nw, w, wg2, bg, wexp, *, t, n_pt):
    in_specs, args = _row_tile_specs(x, n_pt)
    n_x = len(args)
    in_specs += [_resident((1, D_MODEL)), _resident((D_MODEL, GLA_P)), _resident((LANES, GLA_QK)),
                 _resident((1, GLA_QK)), _resident((GLA_NLV * CHUNK, 3 * CHUNK))]
    args += [nw, w, wg2, bg, wexp]
    tile = lambda c: pl.BlockSpec((ROW_TILE, c), lambda i: (i, 0))
    return pl.pallas_call(
        functools.partial(_gla_in_body, n_x=n_x, n_pt=n_pt),
        grid=(t // ROW_TILE,),
        in_specs=in_specs,
        out_specs=[tile(GLA_NLV * GLA_SC), tile(GLA_OPS),
                   pl.BlockSpec((ROW_TILE // CHUNK, 8, GLA_QK), lambda i: (i, 0, 0))],
        out_shape=[jax.ShapeDtypeStruct((t, GLA_NLV * GLA_SC), BF16),
                   jax.ShapeDtypeStruct((t, GLA_OPS), BF16),
                   jax.ShapeDtypeStruct((t // CHUNK, 8, GLA_QK), F32)],
        compiler_params=_params(), name="gla_in")(*args)


def _gla_body(sc_ref, ops_ref, ebl_ref, s0_ref, nw_ref,
              o_ref, soutp_ref, souts_ref, st_scr, s0t_scr, *, n_pb, bps):
    blk, is_prompt, first, last = _block_flags(n_pb, bps)
    heads = range(GLA_HEADS)
    chunks = range(GLA_NB)

    @pl.when(first)
    def _():
        st_scr[...] = jnp.zeros_like(st_scr)

    @pl.when(blk == 0)
    def _():
        s0t_scr[...] = jnp.zeros_like(s0t_scr)

    @pl.when(jnp.logical_not(is_prompt))
    def _():
        for j in chunks:
            for h in heads:
                s0t_scr[j, h] = s0_ref[j, h].T

    row = lax.broadcasted_iota(jnp.int32, (CHUNK, CHUNK), 0)
    col = lax.broadcasted_iota(jnp.int32, (CHUNK, CHUNK), 1)
    masks = [row == col] + [_pair_mask(s) for s in GLA_LEVELS]
    rs = [slice(j * CHUNK, (j + 1) * CHUNK) for j in chunks]
    cv = [slice(h * GLA_DV, (h + 1) * GLA_DV) for h in heads]

    def op(j, base, h, width):
        return ops_ref[rs[j], base + h * width:base + (h + 1) * width]

    def scores(j, h, lvl):
        q0 = lvl * GLA_SC + h * GLA_DK
        k0 = q0 + GLA_QK
        return lax.dot_general(sc_ref[rs[j], q0:q0 + GLA_DK], sc_ref[rs[j], k0:k0 + GLA_DK],
                               (((1,), (1,)), ((), ())), preferred_element_type=F32)

    attn = [[jnp.where(masks[0], scores(j, h, 0), 0.0) for h in heads] for j in chunks]
    for lvl in range(1, GLA_NLV):
        attn = [[attn[j][h] + jnp.where(masks[lvl], scores(j, h, lvl), 0.0) for h in heads]
                for j in chunks]
    av = [[jnp.dot(attn[j][h].astype(BF16), op(j, GLA_OP_V, h, GLA_DV), preferred_element_type=F32)
           for h in heads] for j in chunks]
    kv = [[lax.dot_general(op(j, GLA_OP_V, h, GLA_DV), op(j, GLA_OP_KD, h, GLA_DK),
                           (((0,), (0,)), ((), ())), preferred_element_type=F32)
           for h in heads] for j in chunks]

    st = [st_scr[h] for h in heads]
    for j in chunks:
        st = [jnp.where(is_prompt, st[h], s0t_scr[j, h]) for h in heads]
        o = [_mm_nt(op(j, GLA_OP_QE, h, GLA_DK), st[h]) + av[j][h] for h in heads]
        st = [st[h] * ebl_ref[j, 0:1, h * GLA_DK:(h + 1) * GLA_DK] + kv[j][h] for h in heads]
        for h in heads:
            gate = op(j, GLA_OP_GATE, h, GLA_DV)
            o_ref[rs[j], cv[h]] = (_rms(o[h], nw_ref[h:h + 1, :]) * gate).astype(o_ref.dtype)

        @pl.when(jnp.logical_not(is_prompt))
        def _(j=j, st=st):
            for h in heads:
                souts_ref[j, h] = st[h].T

    for h in heads:
        st_scr[h] = st[h]

    @pl.when(last)
    def _():
        for h in heads:
            soutp_ref[0, h] = st[h].T


def _gla_scan(sc, ops, ebl, s0, nw, *, n_p, cpl, n_s):
    t = sc.shape[0]
    rows = GLA_NB * CHUNK
    bps = cpl // GLA_NB
    n_pb = n_p * bps
    s_in, s_out, s_shapes = _state_specs(n_p, n_s, n_pb, bps, GLA_NB, (GLA_HEADS, GLA_DK, GLA_DV))
    return pl.pallas_call(
        functools.partial(_gla_body, n_pb=n_pb, bps=bps),
        grid=(t // rows,),
        in_specs=[pl.BlockSpec((rows, GLA_NLV * GLA_SC), lambda b: (b, 0)),
                  pl.BlockSpec((rows, GLA_OPS), lambda b: (b, 0)),
                  pl.BlockSpec((GLA_NB, 8, GLA_QK), lambda b: (b, 0, 0)),
                  s_in, _resident((GLA_HEADS, GLA_DV))],
        out_specs=[pl.BlockSpec((rows, D_MODEL), lambda b: (b, 0))] + s_out,
        out_shape=[jax.ShapeDtypeStruct((t, D_MODEL), BF16)] + s_shapes,
        scratch_shapes=[pltpu.VMEM((GLA_HEADS, GLA_DV, GLA_DK), F32),
                        pltpu.VMEM((GLA_NB, GLA_HEADS, GLA_DV, GLA_DK), F32)],
        compiler_params=_params(), name="gla_scan")(sc, ops, ebl, s0, nw)


CNV_WIN = CONV_HALO + CHUNK
TILE_CHUNKS = ROW_TILE // CHUNK
CNV_CK = 256


def _cnv_in_body(*refs, n_x, n_pt, tps, n_p):
    x_refs = refs[:n_x]
    (nw_ref, w_ref, b_ref, wdw_ref, bdw_ref, lnw_ref, lnb_ref, halo0_ref,
     a_ref, tails_ref, xc_scr, y_scr, carry_scr) = refs[n_x:]
    i = pl.program_id(0)
    is_sample = i >= n_pt

    @pl.when(i == 0)
    def _():
        carry_scr[...] = jnp.zeros_like(carry_scr)

    xn = _rms(_read_row_tile(x_refs, n_pt), nw_ref[...]).astype(BF16)
    taps = {}
    for w in range(CONV_W):
        s = CONV_HALO - (CONV_W - 1) + w
        taps.setdefault(-s % 8, []).append((w, s + (-s % 8)))

    def glu_slab(c0):
        cols = slice(c0, c0 + CNV_CK)
        gcols = slice(D_MODEL + c0, D_MODEL + c0 + CNV_CK)
        val = jnp.dot(xn, w_ref[:, cols], preferred_element_type=F32) + b_ref[:, cols]
        gate = jnp.dot(xn, w_ref[:, gcols], preferred_element_type=F32) + b_ref[:, gcols]
        glu = val * _sigmoid(gate)
        for j in range(TILE_CHUNKS):
            r0 = j * CNV_WIN
            xc_scr[r0 + CONV_HALO:r0 + CNV_WIN, cols] = glu[j * CHUNK:(j + 1) * CHUNK]
            if j == 0:
                first = jnp.logical_or(is_sample, i % tps == 0)
                prev = carry_scr[:, cols]
            else:
                first = is_sample
                prev = glu[j * CHUNK - CONV_HALO:j * CHUNK]
            seq = jnp.where(is_sample, n_p + (i - n_pt) * TILE_CHUNKS + j, i // tps)
            xc_scr[r0:r0 + CONV_HALO, cols] = jnp.where(first, halo0_ref[seq, :, cols], prev)

    def conv_slab(c0):
        for j in range(TILE_CHUNKS):
            for b0 in range(c0, c0 + CNV_CK, LANES):
                cs = slice(b0, b0 + LANES)
                wv = xc_scr[j * CNV_WIN:(j + 1) * CNV_WIN, cs]
                y = None
                for r, group in sorted(taps.items()):
                    rolled = wv if r == 0 else pltpu.roll(wv, r, 0)
                    for w, a0 in group:
                        term = wdw_ref[w:w + 1, cs] * rolled[a0:a0 + CHUNK]
                        y = term if y is None else y + term
                y_scr[j * CHUNK:(j + 1) * CHUNK, cs] = y + bdw_ref[:, cs]

    slabs = list(range(0, D_MODEL, CNV_CK))
    for n, c0 in enumerate(slabs + [None]):
        if c0 is not None:
            glu_slab(c0)
        if n > 0:
            conv_slab(slabs[n - 1])
    for j in range(TILE_CHUNKS):
        rs = slice(j * CHUNK, (j + 1) * CHUNK)
        a_ref[rs, :] = _silu(_layernorm(y_scr[rs, :], lnw_ref[...], lnb_ref[...])).astype(BF16)
        tails_ref[j] = xc_scr[(j + 1) * CNV_WIN - CONV_HALO:(j + 1) * CNV_WIN, :]
    carry_scr[...] = xc_scr[TILE_CHUNKS * CNV_WIN - CONV_HALO:TILE_CHUNKS * CNV_WIN, :]


def _cnv_in(x, nw, w, b, wdw, bdw, lnw, lnb, halo0, *, t, n_pt, tps, n_p):
    in_specs, args = _row_tile_specs(x, n_pt)
    n_x = len(args)
    in_specs += [_resident((1, D_MODEL)), _resident((D_MODEL, 2 * D_MODEL)), _resident((1, 2 * D_MODEL)),
                 _resident((CONV_HALO, D_MODEL)), _resident((1, D_MODEL)),
                 _resident((1, D_MODEL)), _resident((1, D_MODEL)), _resident(halo0.shape)]
    args += [nw, w, b, wdw, bdw, lnw, lnb, halo0]
    return pl.pallas_call(
        functools.partial(_cnv_in_body, n_x=n_x, n_pt=n_pt, tps=tps, n_p=n_p),
        grid=(t // ROW_TILE,),
        in_specs=in_specs,
        out_specs=[pl.BlockSpec((ROW_TILE, D_MODEL), lambda i: (i, 0)),
                   pl.BlockSpec((TILE_CHUNKS, CONV_HALO, D_MODEL), lambda i: (i, 0, 0))],
        out_shape=[jax.ShapeDtypeStruct((t, D_MODEL), BF16),
                   jax.ShapeDtypeStruct((t // CHUNK, CONV_HALO, D_MODEL), F32)],
        scratch_shapes=[pltpu.VMEM((TILE_CHUNKS * CNV_WIN, D_MODEL), F32),
                        pltpu.VMEM((ROW_TILE, D_MODEL), F32),
                        pltpu.VMEM((CONV_HALO, D_MODEL), F32)],
        compiler_params=_params(), name="cnv_in")(*args)


def _pad_cols(w, n):
    return jnp.pad(w, ((0, 0), (0, n - w.shape[1])))


def _row(v):
    return v.reshape(1, -1).astype(F32)


def _tail_rows(a, lay, n):
    n_p, cpl, n_s = lay
    ends = np.concatenate([(np.arange(n_p) + 1) * cpl * CHUNK,
                           n_p * cpl * CHUNK + (np.arange(n_s) + 1) * CHUNK])
    idx = (ends[:, None] - n + np.arange(n)[None, :]).reshape(-1)
    return jnp.take(a, jnp.asarray(idx, jnp.int32), axis=0).reshape(n_p + n_s, n, a.shape[1])


def _gdn_mix(x, lay, dims, mix_nw, cache, state, w_in, conv_w, a_log, dt_bias, norm_w):
    n_p, cpl, n_s = lay
    halo0 = jnp.concatenate([jnp.zeros((n_p, 8, GDN_CONV_CH), F32),
                             jnp.pad(cache, ((0, 0), (8 - (GDN_CONV - 1), 0), (0, 0)))], axis=0)
    lane_pad = lambda v: jnp.pad(v.reshape(1, -1), ((0, 0), (GDN_HEADS, LANES - 2 * GDN_HEADS)))
    alog, dtb = lane_pad(a_log), lane_pad(dt_bias)
    ops, kbq, bgp, tails = _gdn_in(
        x, _row(mix_nw), _pad_cols(w_in, GDN_P).astype(BF16), conv_w, alog, dtb, halo0,
        tps=cpl * CHUNK // ROW_TILE, n_p=n_p, **dims)
    u, wq, attn, egl = _gdn_pre(ops, kbq, bgp, alog, dtb)
    o, s_p, s_s = _gdn_scan(u, wq, ops, attn, egl, state, _row(norm_w), n_p=n_p, cpl=cpl, n_s=n_s)
    ends = np.concatenate([(np.arange(n_p) + 1) * cpl - 1, n_p * cpl + np.arange(n_s)])
    conv_rows = tails[jnp.asarray(ends, jnp.int32), 8 - (GDN_CONV - 1):, :]
    return o, (conv_rows[:n_p], conv_rows[n_p:], s_p, s_s)


def _gla_mix(x, lay, dims, mix_nw, state, w_in, w_g2, b_g, norm_w):
    n_p, cpl, n_s = lay
    wg2 = jnp.pad(w_g2, ((0, LANES - GLA_RANK), (0, 0))).astype(BF16)
    wexp = jnp.asarray(_gla_exponent_matrix(), BF16)
    sc, ops, ebl = _gla_in(x, _row(mix_nw), _pad_cols(w_in, GLA_P).astype(BF16), wg2, _row(b_g), wexp,
                           **dims)
    o, s_p, s_s = _gla_scan(sc, ops, ebl, state, norm_w, n_p=n_p, cpl=cpl, n_s=n_s)
    return o, (s_p, s_s)


def _cnv_mix(x, lay, dims, mix_nw, cache, w_pw1, b_pw1, w_dw, b_dw, ln_w, ln_b):
    n_p, cpl, n_s = lay
    pad_rows = CONV_HALO - (CONV_W - 1)
    halo0 = jnp.concatenate([jnp.zeros((n_p, CONV_HALO, D_MODEL), F32),
                             jnp.pad(cache, ((0, 0), (pad_rows, 0), (0, 0)))], axis=0)
    wdw = jnp.pad(w_dw, ((0, CONV_HALO - CONV_W), (0, 0)))
    a, tails = _cnv_in(x, _row(mix_nw), w_pw1.astype(BF16), _row(b_pw1), wdw, _row(b_dw),
                       _row(ln_w), _row(ln_b), halo0, tps=cpl * CHUNK // ROW_TILE, n_p=n_p, **dims)
    ends = np.concatenate([(np.arange(n_p) + 1) * cpl - 1, n_p * cpl + np.arange(n_s)])
    rows = tails[jnp.asarray(ends, jnp.int32), pad_rows:, :]
    return a, (rows[:n_p], rows[n_p:])


def kernel(x_prompt, x_sample, cache_gdn_conv, state_gdn, state_gla, cache_conformer, mix_norm_w, ffn_norm_w, ffn_w1, ffn_w2, final_norm_w, gdn_w_in, gdn_conv_w, gdn_a_log, gdn_dt_bias, gdn_norm_w, gdn_w_out, gla_w_in, gla_w_g2, gla_b_g, gla_norm_w, gla_w_out, gmlp_w_in, gmlp_b_in, gmlp_ln_w, gmlp_ln_b, gmlp_w_s, gmlp_b_s, gmlp_w_out, gmlp_b_out, cnv_w_pw1, cnv_b_pw1, cnv_w_dw, cnv_b_dw, cnv_ln_w, cnv_ln_b, cnv_w_pw2, cnv_b_pw2):
    n_p, l_p, d = x_prompt.shape
    n_s, l_s, _ = x_sample.shape
    assert d == D_MODEL and l_s == CHUNK and l_p % ROW_TILE == 0
    assert n_s % max(SCAN_NB, GDN_NB, GLA_NB) == 0
    t_p, t_s = n_p * l_p, n_s * l_s
    assert t_p % ROW_TILE == 0 and t_s % ROW_TILE == 0
    lay = (n_p, l_p // CHUNK, n_s)
    dims = dict(t=t_p + t_s, n_pt=t_p // ROW_TILE)
    depth = mix_norm_w.shape[0]

    x = (x_prompt.reshape(t_p, d), x_sample.reshape(t_s, d))
    outs = {k: [] for k in ("gdn_cp", "gdn_cs", "gdn_sp", "gdn_ss", "gla_sp", "gla_ss",
                            "gmlp_vs", "cnv_p", "cnv_s")}
    w1_all, w2_all = ffn_w1.astype(BF16), ffn_w2.astype(BF16)
    for i in range(depth):
        kind, j = i % 4, i // 4
        ffn = (_row(ffn_norm_w[i]), w1_all, w2_all, _row(final_norm_w) if i == depth - 1 else None)
        if kind == 0:
            o, (cp, cs, sp, ss) = _gdn_mix(x, lay, dims, mix_norm_w[i], cache_gdn_conv[j], state_gdn[j],
                                           gdn_w_in[j], gdn_conv_w[j], gdn_a_log[j], gdn_dt_bias[j],
                                           gdn_norm_w[j])
            outs["gdn_cp"].append(cp); outs["gdn_cs"].append(cs)
            outs["gdn_sp"].append(sp); outs["gdn_ss"].append(ss)
            res = _post([o], x, gdn_w_out[j].astype(BF16), None, *ffn,
                        prologue="plain", layer=i, name=f"post_{i}", **dims)
        elif kind == 1:
            o, (sp, ss) = _gla_mix(x, lay, dims, mix_norm_w[i], state_gla[j], gla_w_in[j], gla_w_g2[j],
                                   gla_b_g[j], gla_norm_w[j])
            outs["gla_sp"].append(sp); outs["gla_ss"].append(ss)
            res = _post([o], x, gla_w_out[j].astype(BF16), None, *ffn,
                        prologue="plain", layer=i, name=f"post_{i}", **dims)
        elif kind == 2:
            zz = _proj(x, _row(mix_norm_w[i]), gmlp_w_in[j].astype(BF16), _row(gmlp_b_in[j]),
                       n_out=2 * D_MODEL, ck=512, epilogue="gelu", name="gmlp_in", **dims)
            bs_full = jnp.repeat(gmlp_b_s[j].T, GMLP_GC, axis=1)
            res = _post([zz, _row(gmlp_ln_w[j]), _row(gmlp_ln_b[j]), gmlp_w_s[j], bs_full], x,
                        gmlp_w_out[j].astype(BF16), _row(gmlp_b_out[j]), *ffn,
                        prologue="gmlp", layer=i, name=f"post_{i}", **dims)
            outs["gmlp_vs"].append(res[-1].reshape(n_s, l_s, d))
            res = res[:-1]
        else:
            a, (cp, cs) = _cnv_mix(x, lay, dims, mix_norm_w[i], cache_conformer[j], cnv_w_pw1[j],
                                   cnv_b_pw1[j], cnv_w_dw[j], cnv_b_dw[j], cnv_ln_w[j], cnv_ln_b[j])
            outs["cnv_p"].append(cp); outs["cnv_s"].append(cs)
            res = _post([a], x, cnv_w_pw2[j].astype(BF16), _row(cnv_b_pw2[j]), *ffn,
                        prologue="plain", layer=i, name=f"post_{i}", **dims)
        x = tuple(res) if i == depth - 1 else res[0]
    y_p, y_s = x
    stack = lambda xs: xs[0][None] if len(xs) == 1 else jnp.stack(xs)
    return (y_p.reshape(n_p, l_p, d), y_s.reshape(n_s, l_s, d),
            stack(outs["gdn_cp"]), stack(outs["gdn_cs"]),
            stack(outs["gdn_sp"]), stack(outs["gdn_ss"]),
            stack(outs["gla_sp"]), stack(outs["gla_ss"]),
            stack(outs["gmlp_vs"]),
            stack(outs["cnv_p"]), stack(outs["cnv_s"]))
```

```python
import functools
import math

import numpy as np
import jax
import jax.numpy as jnp
from jax import lax
from jax.experimental import pallas as pl
from jax.experimental.pallas import tpu as pltpu

F32 = jnp.float32
BF16 = jnp.bfloat16

D_MODEL = 1024
D_FF = 4 * D_MODEL
EPS = 1e-6
CHUNK = 64
ROW_TILE = 512
SCAN_NB = 8
LANES = 128
VMEM_LIMIT = 56 * 1024 * 1024

GDN_HEADS, GDN_DK, GDN_DV, GDN_CONV = 8, 128, 128, 4
GDN_QK = GDN_HEADS * GDN_DK
GDN_CONV_CH = 2 * GDN_QK + GDN_HEADS * GDN_DV
GDN_Z0 = GDN_CONV_CH
GDN_BG0 = GDN_CONV_CH + GDN_HEADS * GDN_DV
GDN_P = GDN_BG0 + LANES

GLA_HEADS, GLA_DK, GLA_DV, GLA_RANK, GLA_TAU = 4, 128, 256, 16, 16.0
GLA_QK = GLA_HEADS * GLA_DK
GLA_V0, GLA_R0, GLA_G0 = 2 * GLA_QK, 2 * GLA_QK + D_MODEL, 2 * GLA_QK + 2 * D_MODEL
GLA_P = GLA_G0 + LANES
GLA_LEVELS = (32, 16, 8, 4, 2, 1)

GMLP_BLOCK, GMLP_GROUPS, GMLP_GC = 128, 4, 256
CONV_W = 31
CONV_HALO = 32


def _mm(a, b):
    return jnp.dot(a.astype(BF16), b.astype(BF16), preferred_element_type=F32)


def _mm_nt(a, b):
    return lax.dot_general(a.astype(BF16), b.astype(BF16), (((1,), (1,)), ((), ())),
                           preferred_element_type=F32)


def _mm_tn(a, b):
    return lax.dot_general(a.astype(BF16), b.astype(BF16), (((0,), (0,)), ((), ())),
                           preferred_element_type=F32)


def _rms(x, w):
    return x * lax.rsqrt(jnp.mean(x * x, axis=-1, keepdims=True) + EPS) * w


def _layernorm(x, w, b):
    xc = x - jnp.mean(x, axis=-1, keepdims=True)
    return xc * lax.rsqrt(jnp.mean(xc * xc, axis=-1, keepdims=True) + EPS) * w + b


def _sigmoid(x):
    return jax.nn.sigmoid(x)


def _silu(x):
    h = 0.5 * x
    return h + h * jnp.tanh(h)


def _softplus(x):
    return jnp.maximum(x, 0.0) + jnp.log1p(jnp.exp(-jnp.abs(x)))


def _gelu_tanh(x):
    return 0.5 * x * (1.0 + jnp.tanh(math.sqrt(2.0 / math.pi) * (x + 0.044715 * (x * x * x))))


def _cumsum_rows(x):
    row = lax.broadcasted_iota(jnp.int32, x.shape, 0)
    s = 1
    while s < x.shape[0]:
        x = x + jnp.where(row >= s, pltpu.roll(x, s, 0), 0.0)
        s *= 2
    return x


def _pair_mask(s):
    sh = int(math.log2(s))
    bi = lax.broadcasted_iota(jnp.int32, (CHUNK, CHUNK), 0) >> sh
    bj = lax.broadcasted_iota(jnp.int32, (CHUNK, CHUNK), 1) >> sh
    return jnp.logical_and((bi & 1) == 1, bj == bi - 1)


def _block_flags(n_pb, bps):
    blk = pl.program_id(0)
    is_prompt = blk < n_pb
    first = jnp.logical_and(is_prompt, blk % bps == 0)
    last = jnp.logical_and(is_prompt, blk % bps == bps - 1)
    return blk, is_prompt, first, last


def _state_specs(n_p, n_s, n_pb, bps, nb, shape):
    zeros = (0,) * len(shape)
    sample = pl.BlockSpec((nb,) + shape, lambda b: (jnp.maximum(b - n_pb, 0),) + zeros)
    prompt = pl.BlockSpec((1,) + shape, lambda b: (jnp.minimum(b // bps, n_p - 1),) + zeros)
    shapes = [jax.ShapeDtypeStruct((n_p,) + shape, F32), jax.ShapeDtypeStruct((n_s,) + shape, F32)]
    return sample, [prompt, sample], shapes


def _params(n_axes=1):
    return pltpu.CompilerParams(dimension_semantics=("arbitrary",) * n_axes,
                                vmem_limit_bytes=VMEM_LIMIT)


def _resident(shape):
    zeros = (0,) * len(shape)
    return pl.BlockSpec(shape, lambda i: zeros, pipeline_mode=pl.Buffered(1))


def _row_tile_specs(x, n_pt):
    if isinstance(x, tuple):
        return ([pl.BlockSpec((ROW_TILE, D_MODEL), lambda i: (jnp.minimum(i, n_pt - 1), 0)),
                 pl.BlockSpec((ROW_TILE, D_MODEL), lambda i: (jnp.maximum(i - n_pt, 0), 0))], list(x))
    return [pl.BlockSpec((ROW_TILE, D_MODEL), lambda i: (i, 0))], [x]


def _read_row_tile(refs, n_pt):
    if len(refs) == 2:
        return jnp.where(pl.program_id(0) < n_pt, refs[0][...], refs[1][...])
    return refs[0][...]


def _proj_body(*refs, n_x, n_pt, n_out, ck, epilogue, has_bias):
    x_refs, (nw_ref, w_ref), rest = refs[:n_x], refs[n_x:n_x + 2], refs[n_x + 2:]
    if has_bias:
        b_ref, o_ref = rest
    else:
        (o_ref,) = rest
    xn = _rms(_read_row_tile(x_refs, n_pt), nw_ref[...]).astype(BF16)
    for c0 in range(0, n_out, ck):
        sl = slice(c0, min(c0 + ck, n_out))
        y = jnp.dot(xn, w_ref[:, sl], preferred_element_type=F32)
        if has_bias:
            y = y + b_ref[:, sl]
        if epilogue == "gelu":
            y = _gelu_tanh(y)
        o_ref[:, sl] = y.astype(o_ref.dtype)


def _proj(x, nw, w, b, *, t, n_pt, n_out, ck, epilogue="none", out_dtype=F32, name):
    n_w = w.shape[1]
    has_bias = b is not None
    in_specs, args = _row_tile_specs(x, n_pt)
    n_x = len(args)
    in_specs += [_resident((1, D_MODEL)), _resident((D_MODEL, n_w))]
    args += [nw, w]
    if has_bias:
        in_specs.append(_resident((1, n_w)))
        args.append(b)
    return pl.pallas_call(
        functools.partial(_proj_body, n_x=n_x, n_pt=n_pt, n_out=n_out, ck=ck, epilogue=epilogue,
                          has_bias=has_bias),
        grid=(t // ROW_TILE,),
        in_specs=in_specs,
        out_specs=pl.BlockSpec((ROW_TILE, n_out), lambda i: (i, 0)),
        out_shape=jax.ShapeDtypeStruct((t, n_out), out_dtype),
        compiler_params=_params(), name=name)(*args)


FFN_CK = 512


def _gmlp_gate(zz_ref, lnw_ref, lnb_ref, ws_ref, bs_ref, vs_ref, a_scr, is_sample):
    v = _layernorm(zz_ref[:, D_MODEL:2 * D_MODEL], lnw_ref[...], lnb_ref[...])

    @pl.when(is_sample)
    def _():
        vs_ref[...] = v

    vb = v.astype(BF16)
    row = lax.broadcasted_iota(jnp.int32, (GMLP_BLOCK, GMLP_BLOCK), 0)
    col = lax.broadcasted_iota(jnp.int32, (GMLP_BLOCK, GMLP_BLOCK), 1)
    tril = col <= row
    top_left = jnp.logical_and(row < CHUNK, col < CHUNK)
    brow = lax.broadcasted_iota(jnp.int32, (GMLP_BLOCK, D_MODEL), 0)
    bias = bs_ref[...]
    bias = jnp.where(jnp.logical_and(is_sample, brow >= CHUNK), pltpu.roll(bias, CHUNK, 0), bias)
    for grp in range(GMLP_GROUPS):
        cs = slice(grp * GMLP_GC, (grp + 1) * GMLP_GC)
        wt = jnp.where(tril, ws_ref[grp], 0.0)
        w11 = jnp.where(top_left, wt, 0.0)
        wd = w11 + pltpu.roll(pltpu.roll(w11, CHUNK, 0), CHUNK, 1)
        we = jnp.where(is_sample, wd, wt).astype(BF16)
        for blk in range(ROW_TILE // GMLP_BLOCK):
            rs = slice(blk * GMLP_BLOCK, (blk + 1) * GMLP_BLOCK)
            mixed = jnp.dot(we, vb[rs, cs], preferred_element_type=F32)
            a_scr[rs, cs] = (zz_ref[rs, cs] * (mixed + bias[:, cs])).astype(BF16)
    return a_scr[...]


def _post_body(*refs, prologue, n_res, has_bias, final, n_pt):
    n_in = {"plain": 1, "gmlp": 5}[prologue]
    pro, refs = refs[:n_in], refs[n_in:]
    res_refs, refs = refs[:n_res], refs[n_res:]
    wo_ref, refs = refs[0], refs[1:]
    if has_bias:
        bo_ref, refs = refs[0], refs[1:]
    fnw_ref, w1_ref, w2_ref, refs = refs[0], refs[1], refs[2], refs[3:]
    if final:
        finw_ref, refs = refs[0], refs[1:]
    n_main = 2 if final else 1
    outs, refs = refs[:n_main], refs[n_main:]
    is_sample = pl.program_id(0) >= n_pt

    if prologue == "plain":
        a = pro[0][...]
    else:
        vs_ref, a_scr = refs
        a = _gmlp_gate(*pro, vs_ref, a_scr, is_sample)

    x = _read_row_tile(res_refs, n_pt) + jnp.dot(a, wo_ref[...], preferred_element_type=F32)
    if has_bias:
        x = x + bo_ref[...]
    xn = _rms(x, fnw_ref[...]).astype(BF16)
    for c in range(D_FF // FFN_CK):
        sl = slice(c * FFN_CK, (c + 1) * FFN_CK)
        h = jnp.dot(xn, w1_ref[:, sl], preferred_element_type=F32)
        h = jnp.square(jnp.maximum(h, 0.0)).astype(BF16)
        x = x + jnp.dot(h, w2_ref[sl, :], preferred_element_type=F32)
    if final:
        y = _rms(x, finw_ref[...])

        @pl.when(jnp.logical_not(is_sample))
        def _():
            outs[0][...] = y

        @pl.when(is_sample)
        def _():
            outs[1][...] = y
    else:
        outs[0][...] = x


def _post(pro_args, res, wo, bo, fnw, w1, w2, finw, *, layer, prologue, t, n_pt, name):
    has_bias = bo is not None
    final = finw is not None
    tile = lambda c: pl.BlockSpec((ROW_TILE, c), lambda i: (i, 0))
    if prologue == "plain":
        in_specs = [tile(D_MODEL)]
    else:
        in_specs = [tile(2 * D_MODEL), _resident((1, D_MODEL)), _resident((1, D_MODEL)),
                    _resident((GMLP_GROUPS, GMLP_BLOCK, GMLP_BLOCK)),
                    _resident((GMLP_BLOCK, D_MODEL))]
    args = list(pro_args)
    res_specs, res_args = _row_tile_specs(res, n_pt)
    in_specs += res_specs + [_resident((D_MODEL, D_MODEL))]
    args += res_args + [wo]
    if has_bias:
        in_specs.append(_resident((1, D_MODEL)))
        args.append(bo)
    layer_block = lambda shape: pl.BlockSpec((None,) + shape, lambda i: (layer, 0, 0),
                                             pipeline_mode=pl.Buffered(1))
    in_specs += [_resident((1, D_MODEL)), layer_block((D_MODEL, D_FF)), layer_block((D_FF, D_MODEL))]
    args += [fnw, w1, w2]
    if final:
        in_specs.append(_resident((1, D_MODEL)))
        args.append(finw)
        out_specs = [pl.BlockSpec((ROW_TILE, D_MODEL), lambda i: (jnp.minimum(i, n_pt - 1), 0)),
                     pl.BlockSpec((ROW_TILE, D_MODEL), lambda i: (jnp.maximum(i - n_pt, 0), 0))]
        out_shape = [jax.ShapeDtypeStruct((n_pt * ROW_TILE, D_MODEL), F32),
                     jax.ShapeDtypeStruct((t - n_pt * ROW_TILE, D_MODEL), F32)]
    else:
        out_specs = [tile(D_MODEL)]
        out_shape = [jax.ShapeDtypeStruct((t, D_MODEL), F32)]
    scratch = []
    if prologue == "gmlp":
        out_specs.append(pl.BlockSpec((ROW_TILE, D_MODEL), lambda i: (jnp.maximum(i - n_pt, 0), 0)))
        out_shape.append(jax.ShapeDtypeStruct((t - n_pt * ROW_TILE, D_MODEL), F32))
        scratch.append(pltpu.VMEM((ROW_TILE, D_MODEL), BF16))
    return pl.pallas_call(
        functools.partial(_post_body, prologue=prologue, n_res=len(res_args), has_bias=has_bias,
                          final=final, n_pt=n_pt),
        grid=(t // ROW_TILE,),
        in_specs=in_specs, out_specs=out_specs, out_shape=out_shape, scratch_shapes=scratch,
        compiler_params=_params(), name=name)(*args)


GDN_NB = 8
GDN_WIN = CHUNK + 8
GDN_GROUPS = ROW_TILE // CHUNK
GDN_OP_K, GDN_OP_QE, GDN_OP_KD, GDN_OP_GATE, GDN_OP_VK, GDN_OPS = 0, 1, 2, 3, 4, 6


def _gdn_gates(bg, alog_ref, dtb_ref):
    glog = -jnp.exp(alog_ref[...]) * _softplus(bg + dtb_ref[...])
    return _sigmoid(bg), _cumsum_rows(glog)


def _gdn_in_body(*refs, n_x, n_pt, tps, n_p):
    x_refs = refs[:n_x]
    (nw_ref, w_ref, cw_ref, alog_ref, dtb_ref, halo0_ref,
     ops_ref, kbq_ref, bg_ref, tails_ref, xc_scr, carry_scr) = refs[n_x:]
    i = pl.program_id(0)
    is_sample = i >= n_pt

    @pl.when(i == 0)
    def _():
        carry_scr[...] = jnp.zeros_like(carry_scr)

    xn = _rms(_read_row_tile(x_refs, n_pt), nw_ref[...]).astype(BF16)
    bg = jnp.dot(xn, w_ref[:, GDN_BG0:GDN_BG0 + LANES], preferred_element_type=F32)
    bg_ref[...] = bg
    beta, egc, e_rest = [], [], []
    for j in range(GDN_GROUPS):
        b_j, gc = _gdn_gates(bg[j * CHUNK:(j + 1) * CHUNK], alog_ref, dtb_ref)
        beta.append(b_j)
        egc.append(jnp.exp(gc))
        e_rest.append(jnp.exp(gc[CHUNK - 1:CHUNK, :] - gc))

    def project(c0, n):
        cols = slice(c0, c0 + n)
        y = jnp.dot(xn, w_ref[:, cols], preferred_element_type=F32)
        for j in range(GDN_GROUPS):
            r0 = j * GDN_WIN
            xc_scr[r0 + 8:r0 + GDN_WIN, cols] = y[j * CHUNK:(j + 1) * CHUNK]
            if j == 0:
                first = jnp.logical_or(is_sample, i % tps == 0)
                prev = carry_scr[:, cols]
            else:
                first = is_sample
                prev = y[j * CHUNK - 8:j * CHUNK]
            seq = jnp.where(is_sample, n_p + (i - n_pt) * GDN_GROUPS + j, i // tps)
            xc_scr[r0:r0 + 8, cols] = jnp.where(first, halo0_ref[seq, :, cols], prev)

    def conv_silu(j, c0):
        cw = [cw_ref[w:w + 1, c0:c0 + LANES] for w in range(GDN_CONV)]
        wv = xc_scr[j * GDN_WIN:(j + 1) * GDN_WIN, c0:c0 + LANES]
        back2 = pltpu.roll(wv, 2, 0)
        odd = pltpu.roll(cw[2] * wv + cw[0] * back2, 1, 0)
        return _silu(cw[3] * wv[8:GDN_WIN] + cw[1] * back2[8:GDN_WIN] + odd[8:GDN_WIN])

    def epilogue(h):
        cs = slice(h * GDN_DK, (h + 1) * GDN_DK)
        for j in range(GDN_GROUPS):
            rs = slice(j * CHUNK, (j + 1) * CHUNK)
            q = conv_silu(j, h * GDN_DK)
            k = conv_silu(j, GDN_QK + h * GDN_DK)
            v = conv_silu(j, 2 * GDN_QK + h * GDN_DV)
            q = q * lax.rsqrt(jnp.sum(q * q, axis=-1, keepdims=True) + EPS) * (GDN_DK ** -0.5)
            k = k * lax.rsqrt(jnp.sum(k * k, axis=-1, keepdims=True) + EPS)
            beta_h = beta[j][:, h:h + 1]
            e_h = egc[j][:, 8 + h:9 + h]
            kb = k * beta_h
            col = lambda part, c0=h * GDN_DK: slice(part * D_MODEL + c0, part * D_MODEL + c0 + GDN_DK)
            ops_ref[rs, col(GDN_OP_K)] = k.astype(BF16)
            ops_ref[rs, col(GDN_OP_QE)] = (q * e_h).astype(BF16)
            ops_ref[rs, col(GDN_OP_KD)] = (k * e_rest[j][:, 8 + h:9 + h]).astype(BF16)
            vk0 = GDN_OP_VK * D_MODEL + 2 * h * GDN_DV
            ops_ref[rs, vk0:vk0 + GDN_DV] = (v * beta_h).astype(BF16)
            ops_ref[rs, vk0 + GDN_DV:vk0 + 2 * GDN_DV] = (kb * e_h).astype(BF16)
            kbq_ref[j, 0:CHUNK, cs] = kb.astype(BF16)
            kbq_ref[j, CHUNK:2 * CHUNK, cs] = q.astype(BF16)

    pair = 2 * GDN_DK
    n_pairs = GDN_HEADS // 2
    for m in range(n_pairs + 1):
        if m < n_pairs:
            for part in range(3):
                project(part * GDN_QK + m * pair, pair)
        if m > 0:
            epilogue(2 * m - 2)
            epilogue(2 * m - 1)
    for c in range(2):
        z = jnp.dot(xn, w_ref[:, GDN_Z0 + c * FFN_CK:GDN_Z0 + (c + 1) * FFN_CK],
                    preferred_element_type=F32)
        ops_ref[:, GDN_OP_GATE * D_MODEL + c * FFN_CK:GDN_OP_GATE * D_MODEL + (c + 1) * FFN_CK] = (
            _silu(z).astype(BF16))
    for j in range(GDN_GROUPS):
        tails_ref[j] = xc_scr[(j + 1) * GDN_WIN - 8:(j + 1) * GDN_WIN, :]
    carry_scr[...] = xc_scr[GDN_GROUPS * GDN_WIN - 8:GDN_GROUPS * GDN_WIN, :]


def _gdn_in(x, nw, w, cw, alog, dtb, halo0, *, t, n_pt, tps, n_p):
    in_specs, args = _row_tile_specs(x, n_pt)
    n_x = len(args)
    n_seq = halo0.shape[0]
    in_specs += [_resident((1, D_MODEL)), _resident((D_MODEL, GDN_P)),
                 _resident((GDN_CONV, GDN_CONV_CH)), _resident((1, LANES)), _resident((1, LANES)),
                 _resident((n_seq, 8, GDN_CONV_CH))]
    args += [nw, w, cw, alog, dtb, halo0]
    tile = lambda c: pl.BlockSpec((ROW_TILE, c), lambda i: (i, 0))
    n_chunks = t // CHUNK
    bf = lambda c: jax.ShapeDtypeStruct((t, c), BF16)
    return pl.pallas_call(
        functools.partial(_gdn_in_body, n_x=n_x, n_pt=n_pt, tps=tps, n_p=n_p),
        grid=(t // ROW_TILE,),
        in_specs=in_specs,
        out_specs=[tile(GDN_OPS * D_MODEL),
                   pl.BlockSpec((GDN_GROUPS, 2 * CHUNK, D_MODEL), lambda i: (i, 0, 0)),
                   tile(LANES),
                   pl.BlockSpec((GDN_GROUPS, 8, GDN_CONV_CH), lambda i: (i, 0, 0))],
        out_shape=[bf(GDN_OPS * D_MODEL), jax.ShapeDtypeStruct((n_chunks, 2 * CHUNK, D_MODEL), BF16),
                   jax.ShapeDtypeStruct((t, LANES), F32),
                   jax.ShapeDtypeStruct((n_chunks, 8, GDN_CONV_CH), F32)],
        scratch_shapes=[pltpu.VMEM((GDN_GROUPS * GDN_WIN, GDN_CONV_CH), F32),
                        pltpu.VMEM((8, GDN_CONV_CH), F32)],
        compiler_params=_params(), name="gdn_in")(*args)


def _gdn_pre_body(k_ref, kbq_ref, vk_ref, qe_ref, bg_ref, alog_ref, dtb_ref,
                  u_ref, wq_ref, attn_ref, egl_ref):
    pw = 2 * CHUNK
    row = lax.broadcasted_iota(jnp.int32, (CHUNK, pw), 0)
    lane = lax.broadcasted_iota(jnp.int32, (CHUNK, pw), 1)
    col = lane & (CHUNK - 1)
    left = lane < CHUNK
    incl = row >= col
    strict = row > col
    eye = (row == col).astype(F32)

    def level_mask(s):
        sh = int(math.log2(s))
        bi, bj = row >> sh, col >> sh
        return jnp.logical_and((bi & 1) == 1, bj == bi - 1)

    masks = [level_mask(s) for s in (1, 2, 4, 8, 16, 32)]
    r2 = lax.broadcasted_iota(jnp.int32, (pw, pw), 0)
    l2 = lax.broadcasted_iota(jnp.int32, (pw, pw), 1)
    same_head = (r2 < CHUNK) == (l2 < CHUNK)

    def block_diag(x):
        return jnp.where(same_head, jnp.concatenate([x, x], axis=0), jnp.zeros((pw, pw), x.dtype))

    probs = [(j, m) for j in range(GDN_NB) for m in range(GDN_HEADS // 2)]
    n = len(probs)
    rs = [slice(j * CHUNK, (j + 1) * CHUNK) for j, _ in probs]
    hc = lambda h: slice(h * GDN_DK, (h + 1) * GDN_DK)
    gcum, gcum_t = [], []
    for j in range(GDN_NB):
        _, gc = _gdn_gates(bg_ref[j * CHUNK:(j + 1) * CHUNK, :], alog_ref, dtb_ref)
        gcum.append(gc)
        gt = jnp.transpose(jnp.concatenate([gc, jnp.zeros_like(gc)], axis=0))
        gcum_t.append(gt)
        egl_ref[j] = jnp.broadcast_to(jnp.exp(gt[8:8 + GDN_HEADS, CHUNK - 1:CHUNK]),
                                      (GDN_HEADS, LANES))
    decay = []
    for j, m in probs:
        h0, h1 = 8 + 2 * m, 9 + 2 * m
        g_col = jnp.where(left, gcum[j][:, h0:h0 + 1], gcum[j][:, h1:h1 + 1])
        g_row = jnp.concatenate([gcum_t[j][h0:h0 + 1, 0:CHUNK], gcum_t[j][h1:h1 + 1, 0:CHUNK]], axis=1)
        decay.append(jnp.exp(jnp.minimum(g_col - g_row, 0.0)))
    qk = []
    for i, (j, m) in enumerate(probs):
        k_pair = jnp.concatenate([k_ref[rs[i], hc(2 * m)], k_ref[rs[i], hc(2 * m + 1)]], axis=0)
        r0, r1 = (lax.dot_general(kbq_ref[j, :, hc(h)], k_pair, (((1,), (1,)), ((), ())),
                                  preferred_element_type=F32) for h in (2 * m, 2 * m + 1))
        qk.append(jnp.where(l2 < CHUNK, r0, r1))
    a = [jnp.where(strict, qk[i][0:CHUNK] * decay[i], 0.0) for i in range(n)]
    for i, (j, m) in enumerate(probs):
        attn = jnp.where(incl, qk[i][CHUNK:2 * CHUNK] * decay[i], 0.0)
        attn_ref[rs[i], m * pw:(m + 1) * pw] = attn.astype(attn_ref.dtype)
    t = [eye - jnp.where(masks[0], ai, 0.0) for ai in a]
    a_b = [ai.astype(BF16) for ai in a]
    for lm in masks[1:]:
        t_b = [ti.astype(BF16) for ti in t]
        at = [jnp.dot(jnp.where(lm, a_b[i], jnp.zeros_like(a_b[i])), block_diag(t_b[i]),
                      preferred_element_type=F32) for i in range(n)]
        t = [t[i] - jnp.dot(t_b[i], block_diag(at[i].astype(BF16)), preferred_element_type=F32)
             for i in range(n)]
    zeros = jnp.zeros((CHUNK, 2 * GDN_DV), BF16)
    for i, (j, m) in enumerate(probs):
        v0 = vk_ref[rs[i], 4 * m * GDN_DV:(4 * m + 2) * GDN_DV]
        v1 = vk_ref[rs[i], (4 * m + 2) * GDN_DV:(4 * m + 4) * GDN_DV]
        rhs = jnp.concatenate([jnp.concatenate([v0, zeros], axis=1),
                               jnp.concatenate([zeros, v1], axis=1)], axis=0)
        uw = jnp.dot(t[i].astype(BF16), rhs, preferred_element_type=F32)
        for q, h in enumerate((2 * m, 2 * m + 1)):
            u_ref[rs[i], hc(h)] = uw[:, 2 * q * GDN_DV:(2 * q + 1) * GDN_DV]
            wq_ref[j, 0:CHUNK, hc(h)] = uw[:, (2 * q + 1) * GDN_DV:(2 * q + 2) * GDN_DV].astype(wq_ref.dtype)
            wq_ref[j, CHUNK:2 * CHUNK, hc(h)] = qe_ref[rs[i], hc(h)]


def _gdn_pre(ops, kbq, bgp, alog, dtb):
    t = ops.shape[0]
    rows = GDN_NB * CHUNK
    n_chunks = t // CHUNK
    tile = lambda c: pl.BlockSpec((rows, c), lambda b: (b, 0))
    chunked = lambda r, c: pl.BlockSpec((GDN_NB, r, c), lambda b: (b, 0, 0))
    op_block = lambda part, width: pl.BlockSpec((rows, width * D_MODEL), lambda b: (b, part // width))
    return pl.pallas_call(
        _gdn_pre_body,
        grid=(t // rows,),
        in_specs=[op_block(GDN_OP_K, 1), chunked(2 * CHUNK, D_MODEL), op_block(GDN_OP_VK, 2),
                  op_block(GDN_OP_QE, 1), tile(LANES), _resident((1, LANES)), _resident((1, LANES))],
        out_specs=[tile(D_MODEL), chunked(2 * CHUNK, D_MODEL), tile(GDN_HEADS * CHUNK),
                   chunked(GDN_HEADS, LANES)],
        out_shape=[jax.ShapeDtypeStruct((t, D_MODEL), F32),
                   jax.ShapeDtypeStruct((n_chunks, 2 * CHUNK, D_MODEL), BF16),
                   jax.ShapeDtypeStruct((t, GDN_HEADS * CHUNK), BF16),
                   jax.ShapeDtypeStruct((n_chunks, GDN_HEADS, LANES), F32)],
        compiler_params=_params(), name="gdn_pre")(ops, kbq, ops, ops, bgp, alog, dtb)


def _gdn_scan_body(u_ref, wq_ref, kd_ref, attn_ref, egl_ref, gate_ref, s0_ref, nw_ref,
                   o_ref, soutp_ref, souts_ref, s_scr, *, n_pb, bps):
    blk, is_prompt, first, last = _block_flags(n_pb, bps)

    @pl.when(first)
    def _():
        s_scr[...] = jnp.zeros_like(s_scr)

    heads = range(GDN_HEADS)
    cs = [slice(h * GDN_DK, (h + 1) * GDN_DK) for h in heads]
    s = [s_scr[h] for h in heads]
    for j in range(SCAN_NB):
        rs = slice(j * CHUNK, (j + 1) * CHUNK)
        s = [jnp.where(is_prompt, s[h], s0_ref[j, h]) for h in heads]
        ws_qs = [jnp.dot(wq_ref[j, :, cs[h]], s[h].astype(BF16), preferred_element_type=F32)
                 for h in heads]
        v_new = [(u_ref[rs, cs[h]] - ws_qs[h][0:CHUNK]).astype(BF16) for h in heads]
        o = [ws_qs[h][CHUNK:2 * CHUNK]
             + jnp.dot(attn_ref[rs, h * CHUNK:(h + 1) * CHUNK], v_new[h], preferred_element_type=F32)
             for h in heads]
        s = [s[h] * egl_ref[j, h:h + 1, :] + _mm_tn(kd_ref[rs, cs[h]], v_new[h]) for h in heads]
        for h in heads:
            o_ref[rs, cs[h]] = (_rms(o[h], nw_ref[...]) * gate_ref[rs, cs[h]]).astype(o_ref.dtype)

        @pl.when(jnp.logical_not(is_prompt))
        def _(j=j, s=s):
            for h in heads:
                souts_ref[j, h] = s[h]

    for h in heads:
        s_scr[h] = s[h]

    @pl.when(last)
    def _():
        for h in heads:
            soutp_ref[0, h] = s[h]


def _gdn_scan(u, wq, ops, attn, egl, s0, nw, *, n_p, cpl, n_s):
    t = u.shape[0]
    rows = SCAN_NB * CHUNK
    bps = cpl // SCAN_NB
    n_pb = n_p * bps
    row_blk = lambda b: (b, 0)
    s_in, s_out, s_shapes = _state_specs(n_p, n_s, n_pb, bps, SCAN_NB, (GDN_HEADS, GDN_DK, GDN_DV))
    return pl.pallas_call(
        functools.partial(_gdn_scan_body, n_pb=n_pb, bps=bps),
        grid=(t // rows,),
        in_specs=[pl.BlockSpec((rows, D_MODEL), row_blk),
                  pl.BlockSpec((SCAN_NB, 2 * CHUNK, D_MODEL), lambda b: (b, 0, 0)),
                  pl.BlockSpec((rows, D_MODEL), lambda b: (b, GDN_OP_KD)),
                  pl.BlockSpec((rows, GDN_HEADS * CHUNK), row_blk),
                  pl.BlockSpec((SCAN_NB, GDN_HEADS, LANES), lambda b: (b, 0, 0)),
                  pl.BlockSpec((rows, D_MODEL), lambda b: (b, GDN_OP_GATE)),
                  s_in, _resident((1, GDN_DV))],
        out_specs=[pl.BlockSpec((rows, D_MODEL), row_blk)] + s_out,
        out_shape=[jax.ShapeDtypeStruct((t, D_MODEL), BF16)] + s_shapes,
        scratch_shapes=[pltpu.VMEM((GDN_HEADS, GDN_DK, GDN_DV), F32)],
        compiler_params=_params(), name="gdn_scan")(u, wq, ops, attn, egl, ops, s0, nw)


GLA_NLV = len(GLA_LEVELS) + 1
GLA_SC = 2 * GLA_QK
GLA_OP_QE, GLA_OP_KD, GLA_OP_V, GLA_OP_GATE = 0, GLA_QK, 2 * GLA_QK, 2 * GLA_QK + D_MODEL
GLA_OPS = GLA_OP_GATE + D_MODEL
GLA_NB = 8


def _gla_exponent_matrix():
    i = np.arange(CHUNK)[:, None]
    t = np.arange(CHUNK)[None, :]
    blocks = [(t <= i)]
    for s in GLA_LEVELS:
        c = (i // (2 * s)) * (2 * s) + s - 1
        blocks.append(np.where(i > c, (t > c) & (t <= i), (t > i) & (t <= c)))
    w = np.concatenate(blocks, axis=0).astype(np.float32)
    return np.concatenate([w, w, w], axis=1)


def _split3(x):
    hi = x.astype(BF16)
    r1 = x - hi.astype(F32)
    mid = r1.astype(BF16)
    lo = (r1 - mid.astype(F32)).astype(BF16)
    return hi, mid, lo


def _gla_in_body(*refs, n_x, n_pt):
    x_refs = refs[:n_x]
    nw_ref, w_ref, wg2_ref, bg_ref, wexp_ref, sc_ref, ops_ref, ebl_ref = refs[n_x:]
    xn = _rms(_read_row_tile(x_refs, n_pt), nw_ref[...]).astype(BF16)
    chunks = range(ROW_TILE // CHUNK)
    rs = [slice(j * CHUNK, (j + 1) * CHUNK) for j in chunks]

    g_low = jnp.dot(xn, w_ref[:, GLA_G0:GLA_G0 + LANES], preferred_element_type=F32)
    q_all = jnp.dot(xn, w_ref[:, 0:GLA_QK], preferred_element_type=F32) * (GLA_DK ** -0.5)
    k_all = jnp.dot(xn, w_ref[:, GLA_QK:2 * GLA_QK], preferred_element_type=F32)
    sc_ref[:, 0:GLA_QK] = q_all.astype(BF16)
    sc_ref[:, GLA_QK:GLA_SC] = k_all.astype(BF16)
    x = [_mm(g_low[rs[j]], wg2_ref[...]) + bg_ref[...] for j in chunks]
    glog = [-_softplus(-x[j]) * (1.0 / GLA_TAU) for j in chunks]
    e_all = [jnp.dot(wexp_ref[...], jnp.concatenate(_split3(glog[j]), axis=0),
                     preferred_element_type=F32) for j in chunks]
    for lvl in range(1, GLA_NLV):
        for j in chunks:
            f = jnp.exp(e_all[j][lvl * CHUNK:(lvl + 1) * CHUNK, :])
            sc_ref[rs[j], lvl * GLA_SC:lvl * GLA_SC + GLA_QK] = (q_all[rs[j]] * f).astype(BF16)
            sc_ref[rs[j], lvl * GLA_SC + GLA_QK:(lvl + 1) * GLA_SC] = (k_all[rs[j]] * f).astype(BF16)
    for j in chunks:
        b = e_all[j][0:CHUNK, :]
        b_last = b[CHUNK - 1:CHUNK, :]
        ops_ref[rs[j], GLA_OP_QE:GLA_OP_QE + GLA_QK] = (q_all[rs[j]] * jnp.exp(b)).astype(BF16)
        ops_ref[rs[j], GLA_OP_KD:GLA_OP_KD + GLA_QK] = (k_all[rs[j]] * jnp.exp(b_last - b)).astype(BF16)
        ebl_ref[j] = jnp.broadcast_to(jnp.exp(b_last), (8, GLA_QK))
    for c in range(D_MODEL // FFN_CK):
        cols = slice(c * FFN_CK, (c + 1) * FFN_CK)
        v = jnp.dot(xn, w_ref[:, GLA_V0 + c * FFN_CK:GLA_V0 + (c + 1) * FFN_CK],
                    preferred_element_type=F32)
        ops_ref[:, GLA_OP_V + c * FFN_CK:GLA_OP_V + (c + 1) * FFN_CK] = v.astype(BF16)
        r = jnp.dot(xn, w_ref[:, GLA_R0 + c * FFN_CK:GLA_R0 + (c + 1) * FFN_CK],
                    preferred_element_type=F32)
        ops_ref[:, GLA_OP_GATE + c * FFN_CK:GLA_OP_GATE + (c + 1) * FFN_CK] = _silu(r).astype(BF16)


def _gla_in(x, nw, w, wg2, bg, wexp, *, t, n_pt):
    in_specs, args = _row_tile_specs(x, n_pt)
    n_x = len(args)
    in_specs += [_resident((1, D_MODEL)), _resident((D_MODEL, GLA_P)), _resident((LANES, GLA_QK)),
                 _resident((1, GLA_QK)), _resident((GLA_NLV * CHUNK, 3 * CHUNK))]
    args += [nw, w, wg2, bg, wexp]
    tile = lambda c: pl.BlockSpec((ROW_TILE, c), lambda i: (i, 0))
    return pl.pallas_call(
        functools.partial(_gla_in_body, n_x=n_x, n_pt=n_pt),
        grid=(t // ROW_TILE,),
        in_specs=in_specs,
        out_specs=[tile(GLA_NLV * GLA_SC), tile(GLA_OPS),
                   pl.BlockSpec((ROW_TILE // CHUNK, 8, GLA_QK), lambda i: (i, 0, 0))],
        out_shape=[jax.ShapeDtypeStruct((t, GLA_NLV * GLA_SC), BF16),
                   jax.ShapeDtypeStruct((t, GLA_OPS), BF16),
                   jax.ShapeDtypeStruct((t // CHUNK, 8, GLA_QK), F32)],
        compiler_params=_params(), name="gla_in")(*args)


def _gla_body(sc_ref, ops_ref, ebl_ref, s0_ref, nw_ref,
              o_ref, soutp_ref, souts_ref, st_scr, s0t_scr, *, n_pb, bps):
    blk, is_prompt, first, last = _block_flags(n_pb, bps)
    heads = range(GLA_HEADS)
    chunks = range(GLA_NB)

    @pl.when(first)
    def _():
        st_scr[...] = jnp.zeros_like(st_scr)

    @pl.when(blk == 0)
    def _():
        s0t_scr[...] = jnp.zeros_like(s0t_scr)

    @pl.when(jnp.logical_not(is_prompt))
    def _():
        for j in chunks:
            for h in heads:
                s0t_scr[j, h] = s0_ref[j, h].T

    row = lax.broadcasted_iota(jnp.int32, (CHUNK, CHUNK), 0)
    col = lax.broadcasted_iota(jnp.int32, (CHUNK, CHUNK), 1)
    masks = [row == col] + [_pair_mask(s) for s in GLA_LEVELS]
    rs = [slice(j * CHUNK, (j + 1) * CHUNK) for j in chunks]
    cv = [slice(h * GLA_DV, (h + 1) * GLA_DV) for h in heads]

    def op(j, base, h, width):
        return ops_ref[rs[j], base + h * width:base + (h + 1) * width]

    def scores(j, h, lvl):
        q0 = lvl * GLA_SC + h * GLA_DK
        k0 = q0 + GLA_QK
        return lax.dot_general(sc_ref[rs[j], q0:q0 + GLA_DK], sc_ref[rs[j], k0:k0 + GLA_DK],
                               (((1,), (1,)), ((), ())), preferred_element_type=F32)

    attn = [[jnp.where(masks[0], scores(j, h, 0), 0.0) for h in heads] for j in chunks]
    for lvl in range(1, GLA_NLV):
        attn = [[attn[j][h] + jnp.where(masks[lvl], scores(j, h, lvl), 0.0) for h in heads]
                for j in chunks]
    av = [[jnp.dot(attn[j][h].astype(BF16), op(j, GLA_OP_V, h, GLA_DV), preferred_element_type=F32)
           for h in heads] for j in chunks]
    kv = [[lax.dot_general(op(j, GLA_OP_V, h, GLA_DV), op(j, GLA_OP_KD, h, GLA_DK),
                           (((0,), (0,)), ((), ())), preferred_element_type=F32)
           for h in heads] for j in chunks]

    st = [st_scr[h] for h in heads]
    for j in chunks:
        st = [jnp.where(is_prompt, st[h], s0t_scr[j, h]) for h in heads]
        o = [_mm_nt(op(j, GLA_OP_QE, h, GLA_DK), st[h]) + av[j][h] for h in heads]
        st = [st[h] * ebl_ref[j, 0:1, h * GLA_DK:(h + 1) * GLA_DK] + kv[j][h] for h in heads]
        for h in heads:
            gate = op(j, GLA_OP_GATE, h, GLA_DV)
            o_ref[rs[j], cv[h]] = (_rms(o[h], nw_ref[h:h + 1, :]) * gate).astype(o_ref.dtype)

        @pl.when(jnp.logical_not(is_prompt))
        def _(j=j, st=st):
            for h in heads:
                souts_ref[j, h] = st[h].T

    for h in heads:
        st_scr[h] = st[h]

    @pl.when(last)
    def _():
        for h in heads:
            soutp_ref[0, h] = st[h].T


def _gla_scan(sc, ops, ebl, s0, nw, *, n_p, cpl, n_s):
    t = sc.shape[0]
    rows = GLA_NB * CHUNK
    bps = cpl // GLA_NB
    n_pb = n_p * bps
    s_in, s_out, s_shapes = _state_specs(n_p, n_s, n_pb, bps, GLA_NB, (GLA_HEADS, GLA_DK, GLA_DV))
    return pl.pallas_call(
        functools.partial(_gla_body, n_pb=n_pb, bps=bps),
        grid=(t // rows,),
        in_specs=[pl.BlockSpec((rows, GLA_NLV * GLA_SC), lambda b: (b, 0)),
                  pl.BlockSpec((rows, GLA_OPS), lambda b: (b, 0)),
                  pl.BlockSpec((GLA_NB, 8, GLA_QK), lambda b: (b, 0, 0)),
                  s_in, _resident((GLA_HEADS, GLA_DV))],
        out_specs=[pl.BlockSpec((rows, D_MODEL), lambda b: (b, 0))] + s_out,
        out_shape=[jax.ShapeDtypeStruct((t, D_MODEL), BF16)] + s_shapes,
        scratch_shapes=[pltpu.VMEM((GLA_HEADS, GLA_DV, GLA_DK), F32),
                        pltpu.VMEM((GLA_NB, GLA_HEADS, GLA_DV, GLA_DK), F32)],
        compiler_params=_params(), name="gla_scan")(sc, ops, ebl, s0, nw)


CNV_WIN = CONV_HALO + CHUNK
TILE_CHUNKS = ROW_TILE // CHUNK
CNV_CK = 256


def _cnv_in_body(*refs, n_x, n_pt, tps, n_p):
    x_refs = refs[:n_x]
    (nw_ref, w_ref, b_ref, wdw_ref, bdw_ref, lnw_ref, lnb_ref, halo0_ref,
     a_ref, tails_ref, xc_scr, y_scr, carry_scr) = refs[n_x:]
    i = pl.program_id(0)
    is_sample = i >= n_pt

    @pl.when(i == 0)
    def _():
        carry_scr[...] = jnp.zeros_like(carry_scr)

    xn = _rms(_read_row_tile(x_refs, n_pt), nw_ref[...]).astype(BF16)
    taps = {}
    for w in range(CONV_W):
        s = CONV_HALO - (CONV_W - 1) + w
        taps.setdefault(-s % 8, []).append((w, s + (-s % 8)))

    def glu_slab(c0):
        cols = slice(c0, c0 + CNV_CK)
        gcols = slice(D_MODEL + c0, D_MODEL + c0 + CNV_CK)
        val = jnp.dot(xn, w_ref[:, cols], preferred_element_type=F32) + b_ref[:, cols]
        gate = jnp.dot(xn, w_ref[:, gcols], preferred_element_type=F32) + b_ref[:, gcols]
        glu = val * _sigmoid(gate)
        for j in range(TILE_CHUNKS):
            r0 = j * CNV_WIN
            xc_scr[r0 + CONV_HALO:r0 + CNV_WIN, cols] = glu[j * CHUNK:(j + 1) * CHUNK]
            if j == 0:
                first = jnp.logical_or(is_sample, i % tps == 0)
                prev = carry_scr[:, cols]
            else:
                first = is_sample
                prev = glu[j * CHUNK - CONV_HALO:j * CHUNK]
            seq = jnp.where(is_sample, n_p + (i - n_pt) * TILE_CHUNKS + j, i // tps)
            xc_scr[r0:r0 + CONV_HALO, cols] = jnp.where(first, halo0_ref[seq, :, cols], prev)

    def conv_slab(c0):
        for j in range(TILE_CHUNKS):
            for b0 in range(c0, c0 + CNV_CK, LANES):
                cs = slice(b0, b0 + LANES)
                wv = xc_scr[j * CNV_WIN:(j + 1) * CNV_WIN, cs]
                y = None
                for r, group in sorted(taps.items()):
                    rolled = wv if r == 0 else pltpu.roll(wv, r, 0)
                    for w, a0 in group:
                        term = wdw_ref[w:w + 1, cs] * rolled[a0:a0 + CHUNK]
                        y = term if y is None else y + term
                y_scr[j * CHUNK:(j + 1) * CHUNK, cs] = y + bdw_ref[:, cs]

    slabs = list(range(0, D_MODEL, CNV_CK))
    for n, c0 in enumerate(slabs + [None]):
        if c0 is not None:
            glu_slab(c0)
        if n > 0:
            conv_slab(slabs[n - 1])
    for j in range(TILE_CHUNKS):
        rs = slice(j * CHUNK, (j + 1) * CHUNK)
        a_ref[rs, :] = _silu(_layernorm(y_scr[rs, :], lnw_ref[...], lnb_ref[...])).astype(BF16)
        tails_ref[j] = xc_scr[(j + 1) * CNV_WIN - CONV_HALO:(j + 1) * CNV_WIN, :]
    carry_scr[...] = xc_scr[TILE_CHUNKS * CNV_WIN - CONV_HALO:TILE_CHUNKS * CNV_WIN, :]


def _cnv_in(x, nw, w, b, wdw, bdw, lnw, lnb, halo0, *, t, n_pt, tps, n_p):
    in_specs, args = _row_tile_specs(x, n_pt)
    n_x = len(args)
    in_specs += [_resident((1, D_MODEL)), _resident((D_MODEL, 2 * D_MODEL)), _resident((1, 2 * D_MODEL)),
                 _resident((CONV_HALO, D_MODEL)), _resident((1, D_MODEL)),
                 _resident((1, D_MODEL)), _resident((1, D_MODEL)), _resident(halo0.shape)]
    args += [nw, w, b, wdw, bdw, lnw, lnb, halo0]
    return pl.pallas_call(
        functools.partial(_cnv_in_body, n_x=n_x, n_pt=n_pt, tps=tps, n_p=n_p),
        grid=(t // ROW_TILE,),
        in_specs=in_specs,
        out_specs=[pl.BlockSpec((ROW_TILE, D_MODEL), lambda i: (i, 0)),
                   pl.BlockSpec((TILE_CHUNKS, CONV_HALO, D_MODEL), lambda i: (i, 0, 0))],
        out_shape=[jax.ShapeDtypeStruct((t, D_MODEL), BF16),
                   jax.ShapeDtypeStruct((t // CHUNK, CONV_HALO, D_MODEL), F32)],
        scratch_shapes=[pltpu.VMEM((TILE_CHUNKS * CNV_WIN, D_MODEL), F32),
                        pltpu.VMEM((ROW_TILE, D_MODEL), F32),
                        pltpu.VMEM((CONV_HALO, D_MODEL), F32)],
        compiler_params=_params(), name="cnv_in")(*args)


def _pad_cols(w, n):
    return jnp.pad(w, ((0, 0), (0, n - w.shape[1])))


def _row(v):
    return v.reshape(1, -1).astype(F32)


def _gdn_mix(x, lay, dims, mix_nw, cache, state, w_in, conv_w, a_log, dt_bias, norm_w):
    n_p, cpl, n_s = lay
    halo0 = jnp.concatenate([jnp.zeros((n_p, 8, GDN_CONV_CH), F32),
                             jnp.pad(cache, ((0, 0), (8 - (GDN_CONV - 1), 0), (0, 0)))], axis=0)
    lane_pad = lambda v: jnp.pad(v.reshape(1, -1), ((0, 0), (GDN_HEADS, LANES - 2 * GDN_HEADS)))
    alog, dtb = lane_pad(a_log), lane_pad(dt_bias)
    ops, kbq, bgp, tails = _gdn_in(
        x, _row(mix_nw), _pad_cols(w_in, GDN_P).astype(BF16), conv_w, alog, dtb, halo0,
        tps=cpl * CHUNK // ROW_TILE, n_p=n_p, **dims)
    u, wq, attn, egl = _gdn_pre(ops, kbq, bgp, alog, dtb)
    o, s_p, s_s = _gdn_scan(u, wq, ops, attn, egl, state, _row(norm_w), n_p=n_p, cpl=cpl, n_s=n_s)
    ends = np.concatenate([(np.arange(n_p) + 1) * cpl - 1, n_p * cpl + np.arange(n_s)])
    conv_rows = tails[jnp.asarray(ends, jnp.int32), 8 - (GDN_CONV - 1):, :]
    return o, (conv_rows[:n_p], conv_rows[n_p:], s_p, s_s)


def _gla_mix(x, lay, dims, mix_nw, state, w_in, w_g2, b_g, norm_w):
    n_p, cpl, n_s = lay
    wg2 = jnp.pad(w_g2, ((0, LANES - GLA_RANK), (0, 0))).astype(BF16)
    wexp = jnp.asarray(_gla_exponent_matrix(), BF16)
    sc, ops, ebl = _gla_in(x, _row(mix_nw), _pad_cols(w_in, GLA_P).astype(BF16), wg2, _row(b_g), wexp,
                           **dims)
    o, s_p, s_s = _gla_scan(sc, ops, ebl, state, norm_w, n_p=n_p, cpl=cpl, n_s=n_s)
    return o, (s_p, s_s)


def _cnv_mix(x, lay, dims, mix_nw, cache, w_pw1, b_pw1, w_dw, b_dw, ln_w, ln_b):
    n_p, cpl, n_s = lay
    pad_rows = CONV_HALO - (CONV_W - 1)
    halo0 = jnp.concatenate([jnp.zeros((n_p, CONV_HALO, D_MODEL), F32),
                             jnp.pad(cache, ((0, 0), (pad_rows, 0), (0, 0)))], axis=0)
    wdw = jnp.pad(w_dw, ((0, CONV_HALO - CONV_W), (0, 0)))
    a, tails = _cnv_in(x, _row(mix_nw), w_pw1.astype(BF16), _row(b_pw1), wdw, _row(b_dw),
                       _row(ln_w), _row(ln_b), halo0, tps=cpl * CHUNK // ROW_TILE, n_p=n_p, **dims)
    ends = np.concatenate([(np.arange(n_p) + 1) * cpl - 1, n_p * cpl + np.arange(n_s)])
    rows = tails[jnp.asarray(ends, jnp.int32), pad_rows:, :]
    return a, (rows[:n_p], rows[n_p:])


def kernel(x_prompt, x_sample, cache_gdn_conv, state_gdn, state_gla, cache_conformer, mix_norm_w, ffn_norm_w, ffn_w1, ffn_w2, final_norm_w, gdn_w_in, gdn_conv_w, gdn_a_log, gdn_dt_bias, gdn_norm_w, gdn_w_out, gla_w_in, gla_w_g2, gla_b_g, gla_norm_w, gla_w_out, gmlp_w_in, gmlp_b_in, gmlp_ln_w, gmlp_ln_b, gmlp_w_s, gmlp_b_s, gmlp_w_out, gmlp_b_out, cnv_w_pw1, cnv_b_pw1, cnv_w_dw, cnv_b_dw, cnv_ln_w, cnv_ln_b, cnv_w_pw2, cnv_b_pw2):
    n_p, l_p, d = x_prompt.shape
    n_s, l_s, _ = x_sample.shape
    assert d == D_MODEL and l_s == CHUNK and l_p % ROW_TILE == 0
    assert n_s % max(SCAN_NB, GDN_NB, GLA_NB) == 0
    t_p, t_s = n_p * l_p, n_s * l_s
    assert t_p % ROW_TILE == 0 and t_s % ROW_TILE == 0
    lay = (n_p, l_p // CHUNK, n_s)
    dims = dict(t=t_p + t_s, n_pt=t_p // ROW_TILE)
    depth = mix_norm_w.shape[0]

    x = (x_prompt.reshape(t_p, d), x_sample.reshape(t_s, d))
    outs = {k: [] for k in ("gdn_cp", "gdn_cs", "gdn_sp", "gdn_ss", "gla_sp", "gla_ss",
                            "gmlp_vs", "cnv_p", "cnv_s")}
    w1_all, w2_all = ffn_w1.astype(BF16), ffn_w2.astype(BF16)
    for i in range(depth):
        kind, j = i % 4, i // 4
        ffn = (_row(ffn_norm_w[i]), w1_all, w2_all, _row(final_norm_w) if i == depth - 1 else None)
        if kind == 0:
            o, (cp, cs, sp, ss) = _gdn_mix(x, lay, dims, mix_norm_w[i], cache_gdn_conv[j], state_gdn[j],
                                           gdn_w_in[j], gdn_conv_w[j], gdn_a_log[j], gdn_dt_bias[j],
                                           gdn_norm_w[j])
            outs["gdn_cp"].append(cp); outs["gdn_cs"].append(cs)
            outs["gdn_sp"].append(sp); outs["gdn_ss"].append(ss)
            res = _post([o], x, gdn_w_out[j].astype(BF16), None, *ffn,
                        prologue="plain", layer=i, name=f"post_{i}", **dims)
        elif kind == 1:
            o, (sp, ss) = _gla_mix(x, lay, dims, mix_norm_w[i], state_gla[j], gla_w_in[j], gla_w_g2[j],
                                   gla_b_g[j], gla_norm_w[j])
            outs["gla_sp"].append(sp); outs["gla_ss"].append(ss)
            res = _post([o], x, gla_w_out[j].astype(BF16), None, *ffn,
                        prologue="plain", layer=i, name=f"post_{i}", **dims)
        elif kind == 2:
            zz = _proj(x, _row(mix_norm_w[i]), gmlp_w_in[j].astype(BF16), _row(gmlp_b_in[j]),
                       n_out=2 * D_MODEL, ck=512, epilogue="gelu", name="gmlp_in", **dims)
            bs_full = jnp.repeat(gmlp_b_s[j].T, GMLP_GC, axis=1)
            res = _post([zz, _row(gmlp_ln_w[j]), _row(gmlp_ln_b[j]), gmlp_w_s[j], bs_full], x,
                        gmlp_w_out[j].astype(BF16), _row(gmlp_b_out[j]), *ffn,
                        prologue="gmlp", layer=i, name=f"post_{i}", **dims)
            outs["gmlp_vs"].append(res[-1].reshape(n_s, l_s, d))
            res = res[:-1]
        else:
            a, (cp, cs) = _cnv_mix(x, lay, dims, mix_norm_w[i], cache_conformer[j], cnv_w_pw1[j],
                                   cnv_b_pw1[j], cnv_w_dw[j], cnv_b_dw[j], cnv_ln_w[j], cnv_ln_b[j])
            outs["cnv_p"].append(cp); outs["cnv_s"].append(cs)
            res = _post([a], x, cnv_w_pw2[j].astype(BF16), _row(cnv_b_pw2[j]), *ffn,
                        prologue="plain", layer=i, name=f"post_{i}", **dims)
        x = tuple(res) if i == depth - 1 else res[0]
    y_p, y_s = x
    stack = lambda xs: xs[0][None] if len(xs) == 1 else jnp.stack(xs)
    return (y_p.reshape(n_p, l_p, d), y_s.reshape(n_s, l_s, d),
            stack(outs["gdn_cp"]), stack(outs["gdn_cs"]),
            stack(outs["gdn_sp"]), stack(outs["gdn_ss"]),
            stack(outs["gla_sp"]), stack(outs["gla_ss"]),
            stack(outs["gmlp_vs"]),
            stack(outs["cnv_p"]), stack(outs["cnv_s"]))
```
